```python
import math
import jax
import jax.numpy as jnp
from jax import lax
import numpy as np

D_MODEL = 1024
BATCH = 8
SEQ = 2048
DEPTH = 2
DEC_BATCH = 128
DEC_SEQ = 1
PAST_LEN = 16384
PAGE_SIZE = 128

F32 = jnp.float32
EPS = 1e-6
MIX_WIDTH = 512
N_BRANCH = 4
CONV_K = 4
CHUNK = 128
SSD_HEAD_DIM = 64
SSD_HEADS = MIX_WIDTH // SSD_HEAD_DIM
SSD_GROUPS = 2
SSD_STATE = 64
SSD_CONV_DIM = MIX_WIDTH + 2 * SSD_GROUPS * SSD_STATE
S5_GROUP_DIM = 16
S5_GROUPS = MIX_WIDTH // S5_GROUP_DIM
S5_STATE = 64
LRU_BLOCKS = 8
LRU_BLOCK_DIM = MIX_WIDTH // LRU_BLOCKS
LRU_C = 8.0
RET_HEADS = 8
RET_HEAD_DIM = MIX_WIDTH // RET_HEADS
ROPE_BASE = 10000.0
D_FF = 2816
N_EXPERTS = 8
TOP_K = 2
D_FF_EXPERT = 1408
N_DENSE = (DEPTH + 1) // 2
N_MOE = DEPTH // 2

SEG_SIZES = (('ssd_z', MIX_WIDTH), ('ssd_xbc', SSD_CONV_DIM), ('ssd_dt', SSD_HEADS),
             ('s5_u', MIX_WIDTH), ('lru_gate', MIX_WIDTH), ('lru_x', MIX_WIDTH),
             ('ret_q', MIX_WIDTH), ('ret_k', MIX_WIDTH), ('ret_v', MIX_WIDTH), ('ret_g', MIX_WIDTH),
             ('merge', N_BRANCH * D_MODEL))
IN_DIM = sum(size for _, size in SEG_SIZES)
STATE_NAMES = ('ssd', 'ssd_conv', 's5_re', 's5_im', 'lru', 'lru_conv', 'ret')

kernel_name = 'hybrid_ssd_s5_rglru_retention_decode_step'


def rmsnorm(x, g):
    xf = x.astype(F32)
    y = xf * lax.rsqrt(jnp.mean(xf * xf, axis=-1, keepdims=True) + EPS)
    return (y * g.astype(F32)).astype(x.dtype)


def split_proj(z):
    out = {}
    off = 0
    for name, size in SEG_SIZES:
        out[name] = z[..., off:off + size]
        off += size
    return out


def causal_dwconv(x, buf, w, b):
    L = x.shape[1]
    xp = jnp.concatenate([buf.astype(x.dtype), x], axis=1)
    out = b
    for k in range(CONV_K):
        out = out + xp[:, k:k + L] * w[k]
    return out, xp[:, xp.shape[1] - (CONV_K - 1):]


def linear_scan(a, bx, h0):
    bx = bx.at[:, 0].add(a[:, 0] * h0)

    def combine(left, right):
        a_l, b_l = left
        a_r, b_r = right
        return a_l * a_r, a_r * b_l + b_r

    _, h = lax.associative_scan(combine, (a, bx), axis=1)
    return h


def chunked_decay_attn(q, k, v, log_decay, state0):
    bsz, L, nh, dk = q.shape
    dv = v.shape[-1]
    qlen = CHUNK if L % CHUNK == 0 else L
    nc = L // qlen
    qc = q.reshape(bsz, nc, qlen, nh, dk)
    kc = k.reshape(bsz, nc, qlen, nh, dk)
    vc = v.reshape(bsz, nc, qlen, nh, dv)
    acum = jnp.cumsum(log_decay.reshape(bsz, nc, qlen, nh), axis=2)
    diff = acum[:, :, :, None, :] - acum[:, :, None, :, :]
    causal = jnp.tril(jnp.ones((qlen, qlen), dtype=bool))[None, None, :, :, None]
    decay = jnp.where(causal, jnp.exp(jnp.where(causal, diff, 0.0)), 0.0)
    scores = jnp.einsum('bcihd,bcjhd->bcijh', qc, kc) * decay
    y_intra = jnp.einsum('bcijh,bcjhe->bcihe', scores, vc)
    to_end = jnp.exp(acum[:, :, -1:, :] - acum)
    chunk_states = jnp.einsum('bcjh,bcjhd,bcjhe->bchde', to_end, kc, vc)
    chunk_decay = jnp.exp(acum[:, :, -1, :])

    def step(s, inp):
        cs, cd = inp
        return cd[:, :, None, None] * s + cs, s

    final, starts = lax.scan(step, state0, (jnp.moveaxis(chunk_states, 1, 0), jnp.moveaxis(chunk_decay, 1, 0)))
    starts = jnp.moveaxis(starts, 0, 1)
    y_inter = jnp.einsum('bcihd,bchde,bcih->bcihe', qc, starts, jnp.exp(acum))
    return (y_intra + y_inter).reshape(bsz, L, nh, dv), final


def ssd_mixer(z, xbc, dt_raw, conv_buf, ssm0, conv_w, conv_b, dt_bias, a_log, d_skip, norm_g):
    bsz, L, _ = xbc.shape
    xbc_c, new_buf = causal_dwconv(xbc, conv_buf, conv_w, conv_b)
    xbc_c = jax.nn.silu(xbc_c.astype(F32))
    nbc = SSD_GROUPS * SSD_STATE
    xs = xbc_c[..., :MIX_WIDTH].reshape(bsz, L, SSD_HEADS, SSD_HEAD_DIM)
    b_m = xbc_c[..., MIX_WIDTH:MIX_WIDTH + nbc].reshape(bsz, L, SSD_GROUPS, SSD_STATE)
    c_m = xbc_c[..., MIX_WIDTH + nbc:].reshape(bsz, L, SSD_GROUPS, SSD_STATE)
    rep = SSD_HEADS // SSD_GROUPS
    b_h = jnp.repeat(b_m, rep, axis=2)
    c_h = jnp.repeat(c_m, rep, axis=2)
    dt = jax.nn.softplus(dt_raw.astype(F32) + dt_bias.astype(F32))
    a = -jnp.exp(a_log.astype(F32))
    y, ssm_new = chunked_decay_attn(c_h, b_h * dt[..., None], xs, dt * a, ssm0)
    y = y + d_skip.astype(F32)[:, None] * xs
    y = y.reshape(bsz, L, MIX_WIDTH) * jax.nn.silu(z.astype(F32))
    return rmsnorm(y, norm_g), new_buf, ssm_new


def s5_mixer(u, x0, lam_re, lam_im, b_re, b_im, c_re, c_im, d_skip, log_dt, w_glu):
    bsz, L, _ = u.shape
    ug = u.astype(F32).reshape(bsz, L, S5_GROUPS, S5_GROUP_DIM)
    lam = lax.complex(lam_re.astype(F32), lam_im.astype(F32))
    dt = jnp.exp(log_dt.astype(F32))[:, None]
    lam_bar = jnp.exp(lam * dt)
    b_bar = ((lam_bar - 1.0) / lam)[:, :, None] * lax.complex(b_re.astype(F32), b_im.astype(F32))
    c_mat = lax.complex(c_re.astype(F32), c_im.astype(F32))
    bu = jnp.einsum('gnc,blgc->blgn', b_bar, ug.astype(b_bar.dtype))
    xs = linear_scan(jnp.broadcast_to(lam_bar, bu.shape), bu, x0)
    y = jnp.real(jnp.einsum('gcn,blgn->blgc', c_mat, xs)) + d_skip.astype(F32) * ug
    y = jax.nn.gelu(y.reshape(bsz, L, MIX_WIDTH))
    y = y * jax.nn.sigmoid(y @ w_glu.astype(F32))
    return y, xs[:, -1]


def rglru_mixer(x_in, gate_in, conv_buf, h0, conv_w, conv_b, wa, ba, wx, bx, lam):
    bsz, L, _ = x_in.shape
    xc, new_buf = causal_dwconv(x_in, conv_buf, conv_w, conv_b)
    xc = xc.astype(F32)
    xb = xc.reshape(bsz, L, LRU_BLOCKS, LRU_BLOCK_DIM)
    r = jax.nn.sigmoid(jnp.einsum('blhi,hij->blhj', xb, wa.astype(F32)).reshape(bsz, L, MIX_WIDTH) + ba.astype(F32))
    i = jax.nn.sigmoid(jnp.einsum('blhi,hij->blhj', xb, wx.astype(F32)).reshape(bsz, L, MIX_WIDTH) + bx.astype(F32))
    log_a = -LRU_C * r * jax.nn.softplus(-lam.astype(F32))
    a = jnp.exp(log_a)
    h = linear_scan(a, jnp.sqrt(-jnp.expm1(2.0 * log_a)) * (i * xc), h0)
    y = h * jax.nn.gelu(gate_in.astype(F32))
    return y, new_buf, h[:, -1]


def rotary(x, pos):
    half = x.shape[-1] // 2
    inv = ROPE_BASE ** (-jnp.arange(half, dtype=F32) / half)
    ang = pos[:, None] * inv[None, :]
    cos = jnp.cos(ang)[None, :, None, :]
    sin = jnp.sin(ang)[None, :, None, :]
    x1, x2 = x[..., :half], x[..., half:]
    return jnp.concatenate([x1 * cos - x2 * sin, x1 * sin + x2 * cos], axis=-1)


def retention_mixer(q, k, v, g, ret0, pos0, gn_g):
    bsz, L, _ = q.shape
    pos = pos0 + jnp.arange(L, dtype=F32)
    shp = (bsz, L, RET_HEADS, RET_HEAD_DIM)
    qh = rotary(q.astype(F32).reshape(shp), pos) * (RET_HEAD_DIM ** -0.5)
    kh = rotary(k.astype(F32).reshape(shp), pos)
    vh = v.astype(F32).reshape(shp)
    log_gamma = jnp.log1p(-jnp.exp2(-5.0 - jnp.arange(RET_HEADS, dtype=F32)))
    o, ret_new = chunked_decay_attn(qh, kh, vh, jnp.broadcast_to(log_gamma, (bsz, L, RET_HEADS)), ret0)
    mu = jnp.mean(o, axis=-1, keepdims=True)
    var = jnp.mean(jnp.square(o - mu), axis=-1, keepdims=True)
    o = ((o - mu) * lax.rsqrt(var + 1e-5)).reshape(bsz, L, MIX_WIDTH) * gn_g.astype(F32)
    return jax.nn.silu(g.astype(F32)) * o, ret_new


def swiglu(x, w1, w3, w2):
    return (jax.nn.silu(x @ w1) * (x @ w3)) @ w2


def moe_swiglu(x, router, w1, w3, w2):
    logits = (x @ router).astype(F32)
    top_v, top_i = lax.top_k(logits, TOP_K)
    top_w = jax.nn.softmax(top_v, axis=-1)
    gates = jnp.sum(jax.nn.one_hot(top_i, N_EXPERTS, dtype=F32) * top_w[..., None], axis=-2)
    out = jnp.zeros(x.shape, F32)
    for e in range(N_EXPERTS):
        out = out + gates[..., e:e + 1] * swiglu(x, w1[e], w3[e], w2[e]).astype(F32)
    return out.astype(x.dtype)


def zero_states(bsz, dtype):
    return {
        'ssd': jnp.zeros((DEPTH, bsz, SSD_HEADS, SSD_STATE, SSD_HEAD_DIM), F32),
        'ssd_conv': jnp.zeros((DEPTH, bsz, CONV_K - 1, SSD_CONV_DIM), dtype),
        's5_re': jnp.zeros((DEPTH, bsz, S5_GROUPS, S5_STATE), F32),
        's5_im': jnp.zeros((DEPTH, bsz, S5_GROUPS, S5_STATE), F32),
        'lru': jnp.zeros((DEPTH, bsz, MIX_WIDTH), F32),
        'lru_conv': jnp.zeros((DEPTH, bsz, CONV_K - 1, MIX_WIDTH), dtype),
        'ret': jnp.zeros((DEPTH, bsz, RET_HEADS, RET_HEAD_DIM, RET_HEAD_DIM), F32),
    }


def trunk(x, pos0, st, W):
    bsz, L, _ = x.shape
    h = x
    new = {name: [] for name in STATE_NAMES}
    for l in range(DEPTH):
        hn = rmsnorm(h, W['norm_mix'][l])
        p = split_proj(hn @ W['w_in'][l])
        y_ssd, buf_ssd, s_ssd = ssd_mixer(p['ssd_z'], p['ssd_xbc'], p['ssd_dt'], st['ssd_conv'][l],
                                          st['ssd'][l].astype(F32), W['ssd_conv_w'][l], W['ssd_conv_b'][l],
                                          W['ssd_dt_bias'][l], W['ssd_a_log'][l], W['ssd_d'][l], W['ssd_norm'][l])
        s5_x0 = lax.complex(st['s5_re'][l].astype(F32), st['s5_im'][l].astype(F32))
        y_s5, s_s5 = s5_mixer(p['s5_u'], s5_x0, W['s5_lambda_re'][l], W['s5_lambda_im'][l], W['s5_b_re'][l],
                              W['s5_b_im'][l], W['s5_c_re'][l], W['s5_c_im'][l], W['s5_d'][l],
                              W['s5_log_dt'][l], W['s5_glu'][l])
        y_lru, buf_lru, s_lru = rglru_mixer(p['lru_x'], p['lru_gate'], st['lru_conv'][l], st['lru'][l].astype(F32),
                                            W['lru_conv_w'][l], W['lru_conv_b'][l], W['lru_wa'][l], W['lru_ba'][l],
                                            W['lru_wx'][l], W['lru_bx'][l], W['lru_lambda'][l])
        y_ret, s_ret = retention_mixer(p['ret_q'], p['ret_k'], p['ret_v'], p['ret_g'], st['ret'][l].astype(F32),
                                       pos0, W['ret_gn'][l])
        ys = jnp.stack([y_ssd, y_s5, y_lru, y_ret], axis=2).astype(x.dtype)
        branches = jnp.einsum('blkc,kcd->blkd', ys, W['w_branch'][l])
        gates = jax.nn.sigmoid(p['merge'].reshape(bsz, L, N_BRANCH, D_MODEL))
        h = h + jnp.sum(gates * branches, axis=2) @ W['w_out'][l]
        hn = rmsnorm(h, W['norm_ffn'][l])
        j = l // 2
        if l % 2 == 0:
            h = h + swiglu(hn, W['ffn_w1'][j], W['ffn_w3'][j], W['ffn_w2'][j])
        else:
            h = h + moe_swiglu(hn, W['moe_router'][j], W['moe_w1'][j], W['moe_w3'][j], W['moe_w2'][j])
        vals = (s_ssd, buf_ssd, jnp.real(s_s5), jnp.imag(s_s5), s_lru, buf_lru, s_ret)
        for name, val in zip(STATE_NAMES, vals):
            new[name].append(val)
    y = rmsnorm(h, W['norm_final'])
    return y, {name: jnp.stack(new[name]) for name in STATE_NAMES}


def setup_inputs(seed: int = 0) -> dict:
    key = jax.random.key(seed)
    ks = iter(jax.random.split(key, 64))

    def nrm(shape, scale):
        return jax.random.normal(next(ks), shape, F32) * scale

    def unif(shape, lo, hi):
        return jax.random.uniform(next(ks), shape, F32, lo, hi)

    def gain(shape):
        return 1.0 + nrm(shape, 0.02)

    inp = {}
    inp['x_prompt'] = nrm((BATCH, SEQ, D_MODEL), 1.0)
    inp['x_sample'] = nrm((DEC_BATCH, DEC_SEQ, D_MODEL), 1.0)
    inp['state_ssd'] = nrm((DEPTH, DEC_BATCH, SSD_HEADS, SSD_STATE, SSD_HEAD_DIM), 0.3)
    inp['state_ssd_conv'] = nrm((DEPTH, DEC_BATCH, CONV_K - 1, SSD_CONV_DIM), 1.0)
    inp['state_s5_re'] = nrm((DEPTH, DEC_BATCH, S5_GROUPS, S5_STATE), 0.5)
    inp['state_s5_im'] = nrm((DEPTH, DEC_BATCH, S5_GROUPS, S5_STATE), 0.5)
    inp['state_lru'] = nrm((DEPTH, DEC_BATCH, MIX_WIDTH), 0.5)
    inp['state_lru_conv'] = nrm((DEPTH, DEC_BATCH, CONV_K - 1, MIX_WIDTH), 1.0)
    inp['state_ret'] = nrm((DEPTH, DEC_BATCH, RET_HEADS, RET_HEAD_DIM, RET_HEAD_DIM), 0.5)
    inp['norm_mix'] = gain((DEPTH, D_MODEL))
    inp['w_in'] = nrm((DEPTH, D_MODEL, IN_DIM), D_MODEL ** -0.5)
    inp['ssd_conv_w'] = nrm((DEPTH, CONV_K, SSD_CONV_DIM), CONV_K ** -0.5)
    inp['ssd_conv_b'] = nrm((DEPTH, SSD_CONV_DIM), 0.02)
    dt0 = jnp.exp(unif((DEPTH, SSD_HEADS), math.log(1e-3), math.log(1e-1)))
    inp['ssd_dt_bias'] = dt0 + jnp.log(-jnp.expm1(-dt0))
    inp['ssd_a_log'] = jnp.log(unif((DEPTH, SSD_HEADS), 1.0, 16.0))
    inp['ssd_d'] = gain((DEPTH, SSD_HEADS))
    inp['ssd_norm'] = gain((DEPTH, MIX_WIDTH))
    inp['s5_lambda_re'] = -0.5 + nrm((DEPTH, S5_GROUPS, S5_STATE), 0.01)
    inp['s5_lambda_im'] = math.pi * jnp.arange(S5_STATE, dtype=F32) + nrm((DEPTH, S5_GROUPS, S5_STATE), 0.01)
    inp['s5_b_re'] = nrm((DEPTH, S5_GROUPS, S5_STATE, S5_GROUP_DIM), (2 * S5_GROUP_DIM) ** -0.5)
    inp['s5_b_im'] = nrm((DEPTH, S5_GROUPS, S5_STATE, S5_GROUP_DIM), (2 * S5_GROUP_DIM) ** -0.5)
    inp['s5_c_re'] = nrm((DEPTH, S5_GROUPS, S5_GROUP_DIM, S5_STATE), (2 * S5_STATE) ** -0.5)
    inp['s5_c_im'] = nrm((DEPTH, S5_GROUPS, S5_GROUP_DIM, S5_STATE), (2 * S5_STATE) ** -0.5)
    inp['s5_d'] = gain((DEPTH, S5_GROUPS, S5_GROUP_DIM))
    inp['s5_log_dt'] = unif((DEPTH, S5_GROUPS), math.log(1e-3), math.log(1e-1))
    inp['s5_glu'] = nrm((DEPTH, MIX_WIDTH, MIX_WIDTH), MIX_WIDTH ** -0.5)
    inp['lru_conv_w'] = nrm((DEPTH, CONV_K, MIX_WIDTH), CONV_K ** -0.5)
    inp['lru_conv_b'] = nrm((DEPTH, MIX_WIDTH), 0.02)
    inp['lru_wa'] = nrm((DEPTH, LRU_BLOCKS, LRU_BLOCK_DIM, LRU_BLOCK_DIM), LRU_BLOCK_DIM ** -0.5)
    inp['lru_ba'] = nrm((DEPTH, MIX_WIDTH), 0.02)
    inp['lru_wx'] = nrm((DEPTH, LRU_BLOCKS, LRU_BLOCK_DIM, LRU_BLOCK_DIM), LRU_BLOCK_DIM ** -0.5)
    inp['lru_bx'] = nrm((DEPTH, MIX_WIDTH), 0.02)
    s_base = unif((DEPTH, MIX_WIDTH), 0.9, 0.999) ** (1.0 / LRU_C)
    inp['lru_lambda'] = jnp.log(s_base) - jnp.log1p(-s_base)
    inp['ret_gn'] = gain((DEPTH, MIX_WIDTH))
    inp['w_branch'] = nrm((DEPTH, N_BRANCH, MIX_WIDTH, D_MODEL), MIX_WIDTH ** -0.5)
    inp['w_out'] = nrm((DEPTH, D_MODEL, D_MODEL), D_MODEL ** -0.5)
    inp['norm_ffn'] = gain((DEPTH, D_MODEL))
    inp['ffn_w1'] = nrm((N_DENSE, D_MODEL, D_FF), D_MODEL ** -0.5)
    inp['ffn_w3'] = nrm((N_DENSE, D_MODEL, D_FF), D_MODEL ** -0.5)
    inp['ffn_w2'] = nrm((N_DENSE, D_FF, D_MODEL), D_FF ** -0.5)
    inp['moe_router'] = nrm((N_MOE, D_MODEL, N_EXPERTS), D_MODEL ** -0.5)
    inp['moe_w1'] = nrm((N_MOE, N_EXPERTS, D_MODEL, D_FF_EXPERT), D_MODEL ** -0.5)
    inp['moe_w3'] = nrm((N_MOE, N_EXPERTS, D_MODEL, D_FF_EXPERT), D_MODEL ** -0.5)
    inp['moe_w2'] = nrm((N_MOE, N_EXPERTS, D_FF_EXPERT, D_MODEL), D_FF_EXPERT ** -0.5)
    inp['norm_final'] = gain((D_MODEL,))
    return inp


def reference(x_prompt, x_sample, state_ssd, state_ssd_conv, state_s5_re, state_s5_im, state_lru, state_lru_conv,
              state_ret, norm_mix, w_in, ssd_conv_w, ssd_conv_b, ssd_dt_bias, ssd_a_log, ssd_d, ssd_norm,
              s5_lambda_re, s5_lambda_im, s5_b_re, s5_b_im, s5_c_re, s5_c_im, s5_d, s5_log_dt, s5_glu,
              lru_conv_w, lru_conv_b, lru_wa, lru_ba, lru_wx, lru_bx, lru_lambda, ret_gn, w_branch, w_out,
              norm_ffn, ffn_w1, ffn_w3, ffn_w2, moe_router, moe_w1, moe_w3, moe_w2, norm_final):
    W = dict(norm_mix=norm_mix, w_in=w_in, ssd_conv_w=ssd_conv_w, ssd_conv_b=ssd_conv_b, ssd_dt_bias=ssd_dt_bias,
             ssd_a_log=ssd_a_log, ssd_d=ssd_d, ssd_norm=ssd_norm, s5_lambda_re=s5_lambda_re,
             s5_lambda_im=s5_lambda_im, s5_b_re=s5_b_re, s5_b_im=s5_b_im, s5_c_re=s5_c_re, s5_c_im=s5_c_im,
             s5_d=s5_d, s5_log_dt=s5_log_dt, s5_glu=s5_glu, lru_conv_w=lru_conv_w, lru_conv_b=lru_conv_b,
             lru_wa=lru_wa, lru_ba=lru_ba, lru_wx=lru_wx, lru_bx=lru_bx, lru_lambda=lru_lambda, ret_gn=ret_gn,
             w_branch=w_branch, w_out=w_out, norm_ffn=norm_ffn, ffn_w1=ffn_w1, ffn_w3=ffn_w3, ffn_w2=ffn_w2,
             moe_router=moe_router, moe_w1=moe_w1, moe_w3=moe_w3, moe_w2=moe_w2, norm_final=norm_final)
    y_prompt, sp = trunk(x_prompt, 0.0, zero_states(x_prompt.shape[0], x_prompt.dtype), W)
    st_sample = dict(ssd=state_ssd, ssd_conv=state_ssd_conv, s5_re=state_s5_re, s5_im=state_s5_im,
                     lru=state_lru, lru_conv=state_lru_conv, ret=state_ret)
    y_sample, ss = trunk(x_sample, float(PAST_LEN), st_sample, W)
    return (y_prompt, y_sample,
            sp['ssd'], sp['ssd_conv'], sp['s5_re'], sp['s5_im'], sp['lru'], sp['lru_conv'], sp['ret'],
            ss['ssd'], ss['ssd_conv'], ss['s5_re'], ss['s5_im'], ss['lru'], ss['lru_conv'], ss['ret'])
```

```python
import functools
import math

import jax
import jax.numpy as jnp
import numpy as np
from jax import lax
from jax.experimental import pallas as pl
from jax.experimental.pallas import tpu as pltpu

F32 = jnp.float32
BF16 = jnp.bfloat16
EPS = 1e-6

D_MODEL = 1024
MIX = 512
CONV_K = 4
CHUNK = 128
SSD_HEADS = 8
SSD_HD = 64
SSD_STATE = 64
SSD_GROUPS = 2
SSD_CONV = MIX + 2 * SSD_GROUPS * SSD_STATE
S5_GROUPS = 32
S5_GDIM = 16
S5_STATE = 64
S5_CH = S5_GROUPS * S5_STATE
LRU_BLOCKS = 8
LRU_C = 8.0
RET_HEADS = 8
RET_HD = 64
ROPE_BASE = 10000.0
N_EXPERTS = 8
D_FF_TILE = 1408

Z_MERGE = 0
Z_RET = 4096
Z_LRU = 6144
Z_S5 = 7168
Z_SSD = 7680
Z_WIDTH = 9216

VMEM_LIMIT = 56 * 1024 * 1024
LANES = 128


def _cparams(*sem):
    return pltpu.CompilerParams(dimension_semantics=sem, vmem_limit_bytes=VMEM_LIMIT)


def _bdot(a, b):
    return jnp.dot(a.astype(BF16), b.astype(BF16), preferred_element_type=F32)


def _bdot_nt(a, b):
    return lax.dot_general(a.astype(BF16), b.astype(BF16), (((1,), (1,)), ((), ())), preferred_element_type=F32)


def _bdot_tn(a, b):
    return lax.dot_general(a.astype(BF16), b.astype(BF16), (((0,), (0,)), ((), ())), preferred_element_type=F32)


def _hdot(a, b):
    return jnp.dot(a, b, precision=lax.Precision.HIGHEST, preferred_element_type=F32)


def _split3_dot(x, m01):
    hi = x.astype(BF16)
    r1 = x - hi.astype(F32)
    mid = r1.astype(BF16)
    lo = (r1 - mid.astype(F32)).astype(BF16)
    m = m01.astype(BF16)
    d = functools.partial(jnp.dot, preferred_element_type=F32)
    return (d(lo, m) + d(mid, m)) + d(hi, m)


def _rmsnorm(x, g):
    ms = jnp.mean(x * x, axis=-1, keepdims=True)
    return x * lax.rsqrt(ms + EPS) * g


def _silu(x):
    return x * jax.nn.sigmoid(x)


def _neg_expm1_2x(log_a, a):
    return jnp.tanh(-log_a) * (1.0 + a * a)


def _inproj_kernel(x_ref, g_ref, w_ref, o_ref):
    hn = _rmsnorm(x_ref[...], g_ref[...])
    o_ref[...] = _bdot(hn, w_ref[...])


def _inproj(x, g, w, tm):
    T = x.shape[0]
    tn = 1536
    return pl.pallas_call(
        _inproj_kernel,
        out_shape=jax.ShapeDtypeStruct((T, Z_WIDTH), F32),
        grid=(Z_WIDTH // tn, T // tm),
        in_specs=[pl.BlockSpec((tm, D_MODEL), lambda j, i: (i, 0)),
                  pl.BlockSpec((1, D_MODEL), lambda j, i: (0, 0)),
                  pl.BlockSpec((D_MODEL, tn), lambda j, i: (0, j))],
        out_specs=pl.BlockSpec((tm, tn), lambda j, i: (i, j)),
        compiler_params=_cparams("parallel", "parallel"),
        name="inproj",
    )(x, g, w)


def _conv_chunk(c, x, pad_ref, w_ref, b_ref):
    Lc = x.shape[0]

    @pl.when(c == 0)
    def _():
        pad_ref[0:8, :] = jnp.zeros((8, x.shape[1]), F32)

    pad_ref[8:8 + Lc, :] = x
    out = b_ref[...] + pad_ref[5:5 + Lc, :] * w_ref[0:1, :]
    out = out + pad_ref[6:6 + Lc, :] * w_ref[1:2, :]
    out = out + pad_ref[7:7 + Lc, :] * w_ref[2:3, :]
    out = out + x * w_ref[3:4, :]
    return out


def _conv_finish(pad_ref, Lc):
    pad_ref[0:8, :] = pad_ref[Lc:Lc + 8, :]


def _causal(Lc):
    row = lax.broadcasted_iota(jnp.int32, (Lc, Lc), 0)
    col = lax.broadcasted_iota(jnp.int32, (Lc, Lc), 1)
    return row >= col


def _ssd_kernel(z_ref, cw_ref, cb_ref, dtb_ref, alog_ref, dlane_ref, ng_ref, tri_ref,
                y_ref, st_ref, buf_ref, pad_ref, yacc_ref):
    c = pl.program_id(1)
    nc = pl.num_programs(1)
    Lc = z_ref.shape[0]
    zz = z_ref[...]
    zgate = zz[:, 0:MIX]
    xbc = zz[:, MIX:MIX + SSD_CONV]
    dt_raw = zz[:, MIX + SSD_CONV:MIX + SSD_CONV + LANES]

    @pl.when(c == 0)
    def _():
        st_ref[...] = jnp.zeros(st_ref.shape, F32)

    conv = _conv_chunk(c, xbc, pad_ref, cw_ref, cb_ref)
    xc = _silu(conv)
    xs = xc[:, 0:MIX]
    nbc = SSD_GROUPS * SSD_STATE
    bm = xc[:, MIX:MIX + nbc]
    cm = xc[:, MIX + nbc:MIX + 2 * nbc]
    dt = jax.nn.softplus(dt_raw + dtb_ref[...])
    a = -jnp.exp(alog_ref[...])
    ld = dt * a
    acum = _hdot(tri_ref[...], ld)
    acum_t = acum.T
    dt_t = dt.T
    causal = _causal(Lc)
    rep = SSD_HEADS // SSD_GROUPS
    gmats = []
    for g in range(SSD_GROUPS):
        cg = cm[:, g * SSD_STATE:(g + 1) * SSD_STATE]
        bg = bm[:, g * SSD_STATE:(g + 1) * SSD_STATE]
        gmats.append(_bdot_nt(cg, bg))
    for h in range(SSD_HEADS):
        g = h // rep
        cg = cm[:, g * SSD_STATE:(g + 1) * SSD_STATE]
        bg = bm[:, g * SSD_STATE:(g + 1) * SSD_STATE]
        xh = xs[:, h * SSD_HD:(h + 1) * SSD_HD]
        ac_col = acum[:, h:h + 1]
        ac_row = acum_t[h:h + 1, :]
        diff = ac_col - ac_row
        decay = jnp.where(causal, jnp.exp(jnp.where(causal, diff, 0.0)), 0.0)
        m = gmats[g] * decay * dt_t[h:h + 1, :]
        s_old = st_ref[h]
        y = _bdot(m, xh) + _bdot(cg * jnp.exp(ac_col), s_old)
        ac_last = acum[Lc - 1:Lc, h:h + 1]
        w_end = jnp.exp(ac_last - ac_col) * dt[:, h:h + 1]
        st_ref[h] = jnp.exp(ac_last) * s_old + _bdot_tn(bg * w_end, xh)
        yacc_ref[:, h * SSD_HD:(h + 1) * SSD_HD] = y
    y = yacc_ref[...] + dlane_ref[...] * xs
    y = y * _silu(zgate)
    y_ref[...] = _rmsnorm(y, ng_ref[...])

    @pl.when(c == nc - 1)
    def _():
        buf_ref[...] = pad_ref[Lc + 8 - (CONV_K - 1):Lc + 8, :]

    _conv_finish(pad_ref, Lc)


def _ssd_prompt(z3, p):
    B, L, _ = z3.shape
    Lc = CHUNK
    small = lambda shape: pl.BlockSpec(shape, lambda b, c: tuple(0 for _ in shape))
    return pl.pallas_call(
        _ssd_kernel,
        out_shape=(jax.ShapeDtypeStruct((B, L, MIX), F32),
                   jax.ShapeDtypeStruct((B, SSD_HEADS, SSD_STATE, SSD_HD), F32),
                   jax.ShapeDtypeStruct((B, CONV_K - 1, SSD_CONV), F32)),
        grid=(B, L // Lc),
        in_specs=[pl.BlockSpec((None, Lc, 1536), lambda b, c: (b, c, Z_SSD // 1536)),
                  small((CONV_K, SSD_CONV)), small((1, SSD_CONV)), small((1, LANES)), small((1, LANES)),
                  small((1, MIX)), small((1, MIX)), small((Lc, Lc))],
        out_specs=(pl.BlockSpec((None, Lc, MIX), lambda b, c: (b, c, 0)),
                   pl.BlockSpec((None, SSD_HEADS, SSD_STATE, SSD_HD), lambda b, c: (b, 0, 0, 0)),
                   pl.BlockSpec((None, CONV_K - 1, SSD_CONV), lambda b, c: (b, 0, 0))),
        scratch_shapes=[pltpu.VMEM((Lc + 8, SSD_CONV), F32), pltpu.VMEM((Lc, MIX), F32)],
        compiler_params=_cparams("parallel", "arbitrary"),
        name="ssd_prompt",
    )(z3, p["ssd_cw"], p["ssd_cb"], p["ssd_dtb"], p["ssd_alog"], p["ssd_dlane"], p["ssd_norm"], _tri(Lc))


def _tri(Lc):
    return jnp.asarray(np.tril(np.ones((Lc, Lc), np.float32)))


def _ret_gammas():
    return 1.0 - np.exp2(-5.0 - np.arange(RET_HEADS, dtype=np.float64))


def _ret_tables(Lc):
    gam = _ret_gammas()
    i = np.arange(Lc)
    d = i[:, None] - i[None, :]
    decay = np.where(d >= 0, gam[:, None, None] ** np.maximum(d, 0)[None], 0.0)
    grow = np.zeros((Lc, LANES))
    grow[:, :RET_HEADS] = gam[None, :] ** (i[:, None] + 1)
    toend = np.zeros((Lc, LANES))
    toend[:, :RET_HEADS] = gam[None, :] ** (Lc - 1 - i[:, None])
    return (jnp.asarray(decay, F32), jnp.asarray(grow, F32), jnp.asarray(toend, F32))


def _rope_tables(pos):
    half = RET_HD // 2
    inv = ROPE_BASE ** (-np.arange(half, dtype=np.float64) / half)
    ang = np.asarray(pos, np.float64)[:, None] * inv[None, :]
    cos = np.cos(ang)
    sin = np.sin(ang)
    cos_l = np.tile(np.concatenate([cos, cos], axis=1), (1, RET_HEADS))
    sin_l = np.tile(np.concatenate([-sin, sin], axis=1), (1, RET_HEADS))
    return jnp.asarray(cos_l, F32), jnp.asarray(sin_l, F32)


def _rotary_lanes(x, cos_l, sin_l):
    lane = lax.broadcasted_iota(jnp.int32, x.shape, 1)
    first = (lane & (RET_HD - 1)) < (RET_HD // 2)
    n = x.shape[1]
    swapped = jnp.where(first, pltpu.roll(x, n - RET_HD // 2, 1), pltpu.roll(x, RET_HD // 2, 1))
    return x * cos_l + swapped * sin_l


def _group_norm_head(o):
    mu = jnp.mean(o, axis=-1, keepdims=True)
    d = o - mu
    var = jnp.mean(d * d, axis=-1, keepdims=True)
    return d * lax.rsqrt(var + 1e-5)


def _ret_kernel(z_ref, cos_ref, sin_ref, dec_ref, grow_ref, toend_ref, gn_ref,
                y_ref, st_ref, yacc_ref):
    c = pl.program_id(1)
    Lc = z_ref.shape[0]
    zz = z_ref[...]
    q = _rotary_lanes(zz[:, 0:MIX], cos_ref[...], sin_ref[...]) * (RET_HD ** -0.5)
    k = _rotary_lanes(zz[:, MIX:2 * MIX], cos_ref[...], sin_ref[...])
    v = zz[:, 2 * MIX:3 * MIX]
    gate = zz[:, 3 * MIX:4 * MIX]

    @pl.when(c == 0)
    def _():
        st_ref[...] = jnp.zeros(st_ref.shape, F32)

    gam = _ret_gammas()
    for h in range(RET_HEADS):
        sl = slice(h * RET_HD, (h + 1) * RET_HD)
        qh, kh, vh = q[:, sl], k[:, sl], v[:, sl]
        m = _bdot_nt(qh, kh) * dec_ref[h]
        s_old = st_ref[h]
        o = _bdot(m, vh) + _bdot(qh * grow_ref[:, h:h + 1], s_old)
        st_ref[h] = float(gam[h] ** Lc) * s_old + _bdot_tn(kh * toend_ref[:, h:h + 1], vh)
        yacc_ref[:, sl] = _group_norm_head(o)
    y_ref[...] = _silu(gate) * (yacc_ref[...] * gn_ref[...])


def _ret_prompt(z3, p):
    B, L, _ = z3.shape
    Lc = CHUNK
    cos_l, sin_l = _rope_tables(np.arange(L))
    dec, grow, toend = _ret_tables(Lc)
    small = lambda shape: pl.BlockSpec(shape, lambda b, c: tuple(0 for _ in shape))
    return pl.pallas_call(
        _ret_kernel,
        out_shape=(jax.ShapeDtypeStruct((B, L, MIX), F32),
                   jax.ShapeDtypeStruct((B, RET_HEADS, RET_HD, RET_HD), F32)),
        grid=(B, L // Lc),
        in_specs=[pl.BlockSpec((None, Lc, 2048), lambda b, c: (b, c, Z_RET // 2048)),
                  pl.BlockSpec((Lc, MIX), lambda b, c: (c, 0)),
                  pl.BlockSpec((Lc, MIX), lambda b, c: (c, 0)),
                  small((RET_HEADS, Lc, Lc)), small((Lc, LANES)), small((Lc, LANES)), small((1, MIX))],
        out_specs=(pl.BlockSpec((None, Lc, MIX), lambda b, c: (b, c, 0)),
                   pl.BlockSpec((None, RET_HEADS, RET_HD, RET_HD), lambda b, c: (b, 0, 0, 0))),
        scratch_shapes=[pltpu.VMEM((Lc, MIX), F32)],
        compiler_params=_cparams("parallel", "arbitrary"),
        name="ret_prompt",
    )(z3, cos_l, sin_l, dec, grow, toend, p["ret_gn"])


def _lru_gates(xc, wg_ref, bg_ref, lam_ref):
    rg = _bdot(xc, wg_ref[...]) + bg_ref[...]
    r = jax.nn.sigmoid(rg[:, 0:MIX])
    i = jax.nn.sigmoid(rg[:, MIX:2 * MIX])
    log_a = -LRU_C * r * jax.nn.softplus(-lam_ref[...])
    a = jnp.exp(log_a)
    bx = jnp.sqrt(_neg_expm1_2x(log_a, a)) * (i * xc)
    return a, bx


def _lru_kernel(z_ref, cw_ref, cb_ref, wg_ref, bg_ref, lam_ref,
                y_ref, st_ref, buf_ref, pad_ref):
    c = pl.program_id(1)
    nc = pl.num_programs(1)
    Lc = z_ref.shape[0]
    zz = z_ref[...]
    gate = zz[:, 0:MIX]
    x = zz[:, MIX:2 * MIX]

    @pl.when(c == 0)
    def _():
        st_ref[...] = jnp.zeros(st_ref.shape, F32)

    xc = _conv_chunk(c, x, pad_ref, cw_ref, cb_ref)
    a, bx = _lru_gates(xc, wg_ref, bg_ref, lam_ref)
    row = lax.broadcasted_iota(jnp.int32, (Lc, MIX), 0)
    s = 1
    while s < Lc:
        keep = row >= s
        bx = jnp.where(keep, bx + a * pltpu.roll(bx, s, 0), bx)
        a = jnp.where(keep, a * pltpu.roll(a, s, 0), a)
        s *= 2
    h = bx + a * st_ref[...]
    st_ref[...] = h[Lc - 1:Lc, :]
    y_ref[...] = h * jax.nn.gelu(gate)

    @pl.when(c == nc - 1)
    def _():
        buf_ref[...] = pad_ref[Lc + 8 - (CONV_K - 1):Lc + 8, :]

    _conv_finish(pad_ref, Lc)


def _lru_prompt(z3, p):
    B, L, _ = z3.shape
    Lc = CHUNK
    small = lambda shape: pl.BlockSpec(shape, lambda b, c: tuple(0 for _ in shape))
    return pl.pallas_call(
        _lru_kernel,
        out_shape=(jax.ShapeDtypeStruct((B, L, MIX), F32),
                   jax.ShapeDtypeStruct((B, 1, MIX), F32),
                   jax.ShapeDtypeStruct((B, CONV_K - 1, MIX), F32)),
        grid=(B, L // Lc),
        in_specs=[pl.BlockSpec((None, Lc, 1024), lambda b, c: (b, c, Z_LRU // 1024)),
                  small((CONV_K, MIX)), small((1, MIX)), small((MIX, 2 * MIX)), small((1, 2 * MIX)), small((1, MIX))],
        out_specs=(pl.BlockSpec((None, Lc, MIX), lambda b, c: (b, c, 0)),
                   pl.BlockSpec((None, 1, MIX), lambda b, c: (b, 0, 0)),
                   pl.BlockSpec((None, CONV_K - 1, MIX), lambda b, c: (b, 0, 0))),
        scratch_shapes=[pltpu.VMEM((Lc + 8, MIX), F32)],
        compiler_params=_cparams("parallel", "arbitrary"),
        name="lru_prompt",
    )(z3, p["lru_cw"], p["lru_cb"], p["lru_wg"], p["lru_bg"], p["lru_lam"])


S5_LB = 4


def _s5_project_in(u, wbr_ref, wbi_ref, xr_ref, xi_ref):
    for kb in range(S5_LB):
        ub = u[:, kb * 128:(kb + 1) * 128]
        xr_ref[:, kb * 512:(kb + 1) * 512] = _bdot(ub, wbr_ref[kb])
        xi_ref[:, kb * 512:(kb + 1) * 512] = _bdot(ub, wbi_ref[kb])


def _s5_project_out(xr, xi, u, wcr_ref, wci_ref, d_ref, wglu_ref):
    ys = []
    for kb in range(S5_LB):
        sl = slice(kb * 512, (kb + 1) * 512)
        ys.append(_bdot(xr[:, sl], wcr_ref[kb]) - _bdot(xi[:, sl], wci_ref[kb]))
    y = jnp.concatenate(ys, axis=1) + d_ref[...] * u
    y = jax.nn.gelu(y)
    return y * jax.nn.sigmoid(_bdot(y, wglu_ref[...]))


def _s5_kernel(z_ref, wbr_ref, wbi_ref, wcr_ref, wci_ref, pr_ref, pi_ref, d_ref, wglu_ref,
               y_ref, sr_ref, si_ref, xr_ref, xi_ref):
    c = pl.program_id(1)
    Lc = z_ref.shape[0]
    u = z_ref[...]

    @pl.when(c == 0)
    def _():
        sr_ref[...] = jnp.zeros(sr_ref.shape, F32)
        si_ref[...] = jnp.zeros(si_ref.shape, F32)

    _s5_project_in(u, wbr_ref, wbi_ref, xr_ref, xi_ref)
    row = lax.broadcasted_iota(jnp.int32, (Lc, S5_CH), 0)
    first = row == 0
    lr, li = pr_ref[0:1, :], pi_ref[0:1, :]
    s0r, s0i = sr_ref[...], si_ref[...]
    xr = xr_ref[...]
    xi = xi_ref[...]
    xr = jnp.where(first, xr + (lr * s0r - li * s0i), xr)
    xi = jnp.where(first, xi + (lr * s0i + li * s0r), xi)
    s = 1
    k = 0
    while s < Lc:
        keep = row >= s
        pr, pi = pr_ref[k:k + 1, :], pi_ref[k:k + 1, :]
        rr, ri = pltpu.roll(xr, s, 0), pltpu.roll(xi, s, 0)
        nr = xr + (pr * rr - pi * ri)
        ni = xi + (pr * ri + pi * rr)
        xr = jnp.where(keep, nr, xr)
        xi = jnp.where(keep, ni, xi)
        s *= 2
        k += 1
    sr_ref[...] = xr[Lc - 1:Lc, :]
    si_ref[...] = xi[Lc - 1:Lc, :]
    y_ref[...] = _s5_project_out(xr, xi, u, wcr_ref, wci_ref, d_ref, wglu_ref)


def _s5_prompt(z3, p):
    B, L, _ = z3.shape
    Lc = CHUNK
    small = lambda shape: pl.BlockSpec(shape, lambda b, c: tuple(0 for _ in shape))
    return pl.pallas_call(
        _s5_kernel,
        out_shape=(jax.ShapeDtypeStruct((B, L, MIX), F32),
                   jax.ShapeDtypeStruct((B, 1, S5_CH), F32),
                   jax.ShapeDtypeStruct((B, 1, S5_CH), F32)),
        grid=(B, L // Lc),
        in_specs=[pl.BlockSpec((None, Lc, MIX), lambda b, c: (b, c, Z_S5 // MIX)),
                  small((S5_LB, 128, 512)), small((S5_LB, 128, 512)),
                  small((S5_LB, 512, 128)), small((S5_LB, 512, 128)),
                  small((8, S5_CH)), small((8, S5_CH)), small((1, MIX)), small((MIX, MIX))],
        out_specs=(pl.BlockSpec((None, Lc, MIX), lambda b, c: (b, c, 0)),
                   pl.BlockSpec((None, 1, S5_CH), lambda b, c: (b, 0, 0)),
                   pl.BlockSpec((None, 1, S5_CH), lambda b, c: (b, 0, 0))),
        scratch_shapes=[pltpu.VMEM((Lc, S5_CH), F32), pltpu.VMEM((Lc, S5_CH), F32)],
        compiler_params=_cparams("parallel", "arbitrary"),
        name="s5_prompt",
    )(z3, p["s5_wbr"], p["s5_wbi"], p["s5_wcr"], p["s5_wci"], p["s5_pr"], p["s5_pi"], p["s5_d"], p["s5_glu"])


def _merge_kernel(y0_ref, y1_ref, y2_ref, y3_ref, zg_ref, h_ref, wb_ref, wo_ref, o_ref):
    acc = None
    for k, y_ref in enumerate((y0_ref, y1_ref, y2_ref, y3_ref)):
        br = _bdot(y_ref[...], wb_ref[k])
        t = jax.nn.sigmoid(zg_ref[:, k * D_MODEL:(k + 1) * D_MODEL]) * br
        acc = t if acc is None else acc + t
    o_ref[...] = h_ref[...] + _bdot(acc, wo_ref[...])


def _merge(ys, z, h, p, tm):
    T = h.shape[0]
    rows = lambda w: pl.BlockSpec((tm, w), lambda i: (i, 0))
    return pl.pallas_call(
        _merge_kernel,
        out_shape=jax.ShapeDtypeStruct((T, D_MODEL), F32),
        grid=(T // tm,),
        in_specs=[rows(MIX), rows(MIX), rows(MIX), rows(MIX),
                  pl.BlockSpec((tm, 4 * D_MODEL), lambda i: (i, Z_MERGE)),
                  rows(D_MODEL),
                  pl.BlockSpec((4, MIX, D_MODEL), lambda i: (0, 0, 0)),
                  pl.BlockSpec((D_MODEL, D_MODEL), lambda i: (0, 0))],
        out_specs=rows(D_MODEL),
        compiler_params=_cparams("parallel"),
        name="merge",
    )(*ys, z, h, p["w_branch"], p["w_out"])


def _top2_gates(logits):
    lane = lax.broadcasted_iota(jnp.int32, logits.shape, 1).astype(F32)
    big = float(LANES)
    m1 = jnp.max(logits, axis=-1, keepdims=True)
    i1 = jnp.min(jnp.where(logits == m1, lane, big), axis=-1, keepdims=True)
    rest = jnp.where(lane == i1, -jnp.inf, logits)
    m2 = jnp.max(rest, axis=-1, keepdims=True)
    i2 = jnp.min(jnp.where(rest == m2, lane, big), axis=-1, keepdims=True)
    e2 = jnp.exp(m2 - m1)
    den = 1.0 + e2
    return jnp.where(lane == i1, 1.0 / den, 0.0) + jnp.where(lane == i2, e2 / den, 0.0)


def _ffn_kernel(moe, final_norm, *refs):
    if moe:
        h_ref, g_ref, rt_ref, w1_ref, w3_ref, w2_ref, gf_ref, o_ref, hn_ref, acc_ref, gate_ref = refs
    else:
        h_ref, g_ref, w1_ref, w3_ref, w2_ref, gf_ref, o_ref, hn_ref, acc_ref = refs
    e = pl.program_id(1)
    ne = pl.num_programs(1)

    @pl.when(e == 0)
    def _():
        hn = _rmsnorm(h_ref[...], g_ref[...])
        hn_ref[...] = hn.astype(BF16)
        acc_ref[...] = jnp.zeros(acc_ref.shape, F32)
        if moe:
            lane = lax.broadcasted_iota(jnp.int32, (hn.shape[0], LANES), 1)
            logits = jnp.where(lane < N_EXPERTS, _hdot(hn, rt_ref[...]), -jnp.inf)
            gate_ref[...] = _top2_gates(logits)

    hn = hn_ref[...]
    a = jnp.dot(hn, w1_ref[...], preferred_element_type=F32)
    b = jnp.dot(hn, w3_ref[...], preferred_element_type=F32)
    o = _bdot(_silu(a) * b, w2_ref[...])
    if moe:
        lane = lax.broadcasted_iota(jnp.int32, gate_ref.shape, 1)
        ge = jnp.sum(jnp.where(lane == e, gate_ref[...], 0.0), axis=-1, keepdims=True)
        o = ge * o
    acc_ref[...] += o

    @pl.when(e == ne - 1)
    def _():
        out = h_ref[...] + acc_ref[...]
        if final_norm:
            out = _rmsnorm(out, gf_ref[...])
        o_ref[...] = out


def _ffn(h, g, w1, w3, w2, gfinal, tm, router=None, final_norm=False):
    T = h.shape[0]
    moe = router is not None
    tf = D_FF_TILE
    row_spec = pl.BlockSpec((tm, D_MODEL), lambda i, e: (i, 0))
    vec_spec = pl.BlockSpec((1, D_MODEL), lambda i, e: (0, 0))
    if moe:
        ne = w1.shape[0]
        wspecs = [pl.BlockSpec((None, D_MODEL, tf), lambda i, e: (e, 0, 0)),
                  pl.BlockSpec((None, D_MODEL, tf), lambda i, e: (e, 0, 0)),
                  pl.BlockSpec((None, tf, D_MODEL), lambda i, e: (e, 0, 0))]
        in_specs = [row_spec, vec_spec, pl.BlockSpec((D_MODEL, LANES), lambda i, e: (0, 0))] + wspecs + [vec_spec]
        args = (h, g, router, w1, w3, w2, gfinal)
        scratch = [pltpu.VMEM((tm, D_MODEL), BF16), pltpu.VMEM((tm, D_MODEL), F32), pltpu.VMEM((tm, LANES), F32)]
    else:
        ne = w1.shape[1] // tf
        wspecs = [pl.BlockSpec((D_MODEL, tf), lambda i, e: (0, e)),
                  pl.BlockSpec((D_MODEL, tf), lambda i, e: (0, e)),
                  pl.BlockSpec((tf, D_MODEL), lambda i, e: (e, 0))]
        in_specs = [row_spec, vec_spec] + wspecs + [vec_spec]
        args = (h, g, w1, w3, w2, gfinal)
        scratch = [pltpu.VMEM((tm, D_MODEL), BF16), pltpu.VMEM((tm, D_MODEL), F32)]
    return pl.pallas_call(
        functools.partial(_ffn_kernel, moe, final_norm),
        out_shape=jax.ShapeDtypeStruct((T, D_MODEL), F32),
        grid=(T // tm, ne),
        in_specs=in_specs,
        out_specs=row_spec,
        scratch_shapes=scratch,
        compiler_params=_cparams("parallel", "arbitrary"),
        name="moe" if moe else "ffn",
    )(*args)


def _step_pre_kernel(pos_cos_ref, pos_sin_ref, z_ref, sbuf_ref, lbuf_ref, lst_ref, s5r_ref, s5i_ref,
                     scw_ref, scb_ref, dtb_ref, alog_ref,
                     lcw_ref, lcb_ref, wg_ref, bg_ref, lam_ref,
                     wbr_ref, wbi_ref, wcr_ref, wci_ref, pr_ref, pi_ref, s5d_ref, wglu_ref,
                     kqv_ref, dec_ref, sbuf_o, lbuf_o, lst_o, s5r_o, s5i_o, ys5_o, ylru_o, xs_o,
                     xr_ref, xi_ref):
    Bs = z_ref.shape[0]
    zz = z_ref[...]
    xbc = zz[:, Z_SSD + MIX:Z_SSD + MIX + SSD_CONV]
    dt_raw = zz[:, Z_SSD + MIX + SSD_CONV:Z_SSD + MIX + SSD_CONV + LANES]
    W = SSD_CONV
    conv = scb_ref[...] + sbuf_ref[:, 0:W] * scw_ref[0:1, :]
    conv = conv + sbuf_ref[:, W:2 * W] * scw_ref[1:2, :]
    conv = conv + sbuf_ref[:, 2 * W:3 * W] * scw_ref[2:3, :]
    conv = conv + xbc * scw_ref[3:4, :]
    sbuf_o[:, 0:2 * W] = sbuf_ref[:, W:3 * W]
    sbuf_o[:, 2 * W:3 * W] = xbc
    xc = _silu(conv)
    xs = xc[:, 0:MIX]
    xs_o[...] = xs
    nbc = SSD_GROUPS * SSD_STATE
    bm = xc[:, MIX:MIX + nbc]
    cm = xc[:, MIX + nbc:MIX + 2 * nbc]
    dt = jax.nn.softplus(dt_raw + dtb_ref[...])
    a = -jnp.exp(alog_ref[...])
    dec = jnp.exp(dt * a)
    rep = SSD_HEADS // SSD_GROUPS
    for h in range(SSD_HEADS):
        g = h // rep
        kqv_ref[0, 0, h] = bm[:, g * SSD_STATE:(g + 1) * SSD_STATE] * dt[:, h:h + 1]
        kqv_ref[0, 1, h] = cm[:, g * SSD_STATE:(g + 1) * SSD_STATE]
        kqv_ref[0, 2, h] = xs[:, h * SSD_HD:(h + 1) * SSD_HD]
        dec_ref[0, h] = jnp.broadcast_to(dec[:, h:h + 1], (Bs, LANES))
    q = _rotary_lanes(zz[:, Z_RET:Z_RET + MIX], pos_cos_ref[...], pos_sin_ref[...]) * (RET_HD ** -0.5)
    k = _rotary_lanes(zz[:, Z_RET + MIX:Z_RET + 2 * MIX], pos_cos_ref[...], pos_sin_ref[...])
    v = zz[:, Z_RET + 2 * MIX:Z_RET + 3 * MIX]
    gam = _ret_gammas()
    for h in range(RET_HEADS):
        sl = slice(h * RET_HD, (h + 1) * RET_HD)
        kqv_ref[1, 0, h] = k[:, sl]
        kqv_ref[1, 1, h] = q[:, sl]
        kqv_ref[1, 2, h] = v[:, sl]
        dec_ref[1, h] = jnp.full((Bs, LANES), float(gam[h]), F32)
    gate = zz[:, Z_LRU:Z_LRU + MIX]
    lx = zz[:, Z_LRU + MIX:Z_LRU + 2 * MIX]
    W = MIX
    lconv = lcb_ref[...] + lbuf_ref[:, 0:W] * lcw_ref[0:1, :]
    lconv = lconv + lbuf_ref[:, W:2 * W] * lcw_ref[1:2, :]
    lconv = lconv + lbuf_ref[:, 2 * W:3 * W] * lcw_ref[2:3, :]
    lconv = lconv + lx * lcw_ref[3:4, :]
    lbuf_o[:, 0:2 * W] = lbuf_ref[:, W:3 * W]
    lbuf_o[:, 2 * W:3 * W] = lx
    la, lbx = _lru_gates(lconv, wg_ref, bg_ref, lam_ref)
    hl = lbx + la * lst_ref[...]
    lst_o[...] = hl
    ylru_o[...] = hl * jax.nn.gelu(gate)
    u = zz[:, Z_S5:Z_S5 + MIX]
    _s5_project_in(u, wbr_ref, wbi_ref, xr_ref, xi_ref)
    lr, li = pr_ref[0:1, :], pi_ref[0:1, :]
    s0r, s0i = s5r_ref[...], s5i_ref[...]
    xr = xr_ref[...] + (lr * s0r - li * s0i)
    xi = xi_ref[...] + (lr * s0i + li * s0r)
    s5r_o[...] = xr
    s5i_o[...] = xi
    ys5_o[...] = _s5_project_out(xr, xi, u, wcr_ref, wci_ref, s5d_ref, wglu_ref)


def _step_state_kernel(kqv_ref, dec_ref, st_ref, rexp_ref, o_st_ref, y_ref):
    k = kqv_ref[0]
    q = kqv_ref[1]
    v = kqv_ref[2]
    n_e = st_ref.shape[-1]
    reps = n_e // (2 * SSD_HD)
    k_rep = _split3_dot(k, rexp_ref[...])
    q_rep = _split3_dot(q, rexp_ref[...])
    v2 = jnp.concatenate([v, v], axis=1)
    v_rep = jnp.concatenate([v2] * reps, axis=1)
    d = dec_ref[:, 0:1]
    s_new = d * st_ref[...] + k_rep * v_rep
    o_st_ref[...] = s_new
    prod = q_rep * s_new
    acc = prod[:, 0:LANES]
    for j in range(1, reps):
        acc = acc + prod[:, j * LANES:(j + 1) * LANES]
    y_ref[...] = acc[:, 0:SSD_HD] + acc[:, SSD_HD:2 * SSD_HD]


def _step_post_kernel(yh_ref, xs_ref, z_ref, dlane_ref, ng_ref, gn_ref, yssd_o, yret_o, acc_ref):
    zz_gate = z_ref[:, Z_SSD:Z_SSD + MIX]
    for h in range(SSD_HEADS):
        acc_ref[:, h * SSD_HD:(h + 1) * SSD_HD] = yh_ref[0, h]
    y = acc_ref[...] + dlane_ref[...] * xs_ref[...]
    y = y * _silu(zz_gate)
    yssd_o[...] = _rmsnorm(y, ng_ref[...])
    for h in range(RET_HEADS):
        acc_ref[:, h * RET_HD:(h + 1) * RET_HD] = _group_norm_head(yh_ref[1, h])
    rgate = z_ref[:, Z_RET + 3 * MIX:Z_RET + 4 * MIX]
    yret_o[...] = _silu(rgate) * (acc_ref[...] * gn_ref[...])


def _rexp_matrix():
    r = np.zeros((SSD_STATE, SSD_STATE * SSD_HD), np.float32)
    for n in range(SSD_STATE):
        r[n, n * SSD_HD:(n + 1) * SSD_HD] = 1.0
    return jnp.asarray(r)


def _sample_mixers(z, st, p, pos):
    Bs = z.shape[0]
    H = SSD_HEADS
    NE = SSD_STATE * SSD_HD
    cos_l, sin_l = _rope_tables(np.asarray([pos]))
    rexp = _rexp_matrix()
    sbuf = st["ssd_conv"].reshape(Bs, (CONV_K - 1) * SSD_CONV)
    lbuf = st["lru_conv"].reshape(Bs, (CONV_K - 1) * MIX)
    s5r = st["s5_re"].reshape(Bs, S5_CH)
    s5i = st["s5_im"].reshape(Bs, S5_CH)
    pre_in = (cos_l, sin_l, z, sbuf, lbuf, st["lru"], s5r, s5i,
              p["ssd_cw"], p["ssd_cb"], p["ssd_dtb"], p["ssd_alog"],
              p["lru_cw"], p["lru_cb"], p["lru_wg"], p["lru_bg"], p["lru_lam"],
              p["s5_wbr"], p["s5_wbi"], p["s5_wcr"], p["s5_wci"], p["s5_pr"], p["s5_pi"], p["s5_d"], p["s5_glu"])
    pre_out = (jax.ShapeDtypeStruct((2, 3, H, Bs, SSD_HD), F32),
               jax.ShapeDtypeStruct((2, H, Bs, LANES), F32),
               jax.ShapeDtypeStruct(sbuf.shape, F32), jax.ShapeDtypeStruct(lbuf.shape, F32),
               jax.ShapeDtypeStruct((Bs, MIX), F32),
               jax.ShapeDtypeStruct((Bs, S5_CH), F32), jax.ShapeDtypeStruct((Bs, S5_CH), F32),
               jax.ShapeDtypeStruct((Bs, MIX), F32), jax.ShapeDtypeStruct((Bs, MIX), F32),
               jax.ShapeDtypeStruct((Bs, MIX), F32))
    (kqv, dec, sbuf_n, lbuf_n, lst_n, s5r_n, s5i_n, y_s5, y_lru, xs) = pl.pallas_call(
        _step_pre_kernel,
        out_shape=pre_out,
        scratch_shapes=[pltpu.VMEM((Bs, S5_CH), F32), pltpu.VMEM((Bs, S5_CH), F32)],
        compiler_params=pltpu.CompilerParams(vmem_limit_bytes=VMEM_LIMIT),
        name="step_pre",
    )(*pre_in)

    states = jnp.stack([st["ssd"].reshape(Bs, H * NE), st["ret"].reshape(Bs, H * NE)])
    new_states, yh = pl.pallas_call(
        _step_state_kernel,
        out_shape=(jax.ShapeDtypeStruct((2, Bs, H * NE), F32), jax.ShapeDtypeStruct((2, H, Bs, SSD_HD), F32)),
        grid=(2, H),
        in_specs=[pl.BlockSpec((None, 3, None, Bs, SSD_HD), lambda m, h: (m, 0, h, 0, 0)),
                  pl.BlockSpec((None, None, Bs, LANES), lambda m, h: (m, h, 0, 0)),
                  pl.BlockSpec((None, Bs, NE), lambda m, h: (m, 0, h)),
                  pl.BlockSpec(rexp.shape, lambda m, h: (0, 0))],
        out_specs=(pl.BlockSpec((None, Bs, NE), lambda m, h: (m, 0, h)),
                   pl.BlockSpec((None, None, Bs, SSD_HD), lambda m, h: (m, h, 0, 0))),
        compiler_params=_cparams("parallel", "parallel"),
        name="step_state",
    )(kqv, dec, states, rexp)

    post_in = (yh, xs, z, p["ssd_dlane"], p["ssd_norm"], p["ret_gn"])
    y_ssd, y_ret = pl.pallas_call(
        _step_post_kernel,
        out_shape=(jax.ShapeDtypeStruct((Bs, MIX), F32), jax.ShapeDtypeStruct((Bs, MIX), F32)),
        scratch_shapes=[pltpu.VMEM((Bs, MIX), F32)],
        compiler_params=pltpu.CompilerParams(vmem_limit_bytes=VMEM_LIMIT),
        name="step_post",
    )(*post_in)

    new = dict(
        ssd=new_states[0].reshape(Bs, H, SSD_STATE, SSD_HD),
        ssd_conv=sbuf_n.reshape(Bs, CONV_K - 1, SSD_CONV),
        s5_re=s5r_n.reshape(Bs, S5_GROUPS, S5_STATE),
        s5_im=s5i_n.reshape(Bs, S5_GROUPS, S5_STATE),
        lru=lst_n,
        lru_conv=lbuf_n.reshape(Bs, CONV_K - 1, MIX),
        ret=new_states[1].reshape(Bs, H, RET_HD, RET_HD),
    )
    return (y_ssd, y_s5, y_lru, y_ret), new


def _block_diag(w):
    n, r, c = w.shape
    eye = jnp.eye(n, dtype=w.dtype)
    return (eye[:, None, :, None] * w[:, :, None, :]).reshape(n * r, n * c)


def _pad_lanes(v):
    return jnp.zeros((1, LANES), F32).at[0, :v.shape[0]].set(v.astype(F32))


def _prep_layer(l, W):
    p = {}
    w_in = W["w_in"][l]
    p["w_in"] = jnp.concatenate(
        [w_in[:, 4872:8968], w_in[:, 2824:4872], w_in[:, 1800:2824], w_in[:, 1288:1800], w_in[:, 0:1288],
         jnp.zeros((D_MODEL, Z_WIDTH - 8968), w_in.dtype)], axis=1).astype(BF16)
    p["norm_mix"] = W["norm_mix"][l].reshape(1, D_MODEL)
    p["ssd_cw"] = W["ssd_conv_w"][l]
    p["ssd_cb"] = W["ssd_conv_b"][l].reshape(1, SSD_CONV)
    p["ssd_dtb"] = _pad_lanes(W["ssd_dt_bias"][l])
    p["ssd_alog"] = _pad_lanes(W["ssd_a_log"][l])
    p["ssd_dlane"] = jnp.repeat(W["ssd_d"][l], SSD_HD).reshape(1, MIX)
    p["ssd_norm"] = W["ssd_norm"][l].reshape(1, MIX)
    lam = lax.complex(W["s5_lambda_re"][l], W["s5_lambda_im"][l])
    dt = jnp.exp(W["s5_log_dt"][l])[:, None]
    lam_bar = jnp.exp(lam * dt)
    b_bar = ((lam_bar - 1.0) / lam)[:, :, None] * lax.complex(W["s5_b_re"][l], W["s5_b_im"][l])
    gb = S5_GROUPS // S5_LB

    def embed_in(m):
        return jnp.stack([_block_diag(jnp.swapaxes(m[k * gb:(k + 1) * gb], 1, 2)) for k in range(S5_LB)])

    def embed_out(m):
        return jnp.stack([_block_diag(jnp.swapaxes(m[k * gb:(k + 1) * gb], 1, 2)) for k in range(S5_LB)])

    p["s5_wbr"] = embed_in(jnp.real(b_bar)).astype(BF16)
    p["s5_wbi"] = embed_in(jnp.imag(b_bar)).astype(BF16)
    p["s5_wcr"] = embed_out(W["s5_c_re"][l]).astype(BF16)
    p["s5_wci"] = embed_out(W["s5_c_im"][l]).astype(BF16)
    pw = [lam_bar.reshape(1, S5_CH)]
    for _ in range(7):
        pw.append(pw[-1] * pw[-1])
    pw = jnp.concatenate(pw, axis=0)
    p["s5_pr"] = jnp.real(pw)
    p["s5_pi"] = jnp.imag(pw)
    p["s5_d"] = W["s5_d"][l].reshape(1, MIX)
    p["s5_glu"] = W["s5_glu"][l].astype(BF16)
    p["lru_cw"] = W["lru_conv_w"][l]
    p["lru_cb"] = W["lru_conv_b"][l].reshape(1, MIX)
    p["lru_wg"] = jnp.concatenate([_block_diag(W["lru_wa"][l]), _block_diag(W["lru_wx"][l])], axis=1).astype(BF16)
    p["lru_bg"] = jnp.concatenate([W["lru_ba"][l], W["lru_bx"][l]]).reshape(1, 2 * MIX)
    p["lru_lam"] = W["lru_lambda"][l].reshape(1, MIX)
    p["ret_gn"] = W["ret_gn"][l].reshape(1, MIX)
    p["w_branch"] = W["w_branch"][l].astype(BF16)
    p["w_out"] = W["w_out"][l].astype(BF16)
    p["norm_ffn"] = W["norm_ffn"][l].reshape(1, D_MODEL)
    return p


def _channel_mixer(l, h, p, W, tm, final):
    gfin = W["norm_final"].reshape(1, D_MODEL)
    j = l // 2
    if l % 2 == 0:
        return _ffn(h, p["norm_ffn"], W["ffn_w1_bf"][j], W["ffn_w3_bf"][j], W["ffn_w2_bf"][j], gfin, tm,
                    final_norm=final)
    router = jnp.zeros((D_MODEL, LANES), F32).at[:, :N_EXPERTS].set(W["moe_router"][j])
    return _ffn(h, p["norm_ffn"], W["moe_w1_bf"][j], W["moe_w3_bf"][j], W["moe_w2_bf"][j], gfin, tm,
                router=router, final_norm=final)


def _trunk_prompt(x, W, preps):
    B, L, _ = x.shape
    T = B * L
    tm = min(512, T)
    depth = len(preps)
    h = x.reshape(T, D_MODEL)
    new = {k: [] for k in ("ssd", "ssd_conv", "s5_re", "s5_im", "lru", "lru_conv", "ret")}
    for l, p in enumerate(preps):
        z = _inproj(h, p["norm_mix"], p["w_in"], tm)
        z3 = z.reshape(B, L, Z_WIDTH)
        y_ssd, s_ssd, buf_ssd = _ssd_prompt(z3, p)
        y_s5, s5r, s5i = _s5_prompt(z3, p)
        y_lru, s_lru, buf_lru = _lru_prompt(z3, p)
        y_ret, s_ret = _ret_prompt(z3, p)
        ys = tuple(y.reshape(T, MIX) for y in (y_ssd, y_s5, y_lru, y_ret))
        h = _merge(ys, z, h, p, min(256, T))
        h = _channel_mixer(l, h, p, W, tm, final=(l == depth - 1))
        new["ssd"].append(s_ssd)
        new["ssd_conv"].append(buf_ssd)
        new["s5_re"].append(s5r.reshape(B, S5_GROUPS, S5_STATE))
        new["s5_im"].append(s5i.reshape(B, S5_GROUPS, S5_STATE))
        new["lru"].append(s_lru.reshape(B, MIX))
        new["lru_conv"].append(buf_lru)
        new["ret"].append(s_ret)
    return h.reshape(B, L, D_MODEL), {k: jnp.stack(v) for k, v in new.items()}


def _trunk_sample(x, pos, st, W, preps):
    Bs = x.shape[0]
    depth = len(preps)
    h = x.reshape(Bs, D_MODEL)
    new = {k: [] for k in ("ssd", "ssd_conv", "s5_re", "s5_im", "lru", "lru_conv", "ret")}
    for l, p in enumerate(preps):
        z = _inproj(h, p["norm_mix"], p["w_in"], Bs)
        ys, nl = _sample_mixers(z, {k: v[l] for k, v in st.items()}, p, pos)
        h = _merge(ys, z, h, p, Bs)
        h = _channel_mixer(l, h, p, W, Bs, final=(l == depth - 1))
        for k in new:
            new[k].append(nl[k])
    return h.reshape(Bs, 1, D_MODEL), {k: jnp.stack(v) for k, v in new.items()}


def kernel(x_prompt, x_sample, state_ssd, state_ssd_conv, state_s5_re, state_s5_im, state_lru, state_lru_conv, state_ret, norm_mix, w_in, ssd_conv_w, ssd_conv_b, ssd_dt_bias, ssd_a_log, ssd_d, ssd_norm, s5_lambda_re, s5_lambda_im, s5_b_re, s5_b_im, s5_c_re, s5_c_im, s5_d, s5_log_dt, s5_glu, lru_conv_w, lru_conv_b, lru_wa, lru_ba, lru_wx, lru_bx, lru_lambda, ret_gn, w_branch, w_out, norm_ffn, ffn_w1, ffn_w3, ffn_w2, moe_router, moe_w1, moe_w3, moe_w2, norm_final):
    W = dict(norm_mix=norm_mix, w_in=w_in, ssd_conv_w=ssd_conv_w, ssd_conv_b=ssd_conv_b, ssd_dt_bias=ssd_dt_bias,
             ssd_a_log=ssd_a_log, ssd_d=ssd_d, ssd_norm=ssd_norm, s5_lambda_re=s5_lambda_re,
             s5_lambda_im=s5_lambda_im, s5_b_re=s5_b_re, s5_b_im=s5_b_im, s5_c_re=s5_c_re, s5_c_im=s5_c_im,
             s5_d=s5_d, s5_log_dt=s5_log_dt, s5_glu=s5_glu, lru_conv_w=lru_conv_w, lru_conv_b=lru_conv_b,
             lru_wa=lru_wa, lru_ba=lru_ba, lru_wx=lru_wx, lru_bx=lru_bx, lru_lambda=lru_lambda, ret_gn=ret_gn,
             w_branch=w_branch, w_out=w_out, norm_ffn=norm_ffn, moe_router=moe_router, norm_final=norm_final)
    W["ffn_w1_bf"] = ffn_w1.astype(BF16)
    W["ffn_w3_bf"] = ffn_w3.astype(BF16)
    W["ffn_w2_bf"] = ffn_w2.astype(BF16)
    W["moe_w1_bf"] = moe_w1.astype(BF16)
    W["moe_w3_bf"] = moe_w3.astype(BF16)
    W["moe_w2_bf"] = moe_w2.astype(BF16)
    depth = w_in.shape[0]
    preps = [_prep_layer(l, W) for l in range(depth)]
    y_p, sp = _trunk_prompt(x_prompt, W, preps)
    st = dict(ssd=state_ssd, ssd_conv=state_ssd_conv, s5_re=state_s5_re, s5_im=state_s5_im,
              lru=state_lru, lru_conv=state_lru_conv, ret=state_ret)
    past_len = 16384
    y_s, ss = _trunk_sample(x_sample, past_len, st, W, preps)
    names = ("ssd", "ssd_conv", "s5_re", "s5_im", "lru", "lru_conv", "ret")
    return (y_p, y_s) + tuple(sp[n] for n in names) + tuple(ss[n] for n in names)
```

```python
import functools
import math

import jax
import jax.numpy as jnp
import numpy as np
from jax import lax
from jax.experimental import pallas as pl
from jax.experimental.pallas import tpu as pltpu

F32 = jnp.float32
BF16 = jnp.bfloat16
EPS = 1e-6

D_MODEL = 1024
MIX = 512
CONV_K = 4
CHUNK = 128
SSD_HEADS = 8
SSD_HD = 64
SSD_STATE = 64
SSD_GROUPS = 2
SSD_CONV = MIX + 2 * SSD_GROUPS * SSD_STATE
S5_GROUPS = 32
S5_GDIM = 16
S5_STATE = 64
S5_CH = S5_GROUPS * S5_STATE
LRU_BLOCKS = 8
LRU_C = 8.0
RET_HEADS = 8
RET_HD = 64
ROPE_BASE = 10000.0
N_EXPERTS = 8
D_FF_TILE = 1408

Z_MERGE = 0
Z_RET = 4096
Z_LRU = 6144
Z_S5 = 7168
Z_SSD = 7680
Z_WIDTH = 9216

VMEM_LIMIT = 56 * 1024 * 1024
LANES = 128
SUBLANES = 8


def _cparams(*sem):
    return pltpu.CompilerParams(dimension_semantics=sem, vmem_limit_bytes=VMEM_LIMIT)


def _bdot(a, b):
    return jnp.dot(a.astype(BF16), b.astype(BF16), preferred_element_type=F32)


def _bdot_nt(a, b):
    return lax.dot_general(a.astype(BF16), b.astype(BF16), (((1,), (1,)), ((), ())), preferred_element_type=F32)


def _bdot_tn(a, b):
    return lax.dot_general(a.astype(BF16), b.astype(BF16), (((0,), (0,)), ((), ())), preferred_element_type=F32)


def _hdot(a, b):
    return jnp.dot(a, b, precision=lax.Precision.HIGHEST, preferred_element_type=F32)


def _split3_dot(x, m01):
    hi = x.astype(BF16)
    r1 = x - hi.astype(F32)
    mid = r1.astype(BF16)
    lo = (r1 - mid.astype(F32)).astype(BF16)
    m = m01.astype(BF16)
    d = functools.partial(jnp.dot, preferred_element_type=F32)
    return (d(lo, m) + d(mid, m)) + d(hi, m)


def _rmsnorm(x, g):
    ms = jnp.mean(x * x, axis=-1, keepdims=True)
    return x * lax.rsqrt(ms + EPS) * g


def _silu(x):
    return x * jax.nn.sigmoid(x)


def _neg_expm1_2x(log_a, a):
    return jnp.tanh(-log_a) * (1.0 + a * a)


def _inproj_kernel(x_ref, g_ref, w_ref, o_ref, hn_ref):
    @pl.when(pl.program_id(1) == 0)
    def _():
        hn_ref[...] = _rmsnorm(x_ref[...], g_ref[...]).astype(BF16)

    o_ref[...] = jnp.dot(hn_ref[...], w_ref[...], preferred_element_type=F32)


def _inproj(x, g, w, tm):
    T = x.shape[0]
    tn = 1536
    return pl.pallas_call(
        _inproj_kernel,
        out_shape=jax.ShapeDtypeStruct((T, Z_WIDTH), F32),
        grid=(T // tm, Z_WIDTH // tn),
        in_specs=[pl.BlockSpec((tm, D_MODEL), lambda i, j: (i, 0)),
                  pl.BlockSpec((1, D_MODEL), lambda i, j: (0, 0)),
                  pl.BlockSpec((D_MODEL, tn), lambda i, j: (0, j))],
        out_specs=pl.BlockSpec((tm, tn), lambda i, j: (i, j)),
        scratch_shapes=[pltpu.VMEM((tm, D_MODEL), BF16)],
        compiler_params=_cparams("parallel", "arbitrary"),
        name="inproj",
    )(x, g, w)


def _conv_chunk(c, x, pad_ref, w_ref, b_ref):
    Lc = x.shape[0]

    @pl.when(c == 0)
    def _():
        pad_ref[0:8, :] = jnp.zeros((8, x.shape[1]), F32)

    pad_ref[8:8 + Lc, :] = x
    out = b_ref[...] + pad_ref[5:5 + Lc, :] * w_ref[0:1, :]
    out = out + pad_ref[6:6 + Lc, :] * w_ref[1:2, :]
    out = out + pad_ref[7:7 + Lc, :] * w_ref[2:3, :]
    out = out + x * w_ref[3:4, :]
    return out


def _conv_finish(pad_ref, Lc):
    pad_ref[0:8, :] = pad_ref[Lc:Lc + 8, :]


def _pow2_div(x, d):
    return lax.shift_right_logical(x, jnp.int32(int(math.log2(d))))


def _head_block_rows(x, nh, hd):
    L = x.shape[0]
    xt = jnp.concatenate([x.astype(BF16)] * nh, axis=0)
    row = lax.broadcasted_iota(jnp.int32, xt.shape, 0)
    col = lax.broadcasted_iota(jnp.int32, xt.shape, 1)
    return jnp.where(_pow2_div(row, L) == _pow2_div(col, hd), xt, jnp.zeros_like(xt))


def _head_block_cols(xt, nh, hd):
    L = xt.shape[1]
    xc = jnp.concatenate([xt.astype(BF16)] * nh, axis=1)
    row = lax.broadcasted_iota(jnp.int32, xc.shape, 0)
    col = lax.broadcasted_iota(jnp.int32, xc.shape, 1)
    return jnp.where(_pow2_div(row, hd) == _pow2_div(col, L), xc, jnp.zeros_like(xc))


def _head_diag_mask(n, hd):
    row = lax.broadcasted_iota(jnp.int32, (n, n), 0)
    col = lax.broadcasted_iota(jnp.int32, (n, n), 1)
    return _pow2_div(row, hd) == _pow2_div(col, hd)


def _group_repeat_lanes(m, rep):
    lane = lax.broadcasted_iota(jnp.int32, m.shape, 1)
    swapped = pltpu.roll(m, m.shape[1] // 2, 1)
    low = lane < m.shape[1] // 2
    g0 = jnp.where(low, m, swapped)
    g1 = jnp.where(low, swapped, m)
    return jnp.concatenate([g0] * (rep // 2) + [g1] * (rep // 2), axis=1)


def _ssd_kernel(z_ref, cw_ref, cb_ref, dtb_ref, alog_ref, dlane_ref, ng_ref, tri_ref, ehj_ref, ehn_ref,
                y_ref, sto_ref, buf_ref, pad_ref, st_ref):
    c = pl.program_id(1)
    nc = pl.num_programs(1)
    Lc = z_ref.shape[0]
    zz = z_ref[...]
    zgate = zz[:, 0:MIX]
    xbc = zz[:, MIX:MIX + SSD_CONV]
    dt_raw = zz[:, MIX + SSD_CONV:MIX + SSD_CONV + LANES]

    @pl.when(c == 0)
    def _():
        st_ref[...] = jnp.zeros(st_ref.shape, F32)

    conv = _conv_chunk(c, xbc, pad_ref, cw_ref, cb_ref)
    xc = _silu(conv)
    xs = xc[:, 0:MIX]
    nbc = SSD_GROUPS * SSD_STATE
    bm = xc[:, MIX:MIX + nbc]
    cm = xc[:, MIX + nbc:MIX + 2 * nbc]
    dt = jax.nn.softplus(dt_raw + dtb_ref[...])
    a = -jnp.exp(alog_ref[...])
    ld = dt * a
    acum = _hdot(tri_ref[...], ld)
    acum_t = acum.T
    dt_t = dt.T
    a_row = jnp.concatenate([acum_t[h:h + 1, :] for h in range(SSD_HEADS)], axis=1)
    dt_row = jnp.concatenate([dt_t[h:h + 1, :] for h in range(SSD_HEADS)], axis=1)
    a_col = _split3_dot(acum, ehj_ref[...])
    row = lax.broadcasted_iota(jnp.int32, (Lc, SSD_HEADS * Lc), 0)
    col = lax.broadcasted_iota(jnp.int32, (Lc, SSD_HEADS * Lc), 1)
    causal = row >= (col & (Lc - 1))
    decay = jnp.where(causal, jnp.exp(jnp.where(causal, a_col - a_row, 0.0)), 0.0)
    rep = SSD_HEADS // SSD_GROUPS
    gmats = []
    for g in range(SSD_GROUPS):
        cg = cm[:, g * SSD_STATE:(g + 1) * SSD_STATE]
        bg = bm[:, g * SSD_STATE:(g + 1) * SSD_STATE]
        gmats.append(_bdot_nt(cg, bg))
    g_all = jnp.concatenate([gmats[h // rep] for h in range(SSD_HEADS)], axis=1)
    m_all = g_all * decay * dt_row
    y = _bdot(m_all, _head_block_rows(xs, SSD_HEADS, SSD_HD))
    exp_a = jnp.exp(acum)
    exp_a_l = _split3_dot(exp_a, ehn_ref[...])
    c_rep = _group_repeat_lanes(cm, rep)
    y = y + _bdot_nt(c_rep * exp_a_l, st_ref[...])
    a_last = acum[Lc - 1:Lc, :]
    w_end_l = _split3_dot(jnp.exp(a_last - acum) * dt, ehn_ref[...])
    upd = _bdot_tn(xs, _group_repeat_lanes(bm, rep) * w_end_l)
    st_ref[...] = exp_a_l[Lc - 1:Lc, :] * st_ref[...] + jnp.where(_head_diag_mask(MIX, SSD_HD), upd, 0.0)
    y = y + dlane_ref[...] * xs
    y = y * _silu(zgate)
    y_ref[...] = _rmsnorm(y, ng_ref[...])

    @pl.when(c == nc - 1)
    def _():
        buf_ref[...] = pad_ref[Lc + 8 - (CONV_K - 1):Lc + 8, :]
        s_t = st_ref[...].T
        for h in range(SSD_HEADS):
            sl = slice(h * SSD_HD, (h + 1) * SSD_HD)
            sto_ref[h] = s_t[sl, sl]

    _conv_finish(pad_ref, Lc)


def _ssd_prompt(z3, p):
    B, L, _ = z3.shape
    Lc = CHUNK
    small = lambda shape: pl.BlockSpec(shape, lambda b, c: tuple(0 for _ in shape))
    return pl.pallas_call(
        _ssd_kernel,
        out_shape=(jax.ShapeDtypeStruct((B, L, MIX), F32),
                   jax.ShapeDtypeStruct((B, SSD_HEADS, SSD_STATE, SSD_HD), F32),
                   jax.ShapeDtypeStruct((B, CONV_K - 1, SSD_CONV), F32)),
        grid=(B, L // Lc),
        in_specs=[pl.BlockSpec((None, Lc, 1536), lambda b, c: (b, c, Z_SSD // 1536)),
                  small((CONV_K, SSD_CONV)), small((1, SSD_CONV)), small((1, LANES)), small((1, LANES)),
                  small((1, MIX)), small((1, MIX)), small((Lc, Lc)),
                  small((LANES, SSD_HEADS * Lc)), small((LANES, MIX))],
        out_specs=(pl.BlockSpec((None, Lc, MIX), lambda b, c: (b, c, 0)),
                   pl.BlockSpec((None, SSD_HEADS, SSD_STATE, SSD_HD), lambda b, c: (b, 0, 0, 0)),
                   pl.BlockSpec((None, CONV_K - 1, SSD_CONV), lambda b, c: (b, 0, 0))),
        scratch_shapes=[pltpu.VMEM((Lc + 8, SSD_CONV), F32), pltpu.VMEM((MIX, MIX), F32)],
        compiler_params=_cparams("parallel", "arbitrary"),
        name="ssd_prompt",
    )(z3, p["ssd_cw"], p["ssd_cb"], p["ssd_dtb"], p["ssd_alog"], p["ssd_dlane"], p["ssd_norm"], _tri(Lc),
      _head_expand(SSD_HEADS, Lc), _head_expand(SSD_HEADS, SSD_HD))


def _tri(Lc):
    return jnp.asarray(np.tril(np.ones((Lc, Lc), np.float32)))


def _head_expand(nh, width):
    e = np.zeros((LANES, nh * width), np.float32)
    for h in range(nh):
        e[h, h * width:(h + 1) * width] = 1.0
    return jnp.asarray(e, BF16)


def _ret_gammas():
    return 1.0 - np.exp2(-5.0 - np.arange(RET_HEADS, dtype=np.float64))


def _ret_tables(Lc):
    gam = _ret_gammas()
    i = np.arange(Lc)
    d = i[:, None] - i[None, :]
    decay = np.where(d >= 0, gam[:, None, None] ** np.maximum(d, 0)[None], 0.0)
    decay_l = np.transpose(decay, (1, 0, 2)).reshape(Lc, RET_HEADS * Lc)
    grow_l = np.repeat(gam[None, :] ** (i[:, None] + 1), RET_HD, axis=1)
    toend_t = np.repeat(gam[:, None] ** (Lc - 1 - i[None, :]), RET_HD, axis=0)
    hd = np.arange(MIX) // RET_HD
    state_decay = np.where(hd[:, None] == hd[None, :], (gam ** Lc)[hd][:, None], 0.0)
    return tuple(jnp.asarray(t, F32) for t in (decay_l, grow_l, toend_t, state_decay))


def _rope_tables(pos):
    half = RET_HD // 2
    inv = ROPE_BASE ** (-np.arange(half, dtype=np.float64) / half)
    ang = np.asarray(pos, np.float64)[:, None] * inv[None, :]
    cos = np.cos(ang)
    sin = np.sin(ang)
    cos_l = np.tile(np.concatenate([cos, cos], axis=1), (1, RET_HEADS))
    sin_l = np.tile(np.concatenate([-sin, sin], axis=1), (1, RET_HEADS))
    return jnp.asarray(cos_l, F32), jnp.asarray(sin_l, F32)


def _rotary_lanes(x, cos_l, sin_l):
    lane = lax.broadcasted_iota(jnp.int32, x.shape, 1)
    first = (lane & (RET_HD - 1)) < (RET_HD // 2)
    n = x.shape[1]
    swapped = jnp.where(first, pltpu.roll(x, n - RET_HD // 2, 1), pltpu.roll(x, RET_HD // 2, 1))
    return x * cos_l + swapped * sin_l


def _group_norm_head(o):
    mu = jnp.mean(o, axis=-1, keepdims=True)
    d = o - mu
    var = jnp.mean(d * d, axis=-1, keepdims=True)
    return d * lax.rsqrt(var + 1e-5)


def _group_norm_lanes(o, hd):
    cols = []
    for cb in range(o.shape[1] // LANES):
        x = o[:, cb * LANES:(cb + 1) * LANES]
        low = lax.broadcasted_iota(jnp.int32, x.shape, 1) < hd

        def seg_mean(t):
            lo = jnp.sum(jnp.where(low, t, 0.0), axis=1, keepdims=True)
            hi = jnp.sum(jnp.where(low, 0.0, t), axis=1, keepdims=True)
            return jnp.where(low, lo, hi) * (1.0 / hd)

        d = x - seg_mean(x)
        cols.append(d * lax.rsqrt(seg_mean(d * d) + 1e-5))
    return jnp.concatenate(cols, axis=1)


def _ret_kernel(z_ref, cos_ref, sin_ref, dec_ref, grow_ref, toend_ref, sdec_ref, gn_ref,
                y_ref, sto_ref, st_ref):
    c = pl.program_id(1)
    nc = pl.num_programs(1)
    zz = z_ref[...]
    q = _rotary_lanes(zz[:, 0:MIX], cos_ref[...], sin_ref[...]) * (RET_HD ** -0.5)
    k = _rotary_lanes(zz[:, MIX:2 * MIX], cos_ref[...], sin_ref[...])
    v = zz[:, 2 * MIX:3 * MIX]
    gate = zz[:, 3 * MIX:4 * MIX]

    @pl.when(c == 0)
    def _():
        st_ref[...] = jnp.zeros(st_ref.shape, F32)

    k_t = k.T
    g_all = _bdot(q, _head_block_cols(k_t, RET_HEADS, RET_HD))
    o = _bdot(g_all * dec_ref[...], _head_block_rows(v, RET_HEADS, RET_HD))
    o = o + _bdot(q * grow_ref[...], st_ref[...])
    upd = _bdot(k_t * toend_ref[...], v)
    st_ref[...] = sdec_ref[...] * st_ref[...] + jnp.where(_head_diag_mask(MIX, RET_HD), upd, 0.0)
    y_ref[...] = _silu(gate) * (_group_norm_lanes(o, RET_HD) * gn_ref[...])

    @pl.when(c == nc - 1)
    def _():
        for h in range(RET_HEADS):
            sl = slice(h * RET_HD, (h + 1) * RET_HD)
            sto_ref[h] = st_ref[sl, sl]


def _ret_prompt(z3, p):
    B, L, _ = z3.shape
    Lc = CHUNK
    cos_l, sin_l = _rope_tables(np.arange(L))
    dec, grow, toend, sdec = _ret_tables(Lc)
    small = lambda shape: pl.BlockSpec(shape, lambda b, c: tuple(0 for _ in shape))
    return pl.pallas_call(
        _ret_kernel,
        out_shape=(jax.ShapeDtypeStruct((B, L, MIX), F32),
                   jax.ShapeDtypeStruct((B, RET_HEADS, RET_HD, RET_HD), F32)),
        grid=(B, L // Lc),
        in_specs=[pl.BlockSpec((None, Lc, 2048), lambda b, c: (b, c, Z_RET // 2048)),
                  pl.BlockSpec((Lc, MIX), lambda b, c: (c, 0)),
                  pl.BlockSpec((Lc, MIX), lambda b, c: (c, 0)),
                  small((Lc, RET_HEADS * Lc)), small((Lc, MIX)), small((MIX, Lc)), small((MIX, MIX)),
                  small((1, MIX))],
        out_specs=(pl.BlockSpec((None, Lc, MIX), lambda b, c: (b, c, 0)),
                   pl.BlockSpec((None, RET_HEADS, RET_HD, RET_HD), lambda b, c: (b, 0, 0, 0))),
        scratch_shapes=[pltpu.VMEM((MIX, MIX), F32)],
        compiler_params=_cparams("parallel", "arbitrary"),
        name="ret_prompt",
    )(z3, cos_l, sin_l, dec, grow, toend, sdec, p["ret_gn"])


def _lru_gates(xc, wg_ref, bg_ref, lam_ref):
    rg = _bdot(xc, wg_ref[...]) + bg_ref[...]
    r = jax.nn.sigmoid(rg[:, 0:MIX])
    i = jax.nn.sigmoid(rg[:, MIX:2 * MIX])
    log_a = -LRU_C * r * jax.nn.softplus(-lam_ref[...])
    a = jnp.exp(log_a)
    bx = jnp.sqrt(_neg_expm1_2x(log_a, a)) * (i * xc)
    return a, bx


def _lru_kernel(z_ref, cw_ref, cb_ref, wg_ref, bg_ref, lam_ref,
                y_ref, st_ref, buf_ref, pad_ref):
    c = pl.program_id(1)
    nc = pl.num_programs(1)
    Lc = z_ref.shape[0]
    zz = z_ref[...]
    gate = zz[:, 0:MIX]
    x = zz[:, MIX:2 * MIX]

    @pl.when(c == 0)
    def _():
        st_ref[...] = jnp.zeros(st_ref.shape, F32)

    xc = _conv_chunk(c, x, pad_ref, cw_ref, cb_ref)
    a, bx = _lru_gates(xc, wg_ref, bg_ref, lam_ref)
    row = lax.broadcasted_iota(jnp.int32, (Lc, MIX), 0)
    s = 1
    while s < Lc:
        keep = row >= s
        bx = jnp.where(keep, bx + a * pltpu.roll(bx, s, 0), bx)
        a = jnp.where(keep, a * pltpu.roll(a, s, 0), a)
        s *= 2
    h = bx + a * st_ref[...]
    st_ref[...] = h[Lc - 1:Lc, :]
    y_ref[...] = h * jax.nn.gelu(gate)

    @pl.when(c == nc - 1)
    def _():
        buf_ref[...] = pad_ref[Lc + 8 - (CONV_K - 1):Lc + 8, :]

    _conv_finish(pad_ref, Lc)


def _lru_prompt(z3, p):
    B, L, _ = z3.shape
    Lc = CHUNK
    small = lambda shape: pl.BlockSpec(shape, lambda b, c: tuple(0 for _ in shape))
    return pl.pallas_call(
        _lru_kernel,
        out_shape=(jax.ShapeDtypeStruct((B, L, MIX), F32),
                   jax.ShapeDtypeStruct((B, 1, MIX), F32),
                   jax.ShapeDtypeStruct((B, CONV_K - 1, MIX), F32)),
        grid=(B, L // Lc),
        in_specs=[pl.BlockSpec((None, Lc, 1024), lambda b, c: (b, c, Z_LRU // 1024)),
                  small((CONV_K, MIX)), small((1, MIX)), small((MIX, 2 * MIX)), small((1, 2 * MIX)), small((1, MIX))],
        out_specs=(pl.BlockSpec((None, Lc, MIX), lambda b, c: (b, c, 0)),
                   pl.BlockSpec((None, 1, MIX), lambda b, c: (b, 0, 0)),
                   pl.BlockSpec((None, CONV_K - 1, MIX), lambda b, c: (b, 0, 0))),
        scratch_shapes=[pltpu.VMEM((Lc + 8, MIX), F32)],
        compiler_params=_cparams("parallel", "arbitrary"),
        name="lru_prompt",
    )(z3, p["lru_cw"], p["lru_cb"], p["lru_wg"], p["lru_bg"], p["lru_lam"])


S5_LB = 4


def _s5_project_in(u, wbr_ref, wbi_ref, xr_ref, xi_ref):
    for kb in range(S5_LB):
        ub = u[:, kb * 128:(kb + 1) * 128]
        xr_ref[:, kb * 512:(kb + 1) * 512] = _bdot(ub, wbr_ref[kb])
        xi_ref[:, kb * 512:(kb + 1) * 512] = _bdot(ub, wbi_ref[kb])


def _s5_project_out(xr, xi, u, wcr_ref, wci_ref, d_ref, wglu_ref):
    ys = []
    for kb in range(S5_LB):
        sl = slice(kb * 512, (kb + 1) * 512)
        ys.append(_bdot(xr[:, sl], wcr_ref[kb]) - _bdot(xi[:, sl], wci_ref[kb]))
    y = jnp.concatenate(ys, axis=1) + d_ref[...] * u
    y = jax.nn.gelu(y)
    return y * jax.nn.sigmoid(_bdot(y, wglu_ref[...]))


def _s5_kernel(z_ref, wbr_ref, wbi_ref, wcr_ref, wci_ref, pr_ref, pi_ref, d_ref, wglu_ref,
               y_ref, sr_ref, si_ref, xr_ref, xi_ref):
    c = pl.program_id(1)
    Lc = z_ref.shape[0]
    u = z_ref[...]

    @pl.when(c == 0)
    def _():
        sr_ref[...] = jnp.zeros(sr_ref.shape, F32)
        si_ref[...] = jnp.zeros(si_ref.shape, F32)

    _s5_project_in(u, wbr_ref, wbi_ref, xr_ref, xi_ref)
    ng = Lc // SUBLANES
    x3r = xr_ref[...].reshape(ng, SUBLANES, S5_CH)
    x3i = xi_ref[...].reshape(ng, SUBLANES, S5_CH)
    sub = lax.broadcasted_iota(jnp.int32, (ng, SUBLANES, S5_CH), 1)
    s = 1
    while s < SUBLANES:
        pr, pi = pr_ref[s - 1:s, :][None], pi_ref[s - 1:s, :][None]
        rr, ri = pltpu.roll(x3r, s, 1), pltpu.roll(x3i, s, 1)
        keep = sub >= s
        x3r, x3i = (jnp.where(keep, x3r + (pr * rr - pi * ri), x3r),
                    jnp.where(keep, x3i + (pr * ri + pi * rr), x3i))
        s *= 2
    pcr, pci = pr_ref[...], pi_ref[...]
    cr, ci = sr_ref[...], si_ref[...]
    for g in range(ng):
        br = jnp.broadcast_to(cr, (SUBLANES, S5_CH))
        bi = jnp.broadcast_to(ci, (SUBLANES, S5_CH))
        gr = x3r[g] + (pcr * br - pci * bi)
        gi = x3i[g] + (pcr * bi + pci * br)
        xr_ref[g * SUBLANES:(g + 1) * SUBLANES, :] = gr
        xi_ref[g * SUBLANES:(g + 1) * SUBLANES, :] = gi
        cr, ci = gr[SUBLANES - 1:SUBLANES, :], gi[SUBLANES - 1:SUBLANES, :]
    sr_ref[...] = cr
    si_ref[...] = ci
    y_ref[...] = _s5_project_out(xr_ref[...], xi_ref[...], u, wcr_ref, wci_ref, d_ref, wglu_ref)


def _s5_prompt(z3, p):
    B, L, _ = z3.shape
    Lc = CHUNK
    small = lambda shape: pl.BlockSpec(shape, lambda b, c: tuple(0 for _ in shape))
    return pl.pallas_call(
        _s5_kernel,
        out_shape=(jax.ShapeDtypeStruct((B, L, MIX), F32),
                   jax.ShapeDtypeStruct((B, 1, S5_CH), F32),
                   jax.ShapeDtypeStruct((B, 1, S5_CH), F32)),
        grid=(B, L // Lc),
        in_specs=[pl.BlockSpec((None, Lc, MIX), lambda b, c: (b, c, Z_S5 // MIX)),
                  small((S5_LB, 128, 512)), small((S5_LB, 128, 512)),
                  small((S5_LB, 512, 128)), small((S5_LB, 512, 128)),
                  small((8, S5_CH)), small((8, S5_CH)), small((1, MIX)), small((MIX, MIX))],
        out_specs=(pl.BlockSpec((None, Lc, MIX), lambda b, c: (b, c, 0)),
                   pl.BlockSpec((None, 1, S5_CH), lambda b, c: (b, 0, 0)),
                   pl.BlockSpec((None, 1, S5_CH), lambda b, c: (b, 0, 0))),
        scratch_shapes=[pltpu.VMEM((Lc, S5_CH), F32), pltpu.VMEM((Lc, S5_CH), F32)],
        compiler_params=_cparams("parallel", "arbitrary"),
        name="s5_prompt",
    )(z3, p["s5_wbr"], p["s5_wbi"], p["s5_wcr"], p["s5_wci"], p["s5_pr"], p["s5_pi"], p["s5_d"], p["s5_glu"])


def _merge_kernel(y0_ref, y1_ref, y2_ref, y3_ref, zg_ref, h_ref, wb_ref, wo_ref, o_ref):
    acc = None
    for k, y_ref in enumerate((y0_ref, y1_ref, y2_ref, y3_ref)):
        br = _bdot(y_ref[...], wb_ref[k])
        t = jax.nn.sigmoid(zg_ref[:, k * D_MODEL:(k + 1) * D_MODEL]) * br
        acc = t if acc is None else acc + t
    o_ref[...] = h_ref[...] + _bdot(acc, wo_ref[...])


def _merge(ys, z, h, p, tm):
    T = h.shape[0]
    rows = lambda w: pl.BlockSpec((tm, w), lambda i: (i, 0))
    return pl.pallas_call(
        _merge_kernel,
        out_shape=jax.ShapeDtypeStruct((T, D_MODEL), F32),
        grid=(T // tm,),
        in_specs=[rows(MIX), rows(MIX), rows(MIX), rows(MIX),
                  pl.BlockSpec((tm, 4 * D_MODEL), lambda i: (i, Z_MERGE)),
                  rows(D_MODEL),
                  pl.BlockSpec((4, MIX, D_MODEL), lambda i: (0, 0, 0)),
                  pl.BlockSpec((D_MODEL, D_MODEL), lambda i: (0, 0))],
        out_specs=rows(D_MODEL),
        compiler_params=_cparams("parallel"),
        name="merge",
    )(*ys, z, h, p["w_branch"], p["w_out"])


def _top2_gates(logits):
    lane = lax.broadcasted_iota(jnp.int32, logits.shape, 1).astype(F32)
    big = float(LANES)
    m1 = jnp.max(logits, axis=-1, keepdims=True)
    i1 = jnp.min(jnp.where(logits == m1, lane, big), axis=-1, keepdims=True)
    rest = jnp.where(lane == i1, -jnp.inf, logits)
    m2 = jnp.max(rest, axis=-1, keepdims=True)
    i2 = jnp.min(jnp.where(rest == m2, lane, big), axis=-1, keepdims=True)
    e2 = jnp.exp(m2 - m1)
    den = 1.0 + e2
    return jnp.where(lane == i1, 1.0 / den, 0.0) + jnp.where(lane == i2, e2 / den, 0.0)


def _ffn_kernel(moe, final_norm, *refs):
    if moe:
        h_ref, g_ref, rt_ref, w1_ref, w3_ref, w2_ref, gf_ref, o_ref, hn_ref, acc_ref, gate_ref = refs
    else:
        h_ref, g_ref, w1_ref, w3_ref, w2_ref, gf_ref, o_ref, hn_ref, acc_ref = refs
    e = pl.program_id(1)
    ne = pl.num_programs(1)

    @pl.when(e == 0)
    def _():
        hn = _rmsnorm(h_ref[...], g_ref[...])
        hn_ref[...] = hn.astype(BF16)
        acc_ref[...] = jnp.zeros(acc_ref.shape, F32)
        if moe:
            lane = lax.broadcasted_iota(jnp.int32, (hn.shape[0], LANES), 1)
            logits = jnp.where(lane < N_EXPERTS, _hdot(hn, rt_ref[...]), -jnp.inf)
            gate_ref[...] = _top2_gates(logits)

    hn = hn_ref[...]
    a = jnp.dot(hn, w1_ref[...], preferred_element_type=F32)
    b = jnp.dot(hn, w3_ref[...], preferred_element_type=F32)
    o = _bdot(_silu(a) * b, w2_ref[...])
    if moe:
        lane = lax.broadcasted_iota(jnp.int32, gate_ref.shape, 1)
        ge = jnp.sum(jnp.where(lane == e, gate_ref[...], 0.0), axis=-1, keepdims=True)
        o = ge * o
    acc_ref[...] += o

    @pl.when(e == ne - 1)
    def _():
        out = h_ref[...] + acc_ref[...]
        if final_norm:
            out = _rmsnorm(out, gf_ref[...])
        o_ref[...] = out


def _ffn(h, g, w1, w3, w2, gfinal, tm, router=None, final_norm=False):
    T = h.shape[0]
    moe = router is not None
    tf = D_FF_TILE
    row_spec = pl.BlockSpec((tm, D_MODEL), lambda i, e: (i, 0))
    vec_spec = pl.BlockSpec((1, D_MODEL), lambda i, e: (0, 0))
    if moe:
        ne = w1.shape[0]
        wspecs = [pl.BlockSpec((None, D_MODEL, tf), lambda i, e: (e, 0, 0)),
                  pl.BlockSpec((None, D_MODEL, tf), lambda i, e: (e, 0, 0)),
                  pl.BlockSpec((None, tf, D_MODEL), lambda i, e: (e, 0, 0))]
        in_specs = [row_spec, vec_spec, pl.BlockSpec((D_MODEL, LANES), lambda i, e: (0, 0))] + wspecs + [vec_spec]
        args = (h, g, router, w1, w3, w2, gfinal)
        scratch = [pltpu.VMEM((tm, D_MODEL), BF16), pltpu.VMEM((tm, D_MODEL), F32), pltpu.VMEM((tm, LANES), F32)]
    else:
        ne = w1.shape[1] // tf
        wspecs = [pl.BlockSpec((D_MODEL, tf), lambda i, e: (0, e)),
                  pl.BlockSpec((D_MODEL, tf), lambda i, e: (0, e)),
                  pl.BlockSpec((tf, D_MODEL), lambda i, e: (e, 0))]
        in_specs = [row_spec, vec_spec] + wspecs + [vec_spec]
        args = (h, g, w1, w3, w2, gfinal)
        scratch = [pltpu.VMEM((tm, D_MODEL), BF16), pltpu.VMEM((tm, D_MODEL), F32)]
    return pl.pallas_call(
        functools.partial(_ffn_kernel, moe, final_norm),
        out_shape=jax.ShapeDtypeStruct((T, D_MODEL), F32),
        grid=(T // tm, ne),
        in_specs=in_specs,
        out_specs=row_spec,
        scratch_shapes=scratch,
        compiler_params=_cparams("parallel", "arbitrary"),
        name="moe" if moe else "ffn",
    )(*args)


def _step_pre_kernel(pos_cos_ref, pos_sin_ref, z_ref, sbuf_ref, lbuf_ref, lst_ref, s5r_ref, s5i_ref,
                     scw_ref, scb_ref, dtb_ref, alog_ref,
                     lcw_ref, lcb_ref, wg_ref, bg_ref, lam_ref,
                     wbr_ref, wbi_ref, wcr_ref, wci_ref, pr_ref, pi_ref, s5d_ref, wglu_ref,
                     kqv_ref, dec_ref, sbuf_o, lbuf_o, lst_o, s5r_o, s5i_o, ys5_o, ylru_o, xs_o,
                     xr_ref, xi_ref):
    Bs = z_ref.shape[0]
    zz = z_ref[...]
    xbc = zz[:, Z_SSD + MIX:Z_SSD + MIX + SSD_CONV]
    dt_raw = zz[:, Z_SSD + MIX + SSD_CONV:Z_SSD + MIX + SSD_CONV + LANES]
    W = SSD_CONV
    conv = scb_ref[...] + sbuf_ref[:, 0:W] * scw_ref[0:1, :]
    conv = conv + sbuf_ref[:, W:2 * W] * scw_ref[1:2, :]
    conv = conv + sbuf_ref[:, 2 * W:3 * W] * scw_ref[2:3, :]
    conv = conv + xbc * scw_ref[3:4, :]
    sbuf_o[:, 0:2 * W] = sbuf_ref[:, W:3 * W]
    sbuf_o[:, 2 * W:3 * W] = xbc
    xc = _silu(conv)
    xs = xc[:, 0:MIX]
    xs_o[...] = xs
    nbc = SSD_GROUPS * SSD_STATE
    bm = xc[:, MIX:MIX + nbc]
    cm = xc[:, MIX + nbc:MIX + 2 * nbc]
    dt = jax.nn.softplus(dt_raw + dtb_ref[...])
    a = -jnp.exp(alog_ref[...])
    dec = jnp.exp(dt * a)
    rep = SSD_HEADS // SSD_GROUPS
    for h in range(SSD_HEADS):
        g = h // rep
        kqv_ref[0, 0, h] = bm[:, g * SSD_STATE:(g + 1) * SSD_STATE] * dt[:, h:h + 1]
        kqv_ref[0, 1, h] = cm[:, g * SSD_STATE:(g + 1) * SSD_STATE]
        kqv_ref[0, 2, h] = xs[:, h * SSD_HD:(h + 1) * SSD_HD]
        dec_ref[0, h] = jnp.broadcast_to(dec[:, h:h + 1], (Bs, LANES))
    q = _rotary_lanes(zz[:, Z_RET:Z_RET + MIX], pos_cos_ref[...], pos_sin_ref[...]) * (RET_HD ** -0.5)
    k = _rotary_lanes(zz[:, Z_RET + MIX:Z_RET + 2 * MIX], pos_cos_ref[...], pos_sin_ref[...])
    v = zz[:, Z_RET + 2 * MIX:Z_RET + 3 * MIX]
    gam = _ret_gammas()
    for h in range(RET_HEADS):
        sl = slice(h * RET_HD, (h + 1) * RET_HD)
        kqv_ref[1, 0, h] = k[:, sl]
        kqv_ref[1, 1, h] = q[:, sl]
        kqv_ref[1, 2, h] = v[:, sl]
        dec_ref[1, h] = jnp.full((Bs, LANES), float(gam[h]), F32)
    gate = zz[:, Z_LRU:Z_LRU + MIX]
    lx = zz[:, Z_LRU + MIX:Z_LRU + 2 * MIX]
    W = MIX
    lconv = lcb_ref[...] + lbuf_ref[:, 0:W] * lcw_ref[0:1, :]
    lconv = lconv + lbuf_ref[:, W:2 * W] * lcw_ref[1:2, :]
    lconv = lconv + lbuf_ref[:, 2 * W:3 * W] * lcw_ref[2:3, :]
    lconv = lconv + lx * lcw_ref[3:4, :]
    lbuf_o[:, 0:2 * W] = lbuf_ref[:, W:3 * W]
    lbuf_o[:, 2 * W:3 * W] = lx
    la, lbx = _lru_gates(lconv, wg_ref, bg_ref, lam_ref)
    hl = lbx + la * lst_ref[...]
    lst_o[...] = hl
    ylru_o[...] = hl * jax.nn.gelu(gate)
    u = zz[:, Z_S5:Z_S5 + MIX]
    _s5_project_in(u, wbr_ref, wbi_ref, xr_ref, xi_ref)
    lr, li = pr_ref[0:1, :], pi_ref[0:1, :]
    s0r, s0i = s5r_ref[...], s5i_ref[...]
    xr = xr_ref[...] + (lr * s0r - li * s0i)
    xi = xi_ref[...] + (lr * s0i + li * s0r)
    s5r_o[...] = xr
    s5i_o[...] = xi
    ys5_o[...] = _s5_project_out(xr, xi, u, wcr_ref, wci_ref, s5d_ref, wglu_ref)


def _step_state_kernel(kqv_ref, dec_ref, st_ref, rexp_ref, o_st_ref, y_ref):
    k = kqv_ref[0]
    q = kqv_ref[1]
    v = kqv_ref[2]
    n_e = st_ref.shape[-1]
    reps = n_e // (2 * SSD_HD)
    k_rep = _split3_dot(k, rexp_ref[...])
    q_rep = _split3_dot(q, rexp_ref[...])
    v2 = jnp.concatenate([v, v], axis=1)
    v_rep = jnp.concatenate([v2] * reps, axis=1)
    d = dec_ref[:, 0:1]
    s_new = d * st_ref[...] + k_rep * v_rep
    o_st_ref[...] = s_new
    prod = q_rep * s_new
    acc = prod[:, 0:LANES]
    for j in range(1, reps):
        acc = acc + prod[:, j * LANES:(j + 1) * LANES]
    y_ref[...] = acc[:, 0:SSD_HD] + acc[:, SSD_HD:2 * SSD_HD]


def _step_post_kernel(yssd_ref, yret_ref, xs_ref, z_ref, dlane_ref, ng_ref, gn_ref, yssd_o, yret_o, acc_ref):
    zz_gate = z_ref[:, Z_SSD:Z_SSD + MIX]
    for h in range(SSD_HEADS):
        acc_ref[:, h * SSD_HD:(h + 1) * SSD_HD] = yssd_ref[h]
    y = acc_ref[...] + dlane_ref[...] * xs_ref[...]
    y = y * _silu(zz_gate)
    yssd_o[...] = _rmsnorm(y, ng_ref[...])
    for h in range(RET_HEADS):
        acc_ref[:, h * RET_HD:(h + 1) * RET_HD] = _group_norm_head(yret_ref[h])
    rgate = z_ref[:, Z_RET + 3 * MIX:Z_RET + 4 * MIX]
    yret_o[...] = _silu(rgate) * (acc_ref[...] * gn_ref[...])


def _rexp_matrix():
    r = np.zeros((SSD_STATE, SSD_STATE * SSD_HD), np.float32)
    for n in range(SSD_STATE):
        r[n, n * SSD_HD:(n + 1) * SSD_HD] = 1.0
    return jnp.asarray(r)


def _sample_mixers(z, st_all, layer, p, pos):
    Bs = z.shape[0]
    st = {k: v[layer] for k, v in st_all.items() if k not in ("ssd", "ret")}
    H = SSD_HEADS
    NE = SSD_STATE * SSD_HD
    cos_l, sin_l = _rope_tables(np.asarray([pos]))
    rexp = _rexp_matrix()
    sbuf = st["ssd_conv"].reshape(Bs, (CONV_K - 1) * SSD_CONV)
    lbuf = st["lru_conv"].reshape(Bs, (CONV_K - 1) * MIX)
    s5r = st["s5_re"].reshape(Bs, S5_CH)
    s5i = st["s5_im"].reshape(Bs, S5_CH)
    pre_in = (cos_l, sin_l, z, sbuf, lbuf, st["lru"], s5r, s5i,
              p["ssd_cw"], p["ssd_cb"], p["ssd_dtb"], p["ssd_alog"],
              p["lru_cw"], p["lru_cb"], p["lru_wg"], p["lru_bg"], p["lru_lam"],
              p["s5_wbr"], p["s5_wbi"], p["s5_wcr"], p["s5_wci"], p["s5_pr"], p["s5_pi"], p["s5_d"], p["s5_glu"])
    pre_out = (jax.ShapeDtypeStruct((2, 3, H, Bs, SSD_HD), F32),
               jax.ShapeDtypeStruct((2, H, Bs, LANES), F32),
               jax.ShapeDtypeStruct(sbuf.shape, F32), jax.ShapeDtypeStruct(lbuf.shape, F32),
               jax.ShapeDtypeStruct((Bs, MIX), F32),
               jax.ShapeDtypeStruct((Bs, S5_CH), F32), jax.ShapeDtypeStruct((Bs, S5_CH), F32),
               jax.ShapeDtypeStruct((Bs, MIX), F32), jax.ShapeDtypeStruct((Bs, MIX), F32),
               jax.ShapeDtypeStruct((Bs, MIX), F32))
    (kqv, dec, sbuf_n, lbuf_n, lst_n, s5r_n, s5i_n, y_s5, y_lru, xs) = pl.pallas_call(
        _step_pre_kernel,
        out_shape=pre_out,
        scratch_shapes=[pltpu.VMEM((Bs, S5_CH), F32), pltpu.VMEM((Bs, S5_CH), F32)],
        compiler_params=pltpu.CompilerParams(vmem_limit_bytes=VMEM_LIMIT),
        name="step_pre",
    )(*pre_in)

    new_states, yhs = [], []
    for m, name in enumerate(("ssd", "ret")):
        depth = st_all[name].shape[0]
        s_new, yh_m = pl.pallas_call(
            _step_state_kernel,
            out_shape=(jax.ShapeDtypeStruct((Bs, H * NE), F32), jax.ShapeDtypeStruct((H, Bs, SSD_HD), F32)),
            grid=(H,),
            in_specs=[pl.BlockSpec((None, 3, None, Bs, SSD_HD), lambda h, m=m: (m, 0, h, 0, 0)),
                      pl.BlockSpec((None, None, Bs, LANES), lambda h, m=m: (m, h, 0, 0)),
                      pl.BlockSpec((None, Bs, NE), lambda h: (layer, 0, h)),
                      pl.BlockSpec(rexp.shape, lambda h: (0, 0))],
            out_specs=(pl.BlockSpec((Bs, NE), lambda h: (0, h)),
                       pl.BlockSpec((None, Bs, SSD_HD), lambda h: (h, 0, 0))),
            compiler_params=_cparams("parallel"),
            name="step_state_" + name,
        )(kqv, dec, st_all[name].reshape(depth, Bs, H * NE), rexp)
        new_states.append(s_new)
        yhs.append(yh_m)

    post_in = (yhs[0], yhs[1], xs, z, p["ssd_dlane"], p["ssd_norm"], p["ret_gn"])
    y_ssd, y_ret = pl.pallas_call(
        _step_post_kernel,
        out_shape=(jax.ShapeDtypeStruct((Bs, MIX), F32), jax.ShapeDtypeStruct((Bs, MIX), F32)),
        scratch_shapes=[pltpu.VMEM((Bs, MIX), F32)],
        compiler_params=pltpu.CompilerParams(vmem_limit_bytes=VMEM_LIMIT),
        name="step_post",
    )(*post_in)

    new = dict(
        ssd=new_states[0].reshape(Bs, H, SSD_STATE, SSD_HD),
        ssd_conv=sbuf_n.reshape(Bs, CONV_K - 1, SSD_CONV),
        s5_re=s5r_n.reshape(Bs, S5_GROUPS, S5_STATE),
        s5_im=s5i_n.reshape(Bs, S5_GROUPS, S5_STATE),
        lru=lst_n,
        lru_conv=lbuf_n.reshape(Bs, CONV_K - 1, MIX),
        ret=new_states[1].reshape(Bs, H, RET_HD, RET_HD),
    )
    return (y_ssd, y_s5, y_lru, y_ret), new


def _block_diag(w):
    n, r, c = w.shape
    eye = jnp.eye(n, dtype=w.dtype)
    return (eye[:, None, :, None] * w[:, :, None, :]).reshape(n * r, n * c)


def _pad_lanes(v):
    return jnp.zeros((1, LANES), F32).at[0, :v.shape[0]].set(v.astype(F32))


def _prep_layer(l, W):
    p = {}
    w_in = W["w_in"][l]
    p["w_in"] = jnp.concatenate(
        [w_in[:, 4872:8968], w_in[:, 2824:4872], w_in[:, 1800:2824], w_in[:, 1288:1800], w_in[:, 0:1288],
         jnp.zeros((D_MODEL, Z_WIDTH - 8968), w_in.dtype)], axis=1).astype(BF16)
    p["norm_mix"] = W["norm_mix"][l].reshape(1, D_MODEL)
    p["ssd_cw"] = W["ssd_conv_w"][l]
    p["ssd_cb"] = W["ssd_conv_b"][l].reshape(1, SSD_CONV)
    p["ssd_dtb"] = _pad_lanes(W["ssd_dt_bias"][l])
    p["ssd_alog"] = _pad_lanes(W["ssd_a_log"][l])
    p["ssd_dlane"] = jnp.repeat(W["ssd_d"][l], SSD_HD).reshape(1, MIX)
    p["ssd_norm"] = W["ssd_norm"][l].reshape(1, MIX)
    lam = lax.complex(W["s5_lambda_re"][l], W["s5_lambda_im"][l])
    dt = jnp.exp(W["s5_log_dt"][l])[:, None]
    lam_bar = jnp.exp(lam * dt)
    b_bar = ((lam_bar - 1.0) / lam)[:, :, None] * lax.complex(W["s5_b_re"][l], W["s5_b_im"][l])
    gb = S5_GROUPS // S5_LB

    def embed_in(m):
        return jnp.stack([_block_diag(jnp.swapaxes(m[k * gb:(k + 1) * gb], 1, 2)) for k in range(S5_LB)])

    def embed_out(m):
        return jnp.stack([_block_diag(jnp.swapaxes(m[k * gb:(k + 1) * gb], 1, 2)) for k in range(S5_LB)])

    p["s5_wbr"] = embed_in(jnp.real(b_bar)).astype(BF16)
    p["s5_wbi"] = embed_in(jnp.imag(b_bar)).astype(BF16)
    p["s5_wcr"] = embed_out(W["s5_c_re"][l]).astype(BF16)
    p["s5_wci"] = embed_out(W["s5_c_im"][l]).astype(BF16)
    pw = [lam_bar.reshape(1, S5_CH)]
    for _ in range(SUBLANES - 1):
        pw.append(pw[-1] * pw[0])
    pw = jnp.concatenate(pw, axis=0)
    p["s5_pr"] = jnp.real(pw)
    p["s5_pi"] = jnp.imag(pw)
    p["s5_d"] = W["s5_d"][l].reshape(1, MIX)
    p["s5_glu"] = W["s5_glu"][l].astype(BF16)
    p["lru_cw"] = W["lru_conv_w"][l]
    p["lru_cb"] = W["lru_conv_b"][l].reshape(1, MIX)
    p["lru_wg"] = jnp.concatenate([_block_diag(W["lru_wa"][l]), _block_diag(W["lru_wx"][l])], axis=1).astype(BF16)
    p["lru_bg"] = jnp.concatenate([W["lru_ba"][l], W["lru_bx"][l]]).reshape(1, 2 * MIX)
    p["lru_lam"] = W["lru_lambda"][l].reshape(1, MIX)
    p["ret_gn"] = W["ret_gn"][l].reshape(1, MIX)
    p["w_branch"] = W["w_branch"][l].astype(BF16)
    p["w_out"] = W["w_out"][l].astype(BF16)
    p["norm_ffn"] = W["norm_ffn"][l].reshape(1, D_MODEL)
    return p


def _channel_mixer(l, h, p, W, tm, final):
    gfin = W["norm_final"].reshape(1, D_MODEL)
    j = l // 2
    if l % 2 == 0:
        return _ffn(h, p["norm_ffn"], W["ffn_w1_bf"][j], W["ffn_w3_bf"][j], W["ffn_w2_bf"][j], gfin, tm,
                    final_norm=final)
    router = jnp.zeros((D_MODEL, LANES), F32).at[:, :N_EXPERTS].set(W["moe_router"][j])
    return _ffn(h, p["norm_ffn"], W["moe_w1_bf"][j], W["moe_w3_bf"][j], W["moe_w2_bf"][j], gfin, tm,
                router=router, final_norm=final)


def _trunk_prompt(x, W, preps):
    B, L, _ = x.shape
    T = B * L
    tm = min(512, T)
    depth = len(preps)
    h = x.reshape(T, D_MODEL)
    new = {k: [] for k in ("ssd", "ssd_conv", "s5_re", "s5_im", "lru", "lru_conv", "ret")}
    for l, p in enumerate(preps):
        z = _inproj(h, p["norm_mix"], p["w_in"], min(1024, T))
        z3 = z.reshape(B, L, Z_WIDTH)
        y_ssd, s_ssd, buf_ssd = _ssd_prompt(z3, p)
        y_s5, s5r, s5i = _s5_prompt(z3, p)
        y_lru, s_lru, buf_lru = _lru_prompt(z3, p)
        y_ret, s_ret = _ret_prompt(z3, p)
        ys = tuple(y.reshape(T, MIX) for y in (y_ssd, y_s5, y_lru, y_ret))
        h = _merge(ys, z, h, p, min(256, T))
        h = _channel_mixer(l, h, p, W, tm, final=(l == depth - 1))
        new["ssd"].append(s_ssd)
        new["ssd_conv"].append(buf_ssd)
        new["s5_re"].append(s5r.reshape(B, S5_GROUPS, S5_STATE))
        new["s5_im"].append(s5i.reshape(B, S5_GROUPS, S5_STATE))
        new["lru"].append(s_lru.reshape(B, MIX))
        new["lru_conv"].append(buf_lru)
        new["ret"].append(s_ret)
    return h.reshape(B, L, D_MODEL), {k: jnp.stack(v) for k, v in new.items()}


def _trunk_sample(x, pos, st, W, preps):
    Bs = x.shape[0]
    depth = len(preps)
    h = x.reshape(Bs, D_MODEL)
    new = {k: [] for k in ("ssd", "ssd_conv", "s5_re", "s5_im", "lru", "lru_conv", "ret")}
    for l, p in enumerate(preps):
        z = _inproj(h, p["norm_mix"], p["w_in"], Bs)
        ys, nl = _sample_mixers(z, st, l, p, pos)
        h = _merge(ys, z, h, p, Bs)
        h = _channel_mixer(l, h, p, W, Bs, final=(l == depth - 1))
        for k in new:
            new[k].append(nl[k])
    return h.reshape(Bs, 1, D_MODEL), {k: jnp.stack(v) for k, v in new.items()}


def kernel(x_prompt, x_sample, state_ssd, state_ssd_conv, state_s5_re, state_s5_im, state_lru, state_lru_conv, state_ret, norm_mix, w_in, ssd_conv_w, ssd_conv_b, ssd_dt_bias, ssd_a_log, ssd_d, ssd_norm, s5_lambda_re, s5_lambda_im, s5_b_re, s5_b_im, s5_c_re, s5_c_im, s5_d, s5_log_dt, s5_glu, lru_conv_w, lru_conv_b, lru_wa, lru_ba, lru_wx, lru_bx, lru_lambda, ret_gn, w_branch, w_out, norm_ffn, ffn_w1, ffn_w3, ffn_w2, moe_router, moe_w1, moe_w3, moe_w2, norm_final):
    W = dict(norm_mix=norm_mix, w_in=w_in, ssd_conv_w=ssd_conv_w, ssd_conv_b=ssd_conv_b, ssd_dt_bias=ssd_dt_bias,
             ssd_a_log=ssd_a_log, ssd_d=ssd_d, ssd_norm=ssd_norm, s5_lambda_re=s5_lambda_re,
             s5_lambda_im=s5_lambda_im, s5_b_re=s5_b_re, s5_b_im=s5_b_im, s5_c_re=s5_c_re, s5_c_im=s5_c_im,
             s5_d=s5_d, s5_log_dt=s5_log_dt, s5_glu=s5_glu, lru_conv_w=lru_conv_w, lru_conv_b=lru_conv_b,
             lru_wa=lru_wa, lru_ba=lru_ba, lru_wx=lru_wx, lru_bx=lru_bx, lru_lambda=lru_lambda, ret_gn=ret_gn,
             w_branch=w_branch, w_out=w_out, norm_ffn=norm_ffn, moe_router=moe_router, norm_final=norm_final)
    W["ffn_w1_bf"] = ffn_w1.astype(BF16)
    W["ffn_w3_bf"] = ffn_w3.astype(BF16)
    W["ffn_w2_bf"] = ffn_w2.astype(BF16)
    W["moe_w1_bf"] = moe_w1.astype(BF16)
    W["moe_w3_bf"] = moe_w3.astype(BF16)
    W["moe_w2_bf"] = moe_w2.astype(BF16)
    depth = w_in.shape[0]
    preps = [_prep_layer(l, W) for l in range(depth)]
    y_p, sp = _trunk_prompt(x_prompt, W, preps)
    st = dict(ssd=state_ssd, ssd_conv=state_ssd_conv, s5_re=state_s5_re, s5_im=state_s5_im,
              lru=state_lru, lru_conv=state_lru_conv, ret=state_ret)
    past_len = 16384
    y_s, ss = _trunk_sample(x_sample, past_len, st, W, preps)
    names = ("ssd", "ssd_conv", "s5_re", "s5_im", "lru", "lru_conv", "ret")
    return (y_p, y_s) + tuple(sp[n] for n in names) + tuple(ss[n] for n in names)
```

```python
import functools
import math

import jax
import jax.numpy as jnp
import numpy as np
from jax import lax
from jax.experimental import pallas as pl
from jax.experimental.pallas import tpu as pltpu

F32 = jnp.float32
BF16 = jnp.bfloat16
EPS = 1e-6

D_MODEL = 1024
MIX = 512
CONV_K = 4
CHUNK = 128
SSD_HEADS = 8
SSD_HD = 64
SSD_STATE = 64
SSD_GROUPS = 2
SSD_CONV = MIX + 2 * SSD_GROUPS * SSD_STATE
S5_GROUPS = 32
S5_GDIM = 16
S5_STATE = 64
S5_CH = S5_GROUPS * S5_STATE
LRU_BLOCKS = 8
LRU_C = 8.0
RET_HEADS = 8
RET_HD = 64
ROPE_BASE = 10000.0
N_EXPERTS = 8
D_FF_TILE = 1408

Z_MERGE = 0
Z_RET = 4096
Z_LRU = 6144
Z_S5 = 7168
Z_SSD = 7680
Z_WIDTH = 9216

VMEM_LIMIT = 56 * 1024 * 1024
LANES = 128
SUBLANES = 8


def _cparams(*sem):
    return pltpu.CompilerParams(dimension_semantics=sem, vmem_limit_bytes=VMEM_LIMIT)


def _bdot(a, b):
    return jnp.dot(a.astype(BF16), b.astype(BF16), preferred_element_type=F32)


def _bdot_nt(a, b):
    return lax.dot_general(a.astype(BF16), b.astype(BF16), (((1,), (1,)), ((), ())), preferred_element_type=F32)


def _bdot_tn(a, b):
    return lax.dot_general(a.astype(BF16), b.astype(BF16), (((0,), (0,)), ((), ())), preferred_element_type=F32)


def _hdot(a, b):
    return jnp.dot(a, b, precision=lax.Precision.HIGHEST, preferred_element_type=F32)


def _split3_dot(x, m01):
    hi = x.astype(BF16)
    r1 = x - hi.astype(F32)
    mid = r1.astype(BF16)
    lo = (r1 - mid.astype(F32)).astype(BF16)
    m = m01.astype(BF16)
    d = functools.partial(jnp.dot, preferred_element_type=F32)
    return (d(lo, m) + d(mid, m)) + d(hi, m)


def _rmsnorm(x, g):
    ms = jnp.mean(x * x, axis=-1, keepdims=True)
    return x * lax.rsqrt(ms + EPS) * g


def _silu(x):
    return x * jax.nn.sigmoid(x)


def _neg_expm1_2x(log_a, a):
    return jnp.tanh(-log_a) * (1.0 + a * a)


def _inproj_kernel(x_ref, g_ref, w_ref, o_ref, hn_ref):
    @pl.when(pl.program_id(1) == 0)
    def _():
        hn_ref[...] = _rmsnorm(x_ref[...], g_ref[...]).astype(BF16)

    o_ref[...] = jnp.dot(hn_ref[...], w_ref[...], preferred_element_type=F32).astype(o_ref.dtype)


def _inproj(x, g, w, tm, out_dtype):
    T = x.shape[0]
    tn = 1536
    return pl.pallas_call(
        _inproj_kernel,
        out_shape=jax.ShapeDtypeStruct((T, Z_WIDTH), out_dtype),
        grid=(T // tm, Z_WIDTH // tn),
        in_specs=[pl.BlockSpec((tm, D_MODEL), lambda i, j: (i, 0)),
                  pl.BlockSpec((1, D_MODEL), lambda i, j: (0, 0)),
                  pl.BlockSpec((D_MODEL, tn), lambda i, j: (0, j))],
        out_specs=pl.BlockSpec((tm, tn), lambda i, j: (i, j)),
        scratch_shapes=[pltpu.VMEM((tm, D_MODEL), BF16)],
        compiler_params=_cparams("parallel", "arbitrary"),
        name="inproj",
    )(x, g, w)


def _conv_chunk(c, x, pad_ref, w_ref, b_ref):
    Lc = x.shape[0]

    @pl.when(c == 0)
    def _():
        pad_ref[0:8, :] = jnp.zeros((8, x.shape[1]), F32)

    pad_ref[8:8 + Lc, :] = x
    out = b_ref[...] + pad_ref[5:5 + Lc, :] * w_ref[0:1, :]
    out = out + pad_ref[6:6 + Lc, :] * w_ref[1:2, :]
    out = out + pad_ref[7:7 + Lc, :] * w_ref[2:3, :]
    out = out + x * w_ref[3:4, :]
    return out


def _conv_finish(pad_ref, Lc):
    pad_ref[0:8, :] = pad_ref[Lc:Lc + 8, :]


def _pow2_div(x, d):
    return lax.shift_right_logical(x, jnp.int32(int(math.log2(d))))


def _head_block_rows(x, nh, hd):
    L = x.shape[0]
    xt = jnp.concatenate([x.astype(BF16)] * nh, axis=0)
    row = lax.broadcasted_iota(jnp.int32, xt.shape, 0)
    col = lax.broadcasted_iota(jnp.int32, xt.shape, 1)
    return jnp.where(_pow2_div(row, L) == _pow2_div(col, hd), xt, jnp.zeros_like(xt))


def _head_block_cols(xt, nh, hd):
    L = xt.shape[1]
    xc = jnp.concatenate([xt.astype(BF16)] * nh, axis=1)
    row = lax.broadcasted_iota(jnp.int32, xc.shape, 0)
    col = lax.broadcasted_iota(jnp.int32, xc.shape, 1)
    return jnp.where(_pow2_div(row, hd) == _pow2_div(col, L), xc, jnp.zeros_like(xc))


def _head_diag_mask(n, hd):
    row = lax.broadcasted_iota(jnp.int32, (n, n), 0)
    col = lax.broadcasted_iota(jnp.int32, (n, n), 1)
    return _pow2_div(row, hd) == _pow2_div(col, hd)


def _group_repeat_lanes(m, rep):
    lane = lax.broadcasted_iota(jnp.int32, m.shape, 1)
    swapped = pltpu.roll(m, m.shape[1] // 2, 1)
    low = lane < m.shape[1] // 2
    g0 = jnp.where(low, m, swapped)
    g1 = jnp.where(low, swapped, m)
    return jnp.concatenate([g0] * (rep // 2) + [g1] * (rep // 2), axis=1)


def _ssd_kernel(z_ref, cw_ref, cb_ref, dtb_ref, alog_ref, dlane_ref, ng_ref, tri_ref, ehj_ref, ehn_ref,
                y_ref, sto_ref, buf_ref, pad_ref, st_ref):
    c = pl.program_id(1)
    nc = pl.num_programs(1)
    Lc = z_ref.shape[0]
    zz = z_ref[...].astype(F32)
    zgate = zz[:, 0:MIX]
    xbc = zz[:, MIX:MIX + SSD_CONV]
    dt_raw = zz[:, MIX + SSD_CONV:MIX + SSD_CONV + LANES]

    @pl.when(c == 0)
    def _():
        st_ref[...] = jnp.zeros(st_ref.shape, F32)

    conv = _conv_chunk(c, xbc, pad_ref, cw_ref, cb_ref)
    xc = _silu(conv)
    xs = xc[:, 0:MIX]
    nbc = SSD_GROUPS * SSD_STATE
    bm = xc[:, MIX:MIX + nbc]
    cm = xc[:, MIX + nbc:MIX + 2 * nbc]
    dt = jax.nn.softplus(dt_raw + dtb_ref[...])
    a = -jnp.exp(alog_ref[...])
    ld = dt * a
    acum = _hdot(tri_ref[...], ld)
    acum_t = acum.T
    dt_t = dt.T
    a_row = jnp.concatenate([acum_t[h:h + 1, :] for h in range(SSD_HEADS)], axis=1)
    dt_row = jnp.concatenate([dt_t[h:h + 1, :] for h in range(SSD_HEADS)], axis=1)
    a_col = _split3_dot(acum, ehj_ref[...])
    row = lax.broadcasted_iota(jnp.int32, (Lc, SSD_HEADS * Lc), 0)
    col = lax.broadcasted_iota(jnp.int32, (Lc, SSD_HEADS * Lc), 1)
    causal = row >= (col & (Lc - 1))
    decay = jnp.where(causal, jnp.exp(jnp.where(causal, a_col - a_row, 0.0)), 0.0)
    rep = SSD_HEADS // SSD_GROUPS
    gmats = []
    for g in range(SSD_GROUPS):
        cg = cm[:, g * SSD_STATE:(g + 1) * SSD_STATE]
        bg = bm[:, g * SSD_STATE:(g + 1) * SSD_STATE]
        gmats.append(_bdot_nt(cg, bg))
    g_all = jnp.concatenate([gmats[h // rep] for h in range(SSD_HEADS)], axis=1)
    m_all = g_all * decay * dt_row
    y = _bdot(m_all, _head_block_rows(xs, SSD_HEADS, SSD_HD))
    exp_a = jnp.exp(acum)
    exp_a_l = _split3_dot(exp_a, ehn_ref[...])
    c_rep = _group_repeat_lanes(cm, rep)
    y = y + _bdot_nt(c_rep * exp_a_l, st_ref[...])
    a_last = acum[Lc - 1:Lc, :]
    w_end_l = _split3_dot(jnp.exp(a_last - acum) * dt, ehn_ref[...])
    upd = _bdot_tn(xs, _group_repeat_lanes(bm, rep) * w_end_l)
    st_ref[...] = exp_a_l[Lc - 1:Lc, :] * st_ref[...] + jnp.where(_head_diag_mask(MIX, SSD_HD), upd, 0.0)
    y = y + dlane_ref[...] * xs
    y = y * _silu(zgate)
    y_ref[...] = _rmsnorm(y, ng_ref[...]).astype(y_ref.dtype)

    @pl.when(c == nc - 1)
    def _():
        buf_ref[...] = pad_ref[Lc + 8 - (CONV_K - 1):Lc + 8, :]
        s_t = st_ref[...].T
        for h in range(SSD_HEADS):
            sl = slice(h * SSD_HD, (h + 1) * SSD_HD)
            sto_ref[h] = s_t[sl, sl]

    _conv_finish(pad_ref, Lc)


def _ssd_prompt(z3, p):
    B, L, _ = z3.shape
    Lc = CHUNK
    small = lambda shape: pl.BlockSpec(shape, lambda b, c: tuple(0 for _ in shape))
    return pl.pallas_call(
        _ssd_kernel,
        out_shape=(jax.ShapeDtypeStruct((B, L, MIX), BF16),
                   jax.ShapeDtypeStruct((B, SSD_HEADS, SSD_STATE, SSD_HD), F32),
                   jax.ShapeDtypeStruct((B, CONV_K - 1, SSD_CONV), F32)),
        grid=(B, L // Lc),
        in_specs=[pl.BlockSpec((None, Lc, 1536), lambda b, c: (b, c, Z_SSD // 1536)),
                  small((CONV_K, SSD_CONV)), small((1, SSD_CONV)), small((1, LANES)), small((1, LANES)),
                  small((1, MIX)), small((1, MIX)), small((Lc, Lc)),
                  small((LANES, SSD_HEADS * Lc)), small((LANES, MIX))],
        out_specs=(pl.BlockSpec((None, Lc, MIX), lambda b, c: (b, c, 0)),
                   pl.BlockSpec((None, SSD_HEADS, SSD_STATE, SSD_HD), lambda b, c: (b, 0, 0, 0)),
                   pl.BlockSpec((None, CONV_K - 1, SSD_CONV), lambda b, c: (b, 0, 0))),
        scratch_shapes=[pltpu.VMEM((Lc + 8, SSD_CONV), F32), pltpu.VMEM((MIX, MIX), F32)],
        compiler_params=_cparams("parallel", "arbitrary"),
        name="ssd_prompt",
    )(z3, p["ssd_cw"], p["ssd_cb"], p["ssd_dtb"], p["ssd_alog"], p["ssd_dlane"], p["ssd_norm"], _tri(Lc),
      _head_expand(SSD_HEADS, Lc), _head_expand(SSD_HEADS, SSD_HD))


def _tri(Lc):
    return jnp.asarray(np.tril(np.ones((Lc, Lc), np.float32)))


def _head_expand(nh, width):
    e = np.zeros((LANES, nh * width), np.float32)
    for h in range(nh):
        e[h, h * width:(h + 1) * width] = 1.0
    return jnp.asarray(e, BF16)


def _ret_gammas():
    return 1.0 - np.exp2(-5.0 - np.arange(RET_HEADS, dtype=np.float64))


def _ret_tables(Lc):
    gam = _ret_gammas()
    i = np.arange(Lc)
    d = i[:, None] - i[None, :]
    decay = np.where(d >= 0, gam[:, None, None] ** np.maximum(d, 0)[None], 0.0)
    decay_l = np.transpose(decay, (1, 0, 2)).reshape(Lc, RET_HEADS * Lc)
    grow_l = np.repeat(gam[None, :] ** (i[:, None] + 1), RET_HD, axis=1)
    toend_t = np.repeat(gam[:, None] ** (Lc - 1 - i[None, :]), RET_HD, axis=0)
    hd = np.arange(MIX) // RET_HD
    state_decay = np.where(hd[:, None] == hd[None, :], (gam ** Lc)[hd][:, None], 0.0)
    return tuple(jnp.asarray(t, F32) for t in (decay_l, grow_l, toend_t, state_decay))


def _rope_tables(pos):
    half = RET_HD // 2
    inv = ROPE_BASE ** (-np.arange(half, dtype=np.float64) / half)
    ang = np.asarray(pos, np.float64)[:, None] * inv[None, :]
    cos = np.cos(ang)
    sin = np.sin(ang)
    cos_l = np.tile(np.concatenate([cos, cos], axis=1), (1, RET_HEADS))
    sin_l = np.tile(np.concatenate([-sin, sin], axis=1), (1, RET_HEADS))
    return jnp.asarray(cos_l, F32), jnp.asarray(sin_l, F32)


def _rotary_lanes(x, cos_l, sin_l):
    lane = lax.broadcasted_iota(jnp.int32, x.shape, 1)
    first = (lane & (RET_HD - 1)) < (RET_HD // 2)
    n = x.shape[1]
    swapped = jnp.where(first, pltpu.roll(x, n - RET_HD // 2, 1), pltpu.roll(x, RET_HD // 2, 1))
    return x * cos_l + swapped * sin_l


def _group_norm_head(o):
    mu = jnp.mean(o, axis=-1, keepdims=True)
    d = o - mu
    var = jnp.mean(d * d, axis=-1, keepdims=True)
    return d * lax.rsqrt(var + 1e-5)


def _group_norm_lanes(o, hd):
    cols = []
    for cb in range(o.shape[1] // LANES):
        x = o[:, cb * LANES:(cb + 1) * LANES]
        low = lax.broadcasted_iota(jnp.int32, x.shape, 1) < hd

        def seg_mean(t):
            lo = jnp.sum(jnp.where(low, t, 0.0), axis=1, keepdims=True)
            hi = jnp.sum(jnp.where(low, 0.0, t), axis=1, keepdims=True)
            return jnp.where(low, lo, hi) * (1.0 / hd)

        d = x - seg_mean(x)
        cols.append(d * lax.rsqrt(seg_mean(d * d) + 1e-5))
    return jnp.concatenate(cols, axis=1)


def _ret_kernel(z_ref, cos_ref, sin_ref, dec_ref, grow_ref, toend_ref, sdec_ref, gn_ref,
                y_ref, sto_ref, st_ref):
    c = pl.program_id(1)
    nc = pl.num_programs(1)
    zz = z_ref[...].astype(F32)
    q = _rotary_lanes(zz[:, 0:MIX], cos_ref[...], sin_ref[...]) * (RET_HD ** -0.5)
    k = _rotary_lanes(zz[:, MIX:2 * MIX], cos_ref[...], sin_ref[...])
    v = zz[:, 2 * MIX:3 * MIX]
    gate = zz[:, 3 * MIX:4 * MIX]

    @pl.when(c == 0)
    def _():
        st_ref[...] = jnp.zeros(st_ref.shape, F32)

    k_t = k.T
    g_all = _bdot(q, _head_block_cols(k_t, RET_HEADS, RET_HD))
    o = _bdot(g_all * dec_ref[...], _head_block_rows(v, RET_HEADS, RET_HD))
    o = o + _bdot(q * grow_ref[...], st_ref[...])
    upd = _bdot(k_t * toend_ref[...], v)
    st_ref[...] = sdec_ref[...] * st_ref[...] + jnp.where(_head_diag_mask(MIX, RET_HD), upd, 0.0)
    y_ref[...] = (_silu(gate) * (_group_norm_lanes(o, RET_HD) * gn_ref[...])).astype(y_ref.dtype)

    @pl.when(c == nc - 1)
    def _():
        for h in range(RET_HEADS):
            sl = slice(h * RET_HD, (h + 1) * RET_HD)
            sto_ref[h] = st_ref[sl, sl]


def _ret_prompt(z3, p):
    B, L, _ = z3.shape
    Lc = CHUNK
    cos_l, sin_l = _rope_tables(np.arange(L))
    dec, grow, toend, sdec = _ret_tables(Lc)
    small = lambda shape: pl.BlockSpec(shape, lambda b, c: tuple(0 for _ in shape))
    return pl.pallas_call(
        _ret_kernel,
        out_shape=(jax.ShapeDtypeStruct((B, L, MIX), BF16),
                   jax.ShapeDtypeStruct((B, RET_HEADS, RET_HD, RET_HD), F32)),
        grid=(B, L // Lc),
        in_specs=[pl.BlockSpec((None, Lc, 2048), lambda b, c: (b, c, Z_RET // 2048)),
                  pl.BlockSpec((Lc, MIX), lambda b, c: (c, 0)),
                  pl.BlockSpec((Lc, MIX), lambda b, c: (c, 0)),
                  small((Lc, RET_HEADS * Lc)), small((Lc, MIX)), small((MIX, Lc)), small((MIX, MIX)),
                  small((1, MIX))],
        out_specs=(pl.BlockSpec((None, Lc, MIX), lambda b, c: (b, c, 0)),
                   pl.BlockSpec((None, RET_HEADS, RET_HD, RET_HD), lambda b, c: (b, 0, 0, 0))),
        scratch_shapes=[pltpu.VMEM((MIX, MIX), F32)],
        compiler_params=_cparams("parallel", "arbitrary"),
        name="ret_prompt",
    )(z3, cos_l, sin_l, dec, grow, toend, sdec, p["ret_gn"])


def _lru_gates(xc, wg_ref, bg_ref, lam_ref):
    rg = _bdot(xc, wg_ref[...]) + bg_ref[...]
    r = jax.nn.sigmoid(rg[:, 0:MIX])
    i = jax.nn.sigmoid(rg[:, MIX:2 * MIX])
    log_a = -LRU_C * r * jax.nn.softplus(-lam_ref[...])
    a = jnp.exp(log_a)
    bx = jnp.sqrt(_neg_expm1_2x(log_a, a)) * (i * xc)
    return a, bx


def _lru_kernel(z_ref, cw_ref, cb_ref, wg_ref, bg_ref, lam_ref,
                y_ref, st_ref, buf_ref, pad_ref):
    c = pl.program_id(1)
    nc = pl.num_programs(1)
    Lc = z_ref.shape[0]
    zz = z_ref[...].astype(F32)
    gate = zz[:, 0:MIX]
    x = zz[:, MIX:2 * MIX]

    @pl.when(c == 0)
    def _():
        st_ref[...] = jnp.zeros(st_ref.shape, F32)

    xc = _conv_chunk(c, x, pad_ref, cw_ref, cb_ref)
    a, bx = _lru_gates(xc, wg_ref, bg_ref, lam_ref)
    row = lax.broadcasted_iota(jnp.int32, (Lc, MIX), 0)
    s = 1
    while s < Lc:
        keep = row >= s
        bx = jnp.where(keep, bx + a * pltpu.roll(bx, s, 0), bx)
        a = jnp.where(keep, a * pltpu.roll(a, s, 0), a)
        s *= 2
    h = bx + a * st_ref[...]
    st_ref[...] = h[Lc - 1:Lc, :]
    y_ref[...] = (h * jax.nn.gelu(gate)).astype(y_ref.dtype)

    @pl.when(c == nc - 1)
    def _():
        buf_ref[...] = pad_ref[Lc + 8 - (CONV_K - 1):Lc + 8, :]

    _conv_finish(pad_ref, Lc)


def _lru_prompt(z3, p):
    B, L, _ = z3.shape
    Lc = CHUNK
    small = lambda shape: pl.BlockSpec(shape, lambda b, c: tuple(0 for _ in shape))
    return pl.pallas_call(
        _lru_kernel,
        out_shape=(jax.ShapeDtypeStruct((B, L, MIX), BF16),
                   jax.ShapeDtypeStruct((B, 1, MIX), F32),
                   jax.ShapeDtypeStruct((B, CONV_K - 1, MIX), F32)),
        grid=(B, L // Lc),
        in_specs=[pl.BlockSpec((None, Lc, 1024), lambda b, c: (b, c, Z_LRU // 1024)),
                  small((CONV_K, MIX)), small((1, MIX)), small((MIX, 2 * MIX)), small((1, 2 * MIX)), small((1, MIX))],
        out_specs=(pl.BlockSpec((None, Lc, MIX), lambda b, c: (b, c, 0)),
                   pl.BlockSpec((None, 1, MIX), lambda b, c: (b, 0, 0)),
                   pl.BlockSpec((None, CONV_K - 1, MIX), lambda b, c: (b, 0, 0))),
        scratch_shapes=[pltpu.VMEM((Lc + 8, MIX), F32)],
        compiler_params=_cparams("parallel", "arbitrary"),
        name="lru_prompt",
    )(z3, p["lru_cw"], p["lru_cb"], p["lru_wg"], p["lru_bg"], p["lru_lam"])


S5_LB = 4
S5_LAGS = 4


def _s5_project_in(u, wbr_ref, wbi_ref, xr_ref, xi_ref, lags):
    shifted = [u]
    if lags > 1:
        sub = lax.broadcasted_iota(jnp.int32, u.shape, 0) & (lags - 1)
        shifted += [jnp.where(sub >= d, pltpu.roll(u, d, 0), 0.0) for d in range(1, lags)]
    shifted = [s.astype(BF16) for s in shifted]
    for kb in range(S5_LB):
        lhs = jnp.concatenate([s[:, kb * 128:(kb + 1) * 128] for s in shifted], axis=1)
        k = lags * 128
        xr_ref[:, kb * 512:(kb + 1) * 512] = jnp.dot(lhs, wbr_ref[kb, 0:k, :], preferred_element_type=F32)
        xi_ref[:, kb * 512:(kb + 1) * 512] = jnp.dot(lhs, wbi_ref[kb, 0:k, :], preferred_element_type=F32)


def _s5_project_out(xr, xi, u, wcr_ref, wci_ref, d_ref, wglu_ref):
    ys = []
    for kb in range(S5_LB):
        sl = slice(kb * 512, (kb + 1) * 512)
        ys.append(_bdot(xr[:, sl], wcr_ref[kb]) - _bdot(xi[:, sl], wci_ref[kb]))
    y = jnp.concatenate(ys, axis=1) + d_ref[...] * u
    y = jax.nn.gelu(y)
    return y * jax.nn.sigmoid(_bdot(y, wglu_ref[...]))


def _s5_kernel(z_ref, wbr_ref, wbi_ref, wcr_ref, wci_ref, pr_ref, pi_ref, d_ref, wglu_ref,
               y_ref, sr_ref, si_ref, xr_ref, xi_ref):
    c = pl.program_id(1)
    Lc = z_ref.shape[0]
    u = z_ref[...].astype(F32)

    @pl.when(c == 0)
    def _():
        sr_ref[...] = jnp.zeros(sr_ref.shape, F32)
        si_ref[...] = jnp.zeros(si_ref.shape, F32)

    _s5_project_in(u, wbr_ref, wbi_ref, xr_ref, xi_ref, S5_LAGS)
    ng = Lc // SUBLANES
    x3r = xr_ref[...].reshape(ng, SUBLANES, S5_CH)
    x3i = xi_ref[...].reshape(ng, SUBLANES, S5_CH)
    sub = lax.broadcasted_iota(jnp.int32, (ng, SUBLANES, S5_CH), 1)
    s = S5_LAGS
    while s < SUBLANES:
        pr, pi = pr_ref[s - 1:s, :][None], pi_ref[s - 1:s, :][None]
        rr, ri = pltpu.roll(x3r, s, 1), pltpu.roll(x3i, s, 1)
        keep = sub >= s
        x3r, x3i = (jnp.where(keep, x3r + (pr * rr - pi * ri), x3r),
                    jnp.where(keep, x3i + (pr * ri + pi * rr), x3i))
        s *= 2
    pcr, pci = pr_ref[...], pi_ref[...]
    cr, ci = sr_ref[...], si_ref[...]
    for g in range(ng):
        br = jnp.broadcast_to(cr, (SUBLANES, S5_CH))
        bi = jnp.broadcast_to(ci, (SUBLANES, S5_CH))
        gr = x3r[g] + (pcr * br - pci * bi)
        gi = x3i[g] + (pcr * bi + pci * br)
        xr_ref[g * SUBLANES:(g + 1) * SUBLANES, :] = gr
        xi_ref[g * SUBLANES:(g + 1) * SUBLANES, :] = gi
        cr, ci = gr[SUBLANES - 1:SUBLANES, :], gi[SUBLANES - 1:SUBLANES, :]
    sr_ref[...] = cr
    si_ref[...] = ci
    y_ref[...] = _s5_project_out(xr_ref[...], xi_ref[...], u, wcr_ref, wci_ref, d_ref, wglu_ref).astype(y_ref.dtype)


def _s5_prompt(z3, p):
    B, L, _ = z3.shape
    Lc = CHUNK
    small = lambda shape: pl.BlockSpec(shape, lambda b, c: tuple(0 for _ in shape))
    return pl.pallas_call(
        _s5_kernel,
        out_shape=(jax.ShapeDtypeStruct((B, L, MIX), BF16),
                   jax.ShapeDtypeStruct((B, 1, S5_CH), F32),
                   jax.ShapeDtypeStruct((B, 1, S5_CH), F32)),
        grid=(B, L // Lc),
        in_specs=[pl.BlockSpec((None, Lc, MIX), lambda b, c: (b, c, Z_S5 // MIX)),
                  small((S5_LB, S5_LAGS * 128, 512)), small((S5_LB, S5_LAGS * 128, 512)),
                  small((S5_LB, 512, 128)), small((S5_LB, 512, 128)),
                  small((8, S5_CH)), small((8, S5_CH)), small((1, MIX)), small((MIX, MIX))],
        out_specs=(pl.BlockSpec((None, Lc, MIX), lambda b, c: (b, c, 0)),
                   pl.BlockSpec((None, 1, S5_CH), lambda b, c: (b, 0, 0)),
                   pl.BlockSpec((None, 1, S5_CH), lambda b, c: (b, 0, 0))),
        scratch_shapes=[pltpu.VMEM((Lc, S5_CH), F32), pltpu.VMEM((Lc, S5_CH), F32)],
        compiler_params=_cparams("parallel", "arbitrary"),
        name="s5_prompt",
    )(z3, p["s5_wbr"], p["s5_wbi"], p["s5_wcr"], p["s5_wci"], p["s5_pr"], p["s5_pi"], p["s5_d"], p["s5_glu"])


def _merge_kernel(y0_ref, y1_ref, y2_ref, y3_ref, zg_ref, h_ref, wb_ref, wo_ref, o_ref):
    acc = None
    for k, y_ref in enumerate((y0_ref, y1_ref, y2_ref, y3_ref)):
        br = _bdot(y_ref[...], wb_ref[k])
        t = jax.nn.sigmoid(zg_ref[:, k * D_MODEL:(k + 1) * D_MODEL].astype(F32)) * br
        acc = t if acc is None else acc + t
    o_ref[...] = h_ref[...] + _bdot(acc, wo_ref[...])


def _merge(ys, z, h, p, tm):
    T = h.shape[0]
    rows = lambda w: pl.BlockSpec((tm, w), lambda i: (i, 0))
    return pl.pallas_call(
        _merge_kernel,
        out_shape=jax.ShapeDtypeStruct((T, D_MODEL), F32),
        grid=(T // tm,),
        in_specs=[rows(MIX), rows(MIX), rows(MIX), rows(MIX),
                  pl.BlockSpec((tm, 4 * D_MODEL), lambda i: (i, Z_MERGE)),
                  rows(D_MODEL),
                  pl.BlockSpec((4, MIX, D_MODEL), lambda i: (0, 0, 0)),
                  pl.BlockSpec((D_MODEL, D_MODEL), lambda i: (0, 0))],
        out_specs=rows(D_MODEL),
        compiler_params=_cparams("parallel"),
        name="merge",
    )(*ys, z, h, p["w_branch"], p["w_out"])


def _top2_gates(logits):
    lane = lax.broadcasted_iota(jnp.int32, logits.shape, 1).astype(F32)
    big = float(LANES)
    m1 = jnp.max(logits, axis=-1, keepdims=True)
    i1 = jnp.min(jnp.where(logits == m1, lane, big), axis=-1, keepdims=True)
    rest = jnp.where(lane == i1, -jnp.inf, logits)
    m2 = jnp.max(rest, axis=-1, keepdims=True)
    i2 = jnp.min(jnp.where(rest == m2, lane, big), axis=-1, keepdims=True)
    e2 = jnp.exp(m2 - m1)
    den = 1.0 + e2
    return jnp.where(lane == i1, 1.0 / den, 0.0) + jnp.where(lane == i2, e2 / den, 0.0)


def _ffn_kernel(moe, final_norm, *refs):
    if moe:
        h_ref, g_ref, rt_ref, w1_ref, w3_ref, w2_ref, gf_ref, o_ref, hn_ref, acc_ref, gate_ref = refs
    else:
        h_ref, g_ref, w1_ref, w3_ref, w2_ref, gf_ref, o_ref, hn_ref, acc_ref = refs
    e = pl.program_id(1)
    ne = pl.num_programs(1)

    @pl.when(e == 0)
    def _():
        hn = _rmsnorm(h_ref[...], g_ref[...])
        hn_ref[...] = hn.astype(BF16)
        acc_ref[...] = jnp.zeros(acc_ref.shape, F32)
        if moe:
            lane = lax.broadcasted_iota(jnp.int32, (hn.shape[0], LANES), 1)
            logits = jnp.where(lane < N_EXPERTS, _hdot(hn, rt_ref[...]), -jnp.inf)
            gate_ref[...] = _top2_gates(logits)

    hn = hn_ref[...]
    a = jnp.dot(hn, w1_ref[...], preferred_element_type=F32)
    b = jnp.dot(hn, w3_ref[...], preferred_element_type=F32)
    o = _bdot(_silu(a) * b, w2_ref[...])
    if moe:
        lane = lax.broadcasted_iota(jnp.int32, gate_ref.shape, 1)
        ge = jnp.sum(jnp.where(lane == e, gate_ref[...], 0.0), axis=-1, keepdims=True)
        o = ge * o
    acc_ref[...] += o

    @pl.when(e == ne - 1)
    def _():
        out = h_ref[...] + acc_ref[...]
        if final_norm:
            out = _rmsnorm(out, gf_ref[...])
        o_ref[...] = out


def _ffn(h, g, w1, w3, w2, gfinal, tm, router=None, final_norm=False):
    T = h.shape[0]
    moe = router is not None
    tf = D_FF_TILE
    row_spec = pl.BlockSpec((tm, D_MODEL), lambda i, e: (i, 0))
    vec_spec = pl.BlockSpec((1, D_MODEL), lambda i, e: (0, 0))
    if moe:
        ne = w1.shape[0]
        wspecs = [pl.BlockSpec((None, D_MODEL, tf), lambda i, e: (e, 0, 0)),
                  pl.BlockSpec((None, D_MODEL, tf), lambda i, e: (e, 0, 0)),
                  pl.BlockSpec((None, tf, D_MODEL), lambda i, e: (e, 0, 0))]
        in_specs = [row_spec, vec_spec, pl.BlockSpec((D_MODEL, LANES), lambda i, e: (0, 0))] + wspecs + [vec_spec]
        args = (h, g, router, w1, w3, w2, gfinal)
        scratch = [pltpu.VMEM((tm, D_MODEL), BF16), pltpu.VMEM((tm, D_MODEL), F32), pltpu.VMEM((tm, LANES), F32)]
    else:
        ne = w1.shape[1] // tf
        wspecs = [pl.BlockSpec((D_MODEL, tf), lambda i, e: (0, e)),
                  pl.BlockSpec((D_MODEL, tf), lambda i, e: (0, e)),
                  pl.BlockSpec((tf, D_MODEL), lambda i, e: (e, 0))]
        in_specs = [row_spec, vec_spec] + wspecs + [vec_spec]
        args = (h, g, w1, w3, w2, gfinal)
        scratch = [pltpu.VMEM((tm, D_MODEL), BF16), pltpu.VMEM((tm, D_MODEL), F32)]
    return pl.pallas_call(
        functools.partial(_ffn_kernel, moe, final_norm),
        out_shape=jax.ShapeDtypeStruct((T, D_MODEL), F32),
        grid=(T // tm, ne),
        in_specs=in_specs,
        out_specs=row_spec,
        scratch_shapes=scratch,
        compiler_params=_cparams("parallel", "arbitrary"),
        name="moe" if moe else "ffn",
    )(*args)


def _step_pre_kernel(pos_cos_ref, pos_sin_ref, z_ref, sbuf_ref, lbuf_ref, lst_ref, s5r_ref, s5i_ref,
                     scw_ref, scb_ref, dtb_ref, alog_ref,
                     lcw_ref, lcb_ref, wg_ref, bg_ref, lam_ref,
                     wbr_ref, wbi_ref, wcr_ref, wci_ref, pr_ref, pi_ref, s5d_ref, wglu_ref,
                     kqv_ref, dec_ref, sbuf_o, lbuf_o, lst_o, s5r_o, s5i_o, ys5_o, ylru_o, xs_o,
                     xr_ref, xi_ref):
    Bs = z_ref.shape[0]
    zz = z_ref[...].astype(F32)
    xbc = zz[:, Z_SSD + MIX:Z_SSD + MIX + SSD_CONV]
    dt_raw = zz[:, Z_SSD + MIX + SSD_CONV:Z_SSD + MIX + SSD_CONV + LANES]
    W = SSD_CONV
    conv = scb_ref[...] + sbuf_ref[:, 0:W] * scw_ref[0:1, :]
    conv = conv + sbuf_ref[:, W:2 * W] * scw_ref[1:2, :]
    conv = conv + sbuf_ref[:, 2 * W:3 * W] * scw_ref[2:3, :]
    conv = conv + xbc * scw_ref[3:4, :]
    sbuf_o[:, 0:2 * W] = sbuf_ref[:, W:3 * W]
    sbuf_o[:, 2 * W:3 * W] = xbc
    xc = _silu(conv)
    xs = xc[:, 0:MIX]
    xs_o[...] = xs
    nbc = SSD_GROUPS * SSD_STATE
    bm = xc[:, MIX:MIX + nbc]
    cm = xc[:, MIX + nbc:MIX + 2 * nbc]
    dt = jax.nn.softplus(dt_raw + dtb_ref[...])
    a = -jnp.exp(alog_ref[...])
    dec = jnp.exp(dt * a)
    rep = SSD_HEADS // SSD_GROUPS
    for h in range(SSD_HEADS):
        g = h // rep
        kqv_ref[0, 0, h] = bm[:, g * SSD_STATE:(g + 1) * SSD_STATE] * dt[:, h:h + 1]
        kqv_ref[0, 1, h] = cm[:, g * SSD_STATE:(g + 1) * SSD_STATE]
        kqv_ref[0, 2, h] = xs[:, h * SSD_HD:(h + 1) * SSD_HD]
        dec_ref[0, h] = jnp.broadcast_to(dec[:, h:h + 1], (Bs, LANES))
    q = _rotary_lanes(zz[:, Z_RET:Z_RET + MIX], pos_cos_ref[...], pos_sin_ref[...]) * (RET_HD ** -0.5)
    k = _rotary_lanes(zz[:, Z_RET + MIX:Z_RET + 2 * MIX], pos_cos_ref[...], pos_sin_ref[...])
    v = zz[:, Z_RET + 2 * MIX:Z_RET + 3 * MIX]
    gam = _ret_gammas()
    for h in range(RET_HEADS):
        sl = slice(h * RET_HD, (h + 1) * RET_HD)
        kqv_ref[1, 0, h] = k[:, sl]
        kqv_ref[1, 1, h] = q[:, sl]
        kqv_ref[1, 2, h] = v[:, sl]
        dec_ref[1, h] = jnp.full((Bs, LANES), float(gam[h]), F32)
    gate = zz[:, Z_LRU:Z_LRU + MIX]
    lx = zz[:, Z_LRU + MIX:Z_LRU + 2 * MIX]
    W = MIX
    lconv = lcb_ref[...] + lbuf_ref[:, 0:W] * lcw_ref[0:1, :]
    lconv = lconv + lbuf_ref[:, W:2 * W] * lcw_ref[1:2, :]
    lconv = lconv + lbuf_ref[:, 2 * W:3 * W] * lcw_ref[2:3, :]
    lconv = lconv + lx * lcw_ref[3:4, :]
    lbuf_o[:, 0:2 * W] = lbuf_ref[:, W:3 * W]
    lbuf_o[:, 2 * W:3 * W] = lx
    la, lbx = _lru_gates(lconv, wg_ref, bg_ref, lam_ref)
    hl = lbx + la * lst_ref[...]
    lst_o[...] = hl
    ylru_o[...] = hl * jax.nn.gelu(gate)
    u = zz[:, Z_S5:Z_S5 + MIX]
    _s5_project_in(u, wbr_ref, wbi_ref, xr_ref, xi_ref, 1)
    lr, li = pr_ref[0:1, :], pi_ref[0:1, :]
    s0r, s0i = s5r_ref[...], s5i_ref[...]
    xr = xr_ref[...] + (lr * s0r - li * s0i)
    xi = xi_ref[...] + (lr * s0i + li * s0r)
    s5r_o[...] = xr
    s5i_o[...] = xi
    ys5_o[...] = _s5_project_out(xr, xi, u, wcr_ref, wci_ref, s5d_ref, wglu_ref)


def _step_state_kernel(kqv_ref, dec_ref, st_ref, rexp_ref, o_st_ref, y_ref):
    k = kqv_ref[0]
    q = kqv_ref[1]
    v = kqv_ref[2]
    n_e = st_ref.shape[-1]
    reps = n_e // (2 * SSD_HD)
    k_rep = _split3_dot(k, rexp_ref[...])
    q_rep = _split3_dot(q, rexp_ref[...])
    v2 = jnp.concatenate([v, v], axis=1)
    v_rep = jnp.concatenate([v2] * reps, axis=1)
    d = dec_ref[:, 0:1]
    s_new = d * st_ref[...] + k_rep * v_rep
    o_st_ref[...] = s_new
    prod = q_rep * s_new
    acc = prod[:, 0:LANES]
    for j in range(1, reps):
        acc = acc + prod[:, j * LANES:(j + 1) * LANES]
    y_ref[...] = acc[:, 0:SSD_HD] + acc[:, SSD_HD:2 * SSD_HD]


def _step_post_kernel(yssd_ref, yret_ref, xs_ref, z_ref, dlane_ref, ng_ref, gn_ref, yssd_o, yret_o, acc_ref):
    zz_gate = z_ref[:, Z_SSD:Z_SSD + MIX]
    for h in range(SSD_HEADS):
        acc_ref[:, h * SSD_HD:(h + 1) * SSD_HD] = yssd_ref[h]
    y = acc_ref[...] + dlane_ref[...] * xs_ref[...]
    y = y * _silu(zz_gate)
    yssd_o[...] = _rmsnorm(y, ng_ref[...])
    for h in range(RET_HEADS):
        acc_ref[:, h * RET_HD:(h + 1) * RET_HD] = _group_norm_head(yret_ref[h])
    rgate = z_ref[:, Z_RET + 3 * MIX:Z_RET + 4 * MIX]
    yret_o[...] = _silu(rgate) * (acc_ref[...] * gn_ref[...])


def _rexp_matrix():
    r = np.zeros((SSD_STATE, SSD_STATE * SSD_HD), np.float32)
    for n in range(SSD_STATE):
        r[n, n * SSD_HD:(n + 1) * SSD_HD] = 1.0
    return jnp.asarray(r)


def _sample_mixers(z, st_all, big, layer, p, pos):
    Bs = z.shape[0]
    st = {k: v[layer] for k, v in st_all.items() if k not in ("ssd", "ret")}
    H = SSD_HEADS
    NE = SSD_STATE * SSD_HD
    cos_l, sin_l = _rope_tables(np.asarray([pos]))
    rexp = _rexp_matrix()
    sbuf = st["ssd_conv"].reshape(Bs, (CONV_K - 1) * SSD_CONV)
    lbuf = st["lru_conv"].reshape(Bs, (CONV_K - 1) * MIX)
    s5r = st["s5_re"].reshape(Bs, S5_CH)
    s5i = st["s5_im"].reshape(Bs, S5_CH)
    pre_in = (cos_l, sin_l, z, sbuf, lbuf, st["lru"], s5r, s5i,
              p["ssd_cw"], p["ssd_cb"], p["ssd_dtb"], p["ssd_alog"],
              p["lru_cw"], p["lru_cb"], p["lru_wg"], p["lru_bg"], p["lru_lam"],
              p["s5_wbr"], p["s5_wbi"], p["s5_wcr"], p["s5_wci"], p["s5_pr"], p["s5_pi"], p["s5_d"], p["s5_glu"])
    pre_out = (jax.ShapeDtypeStruct((2, 3, H, Bs, SSD_HD), F32),
               jax.ShapeDtypeStruct((2, H, Bs, LANES), F32),
               jax.ShapeDtypeStruct(sbuf.shape, F32), jax.ShapeDtypeStruct(lbuf.shape, F32),
               jax.ShapeDtypeStruct((Bs, MIX), F32),
               jax.ShapeDtypeStruct((Bs, S5_CH), F32), jax.ShapeDtypeStruct((Bs, S5_CH), F32),
               jax.ShapeDtypeStruct((Bs, MIX), F32), jax.ShapeDtypeStruct((Bs, MIX), F32),
               jax.ShapeDtypeStruct((Bs, MIX), F32))
    (kqv, dec, sbuf_n, lbuf_n, lst_n, s5r_n, s5i_n, y_s5, y_lru, xs) = pl.pallas_call(
        _step_pre_kernel,
        out_shape=pre_out,
        scratch_shapes=[pltpu.VMEM((Bs, S5_CH), F32), pltpu.VMEM((Bs, S5_CH), F32)],
        compiler_params=pltpu.CompilerParams(vmem_limit_bytes=VMEM_LIMIT),
        name="step_pre",
    )(*pre_in)

    new_states, yhs = {}, []
    for m, name in enumerate(("ssd", "ret")):
        s_new, yh_m = pl.pallas_call(
            _step_state_kernel,
            out_shape=(jax.ShapeDtypeStruct(big[name].shape, F32), jax.ShapeDtypeStruct((H, Bs, SSD_HD), F32)),
            grid=(H,),
            in_specs=[pl.BlockSpec((None, 3, None, Bs, SSD_HD), lambda h, m=m: (m, 0, h, 0, 0)),
                      pl.BlockSpec((None, None, Bs, LANES), lambda h, m=m: (m, h, 0, 0)),
                      pl.BlockSpec((None, Bs, NE), lambda h: (layer, 0, h)),
                      pl.BlockSpec(rexp.shape, lambda h: (0, 0))],
            out_specs=(pl.BlockSpec((None, Bs, NE), lambda h: (layer, 0, h)),
                       pl.BlockSpec((None, Bs, SSD_HD), lambda h: (h, 0, 0))),
            input_output_aliases={2: 0},
            compiler_params=_cparams("parallel"),
            name="step_state_" + name,
        )(kqv, dec, big[name], rexp)
        new_states[name] = s_new
        yhs.append(yh_m)

    post_in = (yhs[0], yhs[1], xs, z, p["ssd_dlane"], p["ssd_norm"], p["ret_gn"])
    y_ssd, y_ret = pl.pallas_call(
        _step_post_kernel,
        out_shape=(jax.ShapeDtypeStruct((Bs, MIX), F32), jax.ShapeDtypeStruct((Bs, MIX), F32)),
        scratch_shapes=[pltpu.VMEM((Bs, MIX), F32)],
        compiler_params=pltpu.CompilerParams(vmem_limit_bytes=VMEM_LIMIT),
        name="step_post",
    )(*post_in)

    new = dict(
        ssd_conv=sbuf_n.reshape(Bs, CONV_K - 1, SSD_CONV),
        s5_re=s5r_n.reshape(Bs, S5_GROUPS, S5_STATE),
        s5_im=s5i_n.reshape(Bs, S5_GROUPS, S5_STATE),
        lru=lst_n,
        lru_conv=lbuf_n.reshape(Bs, CONV_K - 1, MIX),
    )
    return (y_ssd, y_s5, y_lru, y_ret), new, new_states


def _block_diag(w):
    n, r, c = w.shape
    eye = jnp.eye(n, dtype=w.dtype)
    return (eye[:, None, :, None] * w[:, :, None, :]).reshape(n * r, n * c)


def _pad_lanes(v):
    return jnp.zeros((1, LANES), F32).at[0, :v.shape[0]].set(v.astype(F32))


def _prep_layer(l, W):
    p = {}
    w_in = W["w_in"][l]
    p["w_in"] = jnp.concatenate(
        [w_in[:, 4872:8968], w_in[:, 2824:4872], w_in[:, 1800:2824], w_in[:, 1288:1800], w_in[:, 0:1288],
         jnp.zeros((D_MODEL, Z_WIDTH - 8968), w_in.dtype)], axis=1).astype(BF16)
    p["norm_mix"] = W["norm_mix"][l].reshape(1, D_MODEL)
    p["ssd_cw"] = W["ssd_conv_w"][l]
    p["ssd_cb"] = W["ssd_conv_b"][l].reshape(1, SSD_CONV)
    p["ssd_dtb"] = _pad_lanes(W["ssd_dt_bias"][l])
    p["ssd_alog"] = _pad_lanes(W["ssd_a_log"][l])
    p["ssd_dlane"] = jnp.repeat(W["ssd_d"][l], SSD_HD).reshape(1, MIX)
    p["ssd_norm"] = W["ssd_norm"][l].reshape(1, MIX)
    lam = lax.complex(W["s5_lambda_re"][l], W["s5_lambda_im"][l])
    dt = jnp.exp(W["s5_log_dt"][l])[:, None]
    lam_bar = jnp.exp(lam * dt)
    b_bar = ((lam_bar - 1.0) / lam)[:, :, None] * lax.complex(W["s5_b_re"][l], W["s5_b_im"][l])
    gb = S5_GROUPS // S5_LB

    def embed_in(m):
        return jnp.stack([_block_diag(jnp.swapaxes(m[k * gb:(k + 1) * gb], 1, 2)) for k in range(S5_LB)])

    def embed_out(m):
        return jnp.stack([_block_diag(jnp.swapaxes(m[k * gb:(k + 1) * gb], 1, 2)) for k in range(S5_LB)])

    lagged = [b_bar]
    for _ in range(S5_LAGS - 1):
        lagged.append(lagged[-1] * lam_bar[:, :, None])
    p["s5_wbr"] = jnp.concatenate([embed_in(jnp.real(b)) for b in lagged], axis=1).astype(BF16)
    p["s5_wbi"] = jnp.concatenate([embed_in(jnp.imag(b)) for b in lagged], axis=1).astype(BF16)
    p["s5_wcr"] = embed_out(W["s5_c_re"][l]).astype(BF16)
    p["s5_wci"] = embed_out(W["s5_c_im"][l]).astype(BF16)
    pw = [lam_bar.reshape(1, S5_CH)]
    for _ in range(SUBLANES - 1):
        pw.append(pw[-1] * pw[0])
    pw = jnp.concatenate(pw, axis=0)
    p["s5_pr"] = jnp.real(pw)
    p["s5_pi"] = jnp.imag(pw)
    p["s5_d"] = W["s5_d"][l].reshape(1, MIX)
    p["s5_glu"] = W["s5_glu"][l].astype(BF16)
    p["lru_cw"] = W["lru_conv_w"][l]
    p["lru_cb"] = W["lru_conv_b"][l].reshape(1, MIX)
    p["lru_wg"] = jnp.concatenate([_block_diag(W["lru_wa"][l]), _block_diag(W["lru_wx"][l])], axis=1).astype(BF16)
    p["lru_bg"] = jnp.concatenate([W["lru_ba"][l], W["lru_bx"][l]]).reshape(1, 2 * MIX)
    p["lru_lam"] = W["lru_lambda"][l].reshape(1, MIX)
    p["ret_gn"] = W["ret_gn"][l].reshape(1, MIX)
    p["w_branch"] = W["w_branch"][l].astype(BF16)
    p["w_out"] = W["w_out"][l].astype(BF16)
    p["norm_ffn"] = W["norm_ffn"][l].reshape(1, D_MODEL)
    return p


def _channel_mixer(l, h, p, W, tm, final):
    gfin = W["norm_final"].reshape(1, D_MODEL)
    j = l // 2
    if l % 2 == 0:
        return _ffn(h, p["norm_ffn"], W["ffn_w1_bf"][j], W["ffn_w3_bf"][j], W["ffn_w2_bf"][j], gfin, tm,
                    final_norm=final)
    router = jnp.zeros((D_MODEL, LANES), F32).at[:, :N_EXPERTS].set(W["moe_router"][j])
    return _ffn(h, p["norm_ffn"], W["moe_w1_bf"][j], W["moe_w3_bf"][j], W["moe_w2_bf"][j], gfin, tm,
                router=router, final_norm=final)


def _trunk_prompt(x, W, preps):
    B, L, _ = x.shape
    T = B * L
    tm = min(512, T)
    depth = len(preps)
    h = x.reshape(T, D_MODEL)
    new = {k: [] for k in ("ssd", "ssd_conv", "s5_re", "s5_im", "lru", "lru_conv", "ret")}
    for l, p in enumerate(preps):
        z = _inproj(h, p["norm_mix"], p["w_in"], min(1024, T), BF16)
        z3 = z.reshape(B, L, Z_WIDTH)
        y_ssd, s_ssd, buf_ssd = _ssd_prompt(z3, p)
        y_s5, s5r, s5i = _s5_prompt(z3, p)
        y_lru, s_lru, buf_lru = _lru_prompt(z3, p)
        y_ret, s_ret = _ret_prompt(z3, p)
        ys = tuple(y.reshape(T, MIX) for y in (y_ssd, y_s5, y_lru, y_ret))
        h = _merge(ys, z, h, p, min(256, T))
        h = _channel_mixer(l, h, p, W, tm, final=(l == depth - 1))
        new["ssd"].append(s_ssd)
        new["ssd_conv"].append(buf_ssd)
        new["s5_re"].append(s5r.reshape(B, S5_GROUPS, S5_STATE))
        new["s5_im"].append(s5i.reshape(B, S5_GROUPS, S5_STATE))
        new["lru"].append(s_lru.reshape(B, MIX))
        new["lru_conv"].append(buf_lru)
        new["ret"].append(s_ret)
    return h.reshape(B, L, D_MODEL), {k: jnp.stack(v) for k, v in new.items()}


def _trunk_sample(x, pos, st, W, preps):
    Bs = x.shape[0]
    depth = len(preps)
    h = x.reshape(Bs, D_MODEL)
    new = {k: [] for k in ("ssd_conv", "s5_re", "s5_im", "lru", "lru_conv")}
    flat = SSD_HEADS * SSD_STATE * SSD_HD
    big = dict(ssd=st["ssd"].reshape(depth, Bs, flat), ret=st["ret"].reshape(depth, Bs, flat))
    for l, p in enumerate(preps):
        z = _inproj(h, p["norm_mix"], p["w_in"], Bs, F32)
        ys, nl, big = _sample_mixers(z, st, big, l, p, pos)
        h = _merge(ys, z, h, p, Bs)
        h = _channel_mixer(l, h, p, W, Bs, final=(l == depth - 1))
        for k in new:
            new[k].append(nl[k])
    out = {k: jnp.stack(v) for k, v in new.items()}
    out["ssd"] = big["ssd"].reshape(st["ssd"].shape)
    out["ret"] = big["ret"].reshape(st["ret"].shape)
    return h.reshape(Bs, 1, D_MODEL), out


def kernel(x_prompt, x_sample, state_ssd, state_ssd_conv, state_s5_re, state_s5_im, state_lru, state_lru_conv, state_ret, norm_mix, w_in, ssd_conv_w, ssd_conv_b, ssd_dt_bias, ssd_a_log, ssd_d, ssd_norm, s5_lambda_re, s5_lambda_im, s5_b_re, s5_b_im, s5_c_re, s5_c_im, s5_d, s5_log_dt, s5_glu, lru_conv_w, lru_conv_b, lru_wa, lru_ba, lru_wx, lru_bx, lru_lambda, ret_gn, w_branch, w_out, norm_ffn, ffn_w1, ffn_w3, ffn_w2, moe_router, moe_w1, moe_w3, moe_w2, norm_final):
    W = dict(norm_mix=norm_mix, w_in=w_in, ssd_conv_w=ssd_conv_w, ssd_conv_b=ssd_conv_b, ssd_dt_bias=ssd_dt_bias,
             ssd_a_log=ssd_a_log, ssd_d=ssd_d, ssd_norm=ssd_norm, s5_lambda_re=s5_lambda_re,
             s5_lambda_im=s5_lambda_im, s5_b_re=s5_b_re, s5_b_im=s5_b_im, s5_c_re=s5_c_re, s5_c_im=s5_c_im,
             s5_d=s5_d, s5_log_dt=s5_log_dt, s5_glu=s5_glu, lru_conv_w=lru_conv_w, lru_conv_b=lru_conv_b,
             lru_wa=lru_wa, lru_ba=lru_ba, lru_wx=lru_wx, lru_bx=lru_bx, lru_lambda=lru_lambda, ret_gn=ret_gn,
             w_branch=w_branch, w_out=w_out, norm_ffn=norm_ffn, moe_router=moe_router, norm_final=norm_final)
    W["ffn_w1_bf"] = ffn_w1.astype(BF16)
    W["ffn_w3_bf"] = ffn_w3.astype(BF16)
    W["ffn_w2_bf"] = ffn_w2.astype(BF16)
    W["moe_w1_bf"] = moe_w1.astype(BF16)
    W["moe_w3_bf"] = moe_w3.astype(BF16)
    W["moe_w2_bf"] = moe_w2.astype(BF16)
    depth = w_in.shape[0]
    preps = [_prep_layer(l, W) for l in range(depth)]
    y_p, sp = _trunk_prompt(x_prompt, W, preps)
    st = dict(ssd=state_ssd, ssd_conv=state_ssd_conv, s5_re=state_s5_re, s5_im=state_s5_im,
              lru=state_lru, lru_conv=state_lru_conv, ret=state_ret)
    past_len = 16384
    y_s, ss = _trunk_sample(x_sample, past_len, st, W, preps)
    names = ("ssd", "ssd_conv", "s5_re", "s5_im", "lru", "lru_conv", "ret")
    return (y_p, y_s) + tuple(sp[n] for n in names) + tuple(ss[n] for n in names)
```

```python
import functools
import math

import jax
import jax.numpy as jnp
import numpy as np
from jax import lax
from jax.experimental import pallas as pl
from jax.experimental.pallas import tpu as pltpu

F32 = jnp.float32
BF16 = jnp.bfloat16
EPS = 1e-6

D_MODEL = 1024
MIX = 512
CONV_K = 4
CHUNK = 128
SSD_HEADS = 8
SSD_HD = 64
SSD_STATE = 64
SSD_GROUPS = 2
SSD_CONV = MIX + 2 * SSD_GROUPS * SSD_STATE
S5_GROUPS = 32
S5_GDIM = 16
S5_STATE = 64
S5_CH = S5_GROUPS * S5_STATE
LRU_BLOCKS = 8
LRU_C = 8.0
RET_HEADS = 8
RET_HD = 64
ROPE_BASE = 10000.0
N_EXPERTS = 8
D_FF_TILE = 1408

Z_MERGE = 0
Z_RET = 4096
Z_LRU = 6144
Z_S5 = 7168
Z_SSD = 7680
Z_WIDTH = 9216

VMEM_LIMIT = 56 * 1024 * 1024
LANES = 128
SUBLANES = 8


def _cparams(*sem):
    return pltpu.CompilerParams(dimension_semantics=sem, vmem_limit_bytes=VMEM_LIMIT)


def _bdot(a, b):
    return jnp.dot(a.astype(BF16), b.astype(BF16), preferred_element_type=F32)


def _bdot_nt(a, b):
    return lax.dot_general(a.astype(BF16), b.astype(BF16), (((1,), (1,)), ((), ())), preferred_element_type=F32)


def _bdot_tn(a, b):
    return lax.dot_general(a.astype(BF16), b.astype(BF16), (((0,), (0,)), ((), ())), preferred_element_type=F32)


def _hdot(a, b):
    return jnp.dot(a, b, precision=lax.Precision.HIGHEST, preferred_element_type=F32)


def _split3_dot(x, m01):
    hi = x.astype(BF16)
    r1 = x - hi.astype(F32)
    mid = r1.astype(BF16)
    lo = (r1 - mid.astype(F32)).astype(BF16)
    m = m01.astype(BF16)
    d = functools.partial(jnp.dot, preferred_element_type=F32)
    return (d(lo, m) + d(mid, m)) + d(hi, m)


def _rmsnorm(x, g):
    ms = jnp.mean(x * x, axis=-1, keepdims=True)
    return x * lax.rsqrt(ms + EPS) * g


def _silu(x):
    return x * jax.nn.sigmoid(x)


def _neg_expm1_2x(log_a, a):
    return jnp.tanh(-log_a) * (1.0 + a * a)


def _inproj_kernel(x_ref, g_ref, w_ref, o_ref, hn_ref):
    @pl.when(pl.program_id(1) == 0)
    def _():
        hn_ref[...] = _rmsnorm(x_ref[...], g_ref[...]).astype(BF16)

    o_ref[...] = jnp.dot(hn_ref[...], w_ref[...], preferred_element_type=F32).astype(o_ref.dtype)


def _lspec(a, l):
    rest = tuple(a.shape[1:])
    return pl.BlockSpec((None,) + rest, lambda *_: (l,) + (0,) * len(rest))


def _inproj(x, P, l, tm, out_dtype):
    T = x.shape[0]
    tn = 1536
    return pl.pallas_call(
        _inproj_kernel,
        out_shape=jax.ShapeDtypeStruct((T, Z_WIDTH), out_dtype),
        grid=(T // tm, Z_WIDTH // tn),
        in_specs=[pl.BlockSpec((tm, D_MODEL), lambda i, j: (i, 0)),
                  _lspec(P["norm_mix"], l),
                  pl.BlockSpec((None, D_MODEL, tn), lambda i, j: (l, 0, j))],
        out_specs=pl.BlockSpec((tm, tn), lambda i, j: (i, j)),
        scratch_shapes=[pltpu.VMEM((tm, D_MODEL), BF16)],
        compiler_params=_cparams("parallel", "arbitrary"),
        name="inproj",
    )(x, P["norm_mix"], P["w_in"])


def _conv_chunk(c, x, pad_ref, w_ref, b_ref):
    Lc = x.shape[0]

    @pl.when(c == 0)
    def _():
        pad_ref[0:8, :] = jnp.zeros((8, x.shape[1]), F32)

    pad_ref[8:8 + Lc, :] = x
    out = b_ref[...] + pad_ref[5:5 + Lc, :] * w_ref[0:1, :]
    out = out + pad_ref[6:6 + Lc, :] * w_ref[1:2, :]
    out = out + pad_ref[7:7 + Lc, :] * w_ref[2:3, :]
    out = out + x * w_ref[3:4, :]
    return out


def _conv_finish(pad_ref, Lc):
    pad_ref[0:8, :] = pad_ref[Lc:Lc + 8, :]


def _pow2_div(x, d):
    return lax.shift_right_logical(x, jnp.int32(int(math.log2(d))))


def _head_block_rows(x, nh, hd):
    L = x.shape[0]
    xt = jnp.concatenate([x.astype(BF16)] * nh, axis=0)
    row = lax.broadcasted_iota(jnp.int32, xt.shape, 0)
    col = lax.broadcasted_iota(jnp.int32, xt.shape, 1)
    return jnp.where(_pow2_div(row, L) == _pow2_div(col, hd), xt, jnp.zeros_like(xt))


def _head_block_cols(xt, nh, hd):
    L = xt.shape[1]
    xc = jnp.concatenate([xt.astype(BF16)] * nh, axis=1)
    row = lax.broadcasted_iota(jnp.int32, xc.shape, 0)
    col = lax.broadcasted_iota(jnp.int32, xc.shape, 1)
    return jnp.where(_pow2_div(row, hd) == _pow2_div(col, L), xc, jnp.zeros_like(xc))


def _head_diag_mask(n, hd):
    row = lax.broadcasted_iota(jnp.int32, (n, n), 0)
    col = lax.broadcasted_iota(jnp.int32, (n, n), 1)
    return _pow2_div(row, hd) == _pow2_div(col, hd)


def _group_repeat_lanes(m, rep):
    lane = lax.broadcasted_iota(jnp.int32, m.shape, 1)
    swapped = pltpu.roll(m, m.shape[1] // 2, 1)
    low = lane < m.shape[1] // 2
    g0 = jnp.where(low, m, swapped)
    g1 = jnp.where(low, swapped, m)
    return jnp.concatenate([g0] * (rep // 2) + [g1] * (rep // 2), axis=1)


def _ssd_kernel(z_ref, cw_ref, cb_ref, dtb_ref, alog_ref, dlane_ref, ng_ref, tri_ref, ehj_ref, ehn_ref,
                y_ref, sto_ref, buf_ref, pad_ref, st_ref):
    c = pl.program_id(1)
    nc = pl.num_programs(1)
    Lc = z_ref.shape[0]
    zz = z_ref[...].astype(F32)
    zgate = zz[:, 0:MIX]
    xbc = zz[:, MIX:MIX + SSD_CONV]
    dt_raw = zz[:, MIX + SSD_CONV:MIX + SSD_CONV + LANES]

    @pl.when(c == 0)
    def _():
        st_ref[...] = jnp.zeros(st_ref.shape, F32)

    conv = _conv_chunk(c, xbc, pad_ref, cw_ref, cb_ref)
    xc = _silu(conv)
    xs = xc[:, 0:MIX]
    nbc = SSD_GROUPS * SSD_STATE
    bm = xc[:, MIX:MIX + nbc]
    cm = xc[:, MIX + nbc:MIX + 2 * nbc]
    dt = jax.nn.softplus(dt_raw + dtb_ref[...])
    a = -jnp.exp(alog_ref[...])
    ld = dt * a
    acum = _hdot(tri_ref[...], ld)
    acum_t = acum.T
    dt_t = dt.T
    a_row = jnp.concatenate([acum_t[h:h + 1, :] for h in range(SSD_HEADS)], axis=1)
    dt_row = jnp.concatenate([dt_t[h:h + 1, :] for h in range(SSD_HEADS)], axis=1)
    a_col = _split3_dot(acum, ehj_ref[...])
    row = lax.broadcasted_iota(jnp.int32, (Lc, SSD_HEADS * Lc), 0)
    col = lax.broadcasted_iota(jnp.int32, (Lc, SSD_HEADS * Lc), 1)
    causal = row >= (col & (Lc - 1))
    decay = jnp.where(causal, jnp.exp(jnp.where(causal, a_col - a_row, 0.0)), 0.0)
    rep = SSD_HEADS // SSD_GROUPS
    gmats = []
    for g in range(SSD_GROUPS):
        cg = cm[:, g * SSD_STATE:(g + 1) * SSD_STATE]
        bg = bm[:, g * SSD_STATE:(g + 1) * SSD_STATE]
        gmats.append(_bdot_nt(cg, bg))
    g_all = jnp.concatenate([gmats[h // rep] for h in range(SSD_HEADS)], axis=1)
    m_all = g_all * decay * dt_row
    y = _bdot(m_all, _head_block_rows(xs, SSD_HEADS, SSD_HD))
    exp_a = jnp.exp(acum)
    exp_a_l = _split3_dot(exp_a, ehn_ref[...])
    c_rep = _group_repeat_lanes(cm, rep)
    y = y + _bdot_nt(c_rep * exp_a_l, st_ref[...])
    a_last = acum[Lc - 1:Lc, :]
    w_end_l = _split3_dot(jnp.exp(a_last - acum) * dt, ehn_ref[...])
    upd = _bdot_tn(xs, _group_repeat_lanes(bm, rep) * w_end_l)
    st_ref[...] = exp_a_l[Lc - 1:Lc, :] * st_ref[...] + jnp.where(_head_diag_mask(MIX, SSD_HD), upd, 0.0)
    y = y + dlane_ref[...] * xs
    y = y * _silu(zgate)
    y_ref[...] = _rmsnorm(y, ng_ref[...]).astype(y_ref.dtype)

    @pl.when(c == nc - 1)
    def _():
        buf_ref[...] = pad_ref[Lc + 8 - (CONV_K - 1):Lc + 8, :]
        s_t = st_ref[...].T
        for h in range(SSD_HEADS):
            sl = slice(h * SSD_HD, (h + 1) * SSD_HD)
            sto_ref[h] = s_t[sl, sl]

    _conv_finish(pad_ref, Lc)


def _const_spec(a):
    return pl.BlockSpec(a.shape, lambda *_: (0,) * a.ndim)


def _ssd_prompt(z3, P, l):
    B, L, _ = z3.shape
    Lc = CHUNK
    params = [P[k] for k in ("ssd_cw", "ssd_cb", "ssd_dtb", "ssd_alog", "ssd_dlane", "ssd_norm")]
    consts = [_tri(Lc), _head_expand(SSD_HEADS, Lc), _head_expand(SSD_HEADS, SSD_HD)]
    return pl.pallas_call(
        _ssd_kernel,
        out_shape=(jax.ShapeDtypeStruct((B, L, MIX), BF16),
                   jax.ShapeDtypeStruct((B, SSD_HEADS, SSD_STATE, SSD_HD), F32),
                   jax.ShapeDtypeStruct((B, CONV_K - 1, SSD_CONV), F32)),
        grid=(B, L // Lc),
        in_specs=[pl.BlockSpec((None, Lc, 1536), lambda b, c: (b, c, Z_SSD // 1536))]
                 + [_lspec(a, l) for a in params] + [_const_spec(a) for a in consts],
        out_specs=(pl.BlockSpec((None, Lc, MIX), lambda b, c: (b, c, 0)),
                   pl.BlockSpec((None, SSD_HEADS, SSD_STATE, SSD_HD), lambda b, c: (b, 0, 0, 0)),
                   pl.BlockSpec((None, CONV_K - 1, SSD_CONV), lambda b, c: (b, 0, 0))),
        scratch_shapes=[pltpu.VMEM((Lc + 8, SSD_CONV), F32), pltpu.VMEM((MIX, MIX), F32)],
        compiler_params=_cparams("parallel", "arbitrary"),
        name="ssd_prompt",
    )(z3, *params, *consts)


def _tri(Lc):
    return jnp.asarray(np.tril(np.ones((Lc, Lc), np.float32)))


def _head_expand(nh, width):
    e = np.zeros((LANES, nh * width), np.float32)
    for h in range(nh):
        e[h, h * width:(h + 1) * width] = 1.0
    return jnp.asarray(e, BF16)


def _ret_gammas():
    return 1.0 - np.exp2(-5.0 - np.arange(RET_HEADS, dtype=np.float64))


def _ret_tables(Lc):
    gam = _ret_gammas()
    i = np.arange(Lc)
    d = i[:, None] - i[None, :]
    decay = np.where(d >= 0, gam[:, None, None] ** np.maximum(d, 0)[None], 0.0)
    decay_l = np.transpose(decay, (1, 0, 2)).reshape(Lc, RET_HEADS * Lc)
    grow_l = np.repeat(gam[None, :] ** (i[:, None] + 1), RET_HD, axis=1)
    toend_t = np.repeat(gam[:, None] ** (Lc - 1 - i[None, :]), RET_HD, axis=0)
    hd = np.arange(MIX) // RET_HD
    state_decay = np.where(hd[:, None] == hd[None, :], (gam ** Lc)[hd][:, None], 0.0)
    return tuple(jnp.asarray(t, F32) for t in (decay_l, grow_l, toend_t, state_decay))


def _rope_tables(pos):
    half = RET_HD // 2
    inv = ROPE_BASE ** (-np.arange(half, dtype=np.float64) / half)
    ang = np.asarray(pos, np.float64)[:, None] * inv[None, :]
    cos = np.cos(ang)
    sin = np.sin(ang)
    cos_l = np.tile(np.concatenate([cos, cos], axis=1), (1, RET_HEADS))
    sin_l = np.tile(np.concatenate([-sin, sin], axis=1), (1, RET_HEADS))
    return jnp.asarray(cos_l, F32), jnp.asarray(sin_l, F32)


def _rotary_lanes(x, cos_l, sin_l):
    lane = lax.broadcasted_iota(jnp.int32, x.shape, 1)
    first = (lane & (RET_HD - 1)) < (RET_HD // 2)
    n = x.shape[1]
    swapped = jnp.where(first, pltpu.roll(x, n - RET_HD // 2, 1), pltpu.roll(x, RET_HD // 2, 1))
    return x * cos_l + swapped * sin_l


def _group_norm_head(o):
    mu = jnp.mean(o, axis=-1, keepdims=True)
    d = o - mu
    var = jnp.mean(d * d, axis=-1, keepdims=True)
    return d * lax.rsqrt(var + 1e-5)


def _group_norm_lanes(o, hd):
    cols = []
    for cb in range(o.shape[1] // LANES):
        x = o[:, cb * LANES:(cb + 1) * LANES]
        low = lax.broadcasted_iota(jnp.int32, x.shape, 1) < hd

        def seg_mean(t):
            lo = jnp.sum(jnp.where(low, t, 0.0), axis=1, keepdims=True)
            hi = jnp.sum(jnp.where(low, 0.0, t), axis=1, keepdims=True)
            return jnp.where(low, lo, hi) * (1.0 / hd)

        d = x - seg_mean(x)
        cols.append(d * lax.rsqrt(seg_mean(d * d) + 1e-5))
    return jnp.concatenate(cols, axis=1)


def _ret_kernel(z_ref, cos_ref, sin_ref, dec_ref, grow_ref, toend_ref, sdec_ref, gn_ref,
                y_ref, sto_ref, st_ref):
    c = pl.program_id(1)
    nc = pl.num_programs(1)
    zz = z_ref[...].astype(F32)
    q = _rotary_lanes(zz[:, 0:MIX], cos_ref[...], sin_ref[...]) * (RET_HD ** -0.5)
    k = _rotary_lanes(zz[:, MIX:2 * MIX], cos_ref[...], sin_ref[...])
    v = zz[:, 2 * MIX:3 * MIX]
    gate = zz[:, 3 * MIX:4 * MIX]

    @pl.when(c == 0)
    def _():
        st_ref[...] = jnp.zeros(st_ref.shape, F32)

    k_t = k.T
    g_all = _bdot(q, _head_block_cols(k_t, RET_HEADS, RET_HD))
    o = _bdot(g_all * dec_ref[...], _head_block_rows(v, RET_HEADS, RET_HD))
    o = o + _bdot(q * grow_ref[...], st_ref[...])
    upd = _bdot(k_t * toend_ref[...], v)
    st_ref[...] = sdec_ref[...] * st_ref[...] + jnp.where(_head_diag_mask(MIX, RET_HD), upd, 0.0)
    y_ref[...] = (_silu(gate) * (_group_norm_lanes(o, RET_HD) * gn_ref[...])).astype(y_ref.dtype)

    @pl.when(c == nc - 1)
    def _():
        for h in range(RET_HEADS):
            sl = slice(h * RET_HD, (h + 1) * RET_HD)
            sto_ref[h] = st_ref[sl, sl]


def _ret_prompt(z3, P, l):
    B, L, _ = z3.shape
    Lc = CHUNK
    cos_l, sin_l = _rope_tables(np.arange(L))
    consts = list(_ret_tables(Lc))
    return pl.pallas_call(
        _ret_kernel,
        out_shape=(jax.ShapeDtypeStruct((B, L, MIX), BF16),
                   jax.ShapeDtypeStruct((B, RET_HEADS, RET_HD, RET_HD), F32)),
        grid=(B, L // Lc),
        in_specs=[pl.BlockSpec((None, Lc, 2048), lambda b, c: (b, c, Z_RET // 2048)),
                  pl.BlockSpec((Lc, MIX), lambda b, c: (c, 0)),
                  pl.BlockSpec((Lc, MIX), lambda b, c: (c, 0))]
                 + [_const_spec(a) for a in consts] + [_lspec(P["ret_gn"], l)],
        out_specs=(pl.BlockSpec((None, Lc, MIX), lambda b, c: (b, c, 0)),
                   pl.BlockSpec((None, RET_HEADS, RET_HD, RET_HD), lambda b, c: (b, 0, 0, 0))),
        scratch_shapes=[pltpu.VMEM((MIX, MIX), F32)],
        compiler_params=_cparams("parallel", "arbitrary"),
        name="ret_prompt",
    )(z3, cos_l, sin_l, *consts, P["ret_gn"])


def _lru_gates(xc, wg_ref, bg_ref, lam_ref):
    rg = _bdot(xc, wg_ref[...]) + bg_ref[...]
    r = jax.nn.sigmoid(rg[:, 0:MIX])
    i = jax.nn.sigmoid(rg[:, MIX:2 * MIX])
    log_a = -LRU_C * r * jax.nn.softplus(-lam_ref[...])
    a = jnp.exp(log_a)
    bx = jnp.sqrt(_neg_expm1_2x(log_a, a)) * (i * xc)
    return a, bx


def _lru_kernel(z_ref, cw_ref, cb_ref, wg_ref, bg_ref, lam_ref,
                y_ref, st_ref, buf_ref, pad_ref):
    c = pl.program_id(1)
    nc = pl.num_programs(1)
    Lc = z_ref.shape[0]
    zz = z_ref[...].astype(F32)
    gate = zz[:, 0:MIX]
    x = zz[:, MIX:2 * MIX]

    @pl.when(c == 0)
    def _():
        st_ref[...] = jnp.zeros(st_ref.shape, F32)

    xc = _conv_chunk(c, x, pad_ref, cw_ref, cb_ref)
    a, bx = _lru_gates(xc, wg_ref, bg_ref, lam_ref)
    row = lax.broadcasted_iota(jnp.int32, (Lc, MIX), 0)
    s = 1
    while s < Lc:
        keep = row >= s
        bx = jnp.where(keep, bx + a * pltpu.roll(bx, s, 0), bx)
        a = jnp.where(keep, a * pltpu.roll(a, s, 0), a)
        s *= 2
    h = bx + a * st_ref[...]
    st_ref[...] = h[Lc - 1:Lc, :]
    y_ref[...] = (h * jax.nn.gelu(gate)).astype(y_ref.dtype)

    @pl.when(c == nc - 1)
    def _():
        buf_ref[...] = pad_ref[Lc + 8 - (CONV_K - 1):Lc + 8, :]

    _conv_finish(pad_ref, Lc)


def _lru_prompt(z3, P, l):
    B, L, _ = z3.shape
    Lc = CHUNK
    params = [P[k] for k in ("lru_cw", "lru_cb", "lru_wg", "lru_bg", "lru_lam")]
    return pl.pallas_call(
        _lru_kernel,
        out_shape=(jax.ShapeDtypeStruct((B, L, MIX), BF16),
                   jax.ShapeDtypeStruct((B, 1, MIX), F32),
                   jax.ShapeDtypeStruct((B, CONV_K - 1, MIX), F32)),
        grid=(B, L // Lc),
        in_specs=[pl.BlockSpec((None, Lc, 1024), lambda b, c: (b, c, Z_LRU // 1024))]
                 + [_lspec(a, l) for a in params],
        out_specs=(pl.BlockSpec((None, Lc, MIX), lambda b, c: (b, c, 0)),
                   pl.BlockSpec((None, 1, MIX), lambda b, c: (b, 0, 0)),
                   pl.BlockSpec((None, CONV_K - 1, MIX), lambda b, c: (b, 0, 0))),
        scratch_shapes=[pltpu.VMEM((Lc + 8, MIX), F32)],
        compiler_params=_cparams("parallel", "arbitrary"),
        name="lru_prompt",
    )(z3, *params)


S5_LB = 4
S5_LAGS = 4
assert 2 * S5_LAGS == SUBLANES


def _s5_project_in(u, wbr_ref, wbi_ref, xr_ref, xi_ref, lags):
    shifted = [u]
    if lags > 1:
        sub = lax.broadcasted_iota(jnp.int32, u.shape, 0) & (lags - 1)
        shifted += [jnp.where(sub >= d, pltpu.roll(u, d, 0), 0.0) for d in range(1, lags)]
    shifted = [s.astype(BF16) for s in shifted]
    for kb in range(S5_LB):
        lhs = jnp.concatenate([s[:, kb * 128:(kb + 1) * 128] for s in shifted], axis=1)
        k = lags * 128
        xr_ref[:, kb * 512:(kb + 1) * 512] = jnp.dot(lhs, wbr_ref[kb, 0:k, :], preferred_element_type=F32)
        xi_ref[:, kb * 512:(kb + 1) * 512] = jnp.dot(lhs, wbi_ref[kb, 0:k, :], preferred_element_type=F32)


def _s5_project_out(xr, xi, u, wcr_ref, wci_ref, d_ref, wglu_ref):
    ys = []
    for kb in range(S5_LB):
        sl = slice(kb * 512, (kb + 1) * 512)
        ys.append(_bdot(xr[:, sl], wcr_ref[kb]) - _bdot(xi[:, sl], wci_ref[kb]))
    y = jnp.concatenate(ys, axis=1) + d_ref[...] * u
    y = jax.nn.gelu(y)
    return y * jax.nn.sigmoid(_bdot(y, wglu_ref[...]))


def _s5_kernel(z_ref, wbr_ref, wbi_ref, wcr_ref, wci_ref, pr_ref, pi_ref, d_ref, wglu_ref, qr_ref, qi_ref,
               y_ref, sr_ref, si_ref, xr_ref, xi_ref):
    c = pl.program_id(1)
    Lc = z_ref.shape[0]
    u = z_ref[...].astype(F32)

    @pl.when(c == 0)
    def _():
        sr_ref[...] = jnp.zeros(sr_ref.shape, F32)
        si_ref[...] = jnp.zeros(si_ref.shape, F32)

    _s5_project_in(u, wbr_ref, wbi_ref, xr_ref, xi_ref, S5_LAGS)
    ng = Lc // SUBLANES
    x3r = xr_ref[...].reshape(ng, SUBLANES, S5_CH)
    x3i = xi_ref[...].reshape(ng, SUBLANES, S5_CH)
    tr = jnp.broadcast_to(x3r[:, S5_LAGS - 1:S5_LAGS, :], x3r.shape)
    ti = jnp.broadcast_to(x3i[:, S5_LAGS - 1:S5_LAGS, :], x3i.shape)
    mr, mi = qr_ref[...][None], qi_ref[...][None]
    x3r, x3i = x3r + (mr * tr - mi * ti), x3i + (mr * ti + mi * tr)
    pcr, pci = pr_ref[...], pi_ref[...]
    cr, ci = sr_ref[...], si_ref[...]
    for g in range(ng):
        br = jnp.broadcast_to(cr, (SUBLANES, S5_CH))
        bi = jnp.broadcast_to(ci, (SUBLANES, S5_CH))
        gr = x3r[g] + (pcr * br - pci * bi)
        gi = x3i[g] + (pcr * bi + pci * br)
        xr_ref[g * SUBLANES:(g + 1) * SUBLANES, :] = gr
        xi_ref[g * SUBLANES:(g + 1) * SUBLANES, :] = gi
        cr, ci = gr[SUBLANES - 1:SUBLANES, :], gi[SUBLANES - 1:SUBLANES, :]
    sr_ref[...] = cr
    si_ref[...] = ci
    y_ref[...] = _s5_project_out(xr_ref[...], xi_ref[...], u, wcr_ref, wci_ref, d_ref, wglu_ref).astype(y_ref.dtype)


S5_PARAMS = ("s5_wbr", "s5_wbi", "s5_wcr", "s5_wci", "s5_pr", "s5_pi", "s5_d", "s5_glu", "s5_qr", "s5_qi")


def _s5_prompt(z3, P, l):
    B, L, _ = z3.shape
    Lc = CHUNK
    params = [P[k] for k in S5_PARAMS]
    return pl.pallas_call(
        _s5_kernel,
        out_shape=(jax.ShapeDtypeStruct((B, L, MIX), BF16),
                   jax.ShapeDtypeStruct((B, 1, S5_CH), F32),
                   jax.ShapeDtypeStruct((B, 1, S5_CH), F32)),
        grid=(B, L // Lc),
        in_specs=[pl.BlockSpec((None, Lc, MIX), lambda b, c: (b, c, Z_S5 // MIX))]
                 + [_lspec(a, l) for a in params],
        out_specs=(pl.BlockSpec((None, Lc, MIX), lambda b, c: (b, c, 0)),
                   pl.BlockSpec((None, 1, S5_CH), lambda b, c: (b, 0, 0)),
                   pl.BlockSpec((None, 1, S5_CH), lambda b, c: (b, 0, 0))),
        scratch_shapes=[pltpu.VMEM((Lc, S5_CH), F32), pltpu.VMEM((Lc, S5_CH), F32)],
        compiler_params=_cparams("parallel", "arbitrary"),
        name="s5_prompt",
    )(z3, *params)


def _merge_kernel(y0_ref, y1_ref, y2_ref, y3_ref, zg_ref, h_ref, wb_ref, wo_ref, o_ref):
    acc = None
    for k, y_ref in enumerate((y0_ref, y1_ref, y2_ref, y3_ref)):
        br = _bdot(y_ref[...], wb_ref[k])
        t = jax.nn.sigmoid(zg_ref[:, k * D_MODEL:(k + 1) * D_MODEL].astype(F32)) * br
        acc = t if acc is None else acc + t
    o_ref[...] = h_ref[...] + _bdot(acc, wo_ref[...])


def _merge(ys, z, h, P, l, tm):
    T = h.shape[0]
    rows = lambda w: pl.BlockSpec((tm, w), lambda i: (i, 0))
    return pl.pallas_call(
        _merge_kernel,
        out_shape=jax.ShapeDtypeStruct((T, D_MODEL), F32),
        grid=(T // tm,),
        in_specs=[rows(MIX), rows(MIX), rows(MIX), rows(MIX),
                  pl.BlockSpec((tm, 4 * D_MODEL), lambda i: (i, Z_MERGE)),
                  rows(D_MODEL),
                  _lspec(P["w_branch"], l), _lspec(P["w_out"], l)],
        out_specs=rows(D_MODEL),
        compiler_params=_cparams("parallel"),
        name="merge",
    )(*ys, z, h, P["w_branch"], P["w_out"])


def _top2_gates(logits):
    lane = lax.broadcasted_iota(jnp.int32, logits.shape, 1).astype(F32)
    big = float(LANES)
    m1 = jnp.max(logits, axis=-1, keepdims=True)
    i1 = jnp.min(jnp.where(logits == m1, lane, big), axis=-1, keepdims=True)
    rest = jnp.where(lane == i1, -jnp.inf, logits)
    m2 = jnp.max(rest, axis=-1, keepdims=True)
    i2 = jnp.min(jnp.where(rest == m2, lane, big), axis=-1, keepdims=True)
    e2 = jnp.exp(m2 - m1)
    den = 1.0 + e2
    return jnp.where(lane == i1, 1.0 / den, 0.0) + jnp.where(lane == i2, e2 / den, 0.0)


def _ffn_kernel(moe, final_norm, *refs):
    if moe:
        h_ref, g_ref, rt_ref, w1_ref, w3_ref, w2_ref, gf_ref, o_ref, hn_ref, acc_ref, gate_ref = refs
    else:
        h_ref, g_ref, w1_ref, w3_ref, w2_ref, gf_ref, o_ref, hn_ref, acc_ref = refs
    e = pl.program_id(1)
    ne = pl.num_programs(1)

    @pl.when(e == 0)
    def _():
        hn = _rmsnorm(h_ref[...], g_ref[...])
        hn_ref[...] = hn.astype(BF16)
        acc_ref[...] = jnp.zeros(acc_ref.shape, F32)
        if moe:
            lane = lax.broadcasted_iota(jnp.int32, (hn.shape[0], LANES), 1)
            logits = jnp.where(lane < N_EXPERTS, _hdot(hn, rt_ref[...]), -jnp.inf)
            gate_ref[...] = _top2_gates(logits)

    hn = hn_ref[...]
    a = jnp.dot(hn, w1_ref[...], preferred_element_type=F32)
    b = jnp.dot(hn, w3_ref[...], preferred_element_type=F32)
    o = _bdot(_silu(a) * b, w2_ref[...])
    if moe:
        lane = lax.broadcasted_iota(jnp.int32, gate_ref.shape, 1)
        ge = jnp.sum(jnp.where(lane == e, gate_ref[...], 0.0), axis=-1, keepdims=True)
        o = ge * o
    acc_ref[...] += o

    @pl.when(e == ne - 1)
    def _():
        out = h_ref[...] + acc_ref[...]
        if final_norm:
            out = _rmsnorm(out, gf_ref[...])
        o_ref[...] = out


def _ffn(h, P, l, tm, final_norm):
    T = h.shape[0]
    moe = l % 2 == 1
    j = l // 2
    tf = D_FF_TILE
    row_spec = pl.BlockSpec((tm, D_MODEL), lambda i, e: (i, 0))
    gfinal = P["norm_final"]
    if moe:
        w1, w3, w2 = P["moe_w1"], P["moe_w3"], P["moe_w2"]
        ne = w1.shape[1]
        wspecs = [pl.BlockSpec((None, None, D_MODEL, tf), lambda i, e: (j, e, 0, 0)),
                  pl.BlockSpec((None, None, D_MODEL, tf), lambda i, e: (j, e, 0, 0)),
                  pl.BlockSpec((None, None, tf, D_MODEL), lambda i, e: (j, e, 0, 0))]
        in_specs = ([row_spec, _lspec(P["norm_ffn"], l), _lspec(P["moe_router"], j)] + wspecs
                    + [_const_spec(gfinal)])
        args = (h, P["norm_ffn"], P["moe_router"], w1, w3, w2, gfinal)
        scratch = [pltpu.VMEM((tm, D_MODEL), BF16), pltpu.VMEM((tm, D_MODEL), F32), pltpu.VMEM((tm, LANES), F32)]
    else:
        w1, w3, w2 = P["ffn_w1"], P["ffn_w3"], P["ffn_w2"]
        ne = w1.shape[2] // tf
        wspecs = [pl.BlockSpec((None, D_MODEL, tf), lambda i, e: (j, 0, e)),
                  pl.BlockSpec((None, D_MODEL, tf), lambda i, e: (j, 0, e)),
                  pl.BlockSpec((None, tf, D_MODEL), lambda i, e: (j, e, 0))]
        in_specs = [row_spec, _lspec(P["norm_ffn"], l)] + wspecs + [_const_spec(gfinal)]
        args = (h, P["norm_ffn"], w1, w3, w2, gfinal)
        scratch = [pltpu.VMEM((tm, D_MODEL), BF16), pltpu.VMEM((tm, D_MODEL), F32)]
    return pl.pallas_call(
        functools.partial(_ffn_kernel, moe, final_norm),
        out_shape=jax.ShapeDtypeStruct((T, D_MODEL), F32),
        grid=(T // tm, ne),
        in_specs=in_specs,
        out_specs=row_spec,
        scratch_shapes=scratch,
        compiler_params=_cparams("parallel", "arbitrary"),
        name="moe" if moe else "ffn",
    )(*args)


def _conv_step(x, buf_ref, buf_o, w_ref, b_ref):
    out = b_ref[...]
    for k in range(CONV_K - 1):
        out = out + buf_ref[k] * w_ref[k:k + 1, :]
        if k > 0:
            buf_o[k - 1] = buf_ref[k]
    out = out + x * w_ref[CONV_K - 1:CONV_K, :]
    buf_o[CONV_K - 2] = x
    return out


def _step_pre_kernel(pos_cos_ref, pos_sin_ref, gam_ref, ehn_ref, z_ref, sbuf_ref, lbuf_ref, lst_ref, s5r_ref, s5i_ref,
                     scw_ref, scb_ref, dtb_ref, alog_ref,
                     lcw_ref, lcb_ref, wg_ref, bg_ref, lam_ref,
                     wbr_ref, wbi_ref, wcr_ref, wci_ref, pr_ref, pi_ref, s5d_ref, wglu_ref, qr_ref, qi_ref,
                     kqv_ref, dec_ref, sbuf_o, lbuf_o, lst_o, s5r_o, s5i_o, ys5_o, ylru_o, xs_o,
                     xr_ref, xi_ref):
    zz = z_ref[...].astype(F32)
    xbc = zz[:, Z_SSD + MIX:Z_SSD + MIX + SSD_CONV]
    dt_raw = zz[:, Z_SSD + MIX + SSD_CONV:Z_SSD + MIX + SSD_CONV + LANES]
    xc = _silu(_conv_step(xbc, sbuf_ref, sbuf_o, scw_ref, scb_ref))
    xs = xc[:, 0:MIX]
    xs_o[...] = xs
    nbc = SSD_GROUPS * SSD_STATE
    bm = xc[:, MIX:MIX + nbc]
    cm = xc[:, MIX + nbc:MIX + 2 * nbc]
    dt = jax.nn.softplus(dt_raw + dtb_ref[...])
    a = -jnp.exp(alog_ref[...])
    rep = SSD_HEADS // SSD_GROUPS
    kqv_ref[0, 0] = (_group_repeat_lanes(bm, rep) * _split3_dot(dt, ehn_ref[...])).T
    kqv_ref[0, 1] = _group_repeat_lanes(cm, rep).T
    kqv_ref[0, 2] = xs.T
    dec_ref[0] = jnp.exp(dt * a).T[0:SSD_HEADS, :]
    q = _rotary_lanes(zz[:, Z_RET:Z_RET + MIX], pos_cos_ref[...], pos_sin_ref[...]) * (RET_HD ** -0.5)
    k = _rotary_lanes(zz[:, Z_RET + MIX:Z_RET + 2 * MIX], pos_cos_ref[...], pos_sin_ref[...])
    kqv_ref[1, 0] = k.T
    kqv_ref[1, 1] = q.T
    kqv_ref[1, 2] = zz[:, Z_RET + 2 * MIX:Z_RET + 3 * MIX].T
    dec_ref[1] = gam_ref[...]
    gate = zz[:, Z_LRU:Z_LRU + MIX]
    lx = zz[:, Z_LRU + MIX:Z_LRU + 2 * MIX]
    lconv = _conv_step(lx, lbuf_ref, lbuf_o, lcw_ref, lcb_ref)
    la, lbx = _lru_gates(lconv, wg_ref, bg_ref, lam_ref)
    hl = lbx + la * lst_ref[...]
    lst_o[...] = hl
    ylru_o[...] = hl * jax.nn.gelu(gate)
    u = zz[:, Z_S5:Z_S5 + MIX]
    _s5_project_in(u, wbr_ref, wbi_ref, xr_ref, xi_ref, 1)
    lr, li = pr_ref[0:1, :], pi_ref[0:1, :]
    s0r, s0i = s5r_ref[...], s5i_ref[...]
    xr = xr_ref[...] + (lr * s0r - li * s0i)
    xi = xi_ref[...] + (lr * s0i + li * s0r)
    s5r_o[...] = xr
    s5i_o[...] = xi
    ys5_o[...] = _s5_project_out(xr, xi, u, wcr_ref, wci_ref, s5d_ref, wglu_ref)


def _step_state_kernel(kqv_ref, dec_ref, st_ref, o_st_ref, y_ref):
    h = pl.program_id(0)
    d = dec_ref[pl.ds(h, 1), :]
    v = kqv_ref[2]
    acc = jnp.zeros(v.shape, F32)
    for n in range(st_ref.shape[0]):
        s_new = d * st_ref[n] + kqv_ref[0, n:n + 1, :] * v
        o_st_ref[n] = s_new
        acc = acc + kqv_ref[1, n:n + 1, :] * s_new
    y_ref[...] = acc


def _step_post_kernel(yssd_ref, yret_ref, xs_ref, z_ref, dlane_ref, ng_ref, gn_ref, yssd_o, yret_o):
    zz_gate = z_ref[:, Z_SSD:Z_SSD + MIX]
    y = yssd_ref[...].T + dlane_ref[...] * xs_ref[...]
    y = y * _silu(zz_gate)
    yssd_o[...] = _rmsnorm(y, ng_ref[...])
    rgate = z_ref[:, Z_RET + 3 * MIX:Z_RET + 4 * MIX]
    yret_o[...] = _silu(rgate) * (_group_norm_lanes(yret_ref[...].T, RET_HD) * gn_ref[...])


def _sample_mixers(z, views, big, layer, P, pos):
    Bs = z.shape[0]
    H = SSD_HEADS
    cos_l, sin_l = _rope_tables(np.asarray([pos]))
    gam = jnp.asarray(np.repeat(_ret_gammas()[:, None], Bs, axis=1), F32)
    consts = [cos_l, sin_l, gam, _head_expand(SSD_HEADS, SSD_HD)]
    states = [views[k] for k in ("ssd_conv", "lru_conv", "lru", "s5_re", "s5_im")]
    params = [P[k] for k in ("ssd_cw", "ssd_cb", "ssd_dtb", "ssd_alog",
                             "lru_cw", "lru_cb", "lru_wg", "lru_bg", "lru_lam") + S5_PARAMS]
    pre_out = (jax.ShapeDtypeStruct((2, 3, MIX, Bs), F32),
               jax.ShapeDtypeStruct((2, H, Bs), F32),
               jax.ShapeDtypeStruct(views["ssd_conv"].shape[1:], F32),
               jax.ShapeDtypeStruct(views["lru_conv"].shape[1:], F32),
               jax.ShapeDtypeStruct((Bs, MIX), F32),
               jax.ShapeDtypeStruct((Bs, S5_CH), F32), jax.ShapeDtypeStruct((Bs, S5_CH), F32),
               jax.ShapeDtypeStruct((Bs, MIX), F32), jax.ShapeDtypeStruct((Bs, MIX), F32),
               jax.ShapeDtypeStruct((Bs, MIX), F32))
    (kqv, dec, sbuf_n, lbuf_n, lst_n, s5r_n, s5i_n, y_s5, y_lru, xs) = pl.pallas_call(
        _step_pre_kernel,
        out_shape=pre_out,
        grid=(1,),
        in_specs=[_const_spec(a) for a in consts] + [_const_spec(z)]
                 + [_lspec(a, layer) for a in states] + [_lspec(a, layer) for a in params],
        out_specs=tuple(pl.BlockSpec(o.shape, lambda i, n=len(o.shape): (0,) * n) for o in pre_out),
        scratch_shapes=[pltpu.VMEM((Bs, S5_CH), F32), pltpu.VMEM((Bs, S5_CH), F32)],
        compiler_params=_cparams("arbitrary"),
        name="step_pre",
    )(*consts, z, *states, *params)

    new_big, yts = {}, []
    for m, name in enumerate(("ssd", "ret")):
        s_new, y_t = pl.pallas_call(
            _step_state_kernel,
            out_shape=(jax.ShapeDtypeStruct(big[name].shape, F32), jax.ShapeDtypeStruct((MIX, Bs), F32)),
            grid=(H,),
            in_specs=[pl.BlockSpec((None, 3, SSD_STATE, Bs), lambda h, m=m: (m, 0, h, 0)),
                      pl.BlockSpec((None, H, Bs), lambda h, m=m: (m, 0, 0)),
                      pl.BlockSpec((None, None, SSD_STATE, SSD_HD, Bs), lambda h: (layer, h, 0, 0, 0))],
            out_specs=(pl.BlockSpec((None, None, SSD_STATE, SSD_HD, Bs), lambda h: (layer, h, 0, 0, 0)),
                       pl.BlockSpec((SSD_HD, Bs), lambda h: (h, 0))),
            input_output_aliases={2: 0},
            compiler_params=_cparams("parallel"),
            name="step_state_" + name,
        )(kqv, dec, big[name])
        new_big[name] = s_new
        yts.append(y_t)

    post_in = [yts[0], yts[1], xs, z]
    post_par = [P[k] for k in ("ssd_dlane", "ssd_norm", "ret_gn")]
    post_out = (jax.ShapeDtypeStruct((Bs, MIX), F32), jax.ShapeDtypeStruct((Bs, MIX), F32))
    y_ssd, y_ret = pl.pallas_call(
        _step_post_kernel,
        out_shape=post_out,
        grid=(1,),
        in_specs=[_const_spec(a) for a in post_in] + [_lspec(a, layer) for a in post_par],
        out_specs=tuple(pl.BlockSpec(o.shape, lambda i: (0, 0)) for o in post_out),
        compiler_params=_cparams("arbitrary"),
        name="step_post",
    )(*post_in, *post_par)

    new = dict(ssd_conv=sbuf_n, lru_conv=lbuf_n, lru=lst_n, s5_re=s5r_n, s5_im=s5i_n)
    return (y_ssd, y_s5, y_lru, y_ret), new, new_big


def _block_diag8(w):
    lead = w.shape[:-3]
    n, r, c = w.shape[-3:]
    eye = jnp.eye(n, dtype=w.dtype)
    out = w[..., :, :, None, :] * eye[:, None, :, None]
    return out.reshape(lead + (n * r, n * c))


def _row(v):
    return v[:, None, :]


def _cmul(ar, ai, br, bi):
    return ar * br - ai * bi, ar * bi + ai * br


def _prep_params(W):
    depth = W["w_in"].shape[0]
    P = {}
    w_in = W["w_in"]
    P["w_in"] = jnp.concatenate(
        [w_in[:, :, 4872:8968], w_in[:, :, 2824:4872], w_in[:, :, 1800:2824], w_in[:, :, 1288:1800],
         w_in[:, :, 0:1288], jnp.zeros((depth, D_MODEL, Z_WIDTH - 8968), w_in.dtype)], axis=2).astype(BF16)
    P["norm_mix"] = _row(W["norm_mix"])
    pad = ((0, 0), (0, LANES - SSD_HEADS))
    P["ssd_cw"] = W["ssd_conv_w"]
    P["ssd_cb"] = _row(W["ssd_conv_b"])
    P["ssd_dtb"] = _row(jnp.pad(W["ssd_dt_bias"], pad))
    P["ssd_alog"] = _row(jnp.pad(W["ssd_a_log"], pad))
    P["ssd_dlane"] = _row(jnp.repeat(W["ssd_d"], SSD_HD, axis=1))
    P["ssd_norm"] = _row(W["ssd_norm"])
    lr, li = W["s5_lambda_re"], W["s5_lambda_im"]
    dt = jnp.exp(W["s5_log_dt"])[:, :, None]
    mag = jnp.exp(lr * dt)
    br, bi = mag * jnp.cos(li * dt), mag * jnp.sin(li * dt)
    den = lr * lr + li * li
    qr, qi = _cmul(br - 1.0, bi, lr / den, -li / den)
    wr, wi = _cmul(qr[..., None], qi[..., None], W["s5_b_re"], W["s5_b_im"])
    lag_r, lag_i = [wr], [wi]
    for _ in range(S5_LAGS - 1):
        nr, ni = _cmul(lag_r[-1], lag_i[-1], br[..., None], bi[..., None])
        lag_r.append(nr)
        lag_i.append(ni)
    gb = S5_GROUPS // S5_LB

    def embed_in(lags):
        m = jnp.stack(lags, axis=1).reshape(depth, len(lags), S5_LB, gb, S5_STATE, S5_GDIM)
        m = _block_diag8(jnp.swapaxes(m, -1, -2))
        return jnp.swapaxes(m, 1, 2).reshape(depth, S5_LB, len(lags) * gb * S5_GDIM, gb * S5_STATE)

    def embed_out(m):
        m = m.reshape(depth, S5_LB, gb, S5_GDIM, S5_STATE)
        return _block_diag8(jnp.swapaxes(m, -1, -2))

    P["s5_wbr"] = embed_in(lag_r).astype(BF16)
    P["s5_wbi"] = embed_in(lag_i).astype(BF16)
    P["s5_wcr"] = embed_out(W["s5_c_re"]).astype(BF16)
    P["s5_wci"] = embed_out(W["s5_c_im"]).astype(BF16)
    pr, pi = [br.reshape(depth, 1, S5_CH)], [bi.reshape(depth, 1, S5_CH)]
    for _ in range(SUBLANES - 1):
        nr, ni = _cmul(pr[-1], pi[-1], pr[0], pi[0])
        pr.append(nr)
        pi.append(ni)
    P["s5_pr"] = jnp.concatenate(pr, axis=1)
    P["s5_pi"] = jnp.concatenate(pi, axis=1)
    half = jnp.zeros((depth, S5_LAGS, S5_CH), F32)
    P["s5_qr"] = jnp.concatenate([half, P["s5_pr"][:, :SUBLANES - S5_LAGS]], axis=1)
    P["s5_qi"] = jnp.concatenate([half, P["s5_pi"][:, :SUBLANES - S5_LAGS]], axis=1)
    P["s5_d"] = W["s5_d"].reshape(depth, 1, MIX)
    P["s5_glu"] = W["s5_glu"].astype(BF16)
    P["lru_cw"] = W["lru_conv_w"]
    P["lru_cb"] = _row(W["lru_conv_b"])
    P["lru_wg"] = jnp.concatenate([_block_diag8(W["lru_wa"]), _block_diag8(W["lru_wx"])], axis=2).astype(BF16)
    P["lru_bg"] = _row(jnp.concatenate([W["lru_ba"], W["lru_bx"]], axis=1))
    P["lru_lam"] = _row(W["lru_lambda"])
    P["ret_gn"] = _row(W["ret_gn"])
    P["w_branch"] = W["w_branch"].astype(BF16)
    P["w_out"] = W["w_out"].astype(BF16)
    P["norm_ffn"] = _row(W["norm_ffn"])
    P["norm_final"] = W["norm_final"].reshape(1, D_MODEL)
    for k in ("ffn_w1", "ffn_w3", "ffn_w2", "moe_w1", "moe_w3", "moe_w2"):
        P[k] = W[k].astype(BF16)
    P["moe_router"] = jnp.pad(W["moe_router"], ((0, 0), (0, 0), (0, LANES - N_EXPERTS)))
    return P


def _trunk_prompt(x, P):
    B, L, _ = x.shape
    T = B * L
    depth = P["w_in"].shape[0]
    h = x.reshape(T, D_MODEL)
    new = {k: [] for k in ("ssd", "ssd_conv", "s5_re", "s5_im", "lru", "lru_conv", "ret")}
    for l in range(depth):
        z = _inproj(h, P, l, min(1024, T), BF16)
        z3 = z.reshape(B, L, Z_WIDTH)
        y_ssd, s_ssd, buf_ssd = _ssd_prompt(z3, P, l)
        y_s5, s5r, s5i = _s5_prompt(z3, P, l)
        y_lru, s_lru, buf_lru = _lru_prompt(z3, P, l)
        y_ret, s_ret = _ret_prompt(z3, P, l)
        ys = tuple(y.reshape(T, MIX) for y in (y_ssd, y_s5, y_lru, y_ret))
        h = _merge(ys, z, h, P, l, min(256, T))
        h = _ffn(h, P, l, min(512, T), final_norm=(l == depth - 1))
        new["ssd"].append(s_ssd)
        new["ssd_conv"].append(buf_ssd)
        new["s5_re"].append(s5r.reshape(B, S5_GROUPS, S5_STATE))
        new["s5_im"].append(s5i.reshape(B, S5_GROUPS, S5_STATE))
        new["lru"].append(s_lru.reshape(B, MIX))
        new["lru_conv"].append(buf_lru)
        new["ret"].append(s_ret)
    return h.reshape(B, L, D_MODEL), {k: jnp.stack(v) for k, v in new.items()}


def _trunk_sample(x, pos, st, P):
    Bs = x.shape[0]
    depth = P["w_in"].shape[0]
    h = x.reshape(Bs, D_MODEL)
    big = dict(ssd=jnp.transpose(st["ssd"], (0, 2, 3, 4, 1)), ret=jnp.transpose(st["ret"], (0, 2, 3, 4, 1)))
    views = dict(ssd_conv=jnp.transpose(st["ssd_conv"], (0, 2, 1, 3)),
                 lru_conv=jnp.transpose(st["lru_conv"], (0, 2, 1, 3)),
                 lru=st["lru"],
                 s5_re=st["s5_re"].reshape(depth, Bs, S5_CH),
                 s5_im=st["s5_im"].reshape(depth, Bs, S5_CH))
    new = {k: [] for k in views}
    for l in range(depth):
        z = _inproj(h, P, l, Bs, F32)
        ys, nl, big = _sample_mixers(z, views, big, l, P, pos)
        h = _merge(ys, z, h, P, l, Bs)
        h = _ffn(h, P, l, Bs, final_norm=(l == depth - 1))
        for k in new:
            new[k].append(nl[k])
    out = {k: jnp.stack(v) for k, v in new.items()}
    out["ssd_conv"] = jnp.transpose(out["ssd_conv"], (0, 2, 1, 3))
    out["lru_conv"] = jnp.transpose(out["lru_conv"], (0, 2, 1, 3))
    out["s5_re"] = out["s5_re"].reshape(st["s5_re"].shape)
    out["s5_im"] = out["s5_im"].reshape(st["s5_im"].shape)
    out["ssd"] = jnp.transpose(big["ssd"], (0, 4, 1, 2, 3))
    out["ret"] = jnp.transpose(big["ret"], (0, 4, 1, 2, 3))
    return h.reshape(Bs, 1, D_MODEL), out


def kernel(x_prompt, x_sample, state_ssd, state_ssd_conv, state_s5_re, state_s5_im, state_lru, state_lru_conv, state_ret, norm_mix, w_in, ssd_conv_w, ssd_conv_b, ssd_dt_bias, ssd_a_log, ssd_d, ssd_norm, s5_lambda_re, s5_lambda_im, s5_b_re, s5_b_im, s5_c_re, s5_c_im, s5_d, s5_log_dt, s5_glu, lru_conv_w, lru_conv_b, lru_wa, lru_ba, lru_wx, lru_bx, lru_lambda, ret_gn, w_branch, w_out, norm_ffn, ffn_w1, ffn_w3, ffn_w2, moe_router, moe_w1, moe_w3, moe_w2, norm_final):
    W = dict(norm_mix=norm_mix, w_in=w_in, ssd_conv_w=ssd_conv_w, ssd_conv_b=ssd_conv_b, ssd_dt_bias=ssd_dt_bias,
             ssd_a_log=ssd_a_log, ssd_d=ssd_d, ssd_norm=ssd_norm, s5_lambda_re=s5_lambda_re,
             s5_lambda_im=s5_lambda_im, s5_b_re=s5_b_re, s5_b_im=s5_b_im, s5_c_re=s5_c_re, s5_c_im=s5_c_im,
             s5_d=s5_d, s5_log_dt=s5_log_dt, s5_glu=s5_glu, lru_conv_w=lru_conv_w, lru_conv_b=lru_conv_b,
             lru_wa=lru_wa, lru_ba=lru_ba, lru_wx=lru_wx, lru_bx=lru_bx, lru_lambda=lru_lambda, ret_gn=ret_gn,
             w_branch=w_branch, w_out=w_out, norm_ffn=norm_ffn, moe_router=moe_router, norm_final=norm_final,
             ffn_w1=ffn_w1, ffn_w3=ffn_w3, ffn_w2=ffn_w2, moe_w1=moe_w1, moe_w3=moe_w3, moe_w2=moe_w2)
    P = _prep_params(W)
    y_p, sp = _trunk_prompt(x_prompt, P)
    st = dict(ssd=state_ssd, ssd_conv=state_ssd_conv, s5_re=state_s5_re, s5_im=state_s5_im,
              lru=state_lru, lru_conv=state_lru_conv, ret=state_ret)
    past_len = 16384
    y_s, ss = _trunk_sample(x_sample, past_len, st, P)
    names = ("ssd", "ssd_conv", "s5_re", "s5_im", "lru", "lru_conv", "ret")
    return (y_p, y_s) + tuple(sp[n] for n in names) + tuple(ss[n] for n in names)
```

```python
import functools
import math

import jax
import jax.numpy as jnp
import numpy as np
from jax import lax
from jax.experimental import pallas as pl
from jax.experimental.pallas import tpu as pltpu

F32 = jnp.float32
BF16 = jnp.bfloat16
EPS = 1e-6

D_MODEL = 1024
MIX = 512
CONV_K = 4
CHUNK = 128
SSD_HEADS = 8
SSD_HD = 64
SSD_STATE = 64
SSD_GROUPS = 2
SSD_CONV = MIX + 2 * SSD_GROUPS * SSD_STATE
S5_GROUPS = 32
S5_GDIM = 16
S5_STATE = 64
S5_CH = S5_GROUPS * S5_STATE
LRU_BLOCKS = 8
LRU_C = 8.0
RET_HEADS = 8
RET_HD = 64
ROPE_BASE = 10000.0
N_EXPERTS = 8
D_FF_TILE = 1408

Z_MERGE = 0
Z_RET = 4096
Z_LRU = 6144
Z_S5 = 7168
Z_SSD = 7680
Z_WIDTH = 9216

VMEM_LIMIT = 56 * 1024 * 1024
LANES = 128
SUBLANES = 8


def _cparams(*sem):
    return pltpu.CompilerParams(dimension_semantics=sem, vmem_limit_bytes=VMEM_LIMIT)


def _bdot(a, b):
    return jnp.dot(a.astype(BF16), b.astype(BF16), preferred_element_type=F32)


def _bdot_nt(a, b):
    return lax.dot_general(a.astype(BF16), b.astype(BF16), (((1,), (1,)), ((), ())), preferred_element_type=F32)


def _bdot_tn(a, b):
    return lax.dot_general(a.astype(BF16), b.astype(BF16), (((0,), (0,)), ((), ())), preferred_element_type=F32)


def _hdot(a, b):
    return jnp.dot(a, b, precision=lax.Precision.HIGHEST, preferred_element_type=F32)


def _split3_dot(x, m01):
    hi = x.astype(BF16)
    r1 = x - hi.astype(F32)
    mid = r1.astype(BF16)
    lo = (r1 - mid.astype(F32)).astype(BF16)
    m = m01.astype(BF16)
    d = functools.partial(jnp.dot, preferred_element_type=F32)
    return (d(lo, m) + d(mid, m)) + d(hi, m)


def _rmsnorm(x, g):
    ms = jnp.mean(x * x, axis=-1, keepdims=True)
    return x * lax.rsqrt(ms + EPS) * g


def _silu(x):
    return x * jax.nn.sigmoid(x)


def _neg_expm1_2x(log_a, a):
    return jnp.tanh(-log_a) * (1.0 + a * a)


def _inproj_kernel(x_ref, g_ref, w_ref, o_ref, hn_ref):
    @pl.when(pl.program_id(1) == 0)
    def _():
        hn_ref[...] = _rmsnorm(x_ref[...], g_ref[...]).astype(BF16)

    o_ref[...] = jnp.dot(hn_ref[...], w_ref[...], preferred_element_type=F32).astype(o_ref.dtype)


def _lspec(a, l):
    rest = tuple(a.shape[1:])
    return pl.BlockSpec((None,) + rest, lambda *_: (l,) + (0,) * len(rest))


def _inproj(x, P, l, tm, out_dtype):
    T = x.shape[0]
    tn = 1536
    return pl.pallas_call(
        _inproj_kernel,
        out_shape=jax.ShapeDtypeStruct((T, Z_WIDTH), out_dtype),
        grid=(T // tm, Z_WIDTH // tn),
        in_specs=[pl.BlockSpec((tm, D_MODEL), lambda i, j: (i, 0)),
                  _lspec(P["norm_mix"], l),
                  pl.BlockSpec((None, D_MODEL, tn), lambda i, j: (l, 0, j))],
        out_specs=pl.BlockSpec((tm, tn), lambda i, j: (i, j)),
        scratch_shapes=[pltpu.VMEM((tm, D_MODEL), BF16)],
        compiler_params=_cparams("parallel", "arbitrary"),
        name="inproj",
    )(x, P["norm_mix"], P["w_in"])


def _conv_chunk(c, x, pad_ref, w_ref, b_ref):
    Lc = x.shape[0]

    @pl.when(c == 0)
    def _():
        pad_ref[0:8, :] = jnp.zeros((8, x.shape[1]), F32)

    pad_ref[8:8 + Lc, :] = x
    out = b_ref[...] + pad_ref[5:5 + Lc, :] * w_ref[0:1, :]
    out = out + pad_ref[6:6 + Lc, :] * w_ref[1:2, :]
    out = out + pad_ref[7:7 + Lc, :] * w_ref[2:3, :]
    out = out + x * w_ref[3:4, :]
    return out


def _conv_finish(pad_ref, Lc):
    pad_ref[0:8, :] = pad_ref[Lc:Lc + 8, :]


def _pow2_div(x, d):
    return lax.shift_right_logical(x, jnp.int32(int(math.log2(d))))


def _head_block_rows(x, nh, hd):
    L = x.shape[0]
    xt = jnp.concatenate([x.astype(BF16)] * nh, axis=0)
    row = lax.broadcasted_iota(jnp.int32, xt.shape, 0)
    col = lax.broadcasted_iota(jnp.int32, xt.shape, 1)
    return jnp.where(_pow2_div(row, L) == _pow2_div(col, hd), xt, jnp.zeros_like(xt))


def _head_block_cols(xt, nh, hd):
    L = xt.shape[1]
    xc = jnp.concatenate([xt.astype(BF16)] * nh, axis=1)
    row = lax.broadcasted_iota(jnp.int32, xc.shape, 0)
    col = lax.broadcasted_iota(jnp.int32, xc.shape, 1)
    return jnp.where(_pow2_div(row, hd) == _pow2_div(col, L), xc, jnp.zeros_like(xc))


def _head_diag_mask(n, hd):
    row = lax.broadcasted_iota(jnp.int32, (n, n), 0)
    col = lax.broadcasted_iota(jnp.int32, (n, n), 1)
    return _pow2_div(row, hd) == _pow2_div(col, hd)


def _group_repeat_lanes(m, rep):
    lane = lax.broadcasted_iota(jnp.int32, m.shape, 1)
    swapped = pltpu.roll(m, m.shape[1] // 2, 1)
    low = lane < m.shape[1] // 2
    g0 = jnp.where(low, m, swapped)
    g1 = jnp.where(low, swapped, m)
    return jnp.concatenate([g0] * (rep // 2) + [g1] * (rep // 2), axis=1)


def _ssd_kernel(z_ref, cw_ref, cb_ref, dtb_ref, alog_ref, dlane_ref, ng_ref, tri_ref, ehj_ref, ehn_ref,
                y_ref, sto_ref, buf_ref, pad_ref, st_ref):
    c = pl.program_id(1)
    nc = pl.num_programs(1)
    Lc = z_ref.shape[0]
    zz = z_ref[...].astype(F32)
    zgate = zz[:, 0:MIX]
    xbc = zz[:, MIX:MIX + SSD_CONV]
    dt_raw = zz[:, MIX + SSD_CONV:MIX + SSD_CONV + LANES]

    @pl.when(c == 0)
    def _():
        st_ref[...] = jnp.zeros(st_ref.shape, F32)

    conv = _conv_chunk(c, xbc, pad_ref, cw_ref, cb_ref)
    xc = _silu(conv)
    xs = xc[:, 0:MIX]
    nbc = SSD_GROUPS * SSD_STATE
    bm = xc[:, MIX:MIX + nbc]
    cm = xc[:, MIX + nbc:MIX + 2 * nbc]
    dt = jax.nn.softplus(dt_raw + dtb_ref[...])
    a = -jnp.exp(alog_ref[...])
    ld = dt * a
    acum = _hdot(tri_ref[...], ld)
    acum_t = acum.T
    dt_t = dt.T
    a_row = jnp.concatenate([acum_t[h:h + 1, :] for h in range(SSD_HEADS)], axis=1)
    dt_row = jnp.concatenate([dt_t[h:h + 1, :] for h in range(SSD_HEADS)], axis=1)
    a_col = _split3_dot(acum, ehj_ref[...])
    row = lax.broadcasted_iota(jnp.int32, (Lc, SSD_HEADS * Lc), 0)
    col = lax.broadcasted_iota(jnp.int32, (Lc, SSD_HEADS * Lc), 1)
    causal = row >= (col & (Lc - 1))
    decay = jnp.where(causal, jnp.exp(jnp.where(causal, a_col - a_row, 0.0)), 0.0)
    rep = SSD_HEADS // SSD_GROUPS
    gmats = []
    for g in range(SSD_GROUPS):
        cg = cm[:, g * SSD_STATE:(g + 1) * SSD_STATE]
        bg = bm[:, g * SSD_STATE:(g + 1) * SSD_STATE]
        gmats.append(_bdot_nt(cg, bg))
    g_all = jnp.concatenate([gmats[h // rep] for h in range(SSD_HEADS)], axis=1)
    m_all = g_all * decay * dt_row
    y = _bdot(m_all, _head_block_rows(xs, SSD_HEADS, SSD_HD))
    exp_a = jnp.exp(acum)
    exp_a_l = _split3_dot(exp_a, ehn_ref[...])
    c_rep = _group_repeat_lanes(cm, rep)
    y = y + _bdot_nt(c_rep * exp_a_l, st_ref[...])
    a_last = acum[Lc - 1:Lc, :]
    w_end_l = _split3_dot(jnp.exp(a_last - acum) * dt, ehn_ref[...])
    upd = _bdot_tn(xs, _group_repeat_lanes(bm, rep) * w_end_l)
    st_ref[...] = exp_a_l[Lc - 1:Lc, :] * st_ref[...] + jnp.where(_head_diag_mask(MIX, SSD_HD), upd, 0.0)
    y = y + dlane_ref[...] * xs
    y = y * _silu(zgate)
    y_ref[...] = _rmsnorm(y, ng_ref[...]).astype(y_ref.dtype)

    @pl.when(c == nc - 1)
    def _():
        buf_ref[...] = pad_ref[Lc + 8 - (CONV_K - 1):Lc + 8, :]
        s_t = st_ref[...].T
        for h in range(SSD_HEADS):
            sl = slice(h * SSD_HD, (h + 1) * SSD_HD)
            sto_ref[h] = s_t[sl, sl]

    _conv_finish(pad_ref, Lc)


def _const_spec(a):
    return pl.BlockSpec(a.shape, lambda *_: (0,) * a.ndim)


def _ssd_prompt(z3, P, l):
    B, L, _ = z3.shape
    Lc = CHUNK
    params = [P[k] for k in ("ssd_cw", "ssd_cb", "ssd_dtb", "ssd_alog", "ssd_dlane", "ssd_norm")]
    consts = [_tri(Lc), _head_expand(SSD_HEADS, Lc), _head_expand(SSD_HEADS, SSD_HD)]
    return pl.pallas_call(
        _ssd_kernel,
        out_shape=(jax.ShapeDtypeStruct((B, L, MIX), BF16),
                   jax.ShapeDtypeStruct((B, SSD_HEADS, SSD_STATE, SSD_HD), F32),
                   jax.ShapeDtypeStruct((B, CONV_K - 1, SSD_CONV), F32)),
        grid=(B, L // Lc),
        in_specs=[pl.BlockSpec((None, Lc, 1536), lambda b, c: (b, c, Z_SSD // 1536))]
                 + [_lspec(a, l) for a in params] + [_const_spec(a) for a in consts],
        out_specs=(pl.BlockSpec((None, Lc, MIX), lambda b, c: (b, c, 0)),
                   pl.BlockSpec((None, SSD_HEADS, SSD_STATE, SSD_HD), lambda b, c: (b, 0, 0, 0)),
                   pl.BlockSpec((None, CONV_K - 1, SSD_CONV), lambda b, c: (b, 0, 0))),
        scratch_shapes=[pltpu.VMEM((Lc + 8, SSD_CONV), F32), pltpu.VMEM((MIX, MIX), F32)],
        compiler_params=_cparams("parallel", "arbitrary"),
        name="ssd_prompt",
    )(z3, *params, *consts)


def _tri(Lc):
    return jnp.asarray(np.tril(np.ones((Lc, Lc), np.float32)))


def _head_expand(nh, width):
    e = np.zeros((LANES, nh * width), np.float32)
    for h in range(nh):
        e[h, h * width:(h + 1) * width] = 1.0
    return jnp.asarray(e, BF16)


def _ret_gammas():
    return 1.0 - np.exp2(-5.0 - np.arange(RET_HEADS, dtype=np.float64))


def _ret_tables(Lc):
    gam = _ret_gammas()
    i = np.arange(Lc)
    d = i[:, None] - i[None, :]
    decay = np.where(d >= 0, gam[:, None, None] ** np.maximum(d, 0)[None], 0.0)
    decay_l = np.transpose(decay, (1, 0, 2)).reshape(Lc, RET_HEADS * Lc)
    grow_l = np.repeat(gam[None, :] ** (i[:, None] + 1), RET_HD, axis=1)
    toend_t = np.repeat(gam[:, None] ** (Lc - 1 - i[None, :]), RET_HD, axis=0)
    hd = np.arange(MIX) // RET_HD
    state_decay = np.where(hd[:, None] == hd[None, :], (gam ** Lc)[hd][:, None], 0.0)
    return tuple(jnp.asarray(t, F32) for t in (decay_l, grow_l, toend_t, state_decay))


def _rope_tables(pos):
    half = RET_HD // 2
    inv = ROPE_BASE ** (-np.arange(half, dtype=np.float64) / half)
    ang = np.asarray(pos, np.float64)[:, None] * inv[None, :]
    cos = np.cos(ang)
    sin = np.sin(ang)
    cos_l = np.tile(np.concatenate([cos, cos], axis=1), (1, RET_HEADS))
    sin_l = np.tile(np.concatenate([-sin, sin], axis=1), (1, RET_HEADS))
    return jnp.asarray(cos_l, F32), jnp.asarray(sin_l, F32)


def _rotary_lanes(x, cos_l, sin_l):
    lane = lax.broadcasted_iota(jnp.int32, x.shape, 1)
    first = (lane & (RET_HD - 1)) < (RET_HD // 2)
    n = x.shape[1]
    swapped = jnp.where(first, pltpu.roll(x, n - RET_HD // 2, 1), pltpu.roll(x, RET_HD // 2, 1))
    return x * cos_l + swapped * sin_l


def _group_norm_head(o):
    mu = jnp.mean(o, axis=-1, keepdims=True)
    d = o - mu
    var = jnp.mean(d * d, axis=-1, keepdims=True)
    return d * lax.rsqrt(var + 1e-5)


def _group_norm_lanes(o, hd):
    cols = []
    for cb in range(o.shape[1] // LANES):
        x = o[:, cb * LANES:(cb + 1) * LANES]
        low = lax.broadcasted_iota(jnp.int32, x.shape, 1) < hd

        def seg_mean(t):
            lo = jnp.sum(jnp.where(low, t, 0.0), axis=1, keepdims=True)
            hi = jnp.sum(jnp.where(low, 0.0, t), axis=1, keepdims=True)
            return jnp.where(low, lo, hi) * (1.0 / hd)

        d = x - seg_mean(x)
        cols.append(d * lax.rsqrt(seg_mean(d * d) + 1e-5))
    return jnp.concatenate(cols, axis=1)


def _ret_kernel(z_ref, cos_ref, sin_ref, dec_ref, grow_ref, toend_ref, sdec_ref, gn_ref,
                y_ref, sto_ref, st_ref):
    c = pl.program_id(1)
    nc = pl.num_programs(1)
    zz = z_ref[...].astype(F32)
    q = _rotary_lanes(zz[:, 0:MIX], cos_ref[...], sin_ref[...]) * (RET_HD ** -0.5)
    k = _rotary_lanes(zz[:, MIX:2 * MIX], cos_ref[...], sin_ref[...])
    v = zz[:, 2 * MIX:3 * MIX]
    gate = zz[:, 3 * MIX:4 * MIX]

    @pl.when(c == 0)
    def _():
        st_ref[...] = jnp.zeros(st_ref.shape, F32)

    k_t = k.T
    g_all = _bdot(q, _head_block_cols(k_t, RET_HEADS, RET_HD))
    o = _bdot(g_all * dec_ref[...], _head_block_rows(v, RET_HEADS, RET_HD))
    o = o + _bdot(q * grow_ref[...], st_ref[...])
    upd = _bdot(k_t * toend_ref[...], v)
    st_ref[...] = sdec_ref[...] * st_ref[...] + jnp.where(_head_diag_mask(MIX, RET_HD), upd, 0.0)
    y_ref[...] = (_silu(gate) * (_group_norm_lanes(o, RET_HD) * gn_ref[...])).astype(y_ref.dtype)

    @pl.when(c == nc - 1)
    def _():
        for h in range(RET_HEADS):
            sl = slice(h * RET_HD, (h + 1) * RET_HD)
            sto_ref[h] = st_ref[sl, sl]


def _ret_prompt(z3, P, l):
    B, L, _ = z3.shape
    Lc = CHUNK
    cos_l, sin_l = _rope_tables(np.arange(L))
    consts = list(_ret_tables(Lc))
    return pl.pallas_call(
        _ret_kernel,
        out_shape=(jax.ShapeDtypeStruct((B, L, MIX), BF16),
                   jax.ShapeDtypeStruct((B, RET_HEADS, RET_HD, RET_HD), F32)),
        grid=(B, L // Lc),
        in_specs=[pl.BlockSpec((None, Lc, 2048), lambda b, c: (b, c, Z_RET // 2048)),
                  pl.BlockSpec((Lc, MIX), lambda b, c: (c, 0)),
                  pl.BlockSpec((Lc, MIX), lambda b, c: (c, 0))]
                 + [_const_spec(a) for a in consts] + [_lspec(P["ret_gn"], l)],
        out_specs=(pl.BlockSpec((None, Lc, MIX), lambda b, c: (b, c, 0)),
                   pl.BlockSpec((None, RET_HEADS, RET_HD, RET_HD), lambda b, c: (b, 0, 0, 0))),
        scratch_shapes=[pltpu.VMEM((MIX, MIX), F32)],
        compiler_params=_cparams("parallel", "arbitrary"),
        name="ret_prompt",
    )(z3, cos_l, sin_l, *consts, P["ret_gn"])


def _lru_gates(xc, wg_ref, bg_ref, lam_ref):
    rg = _bdot(xc, wg_ref[...]) + bg_ref[...]
    r = jax.nn.sigmoid(rg[:, 0:MIX])
    i = jax.nn.sigmoid(rg[:, MIX:2 * MIX])
    log_a = -LRU_C * r * jax.nn.softplus(-lam_ref[...])
    a = jnp.exp(log_a)
    bx = jnp.sqrt(_neg_expm1_2x(log_a, a)) * (i * xc)
    return a, bx


def _lru_kernel(z_ref, cw_ref, cb_ref, wg_ref, bg_ref, lam_ref,
                y_ref, st_ref, buf_ref, pad_ref):
    c = pl.program_id(1)
    nc = pl.num_programs(1)
    Lc = z_ref.shape[0]
    zz = z_ref[...].astype(F32)
    gate = zz[:, 0:MIX]
    x = zz[:, MIX:2 * MIX]

    @pl.when(c == 0)
    def _():
        st_ref[...] = jnp.zeros(st_ref.shape, F32)

    xc = _conv_chunk(c, x, pad_ref, cw_ref, cb_ref)
    a, bx = _lru_gates(xc, wg_ref, bg_ref, lam_ref)
    row = lax.broadcasted_iota(jnp.int32, (Lc, MIX), 0)
    s = 1
    while s < Lc:
        keep = row >= s
        bx = jnp.where(keep, bx + a * pltpu.roll(bx, s, 0), bx)
        a = jnp.where(keep, a * pltpu.roll(a, s, 0), a)
        s *= 2
    h = bx + a * st_ref[...]
    st_ref[...] = h[Lc - 1:Lc, :]
    y_ref[...] = (h * jax.nn.gelu(gate)).astype(y_ref.dtype)

    @pl.when(c == nc - 1)
    def _():
        buf_ref[...] = pad_ref[Lc + 8 - (CONV_K - 1):Lc + 8, :]

    _conv_finish(pad_ref, Lc)


def _lru_prompt(z3, P, l):
    B, L, _ = z3.shape
    Lc = CHUNK
    params = [P[k] for k in ("lru_cw", "lru_cb", "lru_wg", "lru_bg", "lru_lam")]
    return pl.pallas_call(
        _lru_kernel,
        out_shape=(jax.ShapeDtypeStruct((B, L, MIX), BF16),
                   jax.ShapeDtypeStruct((B, 1, MIX), F32),
                   jax.ShapeDtypeStruct((B, CONV_K - 1, MIX), F32)),
        grid=(B, L // Lc),
        in_specs=[pl.BlockSpec((None, Lc, 1024), lambda b, c: (b, c, Z_LRU // 1024))]
                 + [_lspec(a, l) for a in params],
        out_specs=(pl.BlockSpec((None, Lc, MIX), lambda b, c: (b, c, 0)),
                   pl.BlockSpec((None, 1, MIX), lambda b, c: (b, 0, 0)),
                   pl.BlockSpec((None, CONV_K - 1, MIX), lambda b, c: (b, 0, 0))),
        scratch_shapes=[pltpu.VMEM((Lc + 8, MIX), F32)],
        compiler_params=_cparams("parallel", "arbitrary"),
        name="lru_prompt",
    )(z3, *params)


S5_LB = 4
S5_LAGS = 4
assert 2 * S5_LAGS == SUBLANES


def _s5_project_in(u, wbr_ref, wbi_ref, xr_ref, xi_ref, lags):
    shifted = [u]
    if lags > 1:
        sub = lax.broadcasted_iota(jnp.int32, u.shape, 0) & (lags - 1)
        shifted += [jnp.where(sub >= d, pltpu.roll(u, d, 0), 0.0) for d in range(1, lags)]
    shifted = [s.astype(BF16) for s in shifted]
    for kb in range(S5_LB):
        lhs = jnp.concatenate([s[:, kb * 128:(kb + 1) * 128] for s in shifted], axis=1)
        k = lags * 128
        xr_ref[:, kb * 512:(kb + 1) * 512] = jnp.dot(lhs, wbr_ref[kb, 0:k, :], preferred_element_type=F32)
        xi_ref[:, kb * 512:(kb + 1) * 512] = jnp.dot(lhs, wbi_ref[kb, 0:k, :], preferred_element_type=F32)


def _s5_project_out(xr, xi, u, wcr_ref, wci_ref, d_ref, wglu_ref):
    ys = []
    for kb in range(S5_LB):
        sl = slice(kb * 512, (kb + 1) * 512)
        ys.append(_bdot(xr[:, sl], wcr_ref[kb]) - _bdot(xi[:, sl], wci_ref[kb]))
    y = jnp.concatenate(ys, axis=1) + d_ref[...] * u
    y = jax.nn.gelu(y)
    return y * jax.nn.sigmoid(_bdot(y, wglu_ref[...]))


def _s5_kernel(z_ref, wbr_ref, wbi_ref, wcr_ref, wci_ref, pr_ref, pi_ref, d_ref, wglu_ref, qr_ref, qi_ref,
               y_ref, sr_ref, si_ref, xr_ref, xi_ref):
    c = pl.program_id(1)
    Lc = z_ref.shape[0]
    u = z_ref[...].astype(F32)

    @pl.when(c == 0)
    def _():
        sr_ref[...] = jnp.zeros(sr_ref.shape, F32)
        si_ref[...] = jnp.zeros(si_ref.shape, F32)

    _s5_project_in(u, wbr_ref, wbi_ref, xr_ref, xi_ref, S5_LAGS)
    ng = Lc // SUBLANES
    x3r = xr_ref[...].reshape(ng, SUBLANES, S5_CH)
    x3i = xi_ref[...].reshape(ng, SUBLANES, S5_CH)
    tr = jnp.broadcast_to(x3r[:, S5_LAGS - 1:S5_LAGS, :], x3r.shape)
    ti = jnp.broadcast_to(x3i[:, S5_LAGS - 1:S5_LAGS, :], x3i.shape)
    mr, mi = qr_ref[...][None], qi_ref[...][None]
    x3r, x3i = x3r + (mr * tr - mi * ti), x3i + (mr * ti + mi * tr)
    pcr, pci = pr_ref[...], pi_ref[...]
    cr, ci = sr_ref[...], si_ref[...]
    for g in range(ng):
        br = jnp.broadcast_to(cr, (SUBLANES, S5_CH))
        bi = jnp.broadcast_to(ci, (SUBLANES, S5_CH))
        gr = x3r[g] + (pcr * br - pci * bi)
        gi = x3i[g] + (pcr * bi + pci * br)
        xr_ref[g * SUBLANES:(g + 1) * SUBLANES, :] = gr
        xi_ref[g * SUBLANES:(g + 1) * SUBLANES, :] = gi
        cr, ci = gr[SUBLANES - 1:SUBLANES, :], gi[SUBLANES - 1:SUBLANES, :]
    sr_ref[...] = cr
    si_ref[...] = ci
    y_ref[...] = _s5_project_out(xr_ref[...], xi_ref[...], u, wcr_ref, wci_ref, d_ref, wglu_ref).astype(y_ref.dtype)


S5_PARAMS = ("s5_wbr", "s5_wbi", "s5_wcr", "s5_wci", "s5_pr", "s5_pi", "s5_d", "s5_glu", "s5_qr", "s5_qi")


def _s5_prompt(z3, P, l):
    B, L, _ = z3.shape
    Lc = CHUNK
    params = [P[k] for k in S5_PARAMS]
    return pl.pallas_call(
        _s5_kernel,
        out_shape=(jax.ShapeDtypeStruct((B, L, MIX), BF16),
                   jax.ShapeDtypeStruct((B, 1, S5_CH), F32),
                   jax.ShapeDtypeStruct((B, 1, S5_CH), F32)),
        grid=(B, L // Lc),
        in_specs=[pl.BlockSpec((None, Lc, MIX), lambda b, c: (b, c, Z_S5 // MIX))]
                 + [_lspec(a, l) for a in params],
        out_specs=(pl.BlockSpec((None, Lc, MIX), lambda b, c: (b, c, 0)),
                   pl.BlockSpec((None, 1, S5_CH), lambda b, c: (b, 0, 0)),
                   pl.BlockSpec((None, 1, S5_CH), lambda b, c: (b, 0, 0))),
        scratch_shapes=[pltpu.VMEM((Lc, S5_CH), F32), pltpu.VMEM((Lc, S5_CH), F32)],
        compiler_params=_cparams("parallel", "arbitrary"),
        name="s5_prompt",
    )(z3, *params)


def _merge_kernel(y0_ref, y1_ref, y2_ref, y3_ref, zg_ref, h_ref, wb_ref, wo_ref, o_ref):
    acc = None
    for k, y_ref in enumerate((y0_ref, y1_ref, y2_ref, y3_ref)):
        br = _bdot(y_ref[...], wb_ref[k])
        t = jax.nn.sigmoid(zg_ref[:, k * D_MODEL:(k + 1) * D_MODEL].astype(F32)) * br
        acc = t if acc is None else acc + t
    o_ref[...] = h_ref[...] + _bdot(acc, wo_ref[...])


def _merge(ys, z, h, P, l, tm):
    T = h.shape[0]
    rows = lambda w: pl.BlockSpec((tm, w), lambda i: (i, 0))
    return pl.pallas_call(
        _merge_kernel,
        out_shape=jax.ShapeDtypeStruct((T, D_MODEL), F32),
        grid=(T // tm,),
        in_specs=[rows(MIX), rows(MIX), rows(MIX), rows(MIX),
                  pl.BlockSpec((tm, 4 * D_MODEL), lambda i: (i, Z_MERGE)),
                  rows(D_MODEL),
                  _lspec(P["w_branch"], l), _lspec(P["w_out"], l)],
        out_specs=rows(D_MODEL),
        compiler_params=_cparams("parallel"),
        name="merge",
    )(*ys, z, h, P["w_branch"], P["w_out"])


def _top2_gates(logits):
    lane, i1, i2, w1, w2 = _top2(logits)
    return jnp.where(lane == i1, w1, 0.0) + jnp.where(lane == i2, w2, 0.0)


def _ffn_kernel(moe, final_norm, *refs):
    if moe:
        h_ref, g_ref, rt_ref, w1_ref, w3_ref, w2_ref, gf_ref, o_ref, hn_ref, acc_ref, gate_ref = refs
    else:
        h_ref, g_ref, w1_ref, w3_ref, w2_ref, gf_ref, o_ref, hn_ref, acc_ref = refs
    e = pl.program_id(1)
    ne = pl.num_programs(1)

    @pl.when(e == 0)
    def _():
        hn = _rmsnorm(h_ref[...], g_ref[...])
        hn_ref[...] = hn.astype(BF16)
        acc_ref[...] = jnp.zeros(acc_ref.shape, F32)
        if moe:
            lane = lax.broadcasted_iota(jnp.int32, (hn.shape[0], LANES), 1)
            logits = jnp.where(lane < N_EXPERTS, _hdot(hn, rt_ref[...]), -jnp.inf)
            gate_ref[...] = _top2_gates(logits)

    hn = hn_ref[...]
    a = jnp.dot(hn, w1_ref[...], preferred_element_type=F32)
    b = jnp.dot(hn, w3_ref[...], preferred_element_type=F32)
    o = _bdot(_silu(a) * b, w2_ref[...])
    if moe:
        lane = lax.broadcasted_iota(jnp.int32, gate_ref.shape, 1)
        ge = jnp.sum(jnp.where(lane == e, gate_ref[...], 0.0), axis=-1, keepdims=True)
        o = ge * o
    acc_ref[...] += o

    @pl.when(e == ne - 1)
    def _():
        out = h_ref[...] + acc_ref[...]
        if final_norm:
            out = _rmsnorm(out, gf_ref[...])
        o_ref[...] = out


def _ffn(h, P, l, tm, final_norm):
    T = h.shape[0]
    moe = l % 2 == 1
    j = l // 2
    tf = D_FF_TILE
    row_spec = pl.BlockSpec((tm, D_MODEL), lambda i, e: (i, 0))
    gfinal = P["norm_final"]
    if moe:
        w1, w3, w2 = P["moe_w1"], P["moe_w3"], P["moe_w2"]
        ne = w1.shape[1]
        wspecs = [pl.BlockSpec((None, None, D_MODEL, tf), lambda i, e: (j, e, 0, 0)),
                  pl.BlockSpec((None, None, D_MODEL, tf), lambda i, e: (j, e, 0, 0)),
                  pl.BlockSpec((None, None, tf, D_MODEL), lambda i, e: (j, e, 0, 0))]
        in_specs = ([row_spec, _lspec(P["norm_ffn"], l), _lspec(P["moe_router"], j)] + wspecs
                    + [_const_spec(gfinal)])
        args = (h, P["norm_ffn"], P["moe_router"], w1, w3, w2, gfinal)
        scratch = [pltpu.VMEM((tm, D_MODEL), BF16), pltpu.VMEM((tm, D_MODEL), F32), pltpu.VMEM((tm, LANES), F32)]
    else:
        w1, w3, w2 = P["ffn_w1"], P["ffn_w3"], P["ffn_w2"]
        ne = w1.shape[2] // tf
        wspecs = [pl.BlockSpec((None, D_MODEL, tf), lambda i, e: (j, 0, e)),
                  pl.BlockSpec((None, D_MODEL, tf), lambda i, e: (j, 0, e)),
                  pl.BlockSpec((None, tf, D_MODEL), lambda i, e: (j, e, 0))]
        in_specs = [row_spec, _lspec(P["norm_ffn"], l)] + wspecs + [_const_spec(gfinal)]
        args = (h, P["norm_ffn"], w1, w3, w2, gfinal)
        scratch = [pltpu.VMEM((tm, D_MODEL), BF16), pltpu.VMEM((tm, D_MODEL), F32)]
    return pl.pallas_call(
        functools.partial(_ffn_kernel, moe, final_norm),
        out_shape=jax.ShapeDtypeStruct((T, D_MODEL), F32),
        grid=(T // tm, ne),
        in_specs=in_specs,
        out_specs=row_spec,
        scratch_shapes=scratch,
        compiler_params=_cparams("parallel", "arbitrary"),
        name="moe" if moe else "ffn",
    )(*args)


MOE_ROWS = 512
ROUTE_LANES = ("e1", "e2", "r1", "r2", "w1", "w2")


def _top2(logits):
    lane = lax.broadcasted_iota(jnp.int32, logits.shape, 1).astype(F32)
    big = float(LANES)
    m1 = jnp.max(logits, axis=-1, keepdims=True)
    i1 = jnp.min(jnp.where(logits == m1, lane, big), axis=-1, keepdims=True)
    rest = jnp.where(lane == i1, -jnp.inf, logits)
    m2 = jnp.max(rest, axis=-1, keepdims=True)
    i2 = jnp.min(jnp.where(rest == m2, lane, big), axis=-1, keepdims=True)
    e2 = jnp.exp(m2 - m1)
    den = 1.0 + e2
    return lane, i1, i2, 1.0 / den, e2 / den


def _moe_route_kernel(h_ref, g_ref, rt_ref, ltri_ref, hn_ref, info_ref, cnt_ref, base_ref):
    i = pl.program_id(0)

    @pl.when(i == 0)
    def _():
        base_ref[...] = jnp.zeros(base_ref.shape, F32)

    hn = _rmsnorm(h_ref[...], g_ref[...])
    hn_ref[...] = hn
    lane_i = lax.broadcasted_iota(jnp.int32, (hn.shape[0], LANES), 1)
    logits = jnp.where(lane_i < N_EXPERTS, _hdot(hn, rt_ref[...]), -jnp.inf)
    lane, i1, i2, w1, w2 = _top2(logits)
    oh1 = (lane == i1).astype(F32)
    oh2 = (lane == i2).astype(F32)
    oh = oh1 + oh2
    before = jnp.dot(ltri_ref[...], oh.astype(BF16), preferred_element_type=F32)
    rank = base_ref[...] + before
    r1 = jnp.sum(oh1 * rank, axis=-1, keepdims=True)
    r2 = jnp.sum(oh2 * rank, axis=-1, keepdims=True)
    base_ref[...] += jnp.sum(oh, axis=0, keepdims=True)
    fields = dict(e1=i1, e2=i2, r1=r1, r2=r2, w1=w1, w2=w2)
    info = jnp.zeros(lane.shape, F32)
    for k, name in enumerate(ROUTE_LANES):
        info = jnp.where(lane_i == k, fields[name], info)
    info_ref[...] = info

    @pl.when(i == pl.num_programs(0) - 1)
    def _():
        cnt_ref[...] = base_ref[...]


def _moe_route(h, P, l, tm):
    T = h.shape[0]
    j = l // 2
    ltri = jnp.asarray(np.tril(np.ones((tm, tm), np.float32), -1), BF16)
    return pl.pallas_call(
        _moe_route_kernel,
        out_shape=(jax.ShapeDtypeStruct((T, D_MODEL), F32), jax.ShapeDtypeStruct((T, LANES), F32),
                   jax.ShapeDtypeStruct((1, LANES), F32)),
        grid=(T // tm,),
        in_specs=[pl.BlockSpec((tm, D_MODEL), lambda i: (i, 0)), _lspec(P["norm_ffn"], l),
                  _lspec(P["moe_router"], j), _const_spec(ltri)],
        out_specs=(pl.BlockSpec((tm, D_MODEL), lambda i: (i, 0)), pl.BlockSpec((tm, LANES), lambda i: (i, 0)),
                   pl.BlockSpec((1, LANES), lambda i: (0, 0))),
        scratch_shapes=[pltpu.VMEM((1, LANES), F32)],
        compiler_params=_cparams("arbitrary"),
        name="moe_route",
    )(h, P["norm_ffn"], P["moe_router"], ltri)


def _moe_plan(info, cnt, tm, n_tiles_max):
    T = info.shape[0]
    rows = MOE_ROWS
    count = cnt[0, :N_EXPERTS].astype(jnp.int32)
    tiles_e = (count + rows - 1) // rows
    first_tile = jnp.cumsum(tiles_e) - tiles_e
    start = first_tile * rows
    experts = jnp.arange(N_EXPERTS, dtype=jnp.int32)

    def row_of(e, r):
        sel = e.astype(jnp.int32)[:, None] == experts[None, :]
        return jnp.sum(jnp.where(sel, start[None, :], 0), axis=1) + r.astype(jnp.int32)

    pos = jnp.stack([row_of(info[:, 0], info[:, 2]), row_of(info[:, 1], info[:, 3])], axis=0)
    pos = jnp.transpose(pos.reshape(2, T // tm, tm), (1, 0, 2))
    n_tiles = jnp.sum(tiles_e)
    t = jnp.minimum(jnp.arange(n_tiles_max, dtype=jnp.int32), n_tiles - 1)
    tile_expert = jnp.sum((first_tile[None, :] <= t[:, None]).astype(jnp.int32), axis=1) - 1
    last_tile = first_tile + tiles_e - 1
    return pos, tile_expert, n_tiles.reshape(1), last_tile, tiles_e


def _moe_dispatch_kernel(last_ref, tiles_ref, nt_ref, pos_ref, hn_ref, xs_ref, zero_ref, sem):
    tm = hn_ref.shape[0]

    @pl.when(pl.program_id(0) == 0)
    def _():
        zero_ref[...] = jnp.zeros(zero_ref.shape, F32)

        def clear(tile):
            row0 = pl.multiple_of(tile * MOE_ROWS, MOE_ROWS)
            cp = pltpu.make_async_copy(zero_ref, xs_ref.at[pl.ds(row0, MOE_ROWS), :], sem)
            cp.start()
            cp.wait()

        for e in range(N_EXPERTS):
            @pl.when(tiles_ref[e] > 0)
            def _():
                clear(last_ref[e])

        def clear_tail(tile, c):
            clear(tile)
            return c

        lax.fori_loop(nt_ref[0], xs_ref.shape[0] // MOE_ROWS, clear_tail, 0)

    def row_copy(j, slot):
        return pltpu.make_async_copy(hn_ref.at[pl.ds(j, 1), :], xs_ref.at[pl.ds(pos_ref[slot, j], 1), :], sem)

    def issue(j, c):
        row_copy(j, 0).start()
        row_copy(j, 1).start()
        return c

    def drain(j, c):
        row_copy(j, 0).wait()
        row_copy(j, 1).wait()
        return c

    lax.fori_loop(0, tm, issue, 0, unroll=8)
    lax.fori_loop(0, tm, drain, 0, unroll=8)


def _moe_dispatch(hn, pos, last_tile, tiles_e, n_tiles, n_rows):
    T = hn.shape[0]
    tm = pos.shape[2]
    gs = pltpu.PrefetchScalarGridSpec(
        num_scalar_prefetch=3, grid=(T // tm,),
        in_specs=[pl.BlockSpec((None, 2, tm), lambda i, *_: (i, 0, 0), memory_space=pltpu.SMEM),
                  pl.BlockSpec((tm, D_MODEL), lambda i, *_: (i, 0))],
        out_specs=pl.BlockSpec(memory_space=pl.ANY),
        scratch_shapes=[pltpu.VMEM((MOE_ROWS, D_MODEL), F32), pltpu.SemaphoreType.DMA])
    return pl.pallas_call(
        _moe_dispatch_kernel, grid_spec=gs,
        out_shape=jax.ShapeDtypeStruct((n_rows, D_MODEL), F32),
        compiler_params=_cparams("arbitrary"),
        name="moe_dispatch",
    )(last_tile, tiles_e, n_tiles, pos, hn)


def _moe_group_kernel(te_ref, nt_ref, x_ref, w1_ref, w3_ref, w2_ref, o_ref):
    live = pl.program_id(0) < nt_ref[0]

    @pl.when(live)
    def _():
        x = x_ref[...].astype(BF16)
        a = jnp.dot(x, w1_ref[...], preferred_element_type=F32)
        b = jnp.dot(x, w3_ref[...], preferred_element_type=F32)
        o_ref[...] = _bdot(_silu(a) * b, w2_ref[...])

    @pl.when(jnp.logical_not(live))
    def _():
        o_ref[...] = jnp.zeros(o_ref.shape, F32)


def _moe_group(xs, P, j, tile_expert, n_tiles):
    n_rows = xs.shape[0]
    tf = D_FF_TILE
    wmap = lambda i, te, nt: (j, te[i], 0, 0)
    rmap = lambda i, te, nt: (i, 0)
    gs = pltpu.PrefetchScalarGridSpec(
        num_scalar_prefetch=2, grid=(n_rows // MOE_ROWS,),
        in_specs=[pl.BlockSpec((MOE_ROWS, D_MODEL), rmap),
                  pl.BlockSpec((None, None, D_MODEL, tf), wmap),
                  pl.BlockSpec((None, None, D_MODEL, tf), wmap),
                  pl.BlockSpec((None, None, tf, D_MODEL), wmap)],
        out_specs=pl.BlockSpec((MOE_ROWS, D_MODEL), rmap))
    return pl.pallas_call(
        _moe_group_kernel, grid_spec=gs,
        out_shape=jax.ShapeDtypeStruct((n_rows, D_MODEL), F32),
        compiler_params=_cparams("arbitrary"),
        name="moe_group",
    )(tile_expert, n_tiles, xs, P["moe_w1"], P["moe_w3"], P["moe_w2"])


def _moe_combine_kernel(final_norm, pos_ref, o_ref, h_ref, info_ref, gf_ref, out_ref, a_ref, b_ref, sem):
    tm = h_ref.shape[0]

    def row_copy(j, slot, dst):
        return pltpu.make_async_copy(o_ref.at[pl.ds(pos_ref[slot, j], 1), :], dst.at[pl.ds(j, 1), :], sem)

    def issue(j, c):
        row_copy(j, 0, a_ref).start()
        row_copy(j, 1, b_ref).start()
        return c

    def drain(j, c):
        row_copy(j, 0, a_ref).wait()
        row_copy(j, 1, b_ref).wait()
        return c

    lax.fori_loop(0, tm, issue, 0, unroll=8)
    lax.fori_loop(0, tm, drain, 0, unroll=8)
    k1, k2 = ROUTE_LANES.index("w1"), ROUTE_LANES.index("w2")
    info = info_ref[...]
    w1, w2 = info[:, k1:k1 + 1], info[:, k2:k2 + 1]
    out = h_ref[...] + (w1 * a_ref[...] + w2 * b_ref[...])
    if final_norm:
        out = _rmsnorm(out, gf_ref[...])
    out_ref[...] = out


def _moe_combine(o, h, info, pos, P, final_norm):
    T = h.shape[0]
    tm = pos.shape[2]
    gfinal = P["norm_final"]
    row_spec = pl.BlockSpec((tm, D_MODEL), lambda i: (i, 0))
    return pl.pallas_call(
        functools.partial(_moe_combine_kernel, final_norm),
        out_shape=jax.ShapeDtypeStruct((T, D_MODEL), F32),
        grid=(T // tm,),
        in_specs=[pl.BlockSpec((None, 2, tm), lambda i: (i, 0, 0), memory_space=pltpu.SMEM),
                  pl.BlockSpec(memory_space=pl.ANY), row_spec,
                  pl.BlockSpec((tm, LANES), lambda i: (i, 0)), _const_spec(gfinal)],
        out_specs=row_spec,
        scratch_shapes=[pltpu.VMEM((tm, D_MODEL), F32), pltpu.VMEM((tm, D_MODEL), F32), pltpu.SemaphoreType.DMA],
        compiler_params=_cparams("arbitrary"),
        name="moe_combine",
    )(pos, o, h, info, gfinal)


def _moe_routed(h, P, l, tm, final_norm):
    T = h.shape[0]
    n_tiles_max = (2 * T) // MOE_ROWS + N_EXPERTS
    hn, info, cnt = _moe_route(h, P, l, tm)
    pos, tile_expert, n_tiles, last_tile, tiles_e = _moe_plan(info, cnt, tm, n_tiles_max)
    xs = _moe_dispatch(hn, pos, last_tile, tiles_e, n_tiles, n_tiles_max * MOE_ROWS)
    o = _moe_group(xs, P, l // 2, tile_expert, n_tiles)
    return _moe_combine(o, h, info, pos, P, final_norm)


def _conv_step(x, buf_ref, buf_o, w_ref, b_ref):
    out = b_ref[...]
    for k in range(CONV_K - 1):
        out = out + buf_ref[k] * w_ref[k:k + 1, :]
        if k > 0:
            buf_o[k - 1] = buf_ref[k]
    out = out + x * w_ref[CONV_K - 1:CONV_K, :]
    buf_o[CONV_K - 2] = x
    return out


def _step_pre_kernel(pos_cos_ref, pos_sin_ref, gam_ref, ehn_ref, z_ref, sbuf_ref, lbuf_ref, lst_ref, s5r_ref, s5i_ref,
                     scw_ref, scb_ref, dtb_ref, alog_ref,
                     lcw_ref, lcb_ref, wg_ref, bg_ref, lam_ref,
                     wbr_ref, wbi_ref, wcr_ref, wci_ref, pr_ref, pi_ref, s5d_ref, wglu_ref, qr_ref, qi_ref,
                     kqv_ref, dec_ref, sbuf_o, lbuf_o, lst_o, s5r_o, s5i_o, ys5_o, ylru_o, xs_o,
                     xr_ref, xi_ref):
    zz = z_ref[...].astype(F32)
    xbc = zz[:, Z_SSD + MIX:Z_SSD + MIX + SSD_CONV]
    dt_raw = zz[:, Z_SSD + MIX + SSD_CONV:Z_SSD + MIX + SSD_CONV + LANES]
    xc = _silu(_conv_step(xbc, sbuf_ref, sbuf_o, scw_ref, scb_ref))
    xs = xc[:, 0:MIX]
    xs_o[...] = xs
    nbc = SSD_GROUPS * SSD_STATE
    bm = xc[:, MIX:MIX + nbc]
    cm = xc[:, MIX + nbc:MIX + 2 * nbc]
    dt = jax.nn.softplus(dt_raw + dtb_ref[...])
    a = -jnp.exp(alog_ref[...])
    rep = SSD_HEADS // SSD_GROUPS
    kqv_ref[0, 0] = (_group_repeat_lanes(bm, rep) * _split3_dot(dt, ehn_ref[...])).T
    kqv_ref[0, 1] = _group_repeat_lanes(cm, rep).T
    kqv_ref[0, 2] = xs.T
    dec_ref[0] = jnp.exp(dt * a).T[0:SSD_HEADS, :]
    q = _rotary_lanes(zz[:, Z_RET:Z_RET + MIX], pos_cos_ref[...], pos_sin_ref[...]) * (RET_HD ** -0.5)
    k = _rotary_lanes(zz[:, Z_RET + MIX:Z_RET + 2 * MIX], pos_cos_ref[...], pos_sin_ref[...])
    kqv_ref[1, 0] = k.T
    kqv_ref[1, 1] = q.T
    kqv_ref[1, 2] = zz[:, Z_RET + 2 * MIX:Z_RET + 3 * MIX].T
    dec_ref[1] = gam_ref[...]
    gate = zz[:, Z_LRU:Z_LRU + MIX]
    lx = zz[:, Z_LRU + MIX:Z_LRU + 2 * MIX]
    lconv = _conv_step(lx, lbuf_ref, lbuf_o, lcw_ref, lcb_ref)
    la, lbx = _lru_gates(lconv, wg_ref, bg_ref, lam_ref)
    hl = lbx + la * lst_ref[...]
    lst_o[...] = hl
    ylru_o[...] = hl * jax.nn.gelu(gate)
    u = zz[:, Z_S5:Z_S5 + MIX]
    _s5_project_in(u, wbr_ref, wbi_ref, xr_ref, xi_ref, 1)
    lr, li = pr_ref[0:1, :], pi_ref[0:1, :]
    s0r, s0i = s5r_ref[...], s5i_ref[...]
    xr = xr_ref[...] + (lr * s0r - li * s0i)
    xi = xi_ref[...] + (lr * s0i + li * s0r)
    s5r_o[...] = xr
    s5i_o[...] = xi
    ys5_o[...] = _s5_project_out(xr, xi, u, wcr_ref, wci_ref, s5d_ref, wglu_ref)


def _step_state_kernel(kqv_ref, dec_ref, st_ref, o_st_ref, y_ref):
    h = pl.program_id(0)
    d = dec_ref[pl.ds(h, 1), :]
    v = kqv_ref[2]
    acc = jnp.zeros(v.shape, F32)
    for n in range(st_ref.shape[0]):
        s_new = d * st_ref[n] + kqv_ref[0, n:n + 1, :] * v
        o_st_ref[n] = s_new
        acc = acc + kqv_ref[1, n:n + 1, :] * s_new
    y_ref[...] = acc


def _step_post_kernel(yssd_ref, yret_ref, xs_ref, z_ref, dlane_ref, ng_ref, gn_ref, yssd_o, yret_o):
    zz_gate = z_ref[:, Z_SSD:Z_SSD + MIX]
    y = yssd_ref[...].T + dlane_ref[...] * xs_ref[...]
    y = y * _silu(zz_gate)
    yssd_o[...] = _rmsnorm(y, ng_ref[...])
    rgate = z_ref[:, Z_RET + 3 * MIX:Z_RET + 4 * MIX]
    yret_o[...] = _silu(rgate) * (_group_norm_lanes(yret_ref[...].T, RET_HD) * gn_ref[...])


def _sample_mixers(z, views, big, layer, P, pos):
    Bs = z.shape[0]
    H = SSD_HEADS
    cos_l, sin_l = _rope_tables(np.asarray([pos]))
    gam = jnp.asarray(np.repeat(_ret_gammas()[:, None], Bs, axis=1), F32)
    consts = [cos_l, sin_l, gam, _head_expand(SSD_HEADS, SSD_HD)]
    states = [views[k] for k in ("ssd_conv", "lru_conv", "lru", "s5_re", "s5_im")]
    params = [P[k] for k in ("ssd_cw", "ssd_cb", "ssd_dtb", "ssd_alog",
                             "lru_cw", "lru_cb", "lru_wg", "lru_bg", "lru_lam") + S5_PARAMS]
    pre_out = (jax.ShapeDtypeStruct((2, 3, MIX, Bs), F32),
               jax.ShapeDtypeStruct((2, H, Bs), F32),
               jax.ShapeDtypeStruct(views["ssd_conv"].shape[1:], F32),
               jax.ShapeDtypeStruct(views["lru_conv"].shape[1:], F32),
               jax.ShapeDtypeStruct((Bs, MIX), F32),
               jax.ShapeDtypeStruct((Bs, S5_CH), F32), jax.ShapeDtypeStruct((Bs, S5_CH), F32),
               jax.ShapeDtypeStruct((Bs, MIX), F32), jax.ShapeDtypeStruct((Bs, MIX), F32),
               jax.ShapeDtypeStruct((Bs, MIX), F32))
    (kqv, dec, sbuf_n, lbuf_n, lst_n, s5r_n, s5i_n, y_s5, y_lru, xs) = pl.pallas_call(
        _step_pre_kernel,
        out_shape=pre_out,
        grid=(1,),
        in_specs=[_const_spec(a) for a in consts] + [_const_spec(z)]
                 + [_lspec(a, layer) for a in states] + [_lspec(a, layer) for a in params],
        out_specs=tuple(pl.BlockSpec(o.shape, lambda i, n=len(o.shape): (0,) * n) for o in pre_out),
        scratch_shapes=[pltpu.VMEM((Bs, S5_CH), F32), pltpu.VMEM((Bs, S5_CH), F32)],
        compiler_params=_cparams("arbitrary"),
        name="step_pre",
    )(*consts, z, *states, *params)

    new_big, yts = {}, []
    for m, name in enumerate(("ssd", "ret")):
        s_new, y_t = pl.pallas_call(
            _step_state_kernel,
            out_shape=(jax.ShapeDtypeStruct(big[name].shape, F32), jax.ShapeDtypeStruct((MIX, Bs), F32)),
            grid=(H,),
            in_specs=[pl.BlockSpec((None, 3, SSD_STATE, Bs), lambda h, m=m: (m, 0, h, 0)),
                      pl.BlockSpec((None, H, Bs), lambda h, m=m: (m, 0, 0)),
                      pl.BlockSpec((None, None, SSD_STATE, SSD_HD, Bs), lambda h: (layer, h, 0, 0, 0))],
            out_specs=(pl.BlockSpec((None, None, SSD_STATE, SSD_HD, Bs), lambda h: (layer, h, 0, 0, 0)),
                       pl.BlockSpec((SSD_HD, Bs), lambda h: (h, 0))),
            input_output_aliases={2: 0},
            compiler_params=_cparams("parallel"),
            name="step_state_" + name,
        )(kqv, dec, big[name])
        new_big[name] = s_new
        yts.append(y_t)

    post_in = [yts[0], yts[1], xs, z]
    post_par = [P[k] for k in ("ssd_dlane", "ssd_norm", "ret_gn")]
    post_out = (jax.ShapeDtypeStruct((Bs, MIX), F32), jax.ShapeDtypeStruct((Bs, MIX), F32))
    y_ssd, y_ret = pl.pallas_call(
        _step_post_kernel,
        out_shape=post_out,
        grid=(1,),
        in_specs=[_const_spec(a) for a in post_in] + [_lspec(a, layer) for a in post_par],
        out_specs=tuple(pl.BlockSpec(o.shape, lambda i: (0, 0)) for o in post_out),
        compiler_params=_cparams("arbitrary"),
        name="step_post",
    )(*post_in, *post_par)

    new = dict(ssd_conv=sbuf_n, lru_conv=lbuf_n, lru=lst_n, s5_re=s5r_n, s5_im=s5i_n)
    return (y_ssd, y_s5, y_lru, y_ret), new, new_big


def _block_diag8(w):
    lead = w.shape[:-3]
    n, r, c = w.shape[-3:]
    eye = jnp.eye(n, dtype=w.dtype)
    out = w[..., :, :, None, :] * eye[:, None, :, None]
    return out.reshape(lead + (n * r, n * c))


def _row(v):
    return v[:, None, :]


def _cmul(ar, ai, br, bi):
    return ar * br - ai * bi, ar * bi + ai * br


def _prep_params(W):
    depth = W["w_in"].shape[0]
    P = {}
    w_in = W["w_in"]
    P["w_in"] = jnp.concatenate(
        [w_in[:, :, 4872:8968], w_in[:, :, 2824:4872], w_in[:, :, 1800:2824], w_in[:, :, 1288:1800],
         w_in[:, :, 0:1288], jnp.zeros((depth, D_MODEL, Z_WIDTH - 8968), w_in.dtype)], axis=2).astype(BF16)
    P["norm_mix"] = _row(W["norm_mix"])
    pad = ((0, 0), (0, LANES - SSD_HEADS))
    P["ssd_cw"] = W["ssd_conv_w"]
    P["ssd_cb"] = _row(W["ssd_conv_b"])
    P["ssd_dtb"] = _row(jnp.pad(W["ssd_dt_bias"], pad))
    P["ssd_alog"] = _row(jnp.pad(W["ssd_a_log"], pad))
    P["ssd_dlane"] = _row(jnp.repeat(W["ssd_d"], SSD_HD, axis=1))
    P["ssd_norm"] = _row(W["ssd_norm"])
    lr, li = W["s5_lambda_re"], W["s5_lambda_im"]
    dt = jnp.exp(W["s5_log_dt"])[:, :, None]
    mag = jnp.exp(lr * dt)
    br, bi = mag * jnp.cos(li * dt), mag * jnp.sin(li * dt)
    den = lr * lr + li * li
    qr, qi = _cmul(br - 1.0, bi, lr / den, -li / den)
    wr, wi = _cmul(qr[..., None], qi[..., None], W["s5_b_re"], W["s5_b_im"])
    lag_r, lag_i = [wr], [wi]
    for _ in range(S5_LAGS - 1):
        nr, ni = _cmul(lag_r[-1], lag_i[-1], br[..., None], bi[..., None])
        lag_r.append(nr)
        lag_i.append(ni)
    gb = S5_GROUPS // S5_LB

    def embed_in(lags):
        m = jnp.stack(lags, axis=1).reshape(depth, len(lags), S5_LB, gb, S5_STATE, S5_GDIM)
        m = _block_diag8(jnp.swapaxes(m, -1, -2))
        return jnp.swapaxes(m, 1, 2).reshape(depth, S5_LB, len(lags) * gb * S5_GDIM, gb * S5_STATE)

    def embed_out(m):
        m = m.reshape(depth, S5_LB, gb, S5_GDIM, S5_STATE)
        return _block_diag8(jnp.swapaxes(m, -1, -2))

    P["s5_wbr"] = embed_in(lag_r).astype(BF16)
    P["s5_wbi"] = embed_in(lag_i).astype(BF16)
    P["s5_wcr"] = embed_out(W["s5_c_re"]).astype(BF16)
    P["s5_wci"] = embed_out(W["s5_c_im"]).astype(BF16)
    pr, pi = [br.reshape(depth, 1, S5_CH)], [bi.reshape(depth, 1, S5_CH)]
    for _ in range(SUBLANES - 1):
        nr, ni = _cmul(pr[-1], pi[-1], pr[0], pi[0])
        pr.append(nr)
        pi.append(ni)
    P["s5_pr"] = jnp.concatenate(pr, axis=1)
    P["s5_pi"] = jnp.concatenate(pi, axis=1)
    half = jnp.zeros((depth, S5_LAGS, S5_CH), F32)
    P["s5_qr"] = jnp.concatenate([half, P["s5_pr"][:, :SUBLANES - S5_LAGS]], axis=1)
    P["s5_qi"] = jnp.concatenate([half, P["s5_pi"][:, :SUBLANES - S5_LAGS]], axis=1)
    P["s5_d"] = W["s5_d"].reshape(depth, 1, MIX)
    P["s5_glu"] = W["s5_glu"].astype(BF16)
    P["lru_cw"] = W["lru_conv_w"]
    P["lru_cb"] = _row(W["lru_conv_b"])
    P["lru_wg"] = jnp.concatenate([_block_diag8(W["lru_wa"]), _block_diag8(W["lru_wx"])], axis=2).astype(BF16)
    P["lru_bg"] = _row(jnp.concatenate([W["lru_ba"], W["lru_bx"]], axis=1))
    P["lru_lam"] = _row(W["lru_lambda"])
    P["ret_gn"] = _row(W["ret_gn"])
    P["w_branch"] = W["w_branch"].astype(BF16)
    P["w_out"] = W["w_out"].astype(BF16)
    P["norm_ffn"] = _row(W["norm_ffn"])
    P["norm_final"] = W["norm_final"].reshape(1, D_MODEL)
    for k in ("ffn_w1", "ffn_w3", "ffn_w2", "moe_w1", "moe_w3", "moe_w2"):
        P[k] = W[k].astype(BF16)
    P["moe_router"] = jnp.pad(W["moe_router"], ((0, 0), (0, 0), (0, LANES - N_EXPERTS)))
    return P


def _trunk_prompt(x, P):
    B, L, _ = x.shape
    T = B * L
    depth = P["w_in"].shape[0]
    h = x.reshape(T, D_MODEL)
    new = {k: [] for k in ("ssd", "ssd_conv", "s5_re", "s5_im", "lru", "lru_conv", "ret")}
    for l in range(depth):
        z = _inproj(h, P, l, min(1024, T), BF16)
        z3 = z.reshape(B, L, Z_WIDTH)
        y_ssd, s_ssd, buf_ssd = _ssd_prompt(z3, P, l)
        y_s5, s5r, s5i = _s5_prompt(z3, P, l)
        y_lru, s_lru, buf_lru = _lru_prompt(z3, P, l)
        y_ret, s_ret = _ret_prompt(z3, P, l)
        ys = tuple(y.reshape(T, MIX) for y in (y_ssd, y_s5, y_lru, y_ret))
        h = _merge(ys, z, h, P, l, min(256, T))
        mixer = _moe_routed if (l % 2 == 1 and T % MOE_ROWS == 0) else _ffn
        h = mixer(h, P, l, min(512, T), final_norm=(l == depth - 1))
        new["ssd"].append(s_ssd)
        new["ssd_conv"].append(buf_ssd)
        new["s5_re"].append(s5r.reshape(B, S5_GROUPS, S5_STATE))
        new["s5_im"].append(s5i.reshape(B, S5_GROUPS, S5_STATE))
        new["lru"].append(s_lru.reshape(B, MIX))
        new["lru_conv"].append(buf_lru)
        new["ret"].append(s_ret)
    return h.reshape(B, L, D_MODEL), {k: jnp.stack(v) for k, v in new.items()}


def _trunk_sample(x, pos, st, P):
    Bs = x.shape[0]
    depth = P["w_in"].shape[0]
    h = x.reshape(Bs, D_MODEL)
    big = dict(ssd=jnp.transpose(st["ssd"], (0, 2, 3, 4, 1)), ret=jnp.transpose(st["ret"], (0, 2, 3, 4, 1)))
    views = dict(ssd_conv=jnp.transpose(st["ssd_conv"], (0, 2, 1, 3)),
                 lru_conv=jnp.transpose(st["lru_conv"], (0, 2, 1, 3)),
                 lru=st["lru"],
                 s5_re=st["s5_re"].reshape(depth, Bs, S5_CH),
                 s5_im=st["s5_im"].reshape(depth, Bs, S5_CH))
    new = {k: [] for k in views}
    for l in range(depth):
        z = _inproj(h, P, l, Bs, F32)
        ys, nl, big = _sample_mixers(z, views, big, l, P, pos)
        h = _merge(ys, z, h, P, l, Bs)
        h = _ffn(h, P, l, Bs, final_norm=(l == depth - 1))
        for k in new:
            new[k].append(nl[k])
    out = {k: jnp.stack(v) for k, v in new.items()}
    out["ssd_conv"] = jnp.transpose(out["ssd_conv"], (0, 2, 1, 3))
    out["lru_conv"] = jnp.transpose(out["lru_conv"], (0, 2, 1, 3))
    out["s5_re"] = out["s5_re"].reshape(st["s5_re"].shape)
    out["s5_im"] = out["s5_im"].reshape(st["s5_im"].shape)
    out["ssd"] = jnp.transpose(big["ssd"], (0, 4, 1, 2, 3))
    out["ret"] = jnp.transpose(big["ret"], (0, 4, 1, 2, 3))
    return h.reshape(Bs, 1, D_MODEL), out


def kernel(x_prompt, x_sample, state_ssd, state_ssd_conv, state_s5_re, state_s5_im, state_lru, state_lru_conv, state_ret, norm_mix, w_in, ssd_conv_w, ssd_conv_b, ssd_dt_bias, ssd_a_log, ssd_d, ssd_norm, s5_lambda_re, s5_lambda_im, s5_b_re, s5_b_im, s5_c_re, s5_c_im, s5_d, s5_log_dt, s5_glu, lru_conv_w, lru_conv_b, lru_wa, lru_ba, lru_wx, lru_bx, lru_lambda, ret_gn, w_branch, w_out, norm_ffn, ffn_w1, ffn_w3, ffn_w2, moe_router, moe_w1, moe_w3, moe_w2, norm_final):
    W = dict(norm_mix=norm_mix, w_in=w_in, ssd_conv_w=ssd_conv_w, ssd_conv_b=ssd_conv_b, ssd_dt_bias=ssd_dt_bias,
             ssd_a_log=ssd_a_log, ssd_d=ssd_d, ssd_norm=ssd_norm, s5_lambda_re=s5_lambda_re,
             s5_lambda_im=s5_lambda_im, s5_b_re=s5_b_re, s5_b_im=s5_b_im, s5_c_re=s5_c_re, s5_c_im=s5_c_im,
             s5_d=s5_d, s5_log_dt=s5_log_dt, s5_glu=s5_glu, lru_conv_w=lru_conv_w, lru_conv_b=lru_conv_b,
             lru_wa=lru_wa, lru_ba=lru_ba, lru_wx=lru_wx, lru_bx=lru_bx, lru_lambda=lru_lambda, ret_gn=ret_gn,
             w_branch=w_branch, w_out=w_out, norm_ffn=norm_ffn, moe_router=moe_router, norm_final=norm_final,
             ffn_w1=ffn_w1, ffn_w3=ffn_w3, ffn_w2=ffn_w2, moe_w1=moe_w1, moe_w3=moe_w3, moe_w2=moe_w2)
    P = _prep_params(W)
    y_p, sp = _trunk_prompt(x_prompt, P)
    st = dict(ssd=state_ssd, ssd_conv=state_ssd_conv, s5_re=state_s5_re, s5_im=state_s5_im,
              lru=state_lru, lru_conv=state_lru_conv, ret=state_ret)
    past_len = 16384
    y_s, ss = _trunk_sample(x_sample, past_len, st, P)
    names = ("ssd", "ssd_conv", "s5_re", "s5_im", "lru", "lru_conv", "ret")
    return (y_p, y_s) + tuple(sp[n] for n in names) + tuple(ss[n] for n in names)
```

```python
import functools
import math

import jax
import jax.numpy as jnp
import numpy as np
from jax import lax
from jax.experimental import pallas as pl
from jax.experimental.pallas import tpu as pltpu

F32 = jnp.float32
BF16 = jnp.bfloat16
EPS = 1e-6

D_MODEL = 1024
MIX = 512
CONV_K = 4
CHUNK = 128
SSD_HEADS = 8
SSD_HD = 64
SSD_STATE = 64
SSD_GROUPS = 2
SSD_CONV = MIX + 2 * SSD_GROUPS * SSD_STATE
S5_GROUPS = 32
S5_GDIM = 16
S5_STATE = 64
S5_CH = S5_GROUPS * S5_STATE
LRU_BLOCKS = 8
LRU_C = 8.0
RET_HEADS = 8
RET_HD = 64
ROPE_BASE = 10000.0
N_EXPERTS = 8
D_FF_TILE = 1408

Z_MERGE = 0
Z_RET = 4096
Z_LRU = 6144
Z_S5 = 7168
Z_SSD = 7680
Z_WIDTH = 9216

VMEM_LIMIT = 56 * 1024 * 1024
LANES = 128
SUBLANES = 8


def _cparams(*sem):
    return pltpu.CompilerParams(dimension_semantics=sem, vmem_limit_bytes=VMEM_LIMIT)


def _bdot(a, b):
    return jnp.dot(a.astype(BF16), b.astype(BF16), preferred_element_type=F32)


def _bdot_nt(a, b):
    return lax.dot_general(a.astype(BF16), b.astype(BF16), (((1,), (1,)), ((), ())), preferred_element_type=F32)


def _bdot_tn(a, b):
    return lax.dot_general(a.astype(BF16), b.astype(BF16), (((0,), (0,)), ((), ())), preferred_element_type=F32)


def _hdot(a, b):
    return jnp.dot(a, b, precision=lax.Precision.HIGHEST, preferred_element_type=F32)


def _split3_dot(x, m01):
    hi = x.astype(BF16)
    r1 = x - hi.astype(F32)
    mid = r1.astype(BF16)
    lo = (r1 - mid.astype(F32)).astype(BF16)
    m = m01.astype(BF16)
    d = functools.partial(jnp.dot, preferred_element_type=F32)
    return (d(lo, m) + d(mid, m)) + d(hi, m)


def _rmsnorm(x, g):
    ms = jnp.mean(x * x, axis=-1, keepdims=True)
    return x * lax.rsqrt(ms + EPS) * g


def _silu(x):
    return x * jax.nn.sigmoid(x)


def _neg_expm1_2x(log_a, a):
    return jnp.tanh(-log_a) * (1.0 + a * a)


def _inproj_kernel(x_ref, g_ref, w_ref, o_ref, hn_ref):
    @pl.when(pl.program_id(1) == 0)
    def _():
        hn_ref[...] = _rmsnorm(x_ref[...], g_ref[...]).astype(BF16)

    z = lax.dot_general(hn_ref[...], w_ref[...], (((1,), (1,)), ((), ())), preferred_element_type=F32)
    o_ref[...] = z.astype(o_ref.dtype)


def _lspec(a, l):
    rest = tuple(a.shape[1:])
    return pl.BlockSpec((None,) + rest, lambda *_: (l,) + (0,) * len(rest))


def _inproj(x, P, l, tm, out_dtype):
    T = x.shape[0]
    tn = 1536
    return pl.pallas_call(
        _inproj_kernel,
        out_shape=jax.ShapeDtypeStruct((T, Z_WIDTH), out_dtype),
        grid=(T // tm, Z_WIDTH // tn),
        in_specs=[pl.BlockSpec((tm, D_MODEL), lambda i, j: (i, 0)),
                  _lspec(P["norm_mix"], l),
                  pl.BlockSpec((None, tn, D_MODEL), lambda i, j: (l, j, 0))],
        out_specs=pl.BlockSpec((tm, tn), lambda i, j: (i, j)),
        scratch_shapes=[pltpu.VMEM((tm, D_MODEL), BF16)],
        compiler_params=_cparams("parallel", "arbitrary"),
        name="inproj",
    )(x, P["norm_mix"], P["w_in"])


def _conv_chunk(c, x, pad_ref, w_ref, b_ref):
    Lc = x.shape[0]

    @pl.when(c == 0)
    def _():
        pad_ref[0:8, :] = jnp.zeros((8, x.shape[1]), F32)

    pad_ref[8:8 + Lc, :] = x
    out = b_ref[...] + pad_ref[5:5 + Lc, :] * w_ref[0:1, :]
    out = out + pad_ref[6:6 + Lc, :] * w_ref[1:2, :]
    out = out + pad_ref[7:7 + Lc, :] * w_ref[2:3, :]
    out = out + x * w_ref[3:4, :]
    return out


def _conv_finish(pad_ref, Lc):
    pad_ref[0:8, :] = pad_ref[Lc:Lc + 8, :]


def _head_masks(nh, hd, L):
    rows = (np.arange(nh * L)[:, None] // L) == (np.arange(nh * hd)[None, :] // hd)
    diag = (np.arange(nh * hd)[:, None] // hd) == (np.arange(nh * hd)[None, :] // hd)
    return jnp.asarray(rows, BF16), jnp.asarray(diag, F32)


def _head_block_rows(x, nh, mask):
    return jnp.concatenate([x.astype(BF16)] * nh, axis=0) * mask


def _pow2_div(x, d):
    return lax.shift_right_logical(x, jnp.int32(int(math.log2(d))))


def _head_block_rows_sel(x, nh, hd):
    L = x.shape[0]
    xt = jnp.concatenate([x.astype(BF16)] * nh, axis=0)
    row = lax.broadcasted_iota(jnp.int32, xt.shape, 0)
    col = lax.broadcasted_iota(jnp.int32, xt.shape, 1)
    return jnp.where(_pow2_div(row, L) == _pow2_div(col, hd), xt, jnp.zeros_like(xt))


def _head_block_cols(xt, nh, hd):
    L = xt.shape[1]
    xc = jnp.concatenate([xt.astype(BF16)] * nh, axis=1)
    row = lax.broadcasted_iota(jnp.int32, xc.shape, 0)
    col = lax.broadcasted_iota(jnp.int32, xc.shape, 1)
    return jnp.where(_pow2_div(row, hd) == _pow2_div(col, L), xc, jnp.zeros_like(xc))


def _group_repeat_lanes(m, rep):
    lane = lax.broadcasted_iota(jnp.int32, m.shape, 1)
    swapped = pltpu.roll(m, m.shape[1] // 2, 1)
    low = lane < m.shape[1] // 2
    g0 = jnp.where(low, m, swapped)
    g1 = jnp.where(low, swapped, m)
    return jnp.concatenate([g0] * (rep // 2) + [g1] * (rep // 2), axis=1)


def _ssd_kernel(z_ref, cw_ref, cb_ref, dtb_ref, alog_ref, dlane_ref, ng_ref, tri_ref, ehj_ref, ehn_ref,
                cbias_ref, hrows_ref, hdiag_ref, y_ref, sto_ref, buf_ref, pad_ref, st_ref):
    c = pl.program_id(1)
    nc = pl.num_programs(1)
    Lc = z_ref.shape[0]
    zz = z_ref[...].astype(F32)
    zgate = zz[:, 0:MIX]
    xbc = zz[:, MIX:MIX + SSD_CONV]
    dt_raw = zz[:, MIX + SSD_CONV:MIX + SSD_CONV + LANES]

    @pl.when(c == 0)
    def _():
        st_ref[...] = jnp.zeros(st_ref.shape, F32)

    conv = _conv_chunk(c, xbc, pad_ref, cw_ref, cb_ref)
    xc = _silu(conv)
    xs = xc[:, 0:MIX]
    nbc = SSD_GROUPS * SSD_STATE
    bm = xc[:, MIX:MIX + nbc]
    cm = xc[:, MIX + nbc:MIX + 2 * nbc]
    dt = jax.nn.softplus(dt_raw + dtb_ref[...])
    a = -jnp.exp(alog_ref[...])
    ld = dt * a
    acum = _hdot(tri_ref[...], ld)
    acum_t = acum.T
    dt_t = dt.T
    a_row = jnp.concatenate([acum_t[h:h + 1, :] for h in range(SSD_HEADS)], axis=1)
    dt_row = jnp.concatenate([dt_t[h:h + 1, :] for h in range(SSD_HEADS)], axis=1)
    a_col = _split3_dot(acum, ehj_ref[...])
    decay = jnp.exp((a_col - a_row) + cbias_ref[...])
    rep = SSD_HEADS // SSD_GROUPS
    gmats = []
    for g in range(SSD_GROUPS):
        cg = cm[:, g * SSD_STATE:(g + 1) * SSD_STATE]
        bg = bm[:, g * SSD_STATE:(g + 1) * SSD_STATE]
        gmats.append(_bdot_nt(cg, bg))
    g_all = jnp.concatenate([gmats[h // rep] for h in range(SSD_HEADS)], axis=1)
    m_all = g_all * decay * dt_row
    y = _bdot(m_all, _head_block_rows(xs, SSD_HEADS, hrows_ref[...]))
    exp_a = jnp.exp(acum)
    exp_a_l = _split3_dot(exp_a, ehn_ref[...])
    c_rep = _group_repeat_lanes(cm, rep)
    y = y + _bdot_nt(c_rep * exp_a_l, st_ref[...])
    a_last = acum[Lc - 1:Lc, :]
    w_end_l = _split3_dot(jnp.exp(a_last - acum) * dt, ehn_ref[...])
    upd = _bdot_tn(xs, _group_repeat_lanes(bm, rep) * w_end_l)
    st_ref[...] = exp_a_l[Lc - 1:Lc, :] * st_ref[...] + upd * hdiag_ref[...]
    y = y + dlane_ref[...] * xs
    y = y * _silu(zgate)
    y_ref[...] = _rmsnorm(y, ng_ref[...]).astype(y_ref.dtype)

    @pl.when(c == nc - 1)
    def _():
        buf_ref[...] = pad_ref[Lc + 8 - (CONV_K - 1):Lc + 8, :]
        s_t = st_ref[...].T
        for h in range(SSD_HEADS):
            sl = slice(h * SSD_HD, (h + 1) * SSD_HD)
            sto_ref[h] = s_t[sl, sl]

    _conv_finish(pad_ref, Lc)


def _const_spec(a):
    return pl.BlockSpec(a.shape, lambda *_: (0,) * a.ndim)


def _ssd_prompt(z3, P, l):
    B, L, _ = z3.shape
    Lc = CHUNK
    params = [P[k] for k in ("ssd_cw", "ssd_cb", "ssd_dtb", "ssd_alog", "ssd_dlane", "ssd_norm")]
    j = np.arange(SSD_HEADS * Lc) % Lc
    cbias = jnp.asarray(np.where(np.arange(Lc)[:, None] >= j[None, :], 0.0, -1e30), F32)
    consts = [_tri(Lc), _head_expand(SSD_HEADS, Lc), _head_expand(SSD_HEADS, SSD_HD), cbias,
              *_head_masks(SSD_HEADS, SSD_HD, Lc)]
    return pl.pallas_call(
        _ssd_kernel,
        out_shape=(jax.ShapeDtypeStruct((B, L, MIX), BF16),
                   jax.ShapeDtypeStruct((B, SSD_HEADS, SSD_STATE, SSD_HD), F32),
                   jax.ShapeDtypeStruct((B, CONV_K - 1, SSD_CONV), F32)),
        grid=(B, L // Lc),
        in_specs=[pl.BlockSpec((None, Lc, 1536), lambda b, c: (b, c, Z_SSD // 1536))]
                 + [_lspec(a, l) for a in params] + [_const_spec(a) for a in consts],
        out_specs=(pl.BlockSpec((None, Lc, MIX), lambda b, c: (b, c, 0)),
                   pl.BlockSpec((None, SSD_HEADS, SSD_STATE, SSD_HD), lambda b, c: (b, 0, 0, 0)),
                   pl.BlockSpec((None, CONV_K - 1, SSD_CONV), lambda b, c: (b, 0, 0))),
        scratch_shapes=[pltpu.VMEM((Lc + 8, SSD_CONV), F32), pltpu.VMEM((MIX, MIX), F32)],
        compiler_params=_cparams("parallel", "arbitrary"),
        name="ssd_prompt",
    )(z3, *params, *consts)


def _tri(Lc):
    return jnp.asarray(np.tril(np.ones((Lc, Lc), np.float32)))


def _head_expand(nh, width):
    e = np.zeros((LANES, nh * width), np.float32)
    for h in range(nh):
        e[h, h * width:(h + 1) * width] = 1.0
    return jnp.asarray(e, BF16)


def _ret_gammas():
    return 1.0 - np.exp2(-5.0 - np.arange(RET_HEADS, dtype=np.float64))


def _ret_tables(Lc):
    gam = _ret_gammas()
    i = np.arange(Lc)
    d = i[:, None] - i[None, :]
    decay = np.where(d >= 0, gam[:, None, None] ** np.maximum(d, 0)[None], 0.0)
    decay_l = np.transpose(decay, (1, 0, 2)).reshape(Lc, RET_HEADS * Lc)
    grow_l = np.repeat(gam[None, :] ** (i[:, None] + 1), RET_HD, axis=1)
    toend_t = np.repeat(gam[:, None] ** (Lc - 1 - i[None, :]), RET_HD, axis=0)
    hd = np.arange(MIX) // RET_HD
    state_decay = np.where(hd[:, None] == hd[None, :], (gam ** Lc)[hd][:, None], 0.0)
    return tuple(jnp.asarray(t, F32) for t in (decay_l, grow_l, toend_t, state_decay))


def _rope_tables(pos):
    half = RET_HD // 2
    inv = ROPE_BASE ** (-np.arange(half, dtype=np.float64) / half)
    ang = np.asarray(pos, np.float64)[:, None] * inv[None, :]
    cos = np.cos(ang)
    sin = np.sin(ang)
    cos_l = np.tile(np.concatenate([cos, cos], axis=1), (1, RET_HEADS))
    sin_l = np.tile(np.concatenate([-sin, sin], axis=1), (1, RET_HEADS))
    return jnp.asarray(cos_l, F32), jnp.asarray(sin_l, F32)


def _rotary_lanes(x, cos_l, sin_l):
    lane = lax.broadcasted_iota(jnp.int32, x.shape, 1)
    first = (lane & (RET_HD - 1)) < (RET_HD // 2)
    n = x.shape[1]
    swapped = jnp.where(first, pltpu.roll(x, n - RET_HD // 2, 1), pltpu.roll(x, RET_HD // 2, 1))
    return x * cos_l + swapped * sin_l


def _group_norm_head(o):
    mu = jnp.mean(o, axis=-1, keepdims=True)
    d = o - mu
    var = jnp.mean(d * d, axis=-1, keepdims=True)
    return d * lax.rsqrt(var + 1e-5)


def _group_norm_lanes(o, hd):
    cols = []
    for cb in range(o.shape[1] // LANES):
        x = o[:, cb * LANES:(cb + 1) * LANES]
        low = lax.broadcasted_iota(jnp.int32, x.shape, 1) < hd

        def seg_mean(t):
            lo = jnp.sum(jnp.where(low, t, 0.0), axis=1, keepdims=True)
            hi = jnp.sum(jnp.where(low, 0.0, t), axis=1, keepdims=True)
            return jnp.where(low, lo, hi) * (1.0 / hd)

        d = x - seg_mean(x)
        cols.append(d * lax.rsqrt(seg_mean(d * d) + 1e-5))
    return jnp.concatenate(cols, axis=1)


def _ret_kernel(z_ref, cos_ref, sin_ref, dec_ref, grow_ref, toend_ref, sdec_ref, hdiag_ref,
                gn_ref, y_ref, sto_ref, st_ref):
    c = pl.program_id(1)
    nc = pl.num_programs(1)
    zz = z_ref[...].astype(F32)
    q = _rotary_lanes(zz[:, 0:MIX], cos_ref[...], sin_ref[...]) * (RET_HD ** -0.5)
    k = _rotary_lanes(zz[:, MIX:2 * MIX], cos_ref[...], sin_ref[...])
    v = zz[:, 2 * MIX:3 * MIX]
    gate = zz[:, 3 * MIX:4 * MIX]

    @pl.when(c == 0)
    def _():
        st_ref[...] = jnp.zeros(st_ref.shape, F32)

    k_t = k.T
    g_all = _bdot(q, _head_block_cols(k_t, RET_HEADS, RET_HD))
    o = _bdot(g_all * dec_ref[...], _head_block_rows_sel(v, RET_HEADS, RET_HD))
    o = o + _bdot(q * grow_ref[...], st_ref[...])
    upd = _bdot(k_t * toend_ref[...], v)
    st_ref[...] = sdec_ref[...] * st_ref[...] + upd * hdiag_ref[...]
    y_ref[...] = (_silu(gate) * (_group_norm_lanes(o, RET_HD) * gn_ref[...])).astype(y_ref.dtype)

    @pl.when(c == nc - 1)
    def _():
        for h in range(RET_HEADS):
            sl = slice(h * RET_HD, (h + 1) * RET_HD)
            sto_ref[h] = st_ref[sl, sl]


def _ret_prompt(z3, P, l):
    B, L, _ = z3.shape
    Lc = CHUNK
    cos_l, sin_l = _rope_tables(np.arange(L))
    consts = list(_ret_tables(Lc)) + [_head_masks(RET_HEADS, RET_HD, Lc)[1]]
    return pl.pallas_call(
        _ret_kernel,
        out_shape=(jax.ShapeDtypeStruct((B, L, MIX), BF16),
                   jax.ShapeDtypeStruct((B, RET_HEADS, RET_HD, RET_HD), F32)),
        grid=(B, L // Lc),
        in_specs=[pl.BlockSpec((None, Lc, 2048), lambda b, c: (b, c, Z_RET // 2048)),
                  pl.BlockSpec((Lc, MIX), lambda b, c: (c, 0)),
                  pl.BlockSpec((Lc, MIX), lambda b, c: (c, 0))]
                 + [_const_spec(a) for a in consts] + [_lspec(P["ret_gn"], l)],
        out_specs=(pl.BlockSpec((None, Lc, MIX), lambda b, c: (b, c, 0)),
                   pl.BlockSpec((None, RET_HEADS, RET_HD, RET_HD), lambda b, c: (b, 0, 0, 0))),
        scratch_shapes=[pltpu.VMEM((MIX, MIX), F32)],
        compiler_params=_cparams("parallel", "arbitrary"),
        name="ret_prompt",
    )(z3, cos_l, sin_l, *consts, P["ret_gn"])


def _lru_gates(xc, wg_ref, bg_ref, lam_ref):
    rg = _bdot(xc, wg_ref[...]) + bg_ref[...]
    r = jax.nn.sigmoid(rg[:, 0:MIX])
    i = jax.nn.sigmoid(rg[:, MIX:2 * MIX])
    log_a = -LRU_C * r * jax.nn.softplus(-lam_ref[...])
    a = jnp.exp(log_a)
    bx = jnp.sqrt(_neg_expm1_2x(log_a, a)) * (i * xc)
    return a, bx


def _lru_kernel(z_ref, cw_ref, cb_ref, wg_ref, bg_ref, lam_ref,
                y_ref, st_ref, buf_ref, pad_ref, h_ref):
    c = pl.program_id(1)
    nc = pl.num_programs(1)
    Lc = z_ref.shape[0]
    zz = z_ref[...].astype(F32)
    gate = zz[:, 0:MIX]
    x = zz[:, MIX:2 * MIX]

    @pl.when(c == 0)
    def _():
        st_ref[...] = jnp.zeros(st_ref.shape, F32)

    xc = _conv_chunk(c, x, pad_ref, cw_ref, cb_ref)
    a, bx = _lru_gates(xc, wg_ref, bg_ref, lam_ref)
    ng = Lc // SUBLANES
    a3 = a.reshape(ng, SUBLANES, MIX)
    b3 = bx.reshape(ng, SUBLANES, MIX)
    sub = lax.broadcasted_iota(jnp.int32, a3.shape, 1)
    s = 1
    while s < SUBLANES:
        keep = sub >= s
        b3 = jnp.where(keep, b3 + a3 * pltpu.roll(b3, s, 1), b3)
        a3 = jnp.where(keep, a3 * pltpu.roll(a3, s, 1), a3)
        s *= 2
    carry = st_ref[...]
    for g in range(ng):
        hg = b3[g] + a3[g] * jnp.broadcast_to(carry, (SUBLANES, MIX))
        h_ref[g * SUBLANES:(g + 1) * SUBLANES, :] = hg
        carry = hg[SUBLANES - 1:SUBLANES, :]
    st_ref[...] = carry
    y_ref[...] = (h_ref[...] * jax.nn.gelu(gate)).astype(y_ref.dtype)

    @pl.when(c == nc - 1)
    def _():
        buf_ref[...] = pad_ref[Lc + 8 - (CONV_K - 1):Lc + 8, :]

    _conv_finish(pad_ref, Lc)


def _lru_prompt(z3, P, l):
    B, L, _ = z3.shape
    Lc = CHUNK
    params = [P[k] for k in ("lru_cw", "lru_cb", "lru_wg", "lru_bg", "lru_lam")]
    return pl.pallas_call(
        _lru_kernel,
        out_shape=(jax.ShapeDtypeStruct((B, L, MIX), BF16),
                   jax.ShapeDtypeStruct((B, 1, MIX), F32),
                   jax.ShapeDtypeStruct((B, CONV_K - 1, MIX), F32)),
        grid=(B, L // Lc),
        in_specs=[pl.BlockSpec((None, Lc, 1024), lambda b, c: (b, c, Z_LRU // 1024))]
                 + [_lspec(a, l) for a in params],
        out_specs=(pl.BlockSpec((None, Lc, MIX), lambda b, c: (b, c, 0)),
                   pl.BlockSpec((None, 1, MIX), lambda b, c: (b, 0, 0)),
                   pl.BlockSpec((None, CONV_K - 1, MIX), lambda b, c: (b, 0, 0))),
        scratch_shapes=[pltpu.VMEM((Lc + 8, MIX), F32), pltpu.VMEM((Lc, MIX), F32)],
        compiler_params=_cparams("parallel", "arbitrary"),
        name="lru_prompt",
    )(z3, *params)


S5_LB = 4
S5_LAGS = 4
assert 2 * S5_LAGS == SUBLANES


def _s5_project_in(u, wbr_ref, wbi_ref, xr_ref, xi_ref, lags):
    shifted = [u]
    if lags > 1:
        sub = lax.broadcasted_iota(jnp.int32, u.shape, 0) & (lags - 1)
        shifted += [jnp.where(sub >= d, pltpu.roll(u, d, 0), 0.0) for d in range(1, lags)]
    shifted = [s.astype(BF16) for s in shifted]
    for kb in range(S5_LB):
        lhs = jnp.concatenate([s[:, kb * 128:(kb + 1) * 128] for s in shifted], axis=1)
        k = lags * 128
        xr_ref[:, kb * 512:(kb + 1) * 512] = jnp.dot(lhs, wbr_ref[kb, 0:k, :], preferred_element_type=F32)
        xi_ref[:, kb * 512:(kb + 1) * 512] = jnp.dot(lhs, wbi_ref[kb, 0:k, :], preferred_element_type=F32)


def _s5_project_out(xr, xi, u, wcr_ref, wci_ref, d_ref, wglu_ref):
    ys = []
    for kb in range(S5_LB):
        sl = slice(kb * 512, (kb + 1) * 512)
        ys.append(_bdot(xr[:, sl], wcr_ref[kb]) - _bdot(xi[:, sl], wci_ref[kb]))
    y = jnp.concatenate(ys, axis=1) + d_ref[...] * u
    y = jax.nn.gelu(y)
    return y * jax.nn.sigmoid(_bdot(y, wglu_ref[...]))


def _s5_kernel(z_ref, wbr_ref, wbi_ref, wcr_ref, wci_ref, pr_ref, pi_ref, d_ref, wglu_ref, qr_ref, qi_ref,
               y_ref, sr_ref, si_ref, xr_ref, xi_ref):
    c = pl.program_id(1)
    Lc = z_ref.shape[0]
    u = z_ref[...].astype(F32)

    @pl.when(c == 0)
    def _():
        sr_ref[...] = jnp.zeros(sr_ref.shape, F32)
        si_ref[...] = jnp.zeros(si_ref.shape, F32)

    _s5_project_in(u, wbr_ref, wbi_ref, xr_ref, xi_ref, S5_LAGS)
    ng = Lc // SUBLANES
    x3r = xr_ref[...].reshape(ng, SUBLANES, S5_CH)
    x3i = xi_ref[...].reshape(ng, SUBLANES, S5_CH)
    tr = jnp.broadcast_to(x3r[:, S5_LAGS - 1:S5_LAGS, :], x3r.shape)
    ti = jnp.broadcast_to(x3i[:, S5_LAGS - 1:S5_LAGS, :], x3i.shape)
    mr, mi = qr_ref[...][None], qi_ref[...][None]
    x3r, x3i = x3r + (mr * tr - mi * ti), x3i + (mr * ti + mi * tr)
    pcr, pci = pr_ref[...], pi_ref[...]
    cr, ci = sr_ref[...], si_ref[...]
    for g in range(ng):
        br = jnp.broadcast_to(cr, (SUBLANES, S5_CH))
        bi = jnp.broadcast_to(ci, (SUBLANES, S5_CH))
        gr = x3r[g] + (pcr * br - pci * bi)
        gi = x3i[g] + (pcr * bi + pci * br)
        xr_ref[g * SUBLANES:(g + 1) * SUBLANES, :] = gr
        xi_ref[g * SUBLANES:(g + 1) * SUBLANES, :] = gi
        cr, ci = gr[SUBLANES - 1:SUBLANES, :], gi[SUBLANES - 1:SUBLANES, :]
    sr_ref[...] = cr
    si_ref[...] = ci
    y_ref[...] = _s5_project_out(xr_ref[...], xi_ref[...], u, wcr_ref, wci_ref, d_ref, wglu_ref).astype(y_ref.dtype)


S5_PARAMS = ("s5_wbr", "s5_wbi", "s5_wcr", "s5_wci", "s5_pr", "s5_pi", "s5_d", "s5_glu", "s5_qr", "s5_qi")


def _s5_prompt(z3, P, l):
    B, L, _ = z3.shape
    Lc = CHUNK
    params = [P[k] for k in S5_PARAMS]
    return pl.pallas_call(
        _s5_kernel,
        out_shape=(jax.ShapeDtypeStruct((B, L, MIX), BF16),
                   jax.ShapeDtypeStruct((B, 1, S5_CH), F32),
                   jax.ShapeDtypeStruct((B, 1, S5_CH), F32)),
        grid=(B, L // Lc),
        in_specs=[pl.BlockSpec((None, Lc, MIX), lambda b, c: (b, c, Z_S5 // MIX))]
                 + [_lspec(a, l) for a in params],
        out_specs=(pl.BlockSpec((None, Lc, MIX), lambda b, c: (b, c, 0)),
                   pl.BlockSpec((None, 1, S5_CH), lambda b, c: (b, 0, 0)),
                   pl.BlockSpec((None, 1, S5_CH), lambda b, c: (b, 0, 0))),
        scratch_shapes=[pltpu.VMEM((Lc, S5_CH), F32), pltpu.VMEM((Lc, S5_CH), F32)],
        compiler_params=_cparams("parallel", "arbitrary"),
        name="s5_prompt",
    )(z3, *params)


def _merge_kernel(y0_ref, y1_ref, y2_ref, y3_ref, zg_ref, h_ref, wb_ref, wo_ref, o_ref):
    acc = None
    for k, y_ref in enumerate((y0_ref, y1_ref, y2_ref, y3_ref)):
        br = _bdot(y_ref[...], wb_ref[k])
        t = jax.nn.sigmoid(zg_ref[:, k * D_MODEL:(k + 1) * D_MODEL].astype(F32)) * br
        acc = t if acc is None else acc + t
    o_ref[...] = h_ref[...] + _bdot(acc, wo_ref[...])


def _merge(ys, z, h, P, l, tm):
    T = h.shape[0]
    rows = lambda w: pl.BlockSpec((tm, w), lambda i: (i, 0))
    return pl.pallas_call(
        _merge_kernel,
        out_shape=jax.ShapeDtypeStruct((T, D_MODEL), F32),
        grid=(T // tm,),
        in_specs=[rows(MIX), rows(MIX), rows(MIX), rows(MIX),
                  pl.BlockSpec((tm, 4 * D_MODEL), lambda i: (i, Z_MERGE)),
                  rows(D_MODEL),
                  _lspec(P["w_branch"], l), _lspec(P["w_out"], l)],
        out_specs=rows(D_MODEL),
        compiler_params=_cparams("parallel"),
        name="merge",
    )(*ys, z, h, P["w_branch"], P["w_out"])


def _top2_gates(logits):
    lane, i1, i2, w1, w2 = _top2(logits)
    return jnp.where(lane == i1, w1, 0.0) + jnp.where(lane == i2, w2, 0.0)


def _ffn_kernel(moe, final_norm, *refs):
    if moe:
        h_ref, g_ref, rt_ref, w1_ref, w3_ref, w2_ref, gf_ref, o_ref, hn_ref, acc_ref, gate_ref = refs
    else:
        h_ref, g_ref, w1_ref, w3_ref, w2_ref, gf_ref, o_ref, hn_ref, acc_ref = refs
    e = pl.program_id(1)
    ne = pl.num_programs(1)

    @pl.when(e == 0)
    def _():
        hn = _rmsnorm(h_ref[...], g_ref[...])
        hn_ref[...] = hn.astype(BF16)
        acc_ref[...] = jnp.zeros(acc_ref.shape, F32)
        if moe:
            lane = lax.broadcasted_iota(jnp.int32, (hn.shape[0], LANES), 1)
            logits = jnp.where(lane < N_EXPERTS, _hdot(hn, rt_ref[...]), -jnp.inf)
            gate_ref[...] = _top2_gates(logits)

    hn = hn_ref[...]
    a = jnp.dot(hn, w1_ref[...], preferred_element_type=F32)
    b = jnp.dot(hn, w3_ref[...], preferred_element_type=F32)
    o = _bdot(_silu(a) * b, w2_ref[...])
    if moe:
        lane = lax.broadcasted_iota(jnp.int32, gate_ref.shape, 1)
        ge = jnp.sum(jnp.where(lane == e, gate_ref[...], 0.0), axis=-1, keepdims=True)
        o = ge * o
    acc_ref[...] += o

    @pl.when(e == ne - 1)
    def _():
        out = h_ref[...] + acc_ref[...]
        if final_norm:
            out = _rmsnorm(out, gf_ref[...])
        o_ref[...] = out


def _ffn(h, P, l, tm, final_norm):
    T = h.shape[0]
    moe = l % 2 == 1
    j = l // 2
    tf = D_FF_TILE
    row_spec = pl.BlockSpec((tm, D_MODEL), lambda i, e: (i, 0))
    gfinal = P["norm_final"]
    if moe:
        w1, w3, w2 = P["moe_w1"], P["moe_w3"], P["moe_w2"]
        ne = w1.shape[1]
        wspecs = [pl.BlockSpec((None, None, D_MODEL, tf), lambda i, e: (j, e, 0, 0)),
                  pl.BlockSpec((None, None, D_MODEL, tf), lambda i, e: (j, e, 0, 0)),
                  pl.BlockSpec((None, None, tf, D_MODEL), lambda i, e: (j, e, 0, 0))]
        in_specs = ([row_spec, _lspec(P["norm_ffn"], l), _lspec(P["moe_router"], j)] + wspecs
                    + [_const_spec(gfinal)])
        args = (h, P["norm_ffn"], P["moe_router"], w1, w3, w2, gfinal)
        scratch = [pltpu.VMEM((tm, D_MODEL), BF16), pltpu.VMEM((tm, D_MODEL), F32), pltpu.VMEM((tm, LANES), F32)]
    else:
        w1, w3, w2 = P["ffn_w1"], P["ffn_w3"], P["ffn_w2"]
        ne = w1.shape[2] // tf
        wspecs = [pl.BlockSpec((None, D_MODEL, tf), lambda i, e: (j, 0, e)),
                  pl.BlockSpec((None, D_MODEL, tf), lambda i, e: (j, 0, e)),
                  pl.BlockSpec((None, tf, D_MODEL), lambda i, e: (j, e, 0))]
        in_specs = [row_spec, _lspec(P["norm_ffn"], l)] + wspecs + [_const_spec(gfinal)]
        args = (h, P["norm_ffn"], w1, w3, w2, gfinal)
        scratch = [pltpu.VMEM((tm, D_MODEL), BF16), pltpu.VMEM((tm, D_MODEL), F32)]
    return pl.pallas_call(
        functools.partial(_ffn_kernel, moe, final_norm),
        out_shape=jax.ShapeDtypeStruct((T, D_MODEL), F32),
        grid=(T // tm, ne),
        in_specs=in_specs,
        out_specs=row_spec,
        scratch_shapes=scratch,
        compiler_params=_cparams("parallel", "arbitrary"),
        name="moe" if moe else "ffn",
    )(*args)


MOE_ROWS = 512
ROUTE_LANES = ("e1", "e2", "r1", "r2", "w1", "w2")


def _top2(logits):
    lane = lax.broadcasted_iota(jnp.int32, logits.shape, 1).astype(F32)
    big = float(LANES)
    m1 = jnp.max(logits, axis=-1, keepdims=True)
    i1 = jnp.min(jnp.where(logits == m1, lane, big), axis=-1, keepdims=True)
    rest = jnp.where(lane == i1, -jnp.inf, logits)
    m2 = jnp.max(rest, axis=-1, keepdims=True)
    i2 = jnp.min(jnp.where(rest == m2, lane, big), axis=-1, keepdims=True)
    e2 = jnp.exp(m2 - m1)
    den = 1.0 + e2
    return lane, i1, i2, 1.0 / den, e2 / den


def _moe_route_kernel(h_ref, g_ref, rt_ref, ltri_ref, hn_ref, info_ref, cnt_ref, base_ref):
    i = pl.program_id(0)

    @pl.when(i == 0)
    def _():
        base_ref[...] = jnp.zeros(base_ref.shape, F32)

    hn = _rmsnorm(h_ref[...], g_ref[...])
    hn_ref[...] = hn
    lane_i = lax.broadcasted_iota(jnp.int32, (hn.shape[0], LANES), 1)
    logits = jnp.where(lane_i < N_EXPERTS, _hdot(hn, rt_ref[...]), -jnp.inf)
    lane, i1, i2, w1, w2 = _top2(logits)
    oh1 = (lane == i1).astype(F32)
    oh2 = (lane == i2).astype(F32)
    oh = oh1 + oh2
    before = jnp.dot(ltri_ref[...], oh.astype(BF16), preferred_element_type=F32)
    rank = base_ref[...] + before
    r1 = jnp.sum(oh1 * rank, axis=-1, keepdims=True)
    r2 = jnp.sum(oh2 * rank, axis=-1, keepdims=True)
    base_ref[...] += jnp.sum(oh, axis=0, keepdims=True)
    fields = dict(e1=i1, e2=i2, r1=r1, r2=r2, w1=w1, w2=w2)
    info = jnp.zeros(lane.shape, F32)
    for k, name in enumerate(ROUTE_LANES):
        info = jnp.where(lane_i == k, fields[name], info)
    info_ref[...] = info

    @pl.when(i == pl.num_programs(0) - 1)
    def _():
        cnt_ref[...] = base_ref[...]


def _moe_route(h, P, l, tm):
    T = h.shape[0]
    j = l // 2
    ltri = jnp.asarray(np.tril(np.ones((tm, tm), np.float32), -1), BF16)
    return pl.pallas_call(
        _moe_route_kernel,
        out_shape=(jax.ShapeDtypeStruct((T, D_MODEL), F32), jax.ShapeDtypeStruct((T, LANES), F32),
                   jax.ShapeDtypeStruct((1, LANES), F32)),
        grid=(T // tm,),
        in_specs=[pl.BlockSpec((tm, D_MODEL), lambda i: (i, 0)), _lspec(P["norm_ffn"], l),
                  _lspec(P["moe_router"], j), _const_spec(ltri)],
        out_specs=(pl.BlockSpec((tm, D_MODEL), lambda i: (i, 0)), pl.BlockSpec((tm, LANES), lambda i: (i, 0)),
                   pl.BlockSpec((1, LANES), lambda i: (0, 0))),
        scratch_shapes=[pltpu.VMEM((1, LANES), F32)],
        compiler_params=_cparams("arbitrary"),
        name="moe_route",
    )(h, P["norm_ffn"], P["moe_router"], ltri)


def _moe_plan(info, cnt, tm, n_tiles_max):
    T = info.shape[0]
    rows = MOE_ROWS
    count = cnt[0, :N_EXPERTS].astype(jnp.int32)
    tiles_e = (count + rows - 1) // rows
    first_tile = jnp.cumsum(tiles_e) - tiles_e
    start = first_tile * rows
    experts = jnp.arange(N_EXPERTS, dtype=jnp.int32)

    def row_of(e, r):
        sel = e.astype(jnp.int32)[:, None] == experts[None, :]
        return jnp.sum(jnp.where(sel, start[None, :], 0), axis=1) + r.astype(jnp.int32)

    pos = jnp.stack([row_of(info[:, 0], info[:, 2]), row_of(info[:, 1], info[:, 3])], axis=0)
    pos = jnp.transpose(pos.reshape(2, T // tm, tm), (1, 0, 2))
    n_tiles = jnp.sum(tiles_e)
    t = jnp.minimum(jnp.arange(n_tiles_max, dtype=jnp.int32), n_tiles - 1)
    tile_expert = jnp.sum((first_tile[None, :] <= t[:, None]).astype(jnp.int32), axis=1) - 1
    last_tile = first_tile + tiles_e - 1
    return pos, tile_expert, n_tiles.reshape(1), last_tile, tiles_e


def _moe_dispatch_kernel(last_ref, tiles_ref, nt_ref, pos_ref, hn_ref, xs_ref, zero_ref, sem):
    tm = hn_ref.shape[0]

    @pl.when(pl.program_id(0) == 0)
    def _():
        zero_ref[...] = jnp.zeros(zero_ref.shape, F32)

        def clear(tile):
            row0 = pl.multiple_of(tile * MOE_ROWS, MOE_ROWS)
            cp = pltpu.make_async_copy(zero_ref, xs_ref.at[pl.ds(row0, MOE_ROWS), :], sem)
            cp.start()
            cp.wait()

        for e in range(N_EXPERTS):
            @pl.when(tiles_ref[e] > 0)
            def _():
                clear(last_ref[e])

        def clear_tail(tile, c):
            clear(tile)
            return c

        lax.fori_loop(nt_ref[0], xs_ref.shape[0] // MOE_ROWS, clear_tail, 0)

    def row_copy(j, slot):
        return pltpu.make_async_copy(hn_ref.at[pl.ds(j, 1), :], xs_ref.at[pl.ds(pos_ref[slot, j], 1), :], sem)

    def issue(j, c):
        row_copy(j, 0).start()
        row_copy(j, 1).start()
        return c

    lax.fori_loop(0, tm, issue, 0, unroll=8)
    for _ in range(2):
        pltpu.make_async_copy(hn_ref, xs_ref.at[pl.ds(0, tm), :], sem).wait()


def _moe_dispatch(hn, pos, last_tile, tiles_e, n_tiles, n_rows):
    T = hn.shape[0]
    tm = pos.shape[2]
    gs = pltpu.PrefetchScalarGridSpec(
        num_scalar_prefetch=3, grid=(T // tm,),
        in_specs=[pl.BlockSpec((None, 2, tm), lambda i, *_: (i, 0, 0), memory_space=pltpu.SMEM),
                  pl.BlockSpec((tm, D_MODEL), lambda i, *_: (i, 0))],
        out_specs=pl.BlockSpec(memory_space=pl.ANY),
        scratch_shapes=[pltpu.VMEM((MOE_ROWS, D_MODEL), F32), pltpu.SemaphoreType.DMA])
    return pl.pallas_call(
        _moe_dispatch_kernel, grid_spec=gs,
        out_shape=jax.ShapeDtypeStruct((n_rows, D_MODEL), F32),
        compiler_params=_cparams("arbitrary"),
        name="moe_dispatch",
    )(last_tile, tiles_e, n_tiles, pos, hn)


def _moe_group_kernel(te_ref, nt_ref, x_ref, w1_ref, w3_ref, w2_ref, o_ref):
    live = pl.program_id(0) < nt_ref[0]

    @pl.when(live)
    def _():
        x = x_ref[...].astype(BF16)
        a = jnp.dot(x, w1_ref[...], preferred_element_type=F32)
        b = jnp.dot(x, w3_ref[...], preferred_element_type=F32)
        o_ref[...] = _bdot(_silu(a) * b, w2_ref[...])

    @pl.when(jnp.logical_not(live))
    def _():
        o_ref[...] = jnp.zeros(o_ref.shape, F32)


def _moe_group(xs, P, j, tile_expert, n_tiles):
    n_rows = xs.shape[0]
    tf = D_FF_TILE
    wmap = lambda i, te, nt: (j, te[i], 0, 0)
    rmap = lambda i, te, nt: (i, 0)
    gs = pltpu.PrefetchScalarGridSpec(
        num_scalar_prefetch=2, grid=(n_rows // MOE_ROWS,),
        in_specs=[pl.BlockSpec((MOE_ROWS, D_MODEL), rmap),
                  pl.BlockSpec((None, None, D_MODEL, tf), wmap),
                  pl.BlockSpec((None, None, D_MODEL, tf), wmap),
                  pl.BlockSpec((None, None, tf, D_MODEL), wmap)],
        out_specs=pl.BlockSpec((MOE_ROWS, D_MODEL), rmap))
    return pl.pallas_call(
        _moe_group_kernel, grid_spec=gs,
        out_shape=jax.ShapeDtypeStruct((n_rows, D_MODEL), F32),
        compiler_params=_cparams("arbitrary"),
        name="moe_group",
    )(tile_expert, n_tiles, xs, P["moe_w1"], P["moe_w3"], P["moe_w2"])


def _moe_combine_kernel(final_norm, pos_ref, o_ref, h_ref, info_ref, gf_ref, out_ref, a_ref, b_ref, sem):
    tm = h_ref.shape[0]

    def row_copy(j, slot, dst):
        return pltpu.make_async_copy(o_ref.at[pl.ds(pos_ref[slot, j], 1), :], dst.at[pl.ds(j, 1), :], sem)

    def issue(j, c):
        row_copy(j, 0, a_ref).start()
        row_copy(j, 1, b_ref).start()
        return c

    lax.fori_loop(0, tm, issue, 0, unroll=8)
    for dst in (a_ref, b_ref):
        pltpu.make_async_copy(o_ref.at[pl.ds(0, tm), :], dst, sem).wait()
    k1, k2 = ROUTE_LANES.index("w1"), ROUTE_LANES.index("w2")
    info = info_ref[...]
    w1, w2 = info[:, k1:k1 + 1], info[:, k2:k2 + 1]
    out = h_ref[...] + (w1 * a_ref[...] + w2 * b_ref[...])
    if final_norm:
        out = _rmsnorm(out, gf_ref[...])
    out_ref[...] = out


def _moe_combine(o, h, info, pos, P, final_norm):
    T = h.shape[0]
    tm = pos.shape[2]
    gfinal = P["norm_final"]
    row_spec = pl.BlockSpec((tm, D_MODEL), lambda i: (i, 0))
    return pl.pallas_call(
        functools.partial(_moe_combine_kernel, final_norm),
        out_shape=jax.ShapeDtypeStruct((T, D_MODEL), F32),
        grid=(T // tm,),
        in_specs=[pl.BlockSpec((None, 2, tm), lambda i: (i, 0, 0), memory_space=pltpu.SMEM),
                  pl.BlockSpec(memory_space=pl.ANY), row_spec,
                  pl.BlockSpec((tm, LANES), lambda i: (i, 0)), _const_spec(gfinal)],
        out_specs=row_spec,
        scratch_shapes=[pltpu.VMEM((tm, D_MODEL), F32), pltpu.VMEM((tm, D_MODEL), F32), pltpu.SemaphoreType.DMA],
        compiler_params=_cparams("arbitrary"),
        name="moe_combine",
    )(pos, o, h, info, gfinal)


def _moe_routed(h, P, l, tm, final_norm):
    T = h.shape[0]
    n_tiles_max = (2 * T) // MOE_ROWS + N_EXPERTS
    hn, info, cnt = _moe_route(h, P, l, tm)
    pos, tile_expert, n_tiles, last_tile, tiles_e = _moe_plan(info, cnt, tm, n_tiles_max)
    xs = _moe_dispatch(hn, pos, last_tile, tiles_e, n_tiles, n_tiles_max * MOE_ROWS)
    o = _moe_group(xs, P, l // 2, tile_expert, n_tiles)
    return _moe_combine(o, h, info, pos, P, final_norm)


def _conv_step(x, buf_ref, buf_o, w_ref, b_ref):
    out = b_ref[...]
    for k in range(CONV_K - 1):
        out = out + buf_ref[k] * w_ref[k:k + 1, :]
        if k > 0:
            buf_o[k - 1] = buf_ref[k]
    out = out + x * w_ref[CONV_K - 1:CONV_K, :]
    buf_o[CONV_K - 2] = x
    return out


def _step_pre_kernel(pos_cos_ref, pos_sin_ref, gam_ref, ehn_ref, z_ref, sbuf_ref, lbuf_ref, lst_ref, s5r_ref, s5i_ref,
                     scw_ref, scb_ref, dtb_ref, alog_ref,
                     lcw_ref, lcb_ref, wg_ref, bg_ref, lam_ref,
                     wbr_ref, wbi_ref, wcr_ref, wci_ref, pr_ref, pi_ref, s5d_ref, wglu_ref, qr_ref, qi_ref,
                     kqv_ref, dec_ref, sbuf_o, lbuf_o, lst_o, s5r_o, s5i_o, ys5_o, ylru_o, xs_o,
                     xr_ref, xi_ref):
    zz = z_ref[...].astype(F32)
    xbc = zz[:, Z_SSD + MIX:Z_SSD + MIX + SSD_CONV]
    dt_raw = zz[:, Z_SSD + MIX + SSD_CONV:Z_SSD + MIX + SSD_CONV + LANES]
    xc = _silu(_conv_step(xbc, sbuf_ref, sbuf_o, scw_ref, scb_ref))
    xs = xc[:, 0:MIX]
    xs_o[...] = xs
    nbc = SSD_GROUPS * SSD_STATE
    bm = xc[:, MIX:MIX + nbc]
    cm = xc[:, MIX + nbc:MIX + 2 * nbc]
    dt = jax.nn.softplus(dt_raw + dtb_ref[...])
    a = -jnp.exp(alog_ref[...])
    rep = SSD_HEADS // SSD_GROUPS
    kqv_ref[0, 0] = (_group_repeat_lanes(bm, rep) * _split3_dot(dt, ehn_ref[...])).T
    kqv_ref[0, 1] = _group_repeat_lanes(cm, rep).T
    kqv_ref[0, 2] = xs.T
    dec_ref[0] = jnp.exp(dt * a).T[0:SSD_HEADS, :]
    q = _rotary_lanes(zz[:, Z_RET:Z_RET + MIX], pos_cos_ref[...], pos_sin_ref[...]) * (RET_HD ** -0.5)
    k = _rotary_lanes(zz[:, Z_RET + MIX:Z_RET + 2 * MIX], pos_cos_ref[...], pos_sin_ref[...])
    kqv_ref[1, 0] = k.T
    kqv_ref[1, 1] = q.T
    kqv_ref[1, 2] = zz[:, Z_RET + 2 * MIX:Z_RET + 3 * MIX].T
    dec_ref[1] = gam_ref[...]
    gate = zz[:, Z_LRU:Z_LRU + MIX]
    lx = zz[:, Z_LRU + MIX:Z_LRU + 2 * MIX]
    lconv = _conv_step(lx, lbuf_ref, lbuf_o, lcw_ref, lcb_ref)
    la, lbx = _lru_gates(lconv, wg_ref, bg_ref, lam_ref)
    hl = lbx + la * lst_ref[...]
    lst_o[...] = hl
    ylru_o[...] = hl * jax.nn.gelu(gate)
    u = zz[:, Z_S5:Z_S5 + MIX]
    _s5_project_in(u, wbr_ref, wbi_ref, xr_ref, xi_ref, 1)
    lr, li = pr_ref[0:1, :], pi_ref[0:1, :]
    s0r, s0i = s5r_ref[...], s5i_ref[...]
    xr = xr_ref[...] + (lr * s0r - li * s0i)
    xi = xi_ref[...] + (lr * s0i + li * s0r)
    s5r_o[...] = xr
    s5i_o[...] = xi
    ys5_o[...] = _s5_project_out(xr, xi, u, wcr_ref, wci_ref, s5d_ref, wglu_ref)


def _step_state_kernel(kqv_ref, dec_ref, st_ref, o_st_ref, y_ref):
    h = pl.program_id(0)
    d = dec_ref[pl.ds(h, 1), :]
    v = kqv_ref[2]
    acc = jnp.zeros(v.shape, F32)
    for n in range(st_ref.shape[0]):
        s_new = d * st_ref[n] + kqv_ref[0, n:n + 1, :] * v
        o_st_ref[n] = s_new
        acc = acc + kqv_ref[1, n:n + 1, :] * s_new
    y_ref[...] = acc


def _step_post_kernel(yssd_ref, yret_ref, xs_ref, z_ref, dlane_ref, ng_ref, gn_ref, yssd_o, yret_o):
    zz_gate = z_ref[:, Z_SSD:Z_SSD + MIX]
    y = yssd_ref[...].T + dlane_ref[...] * xs_ref[...]
    y = y * _silu(zz_gate)
    yssd_o[...] = _rmsnorm(y, ng_ref[...])
    rgate = z_ref[:, Z_RET + 3 * MIX:Z_RET + 4 * MIX]
    yret_o[...] = _silu(rgate) * (_group_norm_lanes(yret_ref[...].T, RET_HD) * gn_ref[...])


def _sample_mixers(z, views, big, layer, P, pos):
    Bs = z.shape[0]
    H = SSD_HEADS
    cos_l, sin_l = _rope_tables(np.asarray([pos]))
    gam = jnp.asarray(np.repeat(_ret_gammas()[:, None], Bs, axis=1), F32)
    consts = [cos_l, sin_l, gam, _head_expand(SSD_HEADS, SSD_HD)]
    states = [views[k] for k in ("ssd_conv", "lru_conv", "lru", "s5_re", "s5_im")]
    params = [P[k] for k in ("ssd_cw", "ssd_cb", "ssd_dtb", "ssd_alog",
                             "lru_cw", "lru_cb", "lru_wg", "lru_bg", "lru_lam") + S5_PARAMS]
    pre_out = (jax.ShapeDtypeStruct((2, 3, MIX, Bs), F32),
               jax.ShapeDtypeStruct((2, H, Bs), F32),
               jax.ShapeDtypeStruct(views["ssd_conv"].shape[1:], F32),
               jax.ShapeDtypeStruct(views["lru_conv"].shape[1:], F32),
               jax.ShapeDtypeStruct((Bs, MIX), F32),
               jax.ShapeDtypeStruct((Bs, S5_CH), F32), jax.ShapeDtypeStruct((Bs, S5_CH), F32),
               jax.ShapeDtypeStruct((Bs, MIX), F32), jax.ShapeDtypeStruct((Bs, MIX), F32),
               jax.ShapeDtypeStruct((Bs, MIX), F32))
    (kqv, dec, sbuf_n, lbuf_n, lst_n, s5r_n, s5i_n, y_s5, y_lru, xs) = pl.pallas_call(
        _step_pre_kernel,
        out_shape=pre_out,
        grid=(1,),
        in_specs=[_const_spec(a) for a in consts] + [_const_spec(z)]
                 + [_lspec(a, layer) for a in states] + [_lspec(a, layer) for a in params],
        out_specs=tuple(pl.BlockSpec(o.shape, lambda i, n=len(o.shape): (0,) * n) for o in pre_out),
        scratch_shapes=[pltpu.VMEM((Bs, S5_CH), F32), pltpu.VMEM((Bs, S5_CH), F32)],
        compiler_params=_cparams("arbitrary"),
        name="step_pre",
    )(*consts, z, *states, *params)

    new_big, yts = {}, []
    for m, name in enumerate(("ssd", "ret")):
        s_new, y_t = pl.pallas_call(
            _step_state_kernel,
            out_shape=(jax.ShapeDtypeStruct(big[name].shape, F32), jax.ShapeDtypeStruct((MIX, Bs), F32)),
            grid=(H,),
            in_specs=[pl.BlockSpec((None, 3, SSD_STATE, Bs), lambda h, m=m: (m, 0, h, 0)),
                      pl.BlockSpec((None, H, Bs), lambda h, m=m: (m, 0, 0)),
                      pl.BlockSpec((None, None, SSD_STATE, SSD_HD, Bs), lambda h: (layer, h, 0, 0, 0))],
            out_specs=(pl.BlockSpec((None, None, SSD_STATE, SSD_HD, Bs), lambda h: (layer, h, 0, 0, 0)),
                       pl.BlockSpec((SSD_HD, Bs), lambda h: (h, 0))),
            input_output_aliases={2: 0},
            compiler_params=_cparams("parallel"),
            name="step_state_" + name,
        )(kqv, dec, big[name])
        new_big[name] = s_new
        yts.append(y_t)

    post_in = [yts[0], yts[1], xs, z]
    post_par = [P[k] for k in ("ssd_dlane", "ssd_norm", "ret_gn")]
    post_out = (jax.ShapeDtypeStruct((Bs, MIX), F32), jax.ShapeDtypeStruct((Bs, MIX), F32))
    y_ssd, y_ret = pl.pallas_call(
        _step_post_kernel,
        out_shape=post_out,
        grid=(1,),
        in_specs=[_const_spec(a) for a in post_in] + [_lspec(a, layer) for a in post_par],
        out_specs=tuple(pl.BlockSpec(o.shape, lambda i: (0, 0)) for o in post_out),
        compiler_params=_cparams("arbitrary"),
        name="step_post",
    )(*post_in, *post_par)

    new = dict(ssd_conv=sbuf_n, lru_conv=lbuf_n, lru=lst_n, s5_re=s5r_n, s5_im=s5i_n)
    return (y_ssd, y_s5, y_lru, y_ret), new, new_big


def _block_diag8(w):
    lead = w.shape[:-3]
    n, r, c = w.shape[-3:]
    eye = jnp.eye(n, dtype=w.dtype)
    out = w[..., :, :, None, :] * eye[:, None, :, None]
    return out.reshape(lead + (n * r, n * c))


def _row(v):
    return v[:, None, :]


def _cmul(ar, ai, br, bi):
    return ar * br - ai * bi, ar * bi + ai * br


def _prep_params(W):
    depth = W["w_in"].shape[0]
    P = {}
    w_t = jnp.transpose(W["w_in"], (0, 2, 1))
    P["w_in"] = jnp.concatenate(
        [w_t[:, 4872:8968], w_t[:, 2824:4872], w_t[:, 1800:2824], w_t[:, 1288:1800], w_t[:, 0:1288],
         jnp.zeros((depth, Z_WIDTH - 8968, D_MODEL), w_t.dtype)], axis=1).astype(BF16)
    P["norm_mix"] = _row(W["norm_mix"])
    pad = ((0, 0), (0, LANES - SSD_HEADS))
    P["ssd_cw"] = W["ssd_conv_w"]
    P["ssd_cb"] = _row(W["ssd_conv_b"])
    P["ssd_dtb"] = _row(jnp.pad(W["ssd_dt_bias"], pad))
    P["ssd_alog"] = _row(jnp.pad(W["ssd_a_log"], pad))
    P["ssd_dlane"] = _row(jnp.repeat(W["ssd_d"], SSD_HD, axis=1))
    P["ssd_norm"] = _row(W["ssd_norm"])
    lr, li = W["s5_lambda_re"], W["s5_lambda_im"]
    dt = jnp.exp(W["s5_log_dt"])[:, :, None]
    mag = jnp.exp(lr * dt)
    br, bi = mag * jnp.cos(li * dt), mag * jnp.sin(li * dt)
    den = lr * lr + li * li
    qr, qi = _cmul(br - 1.0, bi, lr / den, -li / den)
    wr, wi = _cmul(qr[..., None], qi[..., None], W["s5_b_re"], W["s5_b_im"])
    lag_r, lag_i = [wr], [wi]
    for _ in range(S5_LAGS - 1):
        nr, ni = _cmul(lag_r[-1], lag_i[-1], br[..., None], bi[..., None])
        lag_r.append(nr)
        lag_i.append(ni)
    gb = S5_GROUPS // S5_LB

    def embed_in(lags):
        m = jnp.stack(lags, axis=1).reshape(depth, len(lags), S5_LB, gb, S5_STATE, S5_GDIM)
        m = _block_diag8(jnp.swapaxes(m, -1, -2))
        return jnp.swapaxes(m, 1, 2).reshape(depth, S5_LB, len(lags) * gb * S5_GDIM, gb * S5_STATE)

    def embed_out(m):
        m = m.reshape(depth, S5_LB, gb, S5_GDIM, S5_STATE)
        return _block_diag8(jnp.swapaxes(m, -1, -2))

    P["s5_wbr"] = embed_in(lag_r).astype(BF16)
    P["s5_wbi"] = embed_in(lag_i).astype(BF16)
    P["s5_wcr"] = embed_out(W["s5_c_re"]).astype(BF16)
    P["s5_wci"] = embed_out(W["s5_c_im"]).astype(BF16)
    pr, pi = [br.reshape(depth, 1, S5_CH)], [bi.reshape(depth, 1, S5_CH)]
    for _ in range(SUBLANES - 1):
        nr, ni = _cmul(pr[-1], pi[-1], pr[0], pi[0])
        pr.append(nr)
        pi.append(ni)
    P["s5_pr"] = jnp.concatenate(pr, axis=1)
    P["s5_pi"] = jnp.concatenate(pi, axis=1)
    half = jnp.zeros((depth, S5_LAGS, S5_CH), F32)
    P["s5_qr"] = jnp.concatenate([half, P["s5_pr"][:, :SUBLANES - S5_LAGS]], axis=1)
    P["s5_qi"] = jnp.concatenate([half, P["s5_pi"][:, :SUBLANES - S5_LAGS]], axis=1)
    P["s5_d"] = W["s5_d"].reshape(depth, 1, MIX)
    P["s5_glu"] = W["s5_glu"].astype(BF16)
    P["lru_cw"] = W["lru_conv_w"]
    P["lru_cb"] = _row(W["lru_conv_b"])
    P["lru_wg"] = jnp.concatenate([_block_diag8(W["lru_wa"]), _block_diag8(W["lru_wx"])], axis=2).astype(BF16)
    P["lru_bg"] = _row(jnp.concatenate([W["lru_ba"], W["lru_bx"]], axis=1))
    P["lru_lam"] = _row(W["lru_lambda"])
    P["ret_gn"] = _row(W["ret_gn"])
    P["w_branch"] = W["w_branch"].astype(BF16)
    P["w_out"] = W["w_out"].astype(BF16)
    P["norm_ffn"] = _row(W["norm_ffn"])
    P["norm_final"] = W["norm_final"].reshape(1, D_MODEL)
    for k in ("ffn_w1", "ffn_w3", "ffn_w2", "moe_w1", "moe_w3", "moe_w2"):
        P[k] = W[k].astype(BF16)
    P["moe_router"] = jnp.pad(W["moe_router"], ((0, 0), (0, 0), (0, LANES - N_EXPERTS)))
    return P


def _trunk_prompt(x, P):
    B, L, _ = x.shape
    T = B * L
    depth = P["w_in"].shape[0]
    h = x.reshape(T, D_MODEL)
    new = {k: [] for k in ("ssd", "ssd_conv", "s5_re", "s5_im", "lru", "lru_conv", "ret")}
    for l in range(depth):
        z = _inproj(h, P, l, min(1024, T), BF16)
        z3 = z.reshape(B, L, Z_WIDTH)
        y_ssd, s_ssd, buf_ssd = _ssd_prompt(z3, P, l)
        y_s5, s5r, s5i = _s5_prompt(z3, P, l)
        y_lru, s_lru, buf_lru = _lru_prompt(z3, P, l)
        y_ret, s_ret = _ret_prompt(z3, P, l)
        ys = tuple(y.reshape(T, MIX) for y in (y_ssd, y_s5, y_lru, y_ret))
        h = _merge(ys, z, h, P, l, min(256, T))
        mixer = _moe_routed if (l % 2 == 1 and T % MOE_ROWS == 0) else _ffn
        h = mixer(h, P, l, min(512, T), final_norm=(l == depth - 1))
        new["ssd"].append(s_ssd)
        new["ssd_conv"].append(buf_ssd)
        new["s5_re"].append(s5r.reshape(B, S5_GROUPS, S5_STATE))
        new["s5_im"].append(s5i.reshape(B, S5_GROUPS, S5_STATE))
        new["lru"].append(s_lru.reshape(B, MIX))
        new["lru_conv"].append(buf_lru)
        new["ret"].append(s_ret)
    return h.reshape(B, L, D_MODEL), {k: jnp.stack(v) for k, v in new.items()}


def _trunk_sample(x, pos, st, P):
    Bs = x.shape[0]
    depth = P["w_in"].shape[0]
    h = x.reshape(Bs, D_MODEL)
    big = dict(ssd=jnp.transpose(st["ssd"], (0, 2, 3, 4, 1)), ret=jnp.transpose(st["ret"], (0, 2, 3, 4, 1)))
    views = dict(ssd_conv=jnp.transpose(st["ssd_conv"], (0, 2, 1, 3)),
                 lru_conv=jnp.transpose(st["lru_conv"], (0, 2, 1, 3)),
                 lru=st["lru"],
                 s5_re=st["s5_re"].reshape(depth, Bs, S5_CH),
                 s5_im=st["s5_im"].reshape(depth, Bs, S5_CH))
    new = {k: [] for k in views}
    for l in range(depth):
        z = _inproj(h, P, l, Bs, F32)
        ys, nl, big = _sample_mixers(z, views, big, l, P, pos)
        h = _merge(ys, z, h, P, l, Bs)
        h = _ffn(h, P, l, Bs, final_norm=(l == depth - 1))
        for k in new:
            new[k].append(nl[k])
    out = {k: jnp.stack(v) for k, v in new.items()}
    out["ssd_conv"] = jnp.transpose(out["ssd_conv"], (0, 2, 1, 3))
    out["lru_conv"] = jnp.transpose(out["lru_conv"], (0, 2, 1, 3))
    out["s5_re"] = out["s5_re"].reshape(st["s5_re"].shape)
    out["s5_im"] = out["s5_im"].reshape(st["s5_im"].shape)
    out["ssd"] = jnp.transpose(big["ssd"], (0, 4, 1, 2, 3))
    out["ret"] = jnp.transpose(big["ret"], (0, 4, 1, 2, 3))
    return h.reshape(Bs, 1, D_MODEL), out


def kernel(x_prompt, x_sample, state_ssd, state_ssd_conv, state_s5_re, state_s5_im, state_lru, state_lru_conv, state_ret, norm_mix, w_in, ssd_conv_w, ssd_conv_b, ssd_dt_bias, ssd_a_log, ssd_d, ssd_norm, s5_lambda_re, s5_lambda_im, s5_b_re, s5_b_im, s5_c_re, s5_c_im, s5_d, s5_log_dt, s5_glu, lru_conv_w, lru_conv_b, lru_wa, lru_ba, lru_wx, lru_bx, lru_lambda, ret_gn, w_branch, w_out, norm_ffn, ffn_w1, ffn_w3, ffn_w2, moe_router, moe_w1, moe_w3, moe_w2, norm_final):
    W = dict(norm_mix=norm_mix, w_in=w_in, ssd_conv_w=ssd_conv_w, ssd_conv_b=ssd_conv_b, ssd_dt_bias=ssd_dt_bias,
             ssd_a_log=ssd_a_log, ssd_d=ssd_d, ssd_norm=ssd_norm, s5_lambda_re=s5_lambda_re,
             s5_lambda_im=s5_lambda_im, s5_b_re=s5_b_re, s5_b_im=s5_b_im, s5_c_re=s5_c_re, s5_c_im=s5_c_im,
             s5_d=s5_d, s5_log_dt=s5_log_dt, s5_glu=s5_glu, lru_conv_w=lru_conv_w, lru_conv_b=lru_conv_b,
             lru_wa=lru_wa, lru_ba=lru_ba, lru_wx=lru_wx, lru_bx=lru_bx, lru_lambda=lru_lambda, ret_gn=ret_gn,
             w_branch=w_branch, w_out=w_out, norm_ffn=norm_ffn, moe_router=moe_router, norm_final=norm_final,
             ffn_w1=ffn_w1, ffn_w3=ffn_w3, ffn_w2=ffn_w2, moe_w1=moe_w1, moe_w3=moe_w3, moe_w2=moe_w2)
    P = _prep_params(W)
    y_p, sp = _trunk_prompt(x_prompt, P)
    st = dict(ssd=state_ssd, ssd_conv=state_ssd_conv, s5_re=state_s5_re, s5_im=state_s5_im,
              lru=state_lru, lru_conv=state_lru_conv, ret=state_ret)
    past_len = 16384
    y_s, ss = _trunk_sample(x_sample, past_len, st, P)
    names = ("ssd", "ssd_conv", "s5_re", "s5_im", "lru", "lru_conv", "ret")
    return (y_p, y_s) + tuple(sp[n] for n in names) + tuple(ss[n] for n in names)
```

```python
import functools
import math

import jax
import jax.numpy as jnp
import numpy as np
from jax import lax
from jax.experimental import pallas as pl
from jax.experimental.pallas import tpu as pltpu

F32 = jnp.float32
BF16 = jnp.bfloat16
EPS = 1e-6

D_MODEL = 1024
MIX = 512
CONV_K = 4
CHUNK = 128
SSD_HEADS = 8
SSD_HD = 64
SSD_STATE = 64
SSD_GROUPS = 2
SSD_CONV = MIX + 2 * SSD_GROUPS * SSD_STATE
S5_GROUPS = 32
S5_GDIM = 16
S5_STATE = 64
S5_CH = S5_GROUPS * S5_STATE
LRU_BLOCKS = 8
LRU_C = 8.0
RET_HEADS = 8
RET_HD = 64
ROPE_BASE = 10000.0
N_EXPERTS = 8
D_FF_TILE = 1408

Z_MERGE = 0
Z_RET = 4096
Z_LRU = 6144
Z_S5 = 7168
Z_SSD = 7680
Z_WIDTH = 9216

VMEM_LIMIT = 56 * 1024 * 1024
LANES = 128
SUBLANES = 8


def _cparams(*sem):
    return pltpu.CompilerParams(dimension_semantics=sem, vmem_limit_bytes=VMEM_LIMIT)


def _bdot(a, b):
    return jnp.dot(a.astype(BF16), b.astype(BF16), preferred_element_type=F32)


def _bdot_nt(a, b):
    return lax.dot_general(a.astype(BF16), b.astype(BF16), (((1,), (1,)), ((), ())), preferred_element_type=F32)


def _bdot_tn(a, b):
    return lax.dot_general(a.astype(BF16), b.astype(BF16), (((0,), (0,)), ((), ())), preferred_element_type=F32)


def _hdot(a, b):
    return jnp.dot(a, b, precision=lax.Precision.HIGHEST, preferred_element_type=F32)


def _split3_dot(x, m01):
    hi = x.astype(BF16)
    r1 = x - hi.astype(F32)
    mid = r1.astype(BF16)
    lo = (r1 - mid.astype(F32)).astype(BF16)
    m = m01.astype(BF16)
    d = functools.partial(jnp.dot, preferred_element_type=F32)
    return (d(lo, m) + d(mid, m)) + d(hi, m)


def _rmsnorm(x, g):
    ms = jnp.mean(x * x, axis=-1, keepdims=True)
    return x * lax.rsqrt(ms + EPS) * g


def _silu(x):
    return x * jax.nn.sigmoid(x)


def _neg_expm1_2x(log_a, a):
    return jnp.tanh(-log_a) * (1.0 + a * a)


def _inproj_kernel(x_ref, g_ref, w_ref, o_ref, hn_ref):
    @pl.when(pl.program_id(1) == 0)
    def _():
        hn_ref[...] = _rmsnorm(x_ref[...], g_ref[...]).astype(BF16)

    z = lax.dot_general(hn_ref[...], w_ref[...], (((1,), (1,)), ((), ())), preferred_element_type=F32)
    o_ref[...] = z.astype(o_ref.dtype)


def _lspec(a, l):
    rest = tuple(a.shape[1:])
    return pl.BlockSpec((None,) + rest, lambda *_: (l,) + (0,) * len(rest))


def _inproj(x, P, l, tm, out_dtype):
    T = x.shape[0]
    tn = 1536
    return pl.pallas_call(
        _inproj_kernel,
        out_shape=jax.ShapeDtypeStruct((T, Z_WIDTH), out_dtype),
        grid=(T // tm, Z_WIDTH // tn),
        in_specs=[pl.BlockSpec((tm, D_MODEL), lambda i, j: (i, 0)),
                  _lspec(P["norm_mix"], l),
                  pl.BlockSpec((None, tn, D_MODEL), lambda i, j: (l, j, 0))],
        out_specs=pl.BlockSpec((tm, tn), lambda i, j: (i, j)),
        scratch_shapes=[pltpu.VMEM((tm, D_MODEL), BF16)],
        compiler_params=_cparams("parallel", "arbitrary"),
        name="inproj",
    )(x, P["norm_mix"], P["w_in"])


SUBCHUNKS = 4


def _when_first(s, fn):
    if s == 0:
        pl.when(pl.program_id(1) == 0)(fn)


def _when_last(s, fn):
    if s == SUBCHUNKS - 1:
        pl.when(pl.program_id(1) == pl.num_programs(1) - 1)(fn)


def _sub_rows(ref, s):
    return ref.at[pl.ds(s * CHUNK, CHUNK), :]


def _conv_chunk(s, x, pad_ref, w_ref, b_ref):
    Lc = x.shape[0]

    def _():
        pad_ref[0:8, :] = jnp.zeros((8, x.shape[1]), F32)

    _when_first(s, _)
    pad_ref[8:8 + Lc, :] = x
    out = b_ref[...] + pad_ref[5:5 + Lc, :] * w_ref[0:1, :]
    out = out + pad_ref[6:6 + Lc, :] * w_ref[1:2, :]
    out = out + pad_ref[7:7 + Lc, :] * w_ref[2:3, :]
    out = out + x * w_ref[3:4, :]
    return out


def _conv_finish(pad_ref, Lc):
    pad_ref[0:8, :] = pad_ref[Lc:Lc + 8, :]


def _head_masks(nh, hd, L):
    rows = (np.arange(nh * L)[:, None] // L) == (np.arange(nh * hd)[None, :] // hd)
    diag = (np.arange(nh * hd)[:, None] // hd) == (np.arange(nh * hd)[None, :] // hd)
    return jnp.asarray(rows, BF16), jnp.asarray(diag, F32)


def _head_block_rows(x, nh, mask):
    return jnp.concatenate([x.astype(BF16)] * nh, axis=0) * mask


def _pow2_div(x, d):
    return lax.shift_right_logical(x, jnp.int32(int(math.log2(d))))


def _head_block_rows_sel(x, nh, hd):
    L = x.shape[0]
    xt = jnp.concatenate([x.astype(BF16)] * nh, axis=0)
    row = lax.broadcasted_iota(jnp.int32, xt.shape, 0)
    col = lax.broadcasted_iota(jnp.int32, xt.shape, 1)
    return jnp.where(_pow2_div(row, L) == _pow2_div(col, hd), xt, jnp.zeros_like(xt))


def _head_block_cols(xt, nh, hd):
    L = xt.shape[1]
    xc = jnp.concatenate([xt.astype(BF16)] * nh, axis=1)
    row = lax.broadcasted_iota(jnp.int32, xc.shape, 0)
    col = lax.broadcasted_iota(jnp.int32, xc.shape, 1)
    return jnp.where(_pow2_div(row, hd) == _pow2_div(col, L), xc, jnp.zeros_like(xc))


def _group_repeat_lanes(m, rep):
    lane = lax.broadcasted_iota(jnp.int32, m.shape, 1)
    swapped = pltpu.roll(m, m.shape[1] // 2, 1)
    low = lane < m.shape[1] // 2
    g0 = jnp.where(low, m, swapped)
    g1 = jnp.where(low, swapped, m)
    return jnp.concatenate([g0] * (rep // 2) + [g1] * (rep // 2), axis=1)


def _ssd_kernel(z_ref, *refs):
    y_ref = refs[-5]
    for s in range(SUBCHUNKS):
        _ssd_chunk(s, _sub_rows(z_ref, s), *refs[:-5], _sub_rows(y_ref, s), *refs[-4:])


def _ssd_chunk(s, z_ref, cw_ref, cb_ref, dtb_ref, alog_ref, dlane_ref, ng_ref, tri_ref, ehj_ref, ehn_ref,
               cbias_ref, hrows_ref, hdiag_ref, y_ref, sto_ref, buf_ref, pad_ref, st_ref):
    Lc = z_ref.shape[0]
    zz = z_ref[...].astype(F32)
    zgate = zz[:, 0:MIX]
    xbc = zz[:, MIX:MIX + SSD_CONV]
    dt_raw = zz[:, MIX + SSD_CONV:MIX + SSD_CONV + LANES]

    def _():
        st_ref[...] = jnp.zeros(st_ref.shape, F32)

    _when_first(s, _)
    conv = _conv_chunk(s, xbc, pad_ref, cw_ref, cb_ref)
    xc = _silu(conv)
    xs = xc[:, 0:MIX]
    nbc = SSD_GROUPS * SSD_STATE
    bm = xc[:, MIX:MIX + nbc]
    cm = xc[:, MIX + nbc:MIX + 2 * nbc]
    dt = jax.nn.softplus(dt_raw + dtb_ref[...])
    a = -jnp.exp(alog_ref[...])
    ld = dt * a
    acum = _hdot(tri_ref[...], ld)
    acum_t = acum.T
    dt_t = dt.T
    a_row = jnp.concatenate([acum_t[h:h + 1, :] for h in range(SSD_HEADS)], axis=1)
    dt_row = jnp.concatenate([dt_t[h:h + 1, :] for h in range(SSD_HEADS)], axis=1)
    a_col = _split3_dot(acum, ehj_ref[...])
    decay = jnp.exp((a_col - a_row) + cbias_ref[...])
    rep = SSD_HEADS // SSD_GROUPS
    gmats = []
    for g in range(SSD_GROUPS):
        cg = cm[:, g * SSD_STATE:(g + 1) * SSD_STATE]
        bg = bm[:, g * SSD_STATE:(g + 1) * SSD_STATE]
        gmats.append(_bdot_nt(cg, bg))
    g_all = jnp.concatenate([gmats[h // rep] for h in range(SSD_HEADS)], axis=1)
    m_all = g_all * decay * dt_row
    y = _bdot(m_all, _head_block_rows(xs, SSD_HEADS, hrows_ref[...]))
    exp_a = jnp.exp(acum)
    exp_a_l = _split3_dot(exp_a, ehn_ref[...])
    c_rep = _group_repeat_lanes(cm, rep)
    y = y + _bdot_nt(c_rep * exp_a_l, st_ref[...])
    a_last = acum[Lc - 1:Lc, :]
    w_end_l = _split3_dot(jnp.exp(a_last - acum) * dt, ehn_ref[...])
    upd = _bdot_tn(xs, _group_repeat_lanes(bm, rep) * w_end_l)
    st_ref[...] = exp_a_l[Lc - 1:Lc, :] * st_ref[...] + upd * hdiag_ref[...]
    y = y + dlane_ref[...] * xs
    y = y * _silu(zgate)
    y_ref[...] = _rmsnorm(y, ng_ref[...]).astype(y_ref.dtype)

    def _():
        buf_ref[...] = pad_ref[Lc + 8 - (CONV_K - 1):Lc + 8, :]
        s_t = st_ref[...].T
        for h in range(SSD_HEADS):
            sl = slice(h * SSD_HD, (h + 1) * SSD_HD)
            sto_ref[h] = s_t[sl, sl]

    _when_last(s, _)
    _conv_finish(pad_ref, Lc)


def _const_spec(a):
    return pl.BlockSpec(a.shape, lambda *_: (0,) * a.ndim)


def _ssd_prompt(z3, P, l):
    B, L, _ = z3.shape
    Lc = CHUNK
    params = [P[k] for k in ("ssd_cw", "ssd_cb", "ssd_dtb", "ssd_alog", "ssd_dlane", "ssd_norm")]
    j = np.arange(SSD_HEADS * Lc) % Lc
    cbias = jnp.asarray(np.where(np.arange(Lc)[:, None] >= j[None, :], 0.0, -1e30), F32)
    consts = [_tri(Lc), _head_expand(SSD_HEADS, Lc), _head_expand(SSD_HEADS, SSD_HD), cbias,
              *_head_masks(SSD_HEADS, SSD_HD, Lc)]
    return pl.pallas_call(
        _ssd_kernel,
        out_shape=(jax.ShapeDtypeStruct((B, L, MIX), BF16),
                   jax.ShapeDtypeStruct((B, SSD_HEADS, SSD_STATE, SSD_HD), F32),
                   jax.ShapeDtypeStruct((B, CONV_K - 1, SSD_CONV), F32)),
        grid=(B, L // (SUBCHUNKS * Lc)),
        in_specs=[pl.BlockSpec((None, SUBCHUNKS * Lc,1536), lambda b, c: (b, c, Z_SSD // 1536))]
                 + [_lspec(a, l) for a in params] + [_const_spec(a) for a in consts],
        out_specs=(pl.BlockSpec((None, SUBCHUNKS * Lc,MIX), lambda b, c: (b, c, 0)),
                   pl.BlockSpec((None, SSD_HEADS, SSD_STATE, SSD_HD), lambda b, c: (b, 0, 0, 0)),
                   pl.BlockSpec((None, CONV_K - 1, SSD_CONV), lambda b, c: (b, 0, 0))),
        scratch_shapes=[pltpu.VMEM((Lc + 8, SSD_CONV), F32), pltpu.VMEM((MIX, MIX), F32)],
        compiler_params=_cparams("parallel", "arbitrary"),
        name="ssd_prompt",
    )(z3, *params, *consts)


def _tri(Lc):
    return jnp.asarray(np.tril(np.ones((Lc, Lc), np.float32)))


def _head_expand(nh, width):
    e = np.zeros((LANES, nh * width), np.float32)
    for h in range(nh):
        e[h, h * width:(h + 1) * width] = 1.0
    return jnp.asarray(e, BF16)


def _ret_gammas():
    return 1.0 - np.exp2(-5.0 - np.arange(RET_HEADS, dtype=np.float64))


def _ret_tables(Lc):
    gam = _ret_gammas()
    i = np.arange(Lc)
    d = i[:, None] - i[None, :]
    decay = np.where(d >= 0, gam[:, None, None] ** np.maximum(d, 0)[None], 0.0)
    decay_l = np.transpose(decay, (1, 0, 2)).reshape(Lc, RET_HEADS * Lc)
    grow_l = np.repeat(gam[None, :] ** (i[:, None] + 1), RET_HD, axis=1)
    toend_t = np.repeat(gam[:, None] ** (Lc - 1 - i[None, :]), RET_HD, axis=0)
    hd = np.arange(MIX) // RET_HD
    state_decay = np.where(hd[:, None] == hd[None, :], (gam ** Lc)[hd][:, None], 0.0)
    return tuple(jnp.asarray(t, F32) for t in (decay_l, grow_l, toend_t, state_decay))


def _rope_tables(pos):
    half = RET_HD // 2
    inv = ROPE_BASE ** (-np.arange(half, dtype=np.float64) / half)
    ang = np.asarray(pos, np.float64)[:, None] * inv[None, :]
    cos = np.cos(ang)
    sin = np.sin(ang)
    cos_l = np.tile(np.concatenate([cos, cos], axis=1), (1, RET_HEADS))
    sin_l = np.tile(np.concatenate([-sin, sin], axis=1), (1, RET_HEADS))
    return jnp.asarray(cos_l, F32), jnp.asarray(sin_l, F32)


def _rotary_lanes(x, cos_l, sin_l):
    lane = lax.broadcasted_iota(jnp.int32, x.shape, 1)
    first = (lane & (RET_HD - 1)) < (RET_HD // 2)
    n = x.shape[1]
    swapped = jnp.where(first, pltpu.roll(x, n - RET_HD // 2, 1), pltpu.roll(x, RET_HD // 2, 1))
    return x * cos_l + swapped * sin_l


def _group_norm_head(o):
    mu = jnp.mean(o, axis=-1, keepdims=True)
    d = o - mu
    var = jnp.mean(d * d, axis=-1, keepdims=True)
    return d * lax.rsqrt(var + 1e-5)


def _group_norm_lanes(o, hd):
    cols = []
    for cb in range(o.shape[1] // LANES):
        x = o[:, cb * LANES:(cb + 1) * LANES]
        low = lax.broadcasted_iota(jnp.int32, x.shape, 1) < hd

        def seg_mean(t):
            lo = jnp.sum(jnp.where(low, t, 0.0), axis=1, keepdims=True)
            hi = jnp.sum(jnp.where(low, 0.0, t), axis=1, keepdims=True)
            return jnp.where(low, lo, hi) * (1.0 / hd)

        d = x - seg_mean(x)
        cols.append(d * lax.rsqrt(seg_mean(d * d) + 1e-5))
    return jnp.concatenate(cols, axis=1)


def _ret_kernel(z_ref, cos_ref, sin_ref, *refs):
    y_ref = refs[-3]
    for s in range(SUBCHUNKS):
        _ret_chunk(s, _sub_rows(z_ref, s), _sub_rows(cos_ref, s), _sub_rows(sin_ref, s), *refs[:-3],
                   _sub_rows(y_ref, s), *refs[-2:])


def _ret_chunk(s, z_ref, cos_ref, sin_ref, dec_ref, grow_ref, toend_ref, sdec_ref, hdiag_ref,
               gn_ref, y_ref, sto_ref, st_ref):
    zz = z_ref[...].astype(F32)
    q = _rotary_lanes(zz[:, 0:MIX], cos_ref[...], sin_ref[...]) * (RET_HD ** -0.5)
    k = _rotary_lanes(zz[:, MIX:2 * MIX], cos_ref[...], sin_ref[...])
    v = zz[:, 2 * MIX:3 * MIX]
    gate = zz[:, 3 * MIX:4 * MIX]

    def _():
        st_ref[...] = jnp.zeros(st_ref.shape, F32)

    _when_first(s, _)
    k_t = k.T
    g_all = _bdot(q, _head_block_cols(k_t, RET_HEADS, RET_HD))
    o = _bdot(g_all * dec_ref[...], _head_block_rows_sel(v, RET_HEADS, RET_HD))
    o = o + _bdot(q * grow_ref[...], st_ref[...])
    upd = _bdot(k_t * toend_ref[...], v)
    st_ref[...] = sdec_ref[...] * st_ref[...] + upd * hdiag_ref[...]
    y_ref[...] = (_silu(gate) * (_group_norm_lanes(o, RET_HD) * gn_ref[...])).astype(y_ref.dtype)

    def _():
        for h in range(RET_HEADS):
            sl = slice(h * RET_HD, (h + 1) * RET_HD)
            sto_ref[h] = st_ref[sl, sl]

    _when_last(s, _)


def _ret_prompt(z3, P, l):
    B, L, _ = z3.shape
    Lc = CHUNK
    cos_l, sin_l = _rope_tables(np.arange(L))
    consts = list(_ret_tables(Lc)) + [_head_masks(RET_HEADS, RET_HD, Lc)[1]]
    return pl.pallas_call(
        _ret_kernel,
        out_shape=(jax.ShapeDtypeStruct((B, L, MIX), BF16),
                   jax.ShapeDtypeStruct((B, RET_HEADS, RET_HD, RET_HD), F32)),
        grid=(B, L // (SUBCHUNKS * Lc)),
        in_specs=[pl.BlockSpec((None, SUBCHUNKS * Lc,2048), lambda b, c: (b, c, Z_RET // 2048)),
                  pl.BlockSpec((SUBCHUNKS * Lc, MIX), lambda b, c: (c, 0)),
                  pl.BlockSpec((SUBCHUNKS * Lc, MIX), lambda b, c: (c, 0))]
                 + [_const_spec(a) for a in consts] + [_lspec(P["ret_gn"], l)],
        out_specs=(pl.BlockSpec((None, SUBCHUNKS * Lc,MIX), lambda b, c: (b, c, 0)),
                   pl.BlockSpec((None, RET_HEADS, RET_HD, RET_HD), lambda b, c: (b, 0, 0, 0))),
        scratch_shapes=[pltpu.VMEM((MIX, MIX), F32)],
        compiler_params=_cparams("parallel", "arbitrary"),
        name="ret_prompt",
    )(z3, cos_l, sin_l, *consts, P["ret_gn"])


def _lru_gates(xc, wg_ref, bg_ref, lam_ref):
    rg = _bdot(xc, wg_ref[...]) + bg_ref[...]
    r = jax.nn.sigmoid(rg[:, 0:MIX])
    i = jax.nn.sigmoid(rg[:, MIX:2 * MIX])
    log_a = -LRU_C * r * jax.nn.softplus(-lam_ref[...])
    a = jnp.exp(log_a)
    bx = jnp.sqrt(_neg_expm1_2x(log_a, a)) * (i * xc)
    return a, bx


def _lru_kernel(z_ref, *refs):
    y_ref = refs[-5]
    for s in range(SUBCHUNKS):
        _lru_chunk(s, _sub_rows(z_ref, s), *refs[:-5], _sub_rows(y_ref, s), *refs[-4:])


def _lru_chunk(s, z_ref, cw_ref, cb_ref, wg_ref, bg_ref, lam_ref,
               y_ref, st_ref, buf_ref, pad_ref, h_ref):
    Lc = z_ref.shape[0]
    zz = z_ref[...].astype(F32)
    gate = zz[:, 0:MIX]
    x = zz[:, MIX:2 * MIX]

    def _():
        st_ref[...] = jnp.zeros(st_ref.shape, F32)

    _when_first(s, _)
    xc = _conv_chunk(s, x, pad_ref, cw_ref, cb_ref)
    a, bx = _lru_gates(xc, wg_ref, bg_ref, lam_ref)
    ng = Lc // SUBLANES
    a3 = a.reshape(ng, SUBLANES, MIX)
    b3 = bx.reshape(ng, SUBLANES, MIX)
    sub = lax.broadcasted_iota(jnp.int32, a3.shape, 1)
    step = 1
    while step < SUBLANES:
        keep = sub >= step
        b3 = jnp.where(keep, b3 + a3 * pltpu.roll(b3, step, 1), b3)
        a3 = jnp.where(keep, a3 * pltpu.roll(a3, step, 1), a3)
        step *= 2
    carry = st_ref[...]
    for g in range(ng):
        hg = b3[g] + a3[g] * jnp.broadcast_to(carry, (SUBLANES, MIX))
        h_ref[g * SUBLANES:(g + 1) * SUBLANES, :] = hg
        carry = hg[SUBLANES - 1:SUBLANES, :]
    st_ref[...] = carry
    y_ref[...] = (h_ref[...] * jax.nn.gelu(gate)).astype(y_ref.dtype)

    def _():
        buf_ref[...] = pad_ref[Lc + 8 - (CONV_K - 1):Lc + 8, :]

    _when_last(s, _)
    _conv_finish(pad_ref, Lc)


def _lru_prompt(z3, P, l):
    B, L, _ = z3.shape
    Lc = CHUNK
    params = [P[k] for k in ("lru_cw", "lru_cb", "lru_wg", "lru_bg", "lru_lam")]
    return pl.pallas_call(
        _lru_kernel,
        out_shape=(jax.ShapeDtypeStruct((B, L, MIX), BF16),
                   jax.ShapeDtypeStruct((B, 1, MIX), F32),
                   jax.ShapeDtypeStruct((B, CONV_K - 1, MIX), F32)),
        grid=(B, L // (SUBCHUNKS * Lc)),
        in_specs=[pl.BlockSpec((None, SUBCHUNKS * Lc,1024), lambda b, c: (b, c, Z_LRU // 1024))]
                 + [_lspec(a, l) for a in params],
        out_specs=(pl.BlockSpec((None, SUBCHUNKS * Lc,MIX), lambda b, c: (b, c, 0)),
                   pl.BlockSpec((None, 1, MIX), lambda b, c: (b, 0, 0)),
                   pl.BlockSpec((None, CONV_K - 1, MIX), lambda b, c: (b, 0, 0))),
        scratch_shapes=[pltpu.VMEM((Lc + 8, MIX), F32), pltpu.VMEM((Lc, MIX), F32)],
        compiler_params=_cparams("parallel", "arbitrary"),
        name="lru_prompt",
    )(z3, *params)


S5_LB = 4
S5_LAGS = 4
assert 2 * S5_LAGS == SUBLANES


def _s5_project_in(u, wbr_ref, wbi_ref, xr_ref, xi_ref, lags):
    shifted = [u]
    if lags > 1:
        sub = lax.broadcasted_iota(jnp.int32, u.shape, 0) & (lags - 1)
        shifted += [jnp.where(sub >= d, pltpu.roll(u, d, 0), 0.0) for d in range(1, lags)]
    shifted = [s.astype(BF16) for s in shifted]
    for kb in range(S5_LB):
        lhs = jnp.concatenate([s[:, kb * 128:(kb + 1) * 128] for s in shifted], axis=1)
        k = lags * 128
        xr_ref[:, kb * 512:(kb + 1) * 512] = jnp.dot(lhs, wbr_ref[kb, 0:k, :], preferred_element_type=F32)
        xi_ref[:, kb * 512:(kb + 1) * 512] = jnp.dot(lhs, wbi_ref[kb, 0:k, :], preferred_element_type=F32)


def _s5_project_out(xr, xi, u, wcr_ref, wci_ref, d_ref, wglu_ref):
    ys = []
    for kb in range(S5_LB):
        sl = slice(kb * 512, (kb + 1) * 512)
        ys.append(_bdot(xr[:, sl], wcr_ref[kb]) - _bdot(xi[:, sl], wci_ref[kb]))
    y = jnp.concatenate(ys, axis=1) + d_ref[...] * u
    y = jax.nn.gelu(y)
    return y * jax.nn.sigmoid(_bdot(y, wglu_ref[...]))


def _s5_kernel(z_ref, *refs):
    y_ref = refs[-5]
    for s in range(SUBCHUNKS):
        _s5_chunk(s, _sub_rows(z_ref, s), *refs[:-5], _sub_rows(y_ref, s), *refs[-4:])


def _s5_chunk(s, z_ref, wbr_ref, wbi_ref, wcr_ref, wci_ref, pr_ref, pi_ref, d_ref, wglu_ref, qr_ref, qi_ref,
              y_ref, sr_ref, si_ref, xr_ref, xi_ref):
    Lc = z_ref.shape[0]
    u = z_ref[...].astype(F32)

    def _():
        sr_ref[...] = jnp.zeros(sr_ref.shape, F32)
        si_ref[...] = jnp.zeros(si_ref.shape, F32)

    _when_first(s, _)
    _s5_project_in(u, wbr_ref, wbi_ref, xr_ref, xi_ref, S5_LAGS)
    ng = Lc // SUBLANES
    x3r = xr_ref[...].reshape(ng, SUBLANES, S5_CH)
    x3i = xi_ref[...].reshape(ng, SUBLANES, S5_CH)
    tr = jnp.broadcast_to(x3r[:, S5_LAGS - 1:S5_LAGS, :], x3r.shape)
    ti = jnp.broadcast_to(x3i[:, S5_LAGS - 1:S5_LAGS, :], x3i.shape)
    mr, mi = qr_ref[...][None], qi_ref[...][None]
    x3r, x3i = x3r + (mr * tr - mi * ti), x3i + (mr * ti + mi * tr)
    pcr, pci = pr_ref[...], pi_ref[...]
    cr, ci = sr_ref[...], si_ref[...]
    for g in range(ng):
        br = jnp.broadcast_to(cr, (SUBLANES, S5_CH))
        bi = jnp.broadcast_to(ci, (SUBLANES, S5_CH))
        gr = x3r[g] + (pcr * br - pci * bi)
        gi = x3i[g] + (pcr * bi + pci * br)
        xr_ref[g * SUBLANES:(g + 1) * SUBLANES, :] = gr
        xi_ref[g * SUBLANES:(g + 1) * SUBLANES, :] = gi
        cr, ci = gr[SUBLANES - 1:SUBLANES, :], gi[SUBLANES - 1:SUBLANES, :]
    sr_ref[...] = cr
    si_ref[...] = ci
    y_ref[...] = _s5_project_out(xr_ref[...], xi_ref[...], u, wcr_ref, wci_ref, d_ref, wglu_ref).astype(y_ref.dtype)


S5_PARAMS = ("s5_wbr", "s5_wbi", "s5_wcr", "s5_wci", "s5_pr", "s5_pi", "s5_d", "s5_glu", "s5_qr", "s5_qi")


def _s5_prompt(z3, P, l):
    B, L, _ = z3.shape
    Lc = CHUNK
    params = [P[k] for k in S5_PARAMS]
    return pl.pallas_call(
        _s5_kernel,
        out_shape=(jax.ShapeDtypeStruct((B, L, MIX), BF16),
                   jax.ShapeDtypeStruct((B, 1, S5_CH), F32),
                   jax.ShapeDtypeStruct((B, 1, S5_CH), F32)),
        grid=(B, L // (SUBCHUNKS * Lc)),
        in_specs=[pl.BlockSpec((None, SUBCHUNKS * Lc,MIX), lambda b, c: (b, c, Z_S5 // MIX))]
                 + [_lspec(a, l) for a in params],
        out_specs=(pl.BlockSpec((None, SUBCHUNKS * Lc,MIX), lambda b, c: (b, c, 0)),
                   pl.BlockSpec((None, 1, S5_CH), lambda b, c: (b, 0, 0)),
                   pl.BlockSpec((None, 1, S5_CH), lambda b, c: (b, 0, 0))),
        scratch_shapes=[pltpu.VMEM((Lc, S5_CH), F32), pltpu.VMEM((Lc, S5_CH), F32)],
        compiler_params=_cparams("parallel", "arbitrary"),
        name="s5_prompt",
    )(z3, *params)


def _merge_kernel(y0_ref, y1_ref, y2_ref, y3_ref, zg_ref, h_ref, wb_ref, wo_ref, o_ref):
    acc = None
    for k, y_ref in enumerate((y0_ref, y1_ref, y2_ref, y3_ref)):
        br = _bdot(y_ref[...], wb_ref[k])
        t = jax.nn.sigmoid(zg_ref[:, k * D_MODEL:(k + 1) * D_MODEL].astype(F32)) * br
        acc = t if acc is None else acc + t
    o_ref[...] = h_ref[...] + _bdot(acc, wo_ref[...])


def _merge(ys, z, h, P, l, tm):
    T = h.shape[0]
    rows = lambda w: pl.BlockSpec((tm, w), lambda i: (i, 0))
    return pl.pallas_call(
        _merge_kernel,
        out_shape=jax.ShapeDtypeStruct((T, D_MODEL), F32),
        grid=(T // tm,),
        in_specs=[rows(MIX), rows(MIX), rows(MIX), rows(MIX),
                  pl.BlockSpec((tm, 4 * D_MODEL), lambda i: (i, Z_MERGE)),
                  rows(D_MODEL),
                  _lspec(P["w_branch"], l), _lspec(P["w_out"], l)],
        out_specs=rows(D_MODEL),
        compiler_params=_cparams("parallel"),
        name="merge",
    )(*ys, z, h, P["w_branch"], P["w_out"])


def _top2_gates(logits):
    lane, i1, i2, w1, w2 = _top2(logits)
    return jnp.where(lane == i1, w1, 0.0) + jnp.where(lane == i2, w2, 0.0)


def _ffn_kernel(moe, final_norm, *refs):
    if moe:
        h_ref, g_ref, rt_ref, w1_ref, w3_ref, w2_ref, gf_ref, o_ref, hn_ref, acc_ref, gate_ref = refs
    else:
        h_ref, g_ref, w1_ref, w3_ref, w2_ref, gf_ref, o_ref, hn_ref, acc_ref = refs
    e = pl.program_id(1)
    ne = pl.num_programs(1)

    @pl.when(e == 0)
    def _():
        hn = _rmsnorm(h_ref[...], g_ref[...])
        hn_ref[...] = hn.astype(BF16)
        acc_ref[...] = jnp.zeros(acc_ref.shape, F32)
        if moe:
            lane = lax.broadcasted_iota(jnp.int32, (hn.shape[0], LANES), 1)
            logits = jnp.where(lane < N_EXPERTS, _hdot(hn, rt_ref[...]), -jnp.inf)
            gate_ref[...] = _top2_gates(logits)

    hn = hn_ref[...]
    a = jnp.dot(hn, w1_ref[...], preferred_element_type=F32)
    b = jnp.dot(hn, w3_ref[...], preferred_element_type=F32)
    o = _bdot(_silu(a) * b, w2_ref[...])
    if moe:
        lane = lax.broadcasted_iota(jnp.int32, gate_ref.shape, 1)
        ge = jnp.sum(jnp.where(lane == e, gate_ref[...], 0.0), axis=-1, keepdims=True)
        o = ge * o
    acc_ref[...] += o

    @pl.when(e == ne - 1)
    def _():
        out = h_ref[...] + acc_ref[...]
        if final_norm:
            out = _rmsnorm(out, gf_ref[...])
        o_ref[...] = out


def _ffn(h, P, l, tm, final_norm):
    T = h.shape[0]
    moe = l % 2 == 1
    j = l // 2
    tf = D_FF_TILE
    row_spec = pl.BlockSpec((tm, D_MODEL), lambda i, e: (i, 0))
    gfinal = P["norm_final"]
    if moe:
        w1, w3, w2 = P["moe_w1"], P["moe_w3"], P["moe_w2"]
        ne = w1.shape[1]
        wspecs = [pl.BlockSpec((None, None, D_MODEL, tf), lambda i, e: (j, e, 0, 0)),
                  pl.BlockSpec((None, None, D_MODEL, tf), lambda i, e: (j, e, 0, 0)),
                  pl.BlockSpec((None, None, tf, D_MODEL), lambda i, e: (j, e, 0, 0))]
        in_specs = ([row_spec, _lspec(P["norm_ffn"], l), _lspec(P["moe_router"], j)] + wspecs
                    + [_const_spec(gfinal)])
        args = (h, P["norm_ffn"], P["moe_router"], w1, w3, w2, gfinal)
        scratch = [pltpu.VMEM((tm, D_MODEL), BF16), pltpu.VMEM((tm, D_MODEL), F32), pltpu.VMEM((tm, LANES), F32)]
    else:
        w1, w3, w2 = P["ffn_w1"], P["ffn_w3"], P["ffn_w2"]
        ne = w1.shape[2] // tf
        wspecs = [pl.BlockSpec((None, D_MODEL, tf), lambda i, e: (j, 0, e)),
                  pl.BlockSpec((None, D_MODEL, tf), lambda i, e: (j, 0, e)),
                  pl.BlockSpec((None, tf, D_MODEL), lambda i, e: (j, e, 0))]
        in_specs = [row_spec, _lspec(P["norm_ffn"], l)] + wspecs + [_const_spec(gfinal)]
        args = (h, P["norm_ffn"], w1, w3, w2, gfinal)
        scratch = [pltpu.VMEM((tm, D_MODEL), BF16), pltpu.VMEM((tm, D_MODEL), F32)]
    return pl.pallas_call(
        functools.partial(_ffn_kernel, moe, final_norm),
        out_shape=jax.ShapeDtypeStruct((T, D_MODEL), F32),
        grid=(T // tm, ne),
        in_specs=in_specs,
        out_specs=row_spec,
        scratch_shapes=scratch,
        compiler_params=_cparams("parallel", "arbitrary"),
        name="moe" if moe else "ffn",
    )(*args)


MOE_ROWS = 512
ROUTE_LANES = ("e1", "e2", "r1", "r2", "w1", "w2")


def _top2(logits):
    lane = lax.broadcasted_iota(jnp.int32, logits.shape, 1).astype(F32)
    big = float(LANES)
    m1 = jnp.max(logits, axis=-1, keepdims=True)
    i1 = jnp.min(jnp.where(logits == m1, lane, big), axis=-1, keepdims=True)
    rest = jnp.where(lane == i1, -jnp.inf, logits)
    m2 = jnp.max(rest, axis=-1, keepdims=True)
    i2 = jnp.min(jnp.where(rest == m2, lane, big), axis=-1, keepdims=True)
    e2 = jnp.exp(m2 - m1)
    den = 1.0 + e2
    return lane, i1, i2, 1.0 / den, e2 / den


def _moe_route_kernel(h_ref, g_ref, rt_ref, ltri_ref, hn_ref, info_ref, cnt_ref, base_ref):
    i = pl.program_id(0)

    @pl.when(i == 0)
    def _():
        base_ref[...] = jnp.zeros(base_ref.shape, F32)

    hn = _rmsnorm(h_ref[...], g_ref[...])
    hn_ref[...] = hn
    lane_i = lax.broadcasted_iota(jnp.int32, (hn.shape[0], LANES), 1)
    logits = jnp.where(lane_i < N_EXPERTS, _hdot(hn, rt_ref[...]), -jnp.inf)
    lane, i1, i2, w1, w2 = _top2(logits)
    oh1 = (lane == i1).astype(F32)
    oh2 = (lane == i2).astype(F32)
    oh = oh1 + oh2
    before = jnp.dot(ltri_ref[...], oh.astype(BF16), preferred_element_type=F32)
    rank = base_ref[...] + before
    r1 = jnp.sum(oh1 * rank, axis=-1, keepdims=True)
    r2 = jnp.sum(oh2 * rank, axis=-1, keepdims=True)
    base_ref[...] += jnp.sum(oh, axis=0, keepdims=True)
    fields = dict(e1=i1, e2=i2, r1=r1, r2=r2, w1=w1, w2=w2)
    info = jnp.zeros(lane.shape, F32)
    for k, name in enumerate(ROUTE_LANES):
        info = jnp.where(lane_i == k, fields[name], info)
    info_ref[...] = info

    @pl.when(i == pl.num_programs(0) - 1)
    def _():
        cnt_ref[...] = base_ref[...]


def _moe_route(h, P, l, tm):
    T = h.shape[0]
    j = l // 2
    ltri = jnp.asarray(np.tril(np.ones((tm, tm), np.float32), -1), BF16)
    return pl.pallas_call(
        _moe_route_kernel,
        out_shape=(jax.ShapeDtypeStruct((T, D_MODEL), F32), jax.ShapeDtypeStruct((T, LANES), F32),
                   jax.ShapeDtypeStruct((1, LANES), F32)),
        grid=(T // tm,),
        in_specs=[pl.BlockSpec((tm, D_MODEL), lambda i: (i, 0)), _lspec(P["norm_ffn"], l),
                  _lspec(P["moe_router"], j), _const_spec(ltri)],
        out_specs=(pl.BlockSpec((tm, D_MODEL), lambda i: (i, 0)), pl.BlockSpec((tm, LANES), lambda i: (i, 0)),
                   pl.BlockSpec((1, LANES), lambda i: (0, 0))),
        scratch_shapes=[pltpu.VMEM((1, LANES), F32)],
        compiler_params=_cparams("arbitrary"),
        name="moe_route",
    )(h, P["norm_ffn"], P["moe_router"], ltri)


def _moe_plan(info, cnt, tm, n_tiles_max):
    T = info.shape[0]
    rows = MOE_ROWS
    count = cnt[0, :N_EXPERTS].astype(jnp.int32)
    tiles_e = (count + rows - 1) // rows
    first_tile = jnp.cumsum(tiles_e) - tiles_e
    start = first_tile * rows
    experts = jnp.arange(N_EXPERTS, dtype=jnp.int32)

    def row_of(e, r):
        sel = e.astype(jnp.int32)[:, None] == experts[None, :]
        return jnp.sum(jnp.where(sel, start[None, :], 0), axis=1) + r.astype(jnp.int32)

    pos = jnp.stack([row_of(info[:, 0], info[:, 2]), row_of(info[:, 1], info[:, 3])], axis=0)
    pos = jnp.transpose(pos.reshape(2, T // tm, tm), (1, 0, 2))
    n_tiles = jnp.sum(tiles_e)
    t = jnp.minimum(jnp.arange(n_tiles_max, dtype=jnp.int32), n_tiles - 1)
    tile_expert = jnp.sum((first_tile[None, :] <= t[:, None]).astype(jnp.int32), axis=1) - 1
    last_tile = first_tile + tiles_e - 1
    return pos, tile_expert, n_tiles.reshape(1), last_tile, tiles_e


def _moe_dispatch_kernel(last_ref, tiles_ref, nt_ref, pos_ref, hn_ref, xs_ref, zero_ref, sem):
    tm = hn_ref.shape[0]

    @pl.when(pl.program_id(0) == 0)
    def _():
        zero_ref[...] = jnp.zeros(zero_ref.shape, F32)

        def clear(tile):
            row0 = pl.multiple_of(tile * MOE_ROWS, MOE_ROWS)
            cp = pltpu.make_async_copy(zero_ref, xs_ref.at[pl.ds(row0, MOE_ROWS), :], sem)
            cp.start()
            cp.wait()

        for e in range(N_EXPERTS):
            @pl.when(tiles_ref[e] > 0)
            def _():
                clear(last_ref[e])

        def clear_tail(tile, c):
            clear(tile)
            return c

        lax.fori_loop(nt_ref[0], xs_ref.shape[0] // MOE_ROWS, clear_tail, 0)

    def row_copy(j, slot):
        return pltpu.make_async_copy(hn_ref.at[pl.ds(j, 1), :], xs_ref.at[pl.ds(pos_ref[slot, j], 1), :], sem)

    def issue(j, c):
        row_copy(j, 0).start()
        row_copy(j, 1).start()
        return c

    lax.fori_loop(0, tm, issue, 0, unroll=8)
    for _ in range(2):
        pltpu.make_async_copy(hn_ref, xs_ref.at[pl.ds(0, tm), :], sem).wait()


def _moe_dispatch(hn, pos, last_tile, tiles_e, n_tiles, n_rows):
    T = hn.shape[0]
    tm = pos.shape[2]
    gs = pltpu.PrefetchScalarGridSpec(
        num_scalar_prefetch=3, grid=(T // tm,),
        in_specs=[pl.BlockSpec((None, 2, tm), lambda i, *_: (i, 0, 0), memory_space=pltpu.SMEM),
                  pl.BlockSpec((tm, D_MODEL), lambda i, *_: (i, 0))],
        out_specs=pl.BlockSpec(memory_space=pl.ANY),
        scratch_shapes=[pltpu.VMEM((MOE_ROWS, D_MODEL), F32), pltpu.SemaphoreType.DMA])
    return pl.pallas_call(
        _moe_dispatch_kernel, grid_spec=gs,
        out_shape=jax.ShapeDtypeStruct((n_rows, D_MODEL), F32),
        compiler_params=_cparams("arbitrary"),
        name="moe_dispatch",
    )(last_tile, tiles_e, n_tiles, pos, hn)


def _moe_group_kernel(te_ref, nt_ref, x_ref, w1_ref, w3_ref, w2_ref, o_ref):
    live = pl.program_id(0) < nt_ref[0]

    @pl.when(live)
    def _():
        x = x_ref[...].astype(BF16)
        a = jnp.dot(x, w1_ref[...], preferred_element_type=F32)
        b = jnp.dot(x, w3_ref[...], preferred_element_type=F32)
        o_ref[...] = _bdot(_silu(a) * b, w2_ref[...])

    @pl.when(jnp.logical_not(live))
    def _():
        o_ref[...] = jnp.zeros(o_ref.shape, F32)


def _moe_group(xs, P, j, tile_expert, n_tiles):
    n_rows = xs.shape[0]
    tf = D_FF_TILE
    wmap = lambda i, te, nt: (j, te[i], 0, 0)
    rmap = lambda i, te, nt: (i, 0)
    gs = pltpu.PrefetchScalarGridSpec(
        num_scalar_prefetch=2, grid=(n_rows // MOE_ROWS,),
        in_specs=[pl.BlockSpec((MOE_ROWS, D_MODEL), rmap),
                  pl.BlockSpec((None, None, D_MODEL, tf), wmap),
                  pl.BlockSpec((None, None, D_MODEL, tf), wmap),
                  pl.BlockSpec((None, None, tf, D_MODEL), wmap)],
        out_specs=pl.BlockSpec((MOE_ROWS, D_MODEL), rmap))
    return pl.pallas_call(
        _moe_group_kernel, grid_spec=gs,
        out_shape=jax.ShapeDtypeStruct((n_rows, D_MODEL), F32),
        compiler_params=_cparams("arbitrary"),
        name="moe_group",
    )(tile_expert, n_tiles, xs, P["moe_w1"], P["moe_w3"], P["moe_w2"])


def _moe_combine_kernel(final_norm, pos_ref, o_ref, h_ref, info_ref, gf_ref, out_ref, a_ref, b_ref, sem):
    tm = h_ref.shape[0]

    def row_copy(j, slot, dst):
        return pltpu.make_async_copy(o_ref.at[pl.ds(pos_ref[slot, j], 1), :], dst.at[pl.ds(j, 1), :], sem)

    def issue(j, c):
        row_copy(j, 0, a_ref).start()
        row_copy(j, 1, b_ref).start()
        return c

    lax.fori_loop(0, tm, issue, 0, unroll=8)
    for dst in (a_ref, b_ref):
        pltpu.make_async_copy(o_ref.at[pl.ds(0, tm), :], dst, sem).wait()
    k1, k2 = ROUTE_LANES.index("w1"), ROUTE_LANES.index("w2")
    info = info_ref[...]
    w1, w2 = info[:, k1:k1 + 1], info[:, k2:k2 + 1]
    out = h_ref[...] + (w1 * a_ref[...] + w2 * b_ref[...])
    if final_norm:
        out = _rmsnorm(out, gf_ref[...])
    out_ref[...] = out


def _moe_combine(o, h, info, pos, P, final_norm):
    T = h.shape[0]
    tm = pos.shape[2]
    gfinal = P["norm_final"]
    row_spec = pl.BlockSpec((tm, D_MODEL), lambda i: (i, 0))
    return pl.pallas_call(
        functools.partial(_moe_combine_kernel, final_norm),
        out_shape=jax.ShapeDtypeStruct((T, D_MODEL), F32),
        grid=(T // tm,),
        in_specs=[pl.BlockSpec((None, 2, tm), lambda i: (i, 0, 0), memory_space=pltpu.SMEM),
                  pl.BlockSpec(memory_space=pl.ANY), row_spec,
                  pl.BlockSpec((tm, LANES), lambda i: (i, 0)), _const_spec(gfinal)],
        out_specs=row_spec,
        scratch_shapes=[pltpu.VMEM((tm, D_MODEL), F32), pltpu.VMEM((tm, D_MODEL), F32), pltpu.SemaphoreType.DMA],
        compiler_params=_cparams("arbitrary"),
        name="moe_combine",
    )(pos, o, h, info, gfinal)


def _moe_routed(h, P, l, tm, final_norm):
    T = h.shape[0]
    n_tiles_max = (2 * T) // MOE_ROWS + N_EXPERTS
    hn, info, cnt = _moe_route(h, P, l, tm)
    pos, tile_expert, n_tiles, last_tile, tiles_e = _moe_plan(info, cnt, tm, n_tiles_max)
    xs = _moe_dispatch(hn, pos, last_tile, tiles_e, n_tiles, n_tiles_max * MOE_ROWS)
    o = _moe_group(xs, P, l // 2, tile_expert, n_tiles)
    return _moe_combine(o, h, info, pos, P, final_norm)


def _conv_step(x, buf_ref, buf_o, w_ref, b_ref):
    out = b_ref[...]
    for k in range(CONV_K - 1):
        out = out + buf_ref[k] * w_ref[k:k + 1, :]
        if k > 0:
            buf_o[k - 1] = buf_ref[k]
    out = out + x * w_ref[CONV_K - 1:CONV_K, :]
    buf_o[CONV_K - 2] = x
    return out


def _step_pre_kernel(pos_cos_ref, pos_sin_ref, gam_ref, ehn_ref, z_ref, sbuf_ref, lbuf_ref, lst_ref, s5r_ref, s5i_ref,
                     scw_ref, scb_ref, dtb_ref, alog_ref,
                     lcw_ref, lcb_ref, wg_ref, bg_ref, lam_ref,
                     wbr_ref, wbi_ref, wcr_ref, wci_ref, pr_ref, pi_ref, s5d_ref, wglu_ref, qr_ref, qi_ref,
                     kqv_ref, dec_ref, sbuf_o, lbuf_o, lst_o, s5r_o, s5i_o, ys5_o, ylru_o, xs_o,
                     xr_ref, xi_ref):
    zz = z_ref[...].astype(F32)
    xbc = zz[:, Z_SSD + MIX:Z_SSD + MIX + SSD_CONV]
    dt_raw = zz[:, Z_SSD + MIX + SSD_CONV:Z_SSD + MIX + SSD_CONV + LANES]
    xc = _silu(_conv_step(xbc, sbuf_ref, sbuf_o, scw_ref, scb_ref))
    xs = xc[:, 0:MIX]
    xs_o[...] = xs
    nbc = SSD_GROUPS * SSD_STATE
    bm = xc[:, MIX:MIX + nbc]
    cm = xc[:, MIX + nbc:MIX + 2 * nbc]
    dt = jax.nn.softplus(dt_raw + dtb_ref[...])
    a = -jnp.exp(alog_ref[...])
    rep = SSD_HEADS // SSD_GROUPS
    kqv_ref[0, 0] = (_group_repeat_lanes(bm, rep) * _split3_dot(dt, ehn_ref[...])).T
    kqv_ref[0, 1] = _group_repeat_lanes(cm, rep).T
    kqv_ref[0, 2] = xs.T
    dec_ref[0] = jnp.exp(dt * a).T[0:SSD_HEADS, :]
    q = _rotary_lanes(zz[:, Z_RET:Z_RET + MIX], pos_cos_ref[...], pos_sin_ref[...]) * (RET_HD ** -0.5)
    k = _rotary_lanes(zz[:, Z_RET + MIX:Z_RET + 2 * MIX], pos_cos_ref[...], pos_sin_ref[...])
    kqv_ref[1, 0] = k.T
    kqv_ref[1, 1] = q.T
    kqv_ref[1, 2] = zz[:, Z_RET + 2 * MIX:Z_RET + 3 * MIX].T
    dec_ref[1] = gam_ref[...]
    gate = zz[:, Z_LRU:Z_LRU + MIX]
    lx = zz[:, Z_LRU + MIX:Z_LRU + 2 * MIX]
    lconv = _conv_step(lx, lbuf_ref, lbuf_o, lcw_ref, lcb_ref)
    la, lbx = _lru_gates(lconv, wg_ref, bg_ref, lam_ref)
    hl = lbx + la * lst_ref[...]
    lst_o[...] = hl
    ylru_o[...] = hl * jax.nn.gelu(gate)
    u = zz[:, Z_S5:Z_S5 + MIX]
    _s5_project_in(u, wbr_ref, wbi_ref, xr_ref, xi_ref, 1)
    lr, li = pr_ref[0:1, :], pi_ref[0:1, :]
    s0r, s0i = s5r_ref[...], s5i_ref[...]
    xr = xr_ref[...] + (lr * s0r - li * s0i)
    xi = xi_ref[...] + (lr * s0i + li * s0r)
    s5r_o[...] = xr
    s5i_o[...] = xi
    ys5_o[...] = _s5_project_out(xr, xi, u, wcr_ref, wci_ref, s5d_ref, wglu_ref)


def _step_state_kernel(kqv_ref, dec_ref, st_ref, o_st_ref, y_ref):
    h = pl.program_id(0)
    d = dec_ref[pl.ds(h, 1), :]
    v = kqv_ref[2]
    acc = jnp.zeros(v.shape, F32)
    for n in range(st_ref.shape[0]):
        s_new = d * st_ref[n] + kqv_ref[0, n:n + 1, :] * v
        o_st_ref[n] = s_new
        acc = acc + kqv_ref[1, n:n + 1, :] * s_new
    y_ref[...] = acc


def _step_post_kernel(yssd_ref, yret_ref, xs_ref, z_ref, dlane_ref, ng_ref, gn_ref, yssd_o, yret_o):
    zz_gate = z_ref[:, Z_SSD:Z_SSD + MIX]
    y = yssd_ref[...].T + dlane_ref[...] * xs_ref[...]
    y = y * _silu(zz_gate)
    yssd_o[...] = _rmsnorm(y, ng_ref[...])
    rgate = z_ref[:, Z_RET + 3 * MIX:Z_RET + 4 * MIX]
    yret_o[...] = _silu(rgate) * (_group_norm_lanes(yret_ref[...].T, RET_HD) * gn_ref[...])


def _sample_mixers(z, views, big, layer, P, pos):
    Bs = z.shape[0]
    H = SSD_HEADS
    cos_l, sin_l = _rope_tables(np.asarray([pos]))
    gam = jnp.asarray(np.repeat(_ret_gammas()[:, None], Bs, axis=1), F32)
    consts = [cos_l, sin_l, gam, _head_expand(SSD_HEADS, SSD_HD)]
    states = [views[k] for k in ("ssd_conv", "lru_conv", "lru", "s5_re", "s5_im")]
    params = [P[k] for k in ("ssd_cw", "ssd_cb", "ssd_dtb", "ssd_alog",
                             "lru_cw", "lru_cb", "lru_wg", "lru_bg", "lru_lam") + S5_PARAMS]
    pre_out = (jax.ShapeDtypeStruct((2, 3, MIX, Bs), F32),
               jax.ShapeDtypeStruct((2, H, Bs), F32),
               jax.ShapeDtypeStruct(views["ssd_conv"].shape[1:], F32),
               jax.ShapeDtypeStruct(views["lru_conv"].shape[1:], F32),
               jax.ShapeDtypeStruct((Bs, MIX), F32),
               jax.ShapeDtypeStruct((Bs, S5_CH), F32), jax.ShapeDtypeStruct((Bs, S5_CH), F32),
               jax.ShapeDtypeStruct((Bs, MIX), F32), jax.ShapeDtypeStruct((Bs, MIX), F32),
               jax.ShapeDtypeStruct((Bs, MIX), F32))
    (kqv, dec, sbuf_n, lbuf_n, lst_n, s5r_n, s5i_n, y_s5, y_lru, xs) = pl.pallas_call(
        _step_pre_kernel,
        out_shape=pre_out,
        grid=(1,),
        in_specs=[_const_spec(a) for a in consts] + [_const_spec(z)]
                 + [_lspec(a, layer) for a in states] + [_lspec(a, layer) for a in params],
        out_specs=tuple(pl.BlockSpec(o.shape, lambda i, n=len(o.shape): (0,) * n) for o in pre_out),
        scratch_shapes=[pltpu.VMEM((Bs, S5_CH), F32), pltpu.VMEM((Bs, S5_CH), F32)],
        compiler_params=_cparams("arbitrary"),
        name="step_pre",
    )(*consts, z, *states, *params)

    new_big, yts = {}, []
    for m, name in enumerate(("ssd", "ret")):
        s_new, y_t = pl.pallas_call(
            _step_state_kernel,
            out_shape=(jax.ShapeDtypeStruct(big[name].shape, F32), jax.ShapeDtypeStruct((MIX, Bs), F32)),
            grid=(H,),
            in_specs=[pl.BlockSpec((None, 3, SSD_STATE, Bs), lambda h, m=m: (m, 0, h, 0)),
                      pl.BlockSpec((None, H, Bs), lambda h, m=m: (m, 0, 0)),
                      pl.BlockSpec((None, None, SSD_STATE, SSD_HD, Bs), lambda h: (layer, h, 0, 0, 0))],
            out_specs=(pl.BlockSpec((None, None, SSD_STATE, SSD_HD, Bs), lambda h: (layer, h, 0, 0, 0)),
                       pl.BlockSpec((SSD_HD, Bs), lambda h: (h, 0))),
            input_output_aliases={2: 0},
            compiler_params=_cparams("parallel"),
            name="step_state_" + name,
        )(kqv, dec, big[name])
        new_big[name] = s_new
        yts.append(y_t)

    post_in = [yts[0], yts[1], xs, z]
    post_par = [P[k] for k in ("ssd_dlane", "ssd_norm", "ret_gn")]
    post_out = (jax.ShapeDtypeStruct((Bs, MIX), F32), jax.ShapeDtypeStruct((Bs, MIX), F32))
    y_ssd, y_ret = pl.pallas_call(
        _step_post_kernel,
        out_shape=post_out,
        grid=(1,),
        in_specs=[_const_spec(a) for a in post_in] + [_lspec(a, layer) for a in post_par],
        out_specs=tuple(pl.BlockSpec(o.shape, lambda i: (0, 0)) for o in post_out),
        compiler_params=_cparams("arbitrary"),
        name="step_post",
    )(*post_in, *post_par)

    new = dict(ssd_conv=sbuf_n, lru_conv=lbuf_n, lru=lst_n, s5_re=s5r_n, s5_im=s5i_n)
    return (y_ssd, y_s5, y_lru, y_ret), new, new_big


def _block_diag8(w):
    lead = w.shape[:-3]
    n, r, c = w.shape[-3:]
    eye = jnp.eye(n, dtype=w.dtype)
    out = w[..., :, :, None, :] * eye[:, None, :, None]
    return out.reshape(lead + (n * r, n * c))


def _row(v):
    return v[:, None, :]


def _cmul(ar, ai, br, bi):
    return ar * br - ai * bi, ar * bi + ai * br


def _prep_params(W):
    depth = W["w_in"].shape[0]
    P = {}
    w_t = jnp.transpose(W["w_in"], (0, 2, 1))
    P["w_in"] = jnp.concatenate(
        [w_t[:, 4872:8968], w_t[:, 2824:4872], w_t[:, 1800:2824], w_t[:, 1288:1800], w_t[:, 0:1288],
         jnp.zeros((depth, Z_WIDTH - 8968, D_MODEL), w_t.dtype)], axis=1).astype(BF16)
    P["norm_mix"] = _row(W["norm_mix"])
    pad = ((0, 0), (0, LANES - SSD_HEADS))
    P["ssd_cw"] = W["ssd_conv_w"]
    P["ssd_cb"] = _row(W["ssd_conv_b"])
    P["ssd_dtb"] = _row(jnp.pad(W["ssd_dt_bias"], pad))
    P["ssd_alog"] = _row(jnp.pad(W["ssd_a_log"], pad))
    P["ssd_dlane"] = _row(jnp.repeat(W["ssd_d"], SSD_HD, axis=1))
    P["ssd_norm"] = _row(W["ssd_norm"])
    lr, li = W["s5_lambda_re"], W["s5_lambda_im"]
    dt = jnp.exp(W["s5_log_dt"])[:, :, None]
    mag = jnp.exp(lr * dt)
    br, bi = mag * jnp.cos(li * dt), mag * jnp.sin(li * dt)
    den = lr * lr + li * li
    qr, qi = _cmul(br - 1.0, bi, lr / den, -li / den)
    wr, wi = _cmul(qr[..., None], qi[..., None], W["s5_b_re"], W["s5_b_im"])
    lag_r, lag_i = [wr], [wi]
    for _ in range(S5_LAGS - 1):
        nr, ni = _cmul(lag_r[-1], lag_i[-1], br[..., None], bi[..., None])
        lag_r.append(nr)
        lag_i.append(ni)
    gb = S5_GROUPS // S5_LB

    def embed_in(lags):
        m = jnp.stack(lags, axis=1).reshape(depth, len(lags), S5_LB, gb, S5_STATE, S5_GDIM)
        m = _block_diag8(jnp.swapaxes(m, -1, -2))
        return jnp.swapaxes(m, 1, 2).reshape(depth, S5_LB, len(lags) * gb * S5_GDIM, gb * S5_STATE)

    def embed_out(m):
        m = m.reshape(depth, S5_LB, gb, S5_GDIM, S5_STATE)
        return _block_diag8(jnp.swapaxes(m, -1, -2))

    P["s5_wbr"] = embed_in(lag_r).astype(BF16)
    P["s5_wbi"] = embed_in(lag_i).astype(BF16)
    P["s5_wcr"] = embed_out(W["s5_c_re"]).astype(BF16)
    P["s5_wci"] = embed_out(W["s5_c_im"]).astype(BF16)
    pr, pi = [br.reshape(depth, 1, S5_CH)], [bi.reshape(depth, 1, S5_CH)]
    for _ in range(SUBLANES - 1):
        nr, ni = _cmul(pr[-1], pi[-1], pr[0], pi[0])
        pr.append(nr)
        pi.append(ni)
    P["s5_pr"] = jnp.concatenate(pr, axis=1)
    P["s5_pi"] = jnp.concatenate(pi, axis=1)
    half = jnp.zeros((depth, S5_LAGS, S5_CH), F32)
    P["s5_qr"] = jnp.concatenate([half, P["s5_pr"][:, :SUBLANES - S5_LAGS]], axis=1)
    P["s5_qi"] = jnp.concatenate([half, P["s5_pi"][:, :SUBLANES - S5_LAGS]], axis=1)
    P["s5_d"] = W["s5_d"].reshape(depth, 1, MIX)
    P["s5_glu"] = W["s5_glu"].astype(BF16)
    P["lru_cw"] = W["lru_conv_w"]
    P["lru_cb"] = _row(W["lru_conv_b"])
    P["lru_wg"] = jnp.concatenate([_block_diag8(W["lru_wa"]), _block_diag8(W["lru_wx"])], axis=2).astype(BF16)
    P["lru_bg"] = _row(jnp.concatenate([W["lru_ba"], W["lru_bx"]], axis=1))
    P["lru_lam"] = _row(W["lru_lambda"])
    P["ret_gn"] = _row(W["ret_gn"])
    P["w_branch"] = W["w_branch"].astype(BF16)
    P["w_out"] = W["w_out"].astype(BF16)
    P["norm_ffn"] = _row(W["norm_ffn"])
    P["norm_final"] = W["norm_final"].reshape(1, D_MODEL)
    for k in ("ffn_w1", "ffn_w3", "ffn_w2", "moe_w1", "moe_w3", "moe_w2"):
        P[k] = W[k].astype(BF16)
    P["moe_router"] = jnp.pad(W["moe_router"], ((0, 0), (0, 0), (0, LANES - N_EXPERTS)))
    return P


def _trunk_prompt(x, P):
    B, L, _ = x.shape
    T = B * L
    depth = P["w_in"].shape[0]
    h = x.reshape(T, D_MODEL)
    new = {k: [] for k in ("ssd", "ssd_conv", "s5_re", "s5_im", "lru", "lru_conv", "ret")}
    for l in range(depth):
        z = _inproj(h, P, l, min(1024, T), BF16)
        z3 = z.reshape(B, L, Z_WIDTH)
        y_ssd, s_ssd, buf_ssd = _ssd_prompt(z3, P, l)
        y_s5, s5r, s5i = _s5_prompt(z3, P, l)
        y_lru, s_lru, buf_lru = _lru_prompt(z3, P, l)
        y_ret, s_ret = _ret_prompt(z3, P, l)
        ys = tuple(y.reshape(T, MIX) for y in (y_ssd, y_s5, y_lru, y_ret))
        h = _merge(ys, z, h, P, l, min(256, T))
        mixer = _moe_routed if (l % 2 == 1 and T % MOE_ROWS == 0) else _ffn
        h = mixer(h, P, l, min(512, T), final_norm=(l == depth - 1))
        new["ssd"].append(s_ssd)
        new["ssd_conv"].append(buf_ssd)
        new["s5_re"].append(s5r.reshape(B, S5_GROUPS, S5_STATE))
        new["s5_im"].append(s5i.reshape(B, S5_GROUPS, S5_STATE))
        new["lru"].append(s_lru.reshape(B, MIX))
        new["lru_conv"].append(buf_lru)
        new["ret"].append(s_ret)
    return h.reshape(B, L, D_MODEL), {k: jnp.stack(v) for k, v in new.items()}


def _trunk_sample(x, pos, st, P):
    Bs = x.shape[0]
    depth = P["w_in"].shape[0]
    h = x.reshape(Bs, D_MODEL)
    big = dict(ssd=jnp.transpose(st["ssd"], (0, 2, 3, 4, 1)), ret=jnp.transpose(st["ret"], (0, 2, 3, 4, 1)))
    views = dict(ssd_conv=jnp.transpose(st["ssd_conv"], (0, 2, 1, 3)),
                 lru_conv=jnp.transpose(st["lru_conv"], (0, 2, 1, 3)),
                 lru=st["lru"],
                 s5_re=st["s5_re"].reshape(depth, Bs, S5_CH),
                 s5_im=st["s5_im"].reshape(depth, Bs, S5_CH))
    new = {k: [] for k in views}
    for l in range(depth):
        z = _inproj(h, P, l, Bs, F32)
        ys, nl, big = _sample_mixers(z, views, big, l, P, pos)
        h = _merge(ys, z, h, P, l, Bs)
        h = _ffn(h, P, l, Bs, final_norm=(l == depth - 1))
        for k in new:
            new[k].append(nl[k])
    out = {k: jnp.stack(v) for k, v in new.items()}
    out["ssd_conv"] = jnp.transpose(out["ssd_conv"], (0, 2, 1, 3))
    out["lru_conv"] = jnp.transpose(out["lru_conv"], (0, 2, 1, 3))
    out["s5_re"] = out["s5_re"].reshape(st["s5_re"].shape)
    out["s5_im"] = out["s5_im"].reshape(st["s5_im"].shape)
    out["ssd"] = jnp.transpose(big["ssd"], (0, 4, 1, 2, 3))
    out["ret"] = jnp.transpose(big["ret"], (0, 4, 1, 2, 3))
    return h.reshape(Bs, 1, D_MODEL), out


def kernel(x_prompt, x_sample, state_ssd, state_ssd_conv, state_s5_re, state_s5_im, state_lru, state_lru_conv, state_ret, norm_mix, w_in, ssd_conv_w, ssd_conv_b, ssd_dt_bias, ssd_a_log, ssd_d, ssd_norm, s5_lambda_re, s5_lambda_im, s5_b_re, s5_b_im, s5_c_re, s5_c_im, s5_d, s5_log_dt, s5_glu, lru_conv_w, lru_conv_b, lru_wa, lru_ba, lru_wx, lru_bx, lru_lambda, ret_gn, w_branch, w_out, norm_ffn, ffn_w1, ffn_w3, ffn_w2, moe_router, moe_w1, moe_w3, moe_w2, norm_final):
    W = dict(norm_mix=norm_mix, w_in=w_in, ssd_conv_w=ssd_conv_w, ssd_conv_b=ssd_conv_b, ssd_dt_bias=ssd_dt_bias,
             ssd_a_log=ssd_a_log, ssd_d=ssd_d, ssd_norm=ssd_norm, s5_lambda_re=s5_lambda_re,
             s5_lambda_im=s5_lambda_im, s5_b_re=s5_b_re, s5_b_im=s5_b_im, s5_c_re=s5_c_re, s5_c_im=s5_c_im,
             s5_d=s5_d, s5_log_dt=s5_log_dt, s5_glu=s5_glu, lru_conv_w=lru_conv_w, lru_conv_b=lru_conv_b,
             lru_wa=lru_wa, lru_ba=lru_ba, lru_wx=lru_wx, lru_bx=lru_bx, lru_lambda=lru_lambda, ret_gn=ret_gn,
             w_branch=w_branch, w_out=w_out, norm_ffn=norm_ffn, moe_router=moe_router, norm_final=norm_final,
             ffn_w1=ffn_w1, ffn_w3=ffn_w3, ffn_w2=ffn_w2, moe_w1=moe_w1, moe_w3=moe_w3, moe_w2=moe_w2)
    P = _prep_params(W)
    y_p, sp = _trunk_prompt(x_prompt, P)
    st = dict(ssd=state_ssd, ssd_conv=state_ssd_conv, s5_re=state_s5_re, s5_im=state_s5_im,
              lru=state_lru, lru_conv=state_lru_conv, ret=state_ret)
    past_len = 16384
    y_s, ss = _trunk_sample(x_sample, past_len, st, P)
    names = ("ssd", "ssd_conv", "s5_re", "s5_im", "lru", "lru_conv", "ret")
    return (y_p, y_s) + tuple(sp[n] for n in names) + tuple(ss[n] for n in names)
```

```python
import functools
import math

import jax
import jax.numpy as jnp
import numpy as np
from jax import lax
from jax.experimental import pallas as pl
from jax.experimental.pallas import tpu as pltpu

F32 = jnp.float32
BF16 = jnp.bfloat16
EPS = 1e-6

D_MODEL = 1024
MIX = 512
CONV_K = 4
CHUNK = 128
SSD_HEADS = 8
SSD_HD = 64
SSD_STATE = 64
SSD_GROUPS = 2
SSD_CONV = MIX + 2 * SSD_GROUPS * SSD_STATE
S5_GROUPS = 32
S5_GDIM = 16
S5_STATE = 64
S5_CH = S5_GROUPS * S5_STATE
LRU_BLOCKS = 8
LRU_C = 8.0
RET_HEADS = 8
RET_HD = 64
ROPE_BASE = 10000.0
N_EXPERTS = 8
D_FF_TILE = 1408

Z_MERGE = 0
Z_RET = 4096
Z_LRU = 6144
Z_S5 = 7168
Z_SSD = 7680
Z_WIDTH = 9216

VMEM_LIMIT = 56 * 1024 * 1024
LANES = 128
SUBLANES = 8


def _cparams(*sem):
    return pltpu.CompilerParams(dimension_semantics=sem, vmem_limit_bytes=VMEM_LIMIT)


def _bdot(a, b):
    return jnp.dot(a.astype(BF16), b.astype(BF16), preferred_element_type=F32)


def _bdot_nt(a, b):
    return lax.dot_general(a.astype(BF16), b.astype(BF16), (((1,), (1,)), ((), ())), preferred_element_type=F32)


def _bdot_tn(a, b):
    return lax.dot_general(a.astype(BF16), b.astype(BF16), (((0,), (0,)), ((), ())), preferred_element_type=F32)


def _hdot(a, b):
    return jnp.dot(a, b, precision=lax.Precision.HIGHEST, preferred_element_type=F32)


def _split3_dot(x, m01):
    hi = x.astype(BF16)
    r1 = x - hi.astype(F32)
    mid = r1.astype(BF16)
    lo = (r1 - mid.astype(F32)).astype(BF16)
    m = m01.astype(BF16)
    d = functools.partial(jnp.dot, preferred_element_type=F32)
    return (d(lo, m) + d(mid, m)) + d(hi, m)


def _rmsnorm(x, g):
    ms = jnp.mean(x * x, axis=-1, keepdims=True)
    return x * lax.rsqrt(ms + EPS) * g


def _silu(x):
    return x * jax.nn.sigmoid(x)


def _neg_expm1_2x(log_a, a):
    return jnp.tanh(-log_a) * (1.0 + a * a)


def _inproj_kernel(x_ref, g_ref, w_ref, o_ref, hn_ref):
    @pl.when(pl.program_id(1) == 0)
    def _():
        hn_ref[...] = _rmsnorm(x_ref[...], g_ref[...]).astype(BF16)

    z = lax.dot_general(hn_ref[...], w_ref[...], (((1,), (1,)), ((), ())), preferred_element_type=F32)
    o_ref[...] = z.astype(o_ref.dtype)


def _lspec(a, l):
    rest = tuple(a.shape[1:])
    return pl.BlockSpec((None,) + rest, lambda *_: (l,) + (0,) * len(rest))


def _inproj(x, P, l, tm, out_dtype):
    T = x.shape[0]
    tn = 2304
    return pl.pallas_call(
        _inproj_kernel,
        out_shape=jax.ShapeDtypeStruct((T, Z_WIDTH), out_dtype),
        grid=(T // tm, Z_WIDTH // tn),
        in_specs=[pl.BlockSpec((tm, D_MODEL), lambda i, j: (i, 0)),
                  _lspec(P["norm_mix"], l),
                  pl.BlockSpec((None, tn, D_MODEL), lambda i, j: (l, j, 0))],
        out_specs=pl.BlockSpec((tm, tn), lambda i, j: (i, j)),
        scratch_shapes=[pltpu.VMEM((tm, D_MODEL), BF16)],
        compiler_params=_cparams("parallel", "arbitrary"),
        name="inproj",
    )(x, P["norm_mix"], P["w_in"])


SUBCHUNKS = 4
SSD_SUBCHUNKS = 2


def _sub_chunks(z_ref):
    n = z_ref.shape[0] // CHUNK
    return [(s, n) for s in range(n)]


def _when_first(sub, fn):
    if sub[0] == 0:
        pl.when(pl.program_id(1) == 0)(fn)


def _when_last(sub, fn):
    if sub[0] == sub[1] - 1:
        pl.when(pl.program_id(1) == pl.num_programs(1) - 1)(fn)


def _sub_rows(ref, s):
    return ref.at[pl.ds(s * CHUNK, CHUNK), :]


def _conv_chunk(s, x, pad_ref, w_ref, b_ref):
    Lc = x.shape[0]

    def _():
        pad_ref[0:8, :] = jnp.zeros((8, x.shape[1]), F32)

    _when_first(s, _)
    pad_ref[8:8 + Lc, :] = x
    out = b_ref[...] + pad_ref[5:5 + Lc, :] * w_ref[0:1, :]
    out = out + pad_ref[6:6 + Lc, :] * w_ref[1:2, :]
    out = out + pad_ref[7:7 + Lc, :] * w_ref[2:3, :]
    out = out + x * w_ref[3:4, :]
    return out


def _conv_finish(pad_ref, Lc):
    pad_ref[0:8, :] = pad_ref[Lc:Lc + 8, :]


def _head_masks(nh, hd, L):
    rows = (np.arange(nh * L)[:, None] // L) == (np.arange(nh * hd)[None, :] // hd)
    diag = (np.arange(nh * hd)[:, None] // hd) == (np.arange(nh * hd)[None, :] // hd)
    return jnp.asarray(rows, BF16), jnp.asarray(diag, F32)


def _head_block_rows(x, nh, mask):
    return jnp.concatenate([x.astype(BF16)] * nh, axis=0) * mask


def _pow2_div(x, d):
    return lax.shift_right_logical(x, jnp.int32(int(math.log2(d))))


def _head_block_rows_sel(x, nh, hd):
    L = x.shape[0]
    xt = jnp.concatenate([x.astype(BF16)] * nh, axis=0)
    row = lax.broadcasted_iota(jnp.int32, xt.shape, 0)
    col = lax.broadcasted_iota(jnp.int32, xt.shape, 1)
    return jnp.where(_pow2_div(row, L) == _pow2_div(col, hd), xt, jnp.zeros_like(xt))


def _head_block_cols(xt, nh, hd):
    L = xt.shape[1]
    xc = jnp.concatenate([xt.astype(BF16)] * nh, axis=1)
    row = lax.broadcasted_iota(jnp.int32, xc.shape, 0)
    col = lax.broadcasted_iota(jnp.int32, xc.shape, 1)
    return jnp.where(_pow2_div(row, hd) == _pow2_div(col, L), xc, jnp.zeros_like(xc))


def _group_repeat_lanes(m, rep):
    lane = lax.broadcasted_iota(jnp.int32, m.shape, 1)
    swapped = pltpu.roll(m, m.shape[1] // 2, 1)
    low = lane < m.shape[1] // 2
    g0 = jnp.where(low, m, swapped)
    g1 = jnp.where(low, swapped, m)
    return jnp.concatenate([g0] * (rep // 2) + [g1] * (rep // 2), axis=1)


def _ssd_kernel(z_ref, *refs):
    y_ref = refs[-5]
    for sub in _sub_chunks(z_ref):
        _ssd_chunk(sub, _sub_rows(z_ref, sub[0]), *refs[:-5], _sub_rows(y_ref, sub[0]), *refs[-4:])


def _ssd_chunk(s, z_ref, cw_ref, cb_ref, dtb_ref, alog_ref, dlane_ref, ng_ref, tri_ref, ehj_ref, ehn_ref,
               cbias_ref, hrows_ref, hdiag_ref, y_ref, sto_ref, buf_ref, pad_ref, st_ref):
    Lc = z_ref.shape[0]
    zz = z_ref[...].astype(F32)
    zgate = zz[:, 0:MIX]
    xbc = zz[:, MIX:MIX + SSD_CONV]
    dt_raw = zz[:, MIX + SSD_CONV:MIX + SSD_CONV + LANES]

    def _():
        st_ref[...] = jnp.zeros(st_ref.shape, F32)

    _when_first(s, _)
    conv = _conv_chunk(s, xbc, pad_ref, cw_ref, cb_ref)
    xc = _silu(conv)
    xs = xc[:, 0:MIX]
    nbc = SSD_GROUPS * SSD_STATE
    bm = xc[:, MIX:MIX + nbc]
    cm = xc[:, MIX + nbc:MIX + 2 * nbc]
    dt = jax.nn.softplus(dt_raw + dtb_ref[...])
    a = -jnp.exp(alog_ref[...])
    ld = dt * a
    acum = _hdot(tri_ref[...], ld)
    acum_t = acum.T
    dt_t = dt.T
    a_row = jnp.concatenate([acum_t[h:h + 1, :] for h in range(SSD_HEADS)], axis=1)
    dt_row = jnp.concatenate([dt_t[h:h + 1, :] for h in range(SSD_HEADS)], axis=1)
    a_col = _split3_dot(acum, ehj_ref[...])
    decay = jnp.exp((a_col - a_row) + cbias_ref[...])
    rep = SSD_HEADS // SSD_GROUPS
    gmats = []
    for g in range(SSD_GROUPS):
        cg = cm[:, g * SSD_STATE:(g + 1) * SSD_STATE]
        bg = bm[:, g * SSD_STATE:(g + 1) * SSD_STATE]
        gmats.append(_bdot_nt(cg, bg))
    g_all = jnp.concatenate([gmats[h // rep] for h in range(SSD_HEADS)], axis=1)
    m_all = g_all * decay * dt_row
    y = _bdot(m_all, _head_block_rows(xs, SSD_HEADS, hrows_ref[...]))
    exp_a = jnp.exp(acum)
    exp_a_l = _split3_dot(exp_a, ehn_ref[...])
    c_rep = _group_repeat_lanes(cm, rep)
    y = y + _bdot_nt(c_rep * exp_a_l, st_ref[...])
    a_last = acum[Lc - 1:Lc, :]
    w_end_l = _split3_dot(jnp.exp(a_last - acum) * dt, ehn_ref[...])
    upd = _bdot_tn(xs, _group_repeat_lanes(bm, rep) * w_end_l)
    st_ref[...] = exp_a_l[Lc - 1:Lc, :] * st_ref[...] + upd * hdiag_ref[...]
    y = y + dlane_ref[...] * xs
    y = y * _silu(zgate)
    y_ref[...] = _rmsnorm(y, ng_ref[...]).astype(y_ref.dtype)

    def _():
        buf_ref[...] = pad_ref[Lc + 8 - (CONV_K - 1):Lc + 8, :]
        s_t = st_ref[...].T
        for h in range(SSD_HEADS):
            sl = slice(h * SSD_HD, (h + 1) * SSD_HD)
            sto_ref[h] = s_t[sl, sl]

    _when_last(s, _)
    _conv_finish(pad_ref, Lc)


def _const_spec(a):
    return pl.BlockSpec(a.shape, lambda *_: (0,) * a.ndim)


def _ssd_prompt(z3, P, l):
    B, L, _ = z3.shape
    Lc = CHUNK
    assert L % (SUBCHUNKS * Lc) == 0 and L % (SSD_SUBCHUNKS * Lc) == 0, "sequence length must fill whole grid steps"
    params = [P[k] for k in ("ssd_cw", "ssd_cb", "ssd_dtb", "ssd_alog", "ssd_dlane", "ssd_norm")]
    j = np.arange(SSD_HEADS * Lc) % Lc
    cbias = jnp.asarray(np.where(np.arange(Lc)[:, None] >= j[None, :], 0.0, -1e30), F32)
    consts = [_tri(Lc), _head_expand(SSD_HEADS, Lc), _head_expand(SSD_HEADS, SSD_HD), cbias,
              *_head_masks(SSD_HEADS, SSD_HD, Lc)]
    return pl.pallas_call(
        _ssd_kernel,
        out_shape=(jax.ShapeDtypeStruct((B, L, MIX), BF16),
                   jax.ShapeDtypeStruct((B, SSD_HEADS, SSD_STATE, SSD_HD), F32),
                   jax.ShapeDtypeStruct((B, CONV_K - 1, SSD_CONV), F32)),
        grid=(B, L // (SSD_SUBCHUNKS * Lc)),
        in_specs=[pl.BlockSpec((None, SSD_SUBCHUNKS * Lc, 1536), lambda b, c: (b, c, Z_SSD // 1536))]
                 + [_lspec(a, l) for a in params] + [_const_spec(a) for a in consts],
        out_specs=(pl.BlockSpec((None, SSD_SUBCHUNKS * Lc, MIX), lambda b, c: (b, c, 0)),
                   pl.BlockSpec((None, SSD_HEADS, SSD_STATE, SSD_HD), lambda b, c: (b, 0, 0, 0)),
                   pl.BlockSpec((None, CONV_K - 1, SSD_CONV), lambda b, c: (b, 0, 0))),
        scratch_shapes=[pltpu.VMEM((Lc + 8, SSD_CONV), F32), pltpu.VMEM((MIX, MIX), F32)],
        compiler_params=_cparams("parallel", "arbitrary"),
        name="ssd_prompt",
    )(z3, *params, *consts)


def _tri(Lc):
    return jnp.asarray(np.tril(np.ones((Lc, Lc), np.float32)))


def _head_expand(nh, width):
    e = np.zeros((LANES, nh * width), np.float32)
    for h in range(nh):
        e[h, h * width:(h + 1) * width] = 1.0
    return jnp.asarray(e, BF16)


def _ret_gammas():
    return 1.0 - np.exp2(-5.0 - np.arange(RET_HEADS, dtype=np.float64))


def _ret_tables(Lc):
    gam = _ret_gammas()
    i = np.arange(Lc)
    d = i[:, None] - i[None, :]
    decay = np.where(d >= 0, gam[:, None, None] ** np.maximum(d, 0)[None], 0.0)
    decay_l = np.transpose(decay, (1, 0, 2)).reshape(Lc, RET_HEADS * Lc)
    grow_l = np.repeat(gam[None, :] ** (i[:, None] + 1), RET_HD, axis=1)
    toend_t = np.repeat(gam[:, None] ** (Lc - 1 - i[None, :]), RET_HD, axis=0)
    hd = np.arange(MIX) // RET_HD
    state_decay = np.where(hd[:, None] == hd[None, :], (gam ** Lc)[hd][:, None], 0.0)
    return tuple(jnp.asarray(t, F32) for t in (decay_l, grow_l, toend_t, state_decay))


def _rope_tables(pos):
    half = RET_HD // 2
    inv = ROPE_BASE ** (-np.arange(half, dtype=np.float64) / half)
    ang = np.asarray(pos, np.float64)[:, None] * inv[None, :]
    cos = np.cos(ang)
    sin = np.sin(ang)
    cos_l = np.tile(np.concatenate([cos, cos], axis=1), (1, RET_HEADS))
    sin_l = np.tile(np.concatenate([-sin, sin], axis=1), (1, RET_HEADS))
    return jnp.asarray(cos_l, F32), jnp.asarray(sin_l, F32)


def _rotary_lanes(x, cos_l, sin_l):
    lane = lax.broadcasted_iota(jnp.int32, x.shape, 1)
    first = (lane & (RET_HD - 1)) < (RET_HD // 2)
    n = x.shape[1]
    swapped = jnp.where(first, pltpu.roll(x, n - RET_HD // 2, 1), pltpu.roll(x, RET_HD // 2, 1))
    return x * cos_l + swapped * sin_l


def _group_norm_head(o):
    mu = jnp.mean(o, axis=-1, keepdims=True)
    d = o - mu
    var = jnp.mean(d * d, axis=-1, keepdims=True)
    return d * lax.rsqrt(var + 1e-5)


def _group_norm_lanes(o, hd):
    cols = []
    for cb in range(o.shape[1] // LANES):
        x = o[:, cb * LANES:(cb + 1) * LANES]
        low = lax.broadcasted_iota(jnp.int32, x.shape, 1) < hd

        def seg_mean(t):
            lo = jnp.sum(jnp.where(low, t, 0.0), axis=1, keepdims=True)
            hi = jnp.sum(jnp.where(low, 0.0, t), axis=1, keepdims=True)
            return jnp.where(low, lo, hi) * (1.0 / hd)

        d = x - seg_mean(x)
        cols.append(d * lax.rsqrt(seg_mean(d * d) + 1e-5))
    return jnp.concatenate(cols, axis=1)


def _ret_kernel(z_ref, cos_ref, sin_ref, *refs):
    y_ref = refs[-3]
    for sub in _sub_chunks(z_ref):
        s = sub[0]
        _ret_chunk(sub, _sub_rows(z_ref, s), _sub_rows(cos_ref, s), _sub_rows(sin_ref, s), *refs[:-3],
                   _sub_rows(y_ref, s), *refs[-2:])


def _ret_chunk(s, z_ref, cos_ref, sin_ref, dec_ref, grow_ref, toend_ref, sdec_ref, hdiag_ref,
               gn_ref, y_ref, sto_ref, st_ref):
    zz = z_ref[...].astype(F32)
    q = _rotary_lanes(zz[:, 0:MIX], cos_ref[...], sin_ref[...]) * (RET_HD ** -0.5)
    k = _rotary_lanes(zz[:, MIX:2 * MIX], cos_ref[...], sin_ref[...])
    v = zz[:, 2 * MIX:3 * MIX]
    gate = zz[:, 3 * MIX:4 * MIX]

    def _():
        st_ref[...] = jnp.zeros(st_ref.shape, F32)

    _when_first(s, _)
    k_t = k.T
    g_all = _bdot(q, _head_block_cols(k_t, RET_HEADS, RET_HD))
    o = _bdot(g_all * dec_ref[...], _head_block_rows_sel(v, RET_HEADS, RET_HD))
    o = o + _bdot(q * grow_ref[...], st_ref[...])
    upd = _bdot(k_t * toend_ref[...], v)
    st_ref[...] = sdec_ref[...] * st_ref[...] + upd * hdiag_ref[...]
    y_ref[...] = (_silu(gate) * (_group_norm_lanes(o, RET_HD) * gn_ref[...])).astype(y_ref.dtype)

    def _():
        for h in range(RET_HEADS):
            sl = slice(h * RET_HD, (h + 1) * RET_HD)
            sto_ref[h] = st_ref[sl, sl]

    _when_last(s, _)


def _ret_prompt(z3, P, l):
    B, L, _ = z3.shape
    Lc = CHUNK
    assert L % (SUBCHUNKS * Lc) == 0 and L % (SSD_SUBCHUNKS * Lc) == 0, "sequence length must fill whole grid steps"
    cos_l, sin_l = _rope_tables(np.arange(L))
    consts = list(_ret_tables(Lc)) + [_head_masks(RET_HEADS, RET_HD, Lc)[1]]
    return pl.pallas_call(
        _ret_kernel,
        out_shape=(jax.ShapeDtypeStruct((B, L, MIX), BF16),
                   jax.ShapeDtypeStruct((B, RET_HEADS, RET_HD, RET_HD), F32)),
        grid=(B, L // (SUBCHUNKS * Lc)),
        in_specs=[pl.BlockSpec((None, SUBCHUNKS * Lc,2048), lambda b, c: (b, c, Z_RET // 2048)),
                  pl.BlockSpec((SUBCHUNKS * Lc, MIX), lambda b, c: (c, 0)),
                  pl.BlockSpec((SUBCHUNKS * Lc, MIX), lambda b, c: (c, 0))]
                 + [_const_spec(a) for a in consts] + [_lspec(P["ret_gn"], l)],
        out_specs=(pl.BlockSpec((None, SUBCHUNKS * Lc,MIX), lambda b, c: (b, c, 0)),
                   pl.BlockSpec((None, RET_HEADS, RET_HD, RET_HD), lambda b, c: (b, 0, 0, 0))),
        scratch_shapes=[pltpu.VMEM((MIX, MIX), F32)],
        compiler_params=_cparams("parallel", "arbitrary"),
        name="ret_prompt",
    )(z3, cos_l, sin_l, *consts, P["ret_gn"])


def _lru_gates(xc, wg_ref, bg_ref, lam_ref):
    rg = _bdot(xc, wg_ref[...]) + bg_ref[...]
    r = jax.nn.sigmoid(rg[:, 0:MIX])
    i = jax.nn.sigmoid(rg[:, MIX:2 * MIX])
    log_a = -LRU_C * r * jax.nn.softplus(-lam_ref[...])
    a = jnp.exp(log_a)
    bx = jnp.sqrt(_neg_expm1_2x(log_a, a)) * (i * xc)
    return a, bx


def _lru_kernel(z_ref, *refs):
    y_ref = refs[-5]
    for sub in _sub_chunks(z_ref):
        _lru_chunk(sub, _sub_rows(z_ref, sub[0]), *refs[:-5], _sub_rows(y_ref, sub[0]), *refs[-4:])


def _lru_chunk(s, z_ref, cw_ref, cb_ref, wg_ref, bg_ref, lam_ref,
               y_ref, st_ref, buf_ref, pad_ref, h_ref):
    Lc = z_ref.shape[0]
    zz = z_ref[...].astype(F32)
    gate = zz[:, 0:MIX]
    x = zz[:, MIX:2 * MIX]

    def _():
        st_ref[...] = jnp.zeros(st_ref.shape, F32)

    _when_first(s, _)
    xc = _conv_chunk(s, x, pad_ref, cw_ref, cb_ref)
    a, bx = _lru_gates(xc, wg_ref, bg_ref, lam_ref)
    ng = Lc // SUBLANES
    a3 = a.reshape(ng, SUBLANES, MIX)
    b3 = bx.reshape(ng, SUBLANES, MIX)
    sub = lax.broadcasted_iota(jnp.int32, a3.shape, 1)
    step = 1
    while step < SUBLANES:
        keep = sub >= step
        b3 = jnp.where(keep, b3 + a3 * pltpu.roll(b3, step, 1), b3)
        a3 = jnp.where(keep, a3 * pltpu.roll(a3, step, 1), a3)
        step *= 2
    carry = st_ref[...]
    for g in range(ng):
        hg = b3[g] + a3[g] * jnp.broadcast_to(carry, (SUBLANES, MIX))
        h_ref[g * SUBLANES:(g + 1) * SUBLANES, :] = hg
        carry = hg[SUBLANES - 1:SUBLANES, :]
    st_ref[...] = carry
    y_ref[...] = (h_ref[...] * jax.nn.gelu(gate)).astype(y_ref.dtype)

    def _():
        buf_ref[...] = pad_ref[Lc + 8 - (CONV_K - 1):Lc + 8, :]

    _when_last(s, _)
    _conv_finish(pad_ref, Lc)


def _lru_prompt(z3, P, l):
    B, L, _ = z3.shape
    Lc = CHUNK
    assert L % (SUBCHUNKS * Lc) == 0 and L % (SSD_SUBCHUNKS * Lc) == 0, "sequence length must fill whole grid steps"
    params = [P[k] for k in ("lru_cw", "lru_cb", "lru_wg", "lru_bg", "lru_lam")]
    return pl.pallas_call(
        _lru_kernel,
        out_shape=(jax.ShapeDtypeStruct((B, L, MIX), BF16),
                   jax.ShapeDtypeStruct((B, 1, MIX), F32),
                   jax.ShapeDtypeStruct((B, CONV_K - 1, MIX), F32)),
        grid=(B, L // (SUBCHUNKS * Lc)),
        in_specs=[pl.BlockSpec((None, SUBCHUNKS * Lc,1024), lambda b, c: (b, c, Z_LRU // 1024))]
                 + [_lspec(a, l) for a in params],
        out_specs=(pl.BlockSpec((None, SUBCHUNKS * Lc,MIX), lambda b, c: (b, c, 0)),
                   pl.BlockSpec((None, 1, MIX), lambda b, c: (b, 0, 0)),
                   pl.BlockSpec((None, CONV_K - 1, MIX), lambda b, c: (b, 0, 0))),
        scratch_shapes=[pltpu.VMEM((Lc + 8, MIX), F32), pltpu.VMEM((Lc, MIX), F32)],
        compiler_params=_cparams("parallel", "arbitrary"),
        name="lru_prompt",
    )(z3, *params)


S5_LB = 4
S5_LAGS = 4
assert 2 * S5_LAGS == SUBLANES


def _s5_project_in(u, wbr_ref, wbi_ref, xr_ref, xi_ref, lags):
    shifted = [u]
    if lags > 1:
        sub = lax.broadcasted_iota(jnp.int32, u.shape, 0) & (lags - 1)
        shifted += [jnp.where(sub >= d, pltpu.roll(u, d, 0), 0.0) for d in range(1, lags)]
    shifted = [s.astype(BF16) for s in shifted]
    for kb in range(S5_LB):
        lhs = jnp.concatenate([s[:, kb * 128:(kb + 1) * 128] for s in shifted], axis=1)
        k = lags * 128
        xr_ref[:, kb * 512:(kb + 1) * 512] = jnp.dot(lhs, wbr_ref[kb, 0:k, :], preferred_element_type=F32)
        xi_ref[:, kb * 512:(kb + 1) * 512] = jnp.dot(lhs, wbi_ref[kb, 0:k, :], preferred_element_type=F32)


def _s5_project_out(xr, xi, u, wcr_ref, wci_ref, d_ref, wglu_ref):
    ys = []
    for kb in range(S5_LB):
        sl = slice(kb * 512, (kb + 1) * 512)
        ys.append(_bdot(xr[:, sl], wcr_ref[kb]) - _bdot(xi[:, sl], wci_ref[kb]))
    y = jnp.concatenate(ys, axis=1) + d_ref[...] * u
    y = jax.nn.gelu(y)
    return y * jax.nn.sigmoid(_bdot(y, wglu_ref[...]))


def _s5_kernel(z_ref, *refs):
    y_ref = refs[-5]
    for sub in _sub_chunks(z_ref):
        _s5_chunk(sub, _sub_rows(z_ref, sub[0]), *refs[:-5], _sub_rows(y_ref, sub[0]), *refs[-4:])


def _s5_chunk(s, z_ref, wbr_ref, wbi_ref, wcr_ref, wci_ref, pr_ref, pi_ref, d_ref, wglu_ref, qr_ref, qi_ref,
              y_ref, sr_ref, si_ref, xr_ref, xi_ref):
    Lc = z_ref.shape[0]
    u = z_ref[...].astype(F32)

    def _():
        sr_ref[...] = jnp.zeros(sr_ref.shape, F32)
        si_ref[...] = jnp.zeros(si_ref.shape, F32)

    _when_first(s, _)
    _s5_project_in(u, wbr_ref, wbi_ref, xr_ref, xi_ref, S5_LAGS)
    ng = Lc // SUBLANES
    x3r = xr_ref[...].reshape(ng, SUBLANES, S5_CH)
    x3i = xi_ref[...].reshape(ng, SUBLANES, S5_CH)
    tr = jnp.broadcast_to(x3r[:, S5_LAGS - 1:S5_LAGS, :], x3r.shape)
    ti = jnp.broadcast_to(x3i[:, S5_LAGS - 1:S5_LAGS, :], x3i.shape)
    mr, mi = qr_ref[...][None], qi_ref[...][None]
    x3r, x3i = x3r + (mr * tr - mi * ti), x3i + (mr * ti + mi * tr)
    pcr, pci = pr_ref[...], pi_ref[...]
    cr, ci = sr_ref[...], si_ref[...]
    for g in range(ng):
        br = jnp.broadcast_to(cr, (SUBLANES, S5_CH))
        bi = jnp.broadcast_to(ci, (SUBLANES, S5_CH))
        gr = x3r[g] + (pcr * br - pci * bi)
        gi = x3i[g] + (pcr * bi + pci * br)
        xr_ref[g * SUBLANES:(g + 1) * SUBLANES, :] = gr
        xi_ref[g * SUBLANES:(g + 1) * SUBLANES, :] = gi
        cr, ci = gr[SUBLANES - 1:SUBLANES, :], gi[SUBLANES - 1:SUBLANES, :]
    sr_ref[...] = cr
    si_ref[...] = ci
    y_ref[...] = _s5_project_out(xr_ref[...], xi_ref[...], u, wcr_ref, wci_ref, d_ref, wglu_ref).astype(y_ref.dtype)


S5_PARAMS = ("s5_wbr", "s5_wbi", "s5_wcr", "s5_wci", "s5_pr", "s5_pi", "s5_d", "s5_glu", "s5_qr", "s5_qi")


def _s5_prompt(z3, P, l):
    B, L, _ = z3.shape
    Lc = CHUNK
    assert L % (SUBCHUNKS * Lc) == 0 and L % (SSD_SUBCHUNKS * Lc) == 0, "sequence length must fill whole grid steps"
    params = [P[k] for k in S5_PARAMS]
    return pl.pallas_call(
        _s5_kernel,
        out_shape=(jax.ShapeDtypeStruct((B, L, MIX), BF16),
                   jax.ShapeDtypeStruct((B, 1, S5_CH), F32),
                   jax.ShapeDtypeStruct((B, 1, S5_CH), F32)),
        grid=(B, L // (SUBCHUNKS * Lc)),
        in_specs=[pl.BlockSpec((None, SUBCHUNKS * Lc,MIX), lambda b, c: (b, c, Z_S5 // MIX))]
                 + [_lspec(a, l) for a in params],
        out_specs=(pl.BlockSpec((None, SUBCHUNKS * Lc,MIX), lambda b, c: (b, c, 0)),
                   pl.BlockSpec((None, 1, S5_CH), lambda b, c: (b, 0, 0)),
                   pl.BlockSpec((None, 1, S5_CH), lambda b, c: (b, 0, 0))),
        scratch_shapes=[pltpu.VMEM((Lc, S5_CH), F32), pltpu.VMEM((Lc, S5_CH), F32)],
        compiler_params=_cparams("parallel", "arbitrary"),
        name="s5_prompt",
    )(z3, *params)


def _merge_kernel(y0_ref, y1_ref, y2_ref, y3_ref, zg_ref, h_ref, wb_ref, wo_ref, o_ref):
    acc = None
    for k, y_ref in enumerate((y0_ref, y1_ref, y2_ref, y3_ref)):
        br = _bdot(y_ref[...], wb_ref[k])
        t = jax.nn.sigmoid(zg_ref[:, k * D_MODEL:(k + 1) * D_MODEL].astype(F32)) * br
        acc = t if acc is None else acc + t
    o_ref[...] = h_ref[...] + _bdot(acc, wo_ref[...])


def _merge(ys, z, h, P, l, tm):
    T = h.shape[0]
    rows = lambda w: pl.BlockSpec((tm, w), lambda i: (i, 0))
    return pl.pallas_call(
        _merge_kernel,
        out_shape=jax.ShapeDtypeStruct((T, D_MODEL), F32),
        grid=(T // tm,),
        in_specs=[rows(MIX), rows(MIX), rows(MIX), rows(MIX),
                  pl.BlockSpec((tm, 4 * D_MODEL), lambda i: (i, Z_MERGE)),
                  rows(D_MODEL),
                  _lspec(P["w_branch"], l), _lspec(P["w_out"], l)],
        out_specs=rows(D_MODEL),
        compiler_params=_cparams("parallel"),
        name="merge",
    )(*ys, z, h, P["w_branch"], P["w_out"])


def _top2_gates(logits):
    lane, i1, i2, w1, w2 = _top2(logits)
    return jnp.where(lane == i1, w1, 0.0) + jnp.where(lane == i2, w2, 0.0)


def _ffn_kernel(moe, final_norm, *refs):
    if moe:
        h_ref, g_ref, rt_ref, w1_ref, w3_ref, w2_ref, gf_ref, o_ref, hn_ref, acc_ref, gate_ref = refs
    else:
        h_ref, g_ref, w1_ref, w3_ref, w2_ref, gf_ref, o_ref, hn_ref, acc_ref = refs
    e = pl.program_id(1)
    ne = pl.num_programs(1)

    @pl.when(e == 0)
    def _():
        hn = _rmsnorm(h_ref[...], g_ref[...])
        hn_ref[...] = hn.astype(BF16)
        acc_ref[...] = jnp.zeros(acc_ref.shape, F32)
        if moe:
            lane = lax.broadcasted_iota(jnp.int32, (hn.shape[0], LANES), 1)
            logits = jnp.where(lane < N_EXPERTS, _hdot(hn, rt_ref[...]), -jnp.inf)
            gate_ref[...] = _top2_gates(logits)

    hn = hn_ref[...]
    a = jnp.dot(hn, w1_ref[...], preferred_element_type=F32)
    b = jnp.dot(hn, w3_ref[...], preferred_element_type=F32)
    o = _bdot(_silu(a) * b, w2_ref[...])
    if moe:
        lane = lax.broadcasted_iota(jnp.int32, gate_ref.shape, 1)
        ge = jnp.sum(jnp.where(lane == e, gate_ref[...], 0.0), axis=-1, keepdims=True)
        o = ge * o
    acc_ref[...] += o

    @pl.when(e == ne - 1)
    def _():
        out = h_ref[...] + acc_ref[...]
        if final_norm:
            out = _rmsnorm(out, gf_ref[...])
        o_ref[...] = out


def _ffn(h, P, l, tm, final_norm):
    T = h.shape[0]
    moe = l % 2 == 1
    j = l // 2
    tf = D_FF_TILE
    row_spec = pl.BlockSpec((tm, D_MODEL), lambda i, e: (i, 0))
    gfinal = P["norm_final"]
    if moe:
        w1, w3, w2 = P["moe_w1"], P["moe_w3"], P["moe_w2"]
        ne = w1.shape[1]
        wspecs = [pl.BlockSpec((None, None, D_MODEL, tf), lambda i, e: (j, e, 0, 0)),
                  pl.BlockSpec((None, None, D_MODEL, tf), lambda i, e: (j, e, 0, 0)),
                  pl.BlockSpec((None, None, tf, D_MODEL), lambda i, e: (j, e, 0, 0))]
        in_specs = ([row_spec, _lspec(P["norm_ffn"], l), _lspec(P["moe_router"], j)] + wspecs
                    + [_const_spec(gfinal)])
        args = (h, P["norm_ffn"], P["moe_router"], w1, w3, w2, gfinal)
        scratch = [pltpu.VMEM((tm, D_MODEL), BF16), pltpu.VMEM((tm, D_MODEL), F32), pltpu.VMEM((tm, LANES), F32)]
    else:
        w1, w3, w2 = P["ffn_w1"], P["ffn_w3"], P["ffn_w2"]
        ne = w1.shape[2] // tf
        wspecs = [pl.BlockSpec((None, D_MODEL, tf), lambda i, e: (j, 0, e)),
                  pl.BlockSpec((None, D_MODEL, tf), lambda i, e: (j, 0, e)),
                  pl.BlockSpec((None, tf, D_MODEL), lambda i, e: (j, e, 0))]
        in_specs = [row_spec, _lspec(P["norm_ffn"], l)] + wspecs + [_const_spec(gfinal)]
        args = (h, P["norm_ffn"], w1, w3, w2, gfinal)
        scratch = [pltpu.VMEM((tm, D_MODEL), BF16), pltpu.VMEM((tm, D_MODEL), F32)]
    return pl.pallas_call(
        functools.partial(_ffn_kernel, moe, final_norm),
        out_shape=jax.ShapeDtypeStruct((T, D_MODEL), F32),
        grid=(T // tm, ne),
        in_specs=in_specs,
        out_specs=row_spec,
        scratch_shapes=scratch,
        compiler_params=_cparams("parallel", "arbitrary"),
        name="moe" if moe else "ffn",
    )(*args)


MOE_ROWS = 512
ROUTE_LANES = ("e1", "e2", "r1", "r2", "w1", "w2")


def _top2(logits):
    lane = lax.broadcasted_iota(jnp.int32, logits.shape, 1).astype(F32)
    big = float(LANES)
    m1 = jnp.max(logits, axis=-1, keepdims=True)
    i1 = jnp.min(jnp.where(logits == m1, lane, big), axis=-1, keepdims=True)
    rest = jnp.where(lane == i1, -jnp.inf, logits)
    m2 = jnp.max(rest, axis=-1, keepdims=True)
    i2 = jnp.min(jnp.where(rest == m2, lane, big), axis=-1, keepdims=True)
    e2 = jnp.exp(m2 - m1)
    den = 1.0 + e2
    return lane, i1, i2, 1.0 / den, e2 / den


def _moe_route_kernel(h_ref, g_ref, rt_ref, ltri_ref, hn_ref, info_ref, cnt_ref, base_ref):
    i = pl.program_id(0)

    @pl.when(i == 0)
    def _():
        base_ref[...] = jnp.zeros(base_ref.shape, F32)

    hn = _rmsnorm(h_ref[...], g_ref[...])
    hn_ref[...] = hn
    lane_i = lax.broadcasted_iota(jnp.int32, (hn.shape[0], LANES), 1)
    logits = jnp.where(lane_i < N_EXPERTS, _hdot(hn, rt_ref[...]), -jnp.inf)
    lane, i1, i2, w1, w2 = _top2(logits)
    oh1 = (lane == i1).astype(F32)
    oh2 = (lane == i2).astype(F32)
    oh = oh1 + oh2
    before = jnp.dot(ltri_ref[...], oh.astype(BF16), preferred_element_type=F32)
    rank = base_ref[...] + before
    r1 = jnp.sum(oh1 * rank, axis=-1, keepdims=True)
    r2 = jnp.sum(oh2 * rank, axis=-1, keepdims=True)
    base_ref[...] += jnp.sum(oh, axis=0, keepdims=True)
    fields = dict(e1=i1, e2=i2, r1=r1, r2=r2, w1=w1, w2=w2)
    info = jnp.zeros(lane.shape, F32)
    for k, name in enumerate(ROUTE_LANES):
        info = jnp.where(lane_i == k, fields[name], info)
    info_ref[...] = info

    @pl.when(i == pl.num_programs(0) - 1)
    def _():
        cnt_ref[...] = base_ref[...]


def _moe_route(h, P, l, tm):
    T = h.shape[0]
    j = l // 2
    ltri = jnp.asarray(np.tril(np.ones((tm, tm), np.float32), -1), BF16)
    return pl.pallas_call(
        _moe_route_kernel,
        out_shape=(jax.ShapeDtypeStruct((T, D_MODEL), F32), jax.ShapeDtypeStruct((T, LANES), F32),
                   jax.ShapeDtypeStruct((1, LANES), F32)),
        grid=(T // tm,),
        in_specs=[pl.BlockSpec((tm, D_MODEL), lambda i: (i, 0)), _lspec(P["norm_ffn"], l),
                  _lspec(P["moe_router"], j), _const_spec(ltri)],
        out_specs=(pl.BlockSpec((tm, D_MODEL), lambda i: (i, 0)), pl.BlockSpec((tm, LANES), lambda i: (i, 0)),
                   pl.BlockSpec((1, LANES), lambda i: (0, 0))),
        scratch_shapes=[pltpu.VMEM((1, LANES), F32)],
        compiler_params=_cparams("arbitrary"),
        name="moe_route",
    )(h, P["norm_ffn"], P["moe_router"], ltri)


def _moe_plan(info, cnt, tm, n_tiles_max):
    T = info.shape[0]
    rows = MOE_ROWS
    count = cnt[0, :N_EXPERTS].astype(jnp.int32)
    tiles_e = (count + rows - 1) // rows
    first_tile = jnp.cumsum(tiles_e) - tiles_e
    start = first_tile * rows
    experts = jnp.arange(N_EXPERTS, dtype=jnp.int32)

    def row_of(e, r):
        sel = e.astype(jnp.int32)[:, None] == experts[None, :]
        return jnp.sum(jnp.where(sel, start[None, :], 0), axis=1) + r.astype(jnp.int32)

    pos = jnp.stack([row_of(info[:, 0], info[:, 2]), row_of(info[:, 1], info[:, 3])], axis=0)
    pos = jnp.transpose(pos.reshape(2, T // tm, tm), (1, 0, 2))
    n_tiles = jnp.sum(tiles_e)
    t = jnp.minimum(jnp.arange(n_tiles_max, dtype=jnp.int32), n_tiles - 1)
    tile_expert = jnp.sum((first_tile[None, :] <= t[:, None]).astype(jnp.int32), axis=1) - 1
    last_tile = first_tile + tiles_e - 1
    return pos, tile_expert, n_tiles.reshape(1), last_tile, tiles_e


def _moe_dispatch_kernel(last_ref, tiles_ref, nt_ref, pos_ref, hn_ref, xs_ref, zero_ref, sem):
    tm = hn_ref.shape[0]

    @pl.when(pl.program_id(0) == 0)
    def _():
        zero_ref[...] = jnp.zeros(zero_ref.shape, F32)

        def clear(tile):
            row0 = pl.multiple_of(tile * MOE_ROWS, MOE_ROWS)
            cp = pltpu.make_async_copy(zero_ref, xs_ref.at[pl.ds(row0, MOE_ROWS), :], sem)
            cp.start()
            cp.wait()

        for e in range(N_EXPERTS):
            @pl.when(tiles_ref[e] > 0)
            def _():
                clear(last_ref[e])

        def clear_tail(tile, c):
            clear(tile)
            return c

        lax.fori_loop(nt_ref[0], xs_ref.shape[0] // MOE_ROWS, clear_tail, 0)

    def row_copy(j, slot):
        return pltpu.make_async_copy(hn_ref.at[pl.ds(j, 1), :], xs_ref.at[pl.ds(pos_ref[slot, j], 1), :], sem)

    def issue(j, c):
        row_copy(j, 0).start()
        row_copy(j, 1).start()
        return c

    lax.fori_loop(0, tm, issue, 0, unroll=8)
    for _ in range(2):
        pltpu.make_async_copy(hn_ref, xs_ref.at[pl.ds(0, tm), :], sem).wait()


def _moe_dispatch(hn, pos, last_tile, tiles_e, n_tiles, n_rows):
    T = hn.shape[0]
    tm = pos.shape[2]
    gs = pltpu.PrefetchScalarGridSpec(
        num_scalar_prefetch=3, grid=(T // tm,),
        in_specs=[pl.BlockSpec((None, 2, tm), lambda i, *_: (i, 0, 0), memory_space=pltpu.SMEM),
                  pl.BlockSpec((tm, D_MODEL), lambda i, *_: (i, 0))],
        out_specs=pl.BlockSpec(memory_space=pl.ANY),
        scratch_shapes=[pltpu.VMEM((MOE_ROWS, D_MODEL), F32), pltpu.SemaphoreType.DMA])
    return pl.pallas_call(
        _moe_dispatch_kernel, grid_spec=gs,
        out_shape=jax.ShapeDtypeStruct((n_rows, D_MODEL), F32),
        compiler_params=_cparams("arbitrary"),
        name="moe_dispatch",
    )(last_tile, tiles_e, n_tiles, pos, hn)


def _moe_group_kernel(te_ref, nt_ref, x_ref, w1_ref, w3_ref, w2_ref, o_ref):
    live = pl.program_id(0) < nt_ref[0]

    @pl.when(live)
    def _():
        x = x_ref[...].astype(BF16)
        a = jnp.dot(x, w1_ref[...], preferred_element_type=F32)
        b = jnp.dot(x, w3_ref[...], preferred_element_type=F32)
        o_ref[...] = _bdot(_silu(a) * b, w2_ref[...])

    @pl.when(jnp.logical_not(live))
    def _():
        o_ref[...] = jnp.zeros(o_ref.shape, F32)


def _moe_group(xs, P, j, tile_expert, n_tiles):
    n_rows = xs.shape[0]
    tf = D_FF_TILE
    wmap = lambda i, te, nt: (j, te[i], 0, 0)
    rmap = lambda i, te, nt: (i, 0)
    gs = pltpu.PrefetchScalarGridSpec(
        num_scalar_prefetch=2, grid=(n_rows // MOE_ROWS,),
        in_specs=[pl.BlockSpec((MOE_ROWS, D_MODEL), rmap),
                  pl.BlockSpec((None, None, D_MODEL, tf), wmap),
                  pl.BlockSpec((None, None, D_MODEL, tf), wmap),
                  pl.BlockSpec((None, None, tf, D_MODEL), wmap)],
        out_specs=pl.BlockSpec((MOE_ROWS, D_MODEL), rmap))
    return pl.pallas_call(
        _moe_group_kernel, grid_spec=gs,
        out_shape=jax.ShapeDtypeStruct((n_rows, D_MODEL), F32),
        compiler_params=_cparams("arbitrary"),
        name="moe_group",
    )(tile_expert, n_tiles, xs, P["moe_w1"], P["moe_w3"], P["moe_w2"])


def _moe_combine_kernel(final_norm, pos_ref, o_ref, h_ref, info_ref, gf_ref, out_ref, a_ref, b_ref, sem):
    tm = h_ref.shape[0]

    def row_copy(j, slot, dst):
        return pltpu.make_async_copy(o_ref.at[pl.ds(pos_ref[slot, j], 1), :], dst.at[pl.ds(j, 1), :], sem)

    def issue(j, c):
        row_copy(j, 0, a_ref).start()
        row_copy(j, 1, b_ref).start()
        return c

    lax.fori_loop(0, tm, issue, 0, unroll=8)
    for dst in (a_ref, b_ref):
        pltpu.make_async_copy(o_ref.at[pl.ds(0, tm), :], dst, sem).wait()
    k1, k2 = ROUTE_LANES.index("w1"), ROUTE_LANES.index("w2")
    info = info_ref[...]
    w1, w2 = info[:, k1:k1 + 1], info[:, k2:k2 + 1]
    out = h_ref[...] + (w1 * a_ref[...] + w2 * b_ref[...])
    if final_norm:
        out = _rmsnorm(out, gf_ref[...])
    out_ref[...] = out


def _moe_combine(o, h, info, pos, P, final_norm):
    T = h.shape[0]
    tm = pos.shape[2]
    gfinal = P["norm_final"]
    row_spec = pl.BlockSpec((tm, D_MODEL), lambda i: (i, 0))
    return pl.pallas_call(
        functools.partial(_moe_combine_kernel, final_norm),
        out_shape=jax.ShapeDtypeStruct((T, D_MODEL), F32),
        grid=(T // tm,),
        in_specs=[pl.BlockSpec((None, 2, tm), lambda i: (i, 0, 0), memory_space=pltpu.SMEM),
                  pl.BlockSpec(memory_space=pl.ANY), row_spec,
                  pl.BlockSpec((tm, LANES), lambda i: (i, 0)), _const_spec(gfinal)],
        out_specs=row_spec,
        scratch_shapes=[pltpu.VMEM((tm, D_MODEL), F32), pltpu.VMEM((tm, D_MODEL), F32), pltpu.SemaphoreType.DMA],
        compiler_params=_cparams("arbitrary"),
        name="moe_combine",
    )(pos, o, h, info, gfinal)


def _moe_routed(h, P, l, tm, final_norm):
    T = h.shape[0]
    n_tiles_max = (2 * T) // MOE_ROWS + N_EXPERTS
    hn, info, cnt = _moe_route(h, P, l, tm)
    pos, tile_expert, n_tiles, last_tile, tiles_e = _moe_plan(info, cnt, tm, n_tiles_max)
    xs = _moe_dispatch(hn, pos, last_tile, tiles_e, n_tiles, n_tiles_max * MOE_ROWS)
    o = _moe_group(xs, P, l // 2, tile_expert, n_tiles)
    return _moe_combine(o, h, info, pos, P, final_norm)


def _conv_step(x, buf_ref, buf_o, w_ref, b_ref):
    out = b_ref[...]
    for k in range(CONV_K - 1):
        out = out + buf_ref[k] * w_ref[k:k + 1, :]
        if k > 0:
            buf_o[k - 1] = buf_ref[k]
    out = out + x * w_ref[CONV_K - 1:CONV_K, :]
    buf_o[CONV_K - 2] = x
    return out


def _step_pre_kernel(pos_cos_ref, pos_sin_ref, gam_ref, ehn_ref, z_ref, sbuf_ref, lbuf_ref, lst_ref, s5r_ref, s5i_ref,
                     scw_ref, scb_ref, dtb_ref, alog_ref,
                     lcw_ref, lcb_ref, wg_ref, bg_ref, lam_ref,
                     wbr_ref, wbi_ref, wcr_ref, wci_ref, pr_ref, pi_ref, s5d_ref, wglu_ref, qr_ref, qi_ref,
                     kqv_ref, dec_ref, sbuf_o, lbuf_o, lst_o, s5r_o, s5i_o, ys5_o, ylru_o, xs_o,
                     xr_ref, xi_ref):
    zz = z_ref[...].astype(F32)
    xbc = zz[:, Z_SSD + MIX:Z_SSD + MIX + SSD_CONV]
    dt_raw = zz[:, Z_SSD + MIX + SSD_CONV:Z_SSD + MIX + SSD_CONV + LANES]
    xc = _silu(_conv_step(xbc, sbuf_ref, sbuf_o, scw_ref, scb_ref))
    xs = xc[:, 0:MIX]
    xs_o[...] = xs
    nbc = SSD_GROUPS * SSD_STATE
    bm = xc[:, MIX:MIX + nbc]
    cm = xc[:, MIX + nbc:MIX + 2 * nbc]
    dt = jax.nn.softplus(dt_raw + dtb_ref[...])
    a = -jnp.exp(alog_ref[...])
    rep = SSD_HEADS // SSD_GROUPS
    kqv_ref[0, 0] = (_group_repeat_lanes(bm, rep) * _split3_dot(dt, ehn_ref[...])).T
    kqv_ref[0, 1] = _group_repeat_lanes(cm, rep).T
    kqv_ref[0, 2] = xs.T
    dec_ref[0] = jnp.exp(dt * a).T[0:SSD_HEADS, :]
    q = _rotary_lanes(zz[:, Z_RET:Z_RET + MIX], pos_cos_ref[...], pos_sin_ref[...]) * (RET_HD ** -0.5)
    k = _rotary_lanes(zz[:, Z_RET + MIX:Z_RET + 2 * MIX], pos_cos_ref[...], pos_sin_ref[...])
    kqv_ref[1, 0] = k.T
    kqv_ref[1, 1] = q.T
    kqv_ref[1, 2] = zz[:, Z_RET + 2 * MIX:Z_RET + 3 * MIX].T
    dec_ref[1] = gam_ref[...]
    gate = zz[:, Z_LRU:Z_LRU + MIX]
    lx = zz[:, Z_LRU + MIX:Z_LRU + 2 * MIX]
    lconv = _conv_step(lx, lbuf_ref, lbuf_o, lcw_ref, lcb_ref)
    la, lbx = _lru_gates(lconv, wg_ref, bg_ref, lam_ref)
    hl = lbx + la * lst_ref[...]
    lst_o[...] = hl
    ylru_o[...] = hl * jax.nn.gelu(gate)
    u = zz[:, Z_S5:Z_S5 + MIX]
    _s5_project_in(u, wbr_ref, wbi_ref, xr_ref, xi_ref, 1)
    lr, li = pr_ref[0:1, :], pi_ref[0:1, :]
    s0r, s0i = s5r_ref[...], s5i_ref[...]
    xr = xr_ref[...] + (lr * s0r - li * s0i)
    xi = xi_ref[...] + (lr * s0i + li * s0r)
    s5r_o[...] = xr
    s5i_o[...] = xi
    ys5_o[...] = _s5_project_out(xr, xi, u, wcr_ref, wci_ref, s5d_ref, wglu_ref)


def _step_state_kernel(kqv_ref, dec_ref, st_ref, o_st_ref, y_ref):
    h = pl.program_id(0)
    d = dec_ref[pl.ds(h, 1), :]
    v = kqv_ref[2]
    acc = jnp.zeros(v.shape, F32)
    for n in range(st_ref.shape[0]):
        s_new = d * st_ref[n] + kqv_ref[0, n:n + 1, :] * v
        o_st_ref[n] = s_new
        acc = acc + kqv_ref[1, n:n + 1, :] * s_new
    y_ref[...] = acc


def _step_post_kernel(yssd_ref, yret_ref, xs_ref, z_ref, dlane_ref, ng_ref, gn_ref, yssd_o, yret_o):
    zz_gate = z_ref[:, Z_SSD:Z_SSD + MIX]
    y = yssd_ref[...].T + dlane_ref[...] * xs_ref[...]
    y = y * _silu(zz_gate)
    yssd_o[...] = _rmsnorm(y, ng_ref[...])
    rgate = z_ref[:, Z_RET + 3 * MIX:Z_RET + 4 * MIX]
    yret_o[...] = _silu(rgate) * (_group_norm_lanes(yret_ref[...].T, RET_HD) * gn_ref[...])


def _sample_mixers(z, views, big, layer, P, pos):
    Bs = z.shape[0]
    H = SSD_HEADS
    cos_l, sin_l = _rope_tables(np.asarray([pos]))
    gam = jnp.asarray(np.repeat(_ret_gammas()[:, None], Bs, axis=1), F32)
    consts = [cos_l, sin_l, gam, _head_expand(SSD_HEADS, SSD_HD)]
    states = [views[k] for k in ("ssd_conv", "lru_conv", "lru", "s5_re", "s5_im")]
    params = [P[k] for k in ("ssd_cw", "ssd_cb", "ssd_dtb", "ssd_alog",
                             "lru_cw", "lru_cb", "lru_wg", "lru_bg", "lru_lam") + S5_PARAMS]
    pre_out = (jax.ShapeDtypeStruct((2, 3, MIX, Bs), F32),
               jax.ShapeDtypeStruct((2, H, Bs), F32),
               jax.ShapeDtypeStruct(views["ssd_conv"].shape[1:], F32),
               jax.ShapeDtypeStruct(views["lru_conv"].shape[1:], F32),
               jax.ShapeDtypeStruct((Bs, MIX), F32),
               jax.ShapeDtypeStruct((Bs, S5_CH), F32), jax.ShapeDtypeStruct((Bs, S5_CH), F32),
               jax.ShapeDtypeStruct((Bs, MIX), F32), jax.ShapeDtypeStruct((Bs, MIX), F32),
               jax.ShapeDtypeStruct((Bs, MIX), F32))
    (kqv, dec, sbuf_n, lbuf_n, lst_n, s5r_n, s5i_n, y_s5, y_lru, xs) = pl.pallas_call(
        _step_pre_kernel,
        out_shape=pre_out,
        grid=(1,),
        in_specs=[_const_spec(a) for a in consts] + [_const_spec(z)]
                 + [_lspec(a, layer) for a in states] + [_lspec(a, layer) for a in params],
        out_specs=tuple(pl.BlockSpec(o.shape, lambda i, n=len(o.shape): (0,) * n) for o in pre_out),
        scratch_shapes=[pltpu.VMEM((Bs, S5_CH), F32), pltpu.VMEM((Bs, S5_CH), F32)],
        compiler_params=_cparams("arbitrary"),
        name="step_pre",
    )(*consts, z, *states, *params)

    new_big, yts = {}, []
    for m, name in enumerate(("ssd", "ret")):
        s_new, y_t = pl.pallas_call(
            _step_state_kernel,
            out_shape=(jax.ShapeDtypeStruct(big[name].shape, F32), jax.ShapeDtypeStruct((MIX, Bs), F32)),
            grid=(H,),
            in_specs=[pl.BlockSpec((None, 3, SSD_STATE, Bs), lambda h, m=m: (m, 0, h, 0)),
                      pl.BlockSpec((None, H, Bs), lambda h, m=m: (m, 0, 0)),
                      pl.BlockSpec((None, None, SSD_STATE, SSD_HD, Bs), lambda h: (layer, h, 0, 0, 0))],
            out_specs=(pl.BlockSpec((None, None, SSD_STATE, SSD_HD, Bs), lambda h: (layer, h, 0, 0, 0)),
                       pl.BlockSpec((SSD_HD, Bs), lambda h: (h, 0))),
            input_output_aliases={2: 0},
            compiler_params=_cparams("parallel"),
            name="step_state_" + name,
        )(kqv, dec, big[name])
        new_big[name] = s_new
        yts.append(y_t)

    post_in = [yts[0], yts[1], xs, z]
    post_par = [P[k] for k in ("ssd_dlane", "ssd_norm", "ret_gn")]
    post_out = (jax.ShapeDtypeStruct((Bs, MIX), F32), jax.ShapeDtypeStruct((Bs, MIX), F32))
    y_ssd, y_ret = pl.pallas_call(
        _step_post_kernel,
        out_shape=post_out,
        grid=(1,),
        in_specs=[_const_spec(a) for a in post_in] + [_lspec(a, layer) for a in post_par],
        out_specs=tuple(pl.BlockSpec(o.shape, lambda i: (0, 0)) for o in post_out),
        compiler_params=_cparams("arbitrary"),
        name="step_post",
    )(*post_in, *post_par)

    new = dict(ssd_conv=sbuf_n, lru_conv=lbuf_n, lru=lst_n, s5_re=s5r_n, s5_im=s5i_n)
    return (y_ssd, y_s5, y_lru, y_ret), new, new_big


def _block_diag8(w):
    lead = w.shape[:-3]
    n, r, c = w.shape[-3:]
    eye = jnp.eye(n, dtype=w.dtype)
    out = w[..., :, :, None, :] * eye[:, None, :, None]
    return out.reshape(lead + (n * r, n * c))


def _row(v):
    return v[:, None, :]


def _cmul(ar, ai, br, bi):
    return ar * br - ai * bi, ar * bi + ai * br


def _prep_params(W):
    depth = W["w_in"].shape[0]
    P = {}
    w_t = jnp.transpose(W["w_in"], (0, 2, 1))
    P["w_in"] = jnp.concatenate(
        [w_t[:, 4872:8968], w_t[:, 2824:4872], w_t[:, 1800:2824], w_t[:, 1288:1800], w_t[:, 0:1288],
         jnp.zeros((depth, Z_WIDTH - 8968, D_MODEL), w_t.dtype)], axis=1).astype(BF16)
    P["norm_mix"] = _row(W["norm_mix"])
    pad = ((0, 0), (0, LANES - SSD_HEADS))
    P["ssd_cw"] = W["ssd_conv_w"]
    P["ssd_cb"] = _row(W["ssd_conv_b"])
    P["ssd_dtb"] = _row(jnp.pad(W["ssd_dt_bias"], pad))
    P["ssd_alog"] = _row(jnp.pad(W["ssd_a_log"], pad))
    P["ssd_dlane"] = _row(jnp.repeat(W["ssd_d"], SSD_HD, axis=1))
    P["ssd_norm"] = _row(W["ssd_norm"])
    lr, li = W["s5_lambda_re"], W["s5_lambda_im"]
    dt = jnp.exp(W["s5_log_dt"])[:, :, None]
    mag = jnp.exp(lr * dt)
    br, bi = mag * jnp.cos(li * dt), mag * jnp.sin(li * dt)
    den = lr * lr + li * li
    qr, qi = _cmul(br - 1.0, bi, lr / den, -li / den)
    wr, wi = _cmul(qr[..., None], qi[..., None], W["s5_b_re"], W["s5_b_im"])
    lag_r, lag_i = [wr], [wi]
    for _ in range(S5_LAGS - 1):
        nr, ni = _cmul(lag_r[-1], lag_i[-1], br[..., None], bi[..., None])
        lag_r.append(nr)
        lag_i.append(ni)
    gb = S5_GROUPS // S5_LB

    def embed_in(lags):
        m = jnp.stack([x.reshape(depth, S5_LB, gb, S5_STATE, S5_GDIM) for x in lags], axis=2)
        m = _block_diag8(jnp.swapaxes(m, -1, -2))
        return m.reshape(depth, S5_LB, len(lags) * gb * S5_GDIM, gb * S5_STATE)

    def embed_out(m):
        m = m.reshape(depth, S5_LB, gb, S5_GDIM, S5_STATE)
        return _block_diag8(jnp.swapaxes(m, -1, -2))

    P["s5_wbr"] = embed_in(lag_r).astype(BF16)
    P["s5_wbi"] = embed_in(lag_i).astype(BF16)
    P["s5_wcr"] = embed_out(W["s5_c_re"]).astype(BF16)
    P["s5_wci"] = embed_out(W["s5_c_im"]).astype(BF16)
    pr, pi = [br.reshape(depth, 1, S5_CH)], [bi.reshape(depth, 1, S5_CH)]
    for _ in range(SUBLANES - 1):
        nr, ni = _cmul(pr[-1], pi[-1], pr[0], pi[0])
        pr.append(nr)
        pi.append(ni)
    P["s5_pr"] = jnp.concatenate(pr, axis=1)
    P["s5_pi"] = jnp.concatenate(pi, axis=1)
    half = jnp.zeros((depth, S5_LAGS, S5_CH), F32)
    P["s5_qr"] = jnp.concatenate([half, P["s5_pr"][:, :SUBLANES - S5_LAGS]], axis=1)
    P["s5_qi"] = jnp.concatenate([half, P["s5_pi"][:, :SUBLANES - S5_LAGS]], axis=1)
    P["s5_d"] = W["s5_d"].reshape(depth, 1, MIX)
    P["s5_glu"] = W["s5_glu"].astype(BF16)
    P["lru_cw"] = W["lru_conv_w"]
    P["lru_cb"] = _row(W["lru_conv_b"])
    P["lru_wg"] = jnp.concatenate([_block_diag8(W["lru_wa"]), _block_diag8(W["lru_wx"])], axis=2).astype(BF16)
    P["lru_bg"] = _row(jnp.concatenate([W["lru_ba"], W["lru_bx"]], axis=1))
    P["lru_lam"] = _row(W["lru_lambda"])
    P["ret_gn"] = _row(W["ret_gn"])
    P["w_branch"] = W["w_branch"].astype(BF16)
    P["w_out"] = W["w_out"].astype(BF16)
    P["norm_ffn"] = _row(W["norm_ffn"])
    P["norm_final"] = W["norm_final"].reshape(1, D_MODEL)
    for k in ("ffn_w1", "ffn_w3", "ffn_w2", "moe_w1", "moe_w3", "moe_w2"):
        P[k] = W[k].astype(BF16)
    P["moe_router"] = jnp.pad(W["moe_router"], ((0, 0), (0, 0), (0, LANES - N_EXPERTS)))
    return P


def _trunk_prompt(x, P):
    B, L, _ = x.shape
    T = B * L
    depth = P["w_in"].shape[0]
    h = x.reshape(T, D_MODEL)
    new = {k: [] for k in ("ssd", "ssd_conv", "s5_re", "s5_im", "lru", "lru_conv", "ret")}
    for l in range(depth):
        z = _inproj(h, P, l, min(1024, T), BF16)
        z3 = z.reshape(B, L, Z_WIDTH)
        y_ssd, s_ssd, buf_ssd = _ssd_prompt(z3, P, l)
        y_s5, s5r, s5i = _s5_prompt(z3, P, l)
        y_lru, s_lru, buf_lru = _lru_prompt(z3, P, l)
        y_ret, s_ret = _ret_prompt(z3, P, l)
        ys = tuple(y.reshape(T, MIX) for y in (y_ssd, y_s5, y_lru, y_ret))
        h = _merge(ys, z, h, P, l, min(512, T))
        mixer = _moe_routed if (l % 2 == 1 and T % MOE_ROWS == 0) else _ffn
        h = mixer(h, P, l, min(512, T), final_norm=(l == depth - 1))
        new["ssd"].append(s_ssd)
        new["ssd_conv"].append(buf_ssd)
        new["s5_re"].append(s5r.reshape(B, S5_GROUPS, S5_STATE))
        new["s5_im"].append(s5i.reshape(B, S5_GROUPS, S5_STATE))
        new["lru"].append(s_lru.reshape(B, MIX))
        new["lru_conv"].append(buf_lru)
        new["ret"].append(s_ret)
    return h.reshape(B, L, D_MODEL), {k: jnp.stack(v) for k, v in new.items()}


def _trunk_sample(x, pos, st, P):
    Bs = x.shape[0]
    depth = P["w_in"].shape[0]
    h = x.reshape(Bs, D_MODEL)
    big = dict(ssd=jnp.transpose(st["ssd"], (0, 2, 3, 4, 1)), ret=jnp.transpose(st["ret"], (0, 2, 3, 4, 1)))
    views = dict(ssd_conv=jnp.transpose(st["ssd_conv"], (0, 2, 1, 3)),
                 lru_conv=jnp.transpose(st["lru_conv"], (0, 2, 1, 3)),
                 lru=st["lru"],
                 s5_re=st["s5_re"].reshape(depth, Bs, S5_CH),
                 s5_im=st["s5_im"].reshape(depth, Bs, S5_CH))
    new = {k: [] for k in views}
    for l in range(depth):
        z = _inproj(h, P, l, Bs, F32)
        ys, nl, big = _sample_mixers(z, views, big, l, P, pos)
        h = _merge(ys, z, h, P, l, Bs)
        h = _ffn(h, P, l, Bs, final_norm=(l == depth - 1))
        for k in new:
            new[k].append(nl[k])
    out = {k: jnp.stack(v) for k, v in new.items()}
    out["ssd_conv"] = jnp.transpose(out["ssd_conv"], (0, 2, 1, 3))
    out["lru_conv"] = jnp.transpose(out["lru_conv"], (0, 2, 1, 3))
    out["s5_re"] = out["s5_re"].reshape(st["s5_re"].shape)
    out["s5_im"] = out["s5_im"].reshape(st["s5_im"].shape)
    out["ssd"] = jnp.transpose(big["ssd"], (0, 4, 1, 2, 3))
    out["ret"] = jnp.transpose(big["ret"], (0, 4, 1, 2, 3))
    return h.reshape(Bs, 1, D_MODEL), out


def kernel(x_prompt, x_sample, state_ssd, state_ssd_conv, state_s5_re, state_s5_im, state_lru, state_lru_conv, state_ret, norm_mix, w_in, ssd_conv_w, ssd_conv_b, ssd_dt_bias, ssd_a_log, ssd_d, ssd_norm, s5_lambda_re, s5_lambda_im, s5_b_re, s5_b_im, s5_c_re, s5_c_im, s5_d, s5_log_dt, s5_glu, lru_conv_w, lru_conv_b, lru_wa, lru_ba, lru_wx, lru_bx, lru_lambda, ret_gn, w_branch, w_out, norm_ffn, ffn_w1, ffn_w3, ffn_w2, moe_router, moe_w1, moe_w3, moe_w2, norm_final):
    W = dict(norm_mix=norm_mix, w_in=w_in, ssd_conv_w=ssd_conv_w, ssd_conv_b=ssd_conv_b, ssd_dt_bias=ssd_dt_bias,
             ssd_a_log=ssd_a_log, ssd_d=ssd_d, ssd_norm=ssd_norm, s5_lambda_re=s5_lambda_re,
             s5_lambda_im=s5_lambda_im, s5_b_re=s5_b_re, s5_b_im=s5_b_im, s5_c_re=s5_c_re, s5_c_im=s5_c_im,
             s5_d=s5_d, s5_log_dt=s5_log_dt, s5_glu=s5_glu, lru_conv_w=lru_conv_w, lru_conv_b=lru_conv_b,
             lru_wa=lru_wa, lru_ba=lru_ba, lru_wx=lru_wx, lru_bx=lru_bx, lru_lambda=lru_lambda, ret_gn=ret_gn,
             w_branch=w_branch, w_out=w_out, norm_ffn=norm_ffn, moe_router=moe_router, norm_final=norm_final,
             ffn_w1=ffn_w1, ffn_w3=ffn_w3, ffn_w2=ffn_w2, moe_w1=moe_w1, moe_w3=moe_w3, moe_w2=moe_w2)
    P = _prep_params(W)
    y_p, sp = _trunk_prompt(x_prompt, P)
    st = dict(ssd=state_ssd, ssd_conv=state_ssd_conv, s5_re=state_s5_re, s5_im=state_s5_im,
              lru=state_lru, lru_conv=state_lru_conv, ret=state_ret)
    past_len = 16384
    y_s, ss = _trunk_sample(x_sample, past_len, st, P)
    names = ("ssd", "ssd_conv", "s5_re", "s5_im", "lru", "lru_conv", "ret")
    return (y_p, y_s) + tuple(sp[n] for n in names) + tuple(ss[n] for n in names)
```

```python
import functools
import math

import jax
import jax.numpy as jnp
import numpy as np
from jax import lax
from jax.experimental import pallas as pl
from jax.experimental.pallas import tpu as pltpu

F32 = jnp.float32
BF16 = jnp.bfloat16
EPS = 1e-6

D_MODEL = 1024
MIX = 512
CONV_K = 4
CHUNK = 128
SSD_HEADS = 8
SSD_HD = 64
SSD_STATE = 64
SSD_GROUPS = 2
SSD_CONV = MIX + 2 * SSD_GROUPS * SSD_STATE
S5_GROUPS = 32
S5_GDIM = 16
S5_STATE = 64
S5_CH = S5_GROUPS * S5_STATE
LRU_BLOCKS = 8
LRU_C = 8.0
RET_HEADS = 8
RET_HD = 64
ROPE_BASE = 10000.0
N_EXPERTS = 8
D_FF_TILE = 1408

Z_MERGE = 0
Z_RET = 4096
Z_LRU = 6144
Z_S5 = 7168
Z_SSD = 7680
Z_WIDTH = 9216

VMEM_LIMIT = 56 * 1024 * 1024
LANES = 128
SUBLANES = 8


def _cparams(*sem):
    return pltpu.CompilerParams(dimension_semantics=sem, vmem_limit_bytes=VMEM_LIMIT)


def _bdot(a, b):
    return jnp.dot(a.astype(BF16), b.astype(BF16), preferred_element_type=F32)


def _bdot_nt(a, b):
    return lax.dot_general(a.astype(BF16), b.astype(BF16), (((1,), (1,)), ((), ())), preferred_element_type=F32)


def _bdot_tn(a, b):
    return lax.dot_general(a.astype(BF16), b.astype(BF16), (((0,), (0,)), ((), ())), preferred_element_type=F32)


def _hdot(a, b):
    return jnp.dot(a, b, precision=lax.Precision.HIGHEST, preferred_element_type=F32)


def _split3_dot(x, m01):
    hi = x.astype(BF16)
    r1 = x - hi.astype(F32)
    mid = r1.astype(BF16)
    lo = (r1 - mid.astype(F32)).astype(BF16)
    m = m01.astype(BF16)
    d = functools.partial(jnp.dot, preferred_element_type=F32)
    return (d(lo, m) + d(mid, m)) + d(hi, m)


def _rmsnorm(x, g):
    ms = jnp.mean(x * x, axis=-1, keepdims=True)
    return x * lax.rsqrt(ms + EPS) * g


def _silu(x):
    return x * jax.nn.sigmoid(x)


def _neg_expm1_2x(log_a, a):
    return jnp.tanh(-log_a) * (1.0 + a * a)


def _inproj_kernel(x_ref, g_ref, w_ref, o_ref, hn_ref):
    @pl.when(pl.program_id(1) == 0)
    def _():
        hn_ref[...] = _rmsnorm(x_ref[...], g_ref[...]).astype(BF16)

    z = lax.dot_general(hn_ref[...], w_ref[...], (((1,), (1,)), ((), ())), preferred_element_type=F32)
    o_ref[...] = z.astype(o_ref.dtype)


def _lspec(a, l):
    rest = tuple(a.shape[1:])
    return pl.BlockSpec((None,) + rest, lambda *_: (l,) + (0,) * len(rest))


def _inproj(x, P, l, tm, out_dtype):
    T = x.shape[0]
    tn = 2304
    return pl.pallas_call(
        _inproj_kernel,
        out_shape=jax.ShapeDtypeStruct((T, Z_WIDTH), out_dtype),
        grid=(T // tm, Z_WIDTH // tn),
        in_specs=[pl.BlockSpec((tm, D_MODEL), lambda i, j: (i, 0)),
                  _lspec(P["norm_mix"], l),
                  pl.BlockSpec((None, tn, D_MODEL), lambda i, j: (l, j, 0))],
        out_specs=pl.BlockSpec((tm, tn), lambda i, j: (i, j)),
        scratch_shapes=[pltpu.VMEM((tm, D_MODEL), BF16)],
        compiler_params=_cparams("parallel", "arbitrary"),
        name="inproj",
    )(x, P["norm_mix"], P["w_in"])


SUBCHUNKS = 4
SSD_SUBCHUNKS = 2


def _sub_chunks(z_ref):
    n = z_ref.shape[0] // CHUNK
    return [(s, n) for s in range(n)]


def _when_first(sub, fn):
    if sub[0] == 0:
        pl.when(pl.program_id(1) == 0)(fn)


def _when_last(sub, fn):
    if sub[0] == sub[1] - 1:
        pl.when(pl.program_id(1) == pl.num_programs(1) - 1)(fn)


def _sub_rows(ref, s):
    return ref.at[pl.ds(s * CHUNK, CHUNK), :]


def _conv_chunk(s, x, pad_ref, w_ref, b_ref):
    Lc = x.shape[0]

    def _():
        pad_ref[0:8, :] = jnp.zeros((8, x.shape[1]), F32)

    _when_first(s, _)
    pad_ref[8:8 + Lc, :] = x
    out = b_ref[...] + pad_ref[5:5 + Lc, :] * w_ref[0:1, :]
    out = out + pad_ref[6:6 + Lc, :] * w_ref[1:2, :]
    out = out + pad_ref[7:7 + Lc, :] * w_ref[2:3, :]
    out = out + x * w_ref[3:4, :]
    return out


def _conv_finish(pad_ref, Lc):
    pad_ref[0:8, :] = pad_ref[Lc:Lc + 8, :]


def _head_masks(nh, hd, L):
    rows = (np.arange(nh * L)[:, None] // L) == (np.arange(nh * hd)[None, :] // hd)
    diag = (np.arange(nh * hd)[:, None] // hd) == (np.arange(nh * hd)[None, :] // hd)
    return jnp.asarray(rows, BF16), jnp.asarray(diag, F32)


def _head_block_rows(x, nh, mask):
    return jnp.concatenate([x.astype(BF16)] * nh, axis=0) * mask


def _pow2_div(x, d):
    return lax.shift_right_logical(x, jnp.int32(int(math.log2(d))))


def _head_block_rows_sel(x, nh, hd):
    L = x.shape[0]
    xt = jnp.concatenate([x.astype(BF16)] * nh, axis=0)
    row = lax.broadcasted_iota(jnp.int32, xt.shape, 0)
    col = lax.broadcasted_iota(jnp.int32, xt.shape, 1)
    return jnp.where(_pow2_div(row, L) == _pow2_div(col, hd), xt, jnp.zeros_like(xt))


def _head_block_cols(xt, nh, hd):
    L = xt.shape[1]
    xc = jnp.concatenate([xt.astype(BF16)] * nh, axis=1)
    row = lax.broadcasted_iota(jnp.int32, xc.shape, 0)
    col = lax.broadcasted_iota(jnp.int32, xc.shape, 1)
    return jnp.where(_pow2_div(row, hd) == _pow2_div(col, L), xc, jnp.zeros_like(xc))


def _group_repeat_lanes(m, rep):
    lane = lax.broadcasted_iota(jnp.int32, m.shape, 1)
    swapped = pltpu.roll(m, m.shape[1] // 2, 1)
    low = lane < m.shape[1] // 2
    g0 = jnp.where(low, m, swapped)
    g1 = jnp.where(low, swapped, m)
    return jnp.concatenate([g0] * (rep // 2) + [g1] * (rep // 2), axis=1)


def _ssd_kernel(z_ref, *refs):
    y_ref = refs[-5]
    for sub in _sub_chunks(z_ref):
        _ssd_chunk(sub, _sub_rows(z_ref, sub[0]), *refs[:-5], _sub_rows(y_ref, sub[0]), *refs[-4:])


def _ssd_chunk(s, z_ref, cw_ref, cb_ref, dtb_ref, alog_ref, dlane_ref, ng_ref, tri_ref, ehj_ref, ehn_ref,
               cbias_ref, hrows_ref, hdiag_ref, y_ref, sto_ref, buf_ref, pad_ref, st_ref):
    Lc = z_ref.shape[0]
    zz = z_ref[...].astype(F32)
    zgate = zz[:, 0:MIX]
    xbc = zz[:, MIX:MIX + SSD_CONV]
    dt_raw = zz[:, MIX + SSD_CONV:MIX + SSD_CONV + LANES]

    def _():
        st_ref[...] = jnp.zeros(st_ref.shape, F32)

    _when_first(s, _)
    conv = _conv_chunk(s, xbc, pad_ref, cw_ref, cb_ref)
    xc = _silu(conv)
    xs = xc[:, 0:MIX]
    nbc = SSD_GROUPS * SSD_STATE
    bm = xc[:, MIX:MIX + nbc]
    cm = xc[:, MIX + nbc:MIX + 2 * nbc]
    dt = jax.nn.softplus(dt_raw + dtb_ref[...])
    a = -jnp.exp(alog_ref[...])
    ld = dt * a
    acum = _hdot(tri_ref[...], ld)
    acum_t = acum.T
    dt_t = dt.T
    a_row = jnp.concatenate([acum_t[h:h + 1, :] for h in range(SSD_HEADS)], axis=1)
    dt_row = jnp.concatenate([dt_t[h:h + 1, :] for h in range(SSD_HEADS)], axis=1)
    a_col = _split3_dot(acum, ehj_ref[...])
    decay = jnp.exp((a_col - a_row) + cbias_ref[...])
    rep = SSD_HEADS // SSD_GROUPS
    gmats = []
    for g in range(SSD_GROUPS):
        cg = cm[:, g * SSD_STATE:(g + 1) * SSD_STATE]
        bg = bm[:, g * SSD_STATE:(g + 1) * SSD_STATE]
        gmats.append(_bdot_nt(cg, bg))
    g_all = jnp.concatenate([gmats[h // rep] for h in range(SSD_HEADS)], axis=1)
    m_all = g_all * decay * dt_row
    y = _bdot(m_all, _head_block_rows(xs, SSD_HEADS, hrows_ref[...]))
    exp_a = jnp.exp(acum)
    exp_a_l = _split3_dot(exp_a, ehn_ref[...])
    c_rep = _group_repeat_lanes(cm, rep)
    y = y + _bdot_nt(c_rep * exp_a_l, st_ref[...])
    a_last = acum[Lc - 1:Lc, :]
    w_end_l = _split3_dot(jnp.exp(a_last - acum) * dt, ehn_ref[...])
    upd = _bdot_tn(xs, _group_repeat_lanes(bm, rep) * w_end_l)
    st_ref[...] = exp_a_l[Lc - 1:Lc, :] * st_ref[...] + upd * hdiag_ref[...]
    y = y + dlane_ref[...] * xs
    y = y * _silu(zgate)
    y_ref[...] = _rmsnorm(y, ng_ref[...]).astype(y_ref.dtype)

    def _():
        buf_ref[...] = pad_ref[Lc + 8 - (CONV_K - 1):Lc + 8, :]
        s_t = st_ref[...].T
        for h in range(SSD_HEADS):
            sl = slice(h * SSD_HD, (h + 1) * SSD_HD)
            sto_ref[h] = s_t[sl, sl]

    _when_last(s, _)
    _conv_finish(pad_ref, Lc)


def _const_spec(a):
    return pl.BlockSpec(a.shape, lambda *_: (0,) * a.ndim)


def _ssd_prompt(z3, P, l):
    B, L, _ = z3.shape
    Lc = CHUNK
    assert L % (SUBCHUNKS * Lc) == 0 and L % (SSD_SUBCHUNKS * Lc) == 0, "sequence length must fill whole grid steps"
    params = [P[k] for k in ("ssd_cw", "ssd_cb", "ssd_dtb", "ssd_alog", "ssd_dlane", "ssd_norm")]
    j = np.arange(SSD_HEADS * Lc) % Lc
    cbias = jnp.asarray(np.where(np.arange(Lc)[:, None] >= j[None, :], 0.0, -1e30), F32)
    consts = [_tri(Lc), _head_expand(SSD_HEADS, Lc), _head_expand(SSD_HEADS, SSD_HD), cbias,
              *_head_masks(SSD_HEADS, SSD_HD, Lc)]
    return pl.pallas_call(
        _ssd_kernel,
        out_shape=(jax.ShapeDtypeStruct((B, L, MIX), BF16),
                   jax.ShapeDtypeStruct((B, SSD_HEADS, SSD_STATE, SSD_HD), F32),
                   jax.ShapeDtypeStruct((B, CONV_K - 1, SSD_CONV), F32)),
        grid=(B, L // (SSD_SUBCHUNKS * Lc)),
        in_specs=[pl.BlockSpec((None, SSD_SUBCHUNKS * Lc, 1536), lambda b, c: (b, c, Z_SSD // 1536))]
                 + [_lspec(a, l) for a in params] + [_const_spec(a) for a in consts],
        out_specs=(pl.BlockSpec((None, SSD_SUBCHUNKS * Lc, MIX), lambda b, c: (b, c, 0)),
                   pl.BlockSpec((None, SSD_HEADS, SSD_STATE, SSD_HD), lambda b, c: (b, 0, 0, 0)),
                   pl.BlockSpec((None, CONV_K - 1, SSD_CONV), lambda b, c: (b, 0, 0))),
        scratch_shapes=[pltpu.VMEM((Lc + 8, SSD_CONV), F32), pltpu.VMEM((MIX, MIX), F32)],
        compiler_params=_cparams("parallel", "arbitrary"),
        name="ssd_prompt",
    )(z3, *params, *consts)


def _tri(Lc):
    return jnp.asarray(np.tril(np.ones((Lc, Lc), np.float32)))


def _head_expand(nh, width):
    e = np.zeros((LANES, nh * width), np.float32)
    for h in range(nh):
        e[h, h * width:(h + 1) * width] = 1.0
    return jnp.asarray(e, BF16)


def _ret_gammas():
    return 1.0 - np.exp2(-5.0 - np.arange(RET_HEADS, dtype=np.float64))


def _ret_tables(Lc):
    gam = _ret_gammas()
    i = np.arange(Lc)
    d = i[:, None] - i[None, :]
    decay = np.where(d >= 0, gam[:, None, None] ** np.maximum(d, 0)[None], 0.0)
    decay_l = np.transpose(decay, (1, 0, 2)).reshape(Lc, RET_HEADS * Lc)
    grow_l = np.repeat(gam[None, :] ** (i[:, None] + 1), RET_HD, axis=1)
    toend_t = np.repeat(gam[:, None] ** (Lc - 1 - i[None, :]), RET_HD, axis=0)
    hd = np.arange(MIX) // RET_HD
    state_decay = np.where(hd[:, None] == hd[None, :], (gam ** Lc)[hd][:, None], 0.0)
    return tuple(jnp.asarray(t, F32) for t in (decay_l, grow_l, toend_t, state_decay))


def _rope_tables(pos):
    half = RET_HD // 2
    inv = ROPE_BASE ** (-np.arange(half, dtype=np.float64) / half)
    ang = np.asarray(pos, np.float64)[:, None] * inv[None, :]
    cos = np.cos(ang)
    sin = np.sin(ang)
    cos_l = np.tile(np.concatenate([cos, cos], axis=1), (1, RET_HEADS))
    sin_l = np.tile(np.concatenate([-sin, sin], axis=1), (1, RET_HEADS))
    return jnp.asarray(cos_l, F32), jnp.asarray(sin_l, F32)


def _rotary_lanes(x, cos_l, sin_l):
    lane = lax.broadcasted_iota(jnp.int32, x.shape, 1)
    first = (lane & (RET_HD - 1)) < (RET_HD // 2)
    n = x.shape[1]
    swapped = jnp.where(first, pltpu.roll(x, n - RET_HD // 2, 1), pltpu.roll(x, RET_HD // 2, 1))
    return x * cos_l + swapped * sin_l


def _group_norm_head(o):
    mu = jnp.mean(o, axis=-1, keepdims=True)
    d = o - mu
    var = jnp.mean(d * d, axis=-1, keepdims=True)
    return d * lax.rsqrt(var + 1e-5)


def _group_norm_lanes(o, hd):
    cols = []
    for cb in range(o.shape[1] // LANES):
        x = o[:, cb * LANES:(cb + 1) * LANES]
        low = lax.broadcasted_iota(jnp.int32, x.shape, 1) < hd

        def seg_mean(t):
            lo = jnp.sum(jnp.where(low, t, 0.0), axis=1, keepdims=True)
            hi = jnp.sum(jnp.where(low, 0.0, t), axis=1, keepdims=True)
            return jnp.where(low, lo, hi) * (1.0 / hd)

        d = x - seg_mean(x)
        cols.append(d * lax.rsqrt(seg_mean(d * d) + 1e-5))
    return jnp.concatenate(cols, axis=1)


def _ret_kernel(z_ref, cos_ref, sin_ref, *refs):
    y_ref = refs[-3]
    for sub in _sub_chunks(z_ref):
        s = sub[0]
        _ret_chunk(sub, _sub_rows(z_ref, s), _sub_rows(cos_ref, s), _sub_rows(sin_ref, s), *refs[:-3],
                   _sub_rows(y_ref, s), *refs[-2:])


def _ret_chunk(s, z_ref, cos_ref, sin_ref, dec_ref, grow_ref, toend_ref, sdec_ref, hdiag_ref,
               gn_ref, y_ref, sto_ref, st_ref):
    zz = z_ref[...].astype(F32)
    q = _rotary_lanes(zz[:, 0:MIX], cos_ref[...], sin_ref[...]) * (RET_HD ** -0.5)
    k = _rotary_lanes(zz[:, MIX:2 * MIX], cos_ref[...], sin_ref[...])
    v = zz[:, 2 * MIX:3 * MIX]
    gate = zz[:, 3 * MIX:4 * MIX]

    def _():
        st_ref[...] = jnp.zeros(st_ref.shape, F32)

    _when_first(s, _)
    k_t = k.T
    g_all = _bdot(q, _head_block_cols(k_t, RET_HEADS, RET_HD))
    o = _bdot(g_all * dec_ref[...], _head_block_rows_sel(v, RET_HEADS, RET_HD))
    o = o + _bdot(q * grow_ref[...], st_ref[...])
    upd = _bdot(k_t * toend_ref[...], v)
    st_ref[...] = sdec_ref[...] * st_ref[...] + upd * hdiag_ref[...]
    y_ref[...] = (_silu(gate) * (_group_norm_lanes(o, RET_HD) * gn_ref[...])).astype(y_ref.dtype)

    def _():
        for h in range(RET_HEADS):
            sl = slice(h * RET_HD, (h + 1) * RET_HD)
            sto_ref[h] = st_ref[sl, sl]

    _when_last(s, _)


def _ret_prompt(z3, P, l):
    B, L, _ = z3.shape
    Lc = CHUNK
    assert L % (SUBCHUNKS * Lc) == 0 and L % (SSD_SUBCHUNKS * Lc) == 0, "sequence length must fill whole grid steps"
    cos_l, sin_l = _rope_tables(np.arange(L))
    consts = list(_ret_tables(Lc)) + [_head_masks(RET_HEADS, RET_HD, Lc)[1]]
    return pl.pallas_call(
        _ret_kernel,
        out_shape=(jax.ShapeDtypeStruct((B, L, MIX), BF16),
                   jax.ShapeDtypeStruct((B, RET_HEADS, RET_HD, RET_HD), F32)),
        grid=(B, L // (SUBCHUNKS * Lc)),
        in_specs=[pl.BlockSpec((None, SUBCHUNKS * Lc,2048), lambda b, c: (b, c, Z_RET // 2048)),
                  pl.BlockSpec((SUBCHUNKS * Lc, MIX), lambda b, c: (c, 0)),
                  pl.BlockSpec((SUBCHUNKS * Lc, MIX), lambda b, c: (c, 0))]
                 + [_const_spec(a) for a in consts] + [_lspec(P["ret_gn"], l)],
        out_specs=(pl.BlockSpec((None, SUBCHUNKS * Lc,MIX), lambda b, c: (b, c, 0)),
                   pl.BlockSpec((None, RET_HEADS, RET_HD, RET_HD), lambda b, c: (b, 0, 0, 0))),
        scratch_shapes=[pltpu.VMEM((MIX, MIX), F32)],
        compiler_params=_cparams("parallel", "arbitrary"),
        name="ret_prompt",
    )(z3, cos_l, sin_l, *consts, P["ret_gn"])


def _lru_gates(xc, wg_ref, bg_ref, lam_ref):
    rg = _bdot(xc, wg_ref[...]) + bg_ref[...]
    r = jax.nn.sigmoid(rg[:, 0:MIX])
    i = jax.nn.sigmoid(rg[:, MIX:2 * MIX])
    log_a = -LRU_C * r * jax.nn.softplus(-lam_ref[...])
    a = jnp.exp(log_a)
    bx = jnp.sqrt(_neg_expm1_2x(log_a, a)) * (i * xc)
    return a, bx


def _lru_kernel(z_ref, *refs):
    y_ref = refs[-5]
    for sub in _sub_chunks(z_ref):
        _lru_chunk(sub, _sub_rows(z_ref, sub[0]), *refs[:-5], _sub_rows(y_ref, sub[0]), *refs[-4:])


def _lru_chunk(s, z_ref, cw_ref, cb_ref, wg_ref, bg_ref, lam_ref,
               y_ref, st_ref, buf_ref, pad_ref, h_ref):
    Lc = z_ref.shape[0]
    zz = z_ref[...].astype(F32)
    gate = zz[:, 0:MIX]
    x = zz[:, MIX:2 * MIX]

    def _():
        st_ref[...] = jnp.zeros(st_ref.shape, F32)

    _when_first(s, _)
    xc = _conv_chunk(s, x, pad_ref, cw_ref, cb_ref)
    a, bx = _lru_gates(xc, wg_ref, bg_ref, lam_ref)
    ng = Lc // SUBLANES
    a3 = a.reshape(ng, SUBLANES, MIX)
    b3 = bx.reshape(ng, SUBLANES, MIX)
    sub = lax.broadcasted_iota(jnp.int32, a3.shape, 1)
    step = 1
    while step < SUBLANES:
        keep = sub >= step
        b3 = jnp.where(keep, b3 + a3 * pltpu.roll(b3, step, 1), b3)
        a3 = jnp.where(keep, a3 * pltpu.roll(a3, step, 1), a3)
        step *= 2
    carry = st_ref[...]
    for g in range(ng):
        hg = b3[g] + a3[g] * jnp.broadcast_to(carry, (SUBLANES, MIX))
        h_ref[g * SUBLANES:(g + 1) * SUBLANES, :] = hg
        carry = hg[SUBLANES - 1:SUBLANES, :]
    st_ref[...] = carry
    y_ref[...] = (h_ref[...] * jax.nn.gelu(gate)).astype(y_ref.dtype)

    def _():
        buf_ref[...] = pad_ref[Lc + 8 - (CONV_K - 1):Lc + 8, :]

    _when_last(s, _)
    _conv_finish(pad_ref, Lc)


def _lru_prompt(z3, P, l):
    B, L, _ = z3.shape
    Lc = CHUNK
    assert L % (SUBCHUNKS * Lc) == 0 and L % (SSD_SUBCHUNKS * Lc) == 0, "sequence length must fill whole grid steps"
    params = [P[k] for k in ("lru_cw", "lru_cb", "lru_wg", "lru_bg", "lru_lam")]
    return pl.pallas_call(
        _lru_kernel,
        out_shape=(jax.ShapeDtypeStruct((B, L, MIX), BF16),
                   jax.ShapeDtypeStruct((B, 1, MIX), F32),
                   jax.ShapeDtypeStruct((B, CONV_K - 1, MIX), F32)),
        grid=(B, L // (SUBCHUNKS * Lc)),
        in_specs=[pl.BlockSpec((None, SUBCHUNKS * Lc,1024), lambda b, c: (b, c, Z_LRU // 1024))]
                 + [_lspec(a, l) for a in params],
        out_specs=(pl.BlockSpec((None, SUBCHUNKS * Lc,MIX), lambda b, c: (b, c, 0)),
                   pl.BlockSpec((None, 1, MIX), lambda b, c: (b, 0, 0)),
                   pl.BlockSpec((None, CONV_K - 1, MIX), lambda b, c: (b, 0, 0))),
        scratch_shapes=[pltpu.VMEM((Lc + 8, MIX), F32), pltpu.VMEM((Lc, MIX), F32)],
        compiler_params=_cparams("parallel", "arbitrary"),
        name="lru_prompt",
    )(z3, *params)


S5_LB = 4
S5_LAGS = 4
assert 2 * S5_LAGS == SUBLANES


def _s5_project_in(u, wbr_ref, wbi_ref, xr_ref, xi_ref, lags):
    shifted = [u]
    if lags > 1:
        sub = lax.broadcasted_iota(jnp.int32, u.shape, 0) & (lags - 1)
        shifted += [jnp.where(sub >= d, pltpu.roll(u, d, 0), 0.0) for d in range(1, lags)]
    shifted = [s.astype(BF16) for s in shifted]
    for kb in range(S5_LB):
        lhs = jnp.concatenate([s[:, kb * 128:(kb + 1) * 128] for s in shifted], axis=1)
        k = lags * 128
        xr_ref[:, kb * 512:(kb + 1) * 512] = jnp.dot(lhs, wbr_ref[kb, 0:k, :], preferred_element_type=F32)
        xi_ref[:, kb * 512:(kb + 1) * 512] = jnp.dot(lhs, wbi_ref[kb, 0:k, :], preferred_element_type=F32)


def _s5_project_out(xr, xi, u, wcr_ref, wci_ref, d_ref, wglu_ref):
    ys = []
    for kb in range(S5_LB):
        sl = slice(kb * 512, (kb + 1) * 512)
        ys.append(_bdot(xr[:, sl], wcr_ref[kb]) - _bdot(xi[:, sl], wci_ref[kb]))
    y = jnp.concatenate(ys, axis=1) + d_ref[...] * u
    y = jax.nn.gelu(y)
    return y * jax.nn.sigmoid(_bdot(y, wglu_ref[...]))


def _s5_kernel(z_ref, *refs):
    y_ref = refs[-5]
    for sub in _sub_chunks(z_ref):
        _s5_chunk(sub, _sub_rows(z_ref, sub[0]), *refs[:-5], _sub_rows(y_ref, sub[0]), *refs[-4:])


def _s5_chunk(s, z_ref, wbr_ref, wbi_ref, wcr_ref, wci_ref, pr_ref, pi_ref, d_ref, wglu_ref, qr_ref, qi_ref,
              y_ref, sr_ref, si_ref, xr_ref, xi_ref):
    Lc = z_ref.shape[0]
    u = z_ref[...].astype(F32)

    def _():
        sr_ref[...] = jnp.zeros(sr_ref.shape, F32)
        si_ref[...] = jnp.zeros(si_ref.shape, F32)

    _when_first(s, _)
    _s5_project_in(u, wbr_ref, wbi_ref, xr_ref, xi_ref, S5_LAGS)
    ng = Lc // SUBLANES
    x3r = xr_ref[...].reshape(ng, SUBLANES, S5_CH)
    x3i = xi_ref[...].reshape(ng, SUBLANES, S5_CH)
    tr = jnp.broadcast_to(x3r[:, S5_LAGS - 1:S5_LAGS, :], x3r.shape)
    ti = jnp.broadcast_to(x3i[:, S5_LAGS - 1:S5_LAGS, :], x3i.shape)
    mr, mi = qr_ref[...][None], qi_ref[...][None]
    x3r, x3i = x3r + (mr * tr - mi * ti), x3i + (mr * ti + mi * tr)
    pcr, pci = pr_ref[...], pi_ref[...]
    cr, ci = sr_ref[...], si_ref[...]
    for g in range(ng):
        br = jnp.broadcast_to(cr, (SUBLANES, S5_CH))
        bi = jnp.broadcast_to(ci, (SUBLANES, S5_CH))
        gr = x3r[g] + (pcr * br - pci * bi)
        gi = x3i[g] + (pcr * bi + pci * br)
        xr_ref[g * SUBLANES:(g + 1) * SUBLANES, :] = gr
        xi_ref[g * SUBLANES:(g + 1) * SUBLANES, :] = gi
        cr, ci = gr[SUBLANES - 1:SUBLANES, :], gi[SUBLANES - 1:SUBLANES, :]
    sr_ref[...] = cr
    si_ref[...] = ci
    y_ref[...] = _s5_project_out(xr_ref[...], xi_ref[...], u, wcr_ref, wci_ref, d_ref, wglu_ref).astype(y_ref.dtype)


S5_PARAMS = ("s5_wbr", "s5_wbi", "s5_wcr", "s5_wci", "s5_pr", "s5_pi", "s5_d", "s5_glu", "s5_qr", "s5_qi")


def _s5_prompt(z3, P, l):
    B, L, _ = z3.shape
    Lc = CHUNK
    assert L % (SUBCHUNKS * Lc) == 0 and L % (SSD_SUBCHUNKS * Lc) == 0, "sequence length must fill whole grid steps"
    params = [P[k] for k in S5_PARAMS]
    return pl.pallas_call(
        _s5_kernel,
        out_shape=(jax.ShapeDtypeStruct((B, L, MIX), BF16),
                   jax.ShapeDtypeStruct((B, 1, S5_CH), F32),
                   jax.ShapeDtypeStruct((B, 1, S5_CH), F32)),
        grid=(B, L // (SUBCHUNKS * Lc)),
        in_specs=[pl.BlockSpec((None, SUBCHUNKS * Lc,MIX), lambda b, c: (b, c, Z_S5 // MIX))]
                 + [_lspec(a, l) for a in params],
        out_specs=(pl.BlockSpec((None, SUBCHUNKS * Lc,MIX), lambda b, c: (b, c, 0)),
                   pl.BlockSpec((None, 1, S5_CH), lambda b, c: (b, 0, 0)),
                   pl.BlockSpec((None, 1, S5_CH), lambda b, c: (b, 0, 0))),
        scratch_shapes=[pltpu.VMEM((Lc, S5_CH), F32), pltpu.VMEM((Lc, S5_CH), F32)],
        compiler_params=_cparams("parallel", "arbitrary"),
        name="s5_prompt",
    )(z3, *params)


def _merge_kernel(y0_ref, y1_ref, y2_ref, y3_ref, zg_ref, h_ref, wb_ref, wo_ref, o_ref):
    acc = None
    for k, y_ref in enumerate((y0_ref, y1_ref, y2_ref, y3_ref)):
        br = _bdot(y_ref[...], wb_ref[k])
        t = jax.nn.sigmoid(zg_ref[:, k * D_MODEL:(k + 1) * D_MODEL].astype(F32)) * br
        acc = t if acc is None else acc + t
    o_ref[...] = h_ref[...] + _bdot(acc, wo_ref[...])


def _merge(ys, z, h, P, l, tm):
    T = h.shape[0]
    rows = lambda w: pl.BlockSpec((tm, w), lambda i: (i, 0))
    return pl.pallas_call(
        _merge_kernel,
        out_shape=jax.ShapeDtypeStruct((T, D_MODEL), F32),
        grid=(T // tm,),
        in_specs=[rows(MIX), rows(MIX), rows(MIX), rows(MIX),
                  pl.BlockSpec((tm, 4 * D_MODEL), lambda i: (i, Z_MERGE)),
                  rows(D_MODEL),
                  _lspec(P["w_branch"], l), _lspec(P["w_out"], l)],
        out_specs=rows(D_MODEL),
        compiler_params=_cparams("parallel"),
        name="merge",
    )(*ys, z, h, P["w_branch"], P["w_out"])


def _top2_gates(logits):
    lane, i1, i2, w1, w2 = _top2(logits)
    return jnp.where(lane == i1, w1, 0.0) + jnp.where(lane == i2, w2, 0.0)


def _ffn_kernel(moe, final_norm, *refs):
    if moe:
        h_ref, g_ref, rt_ref, w1_ref, w3_ref, w2_ref, gf_ref, o_ref, hn_ref, acc_ref, gate_ref = refs
    else:
        h_ref, g_ref, w1_ref, w3_ref, w2_ref, gf_ref, o_ref, hn_ref, acc_ref = refs
    e = pl.program_id(1)
    ne = pl.num_programs(1)

    @pl.when(e == 0)
    def _():
        hn = _rmsnorm(h_ref[...], g_ref[...])
        hn_ref[...] = hn.astype(BF16)
        acc_ref[...] = jnp.zeros(acc_ref.shape, F32)
        if moe:
            lane = lax.broadcasted_iota(jnp.int32, (hn.shape[0], LANES), 1)
            logits = jnp.where(lane < N_EXPERTS, _hdot(hn, rt_ref[...]), -jnp.inf)
            gate_ref[...] = _top2_gates(logits)

    hn = hn_ref[...]
    a = jnp.dot(hn, w1_ref[...], preferred_element_type=F32)
    b = jnp.dot(hn, w3_ref[...], preferred_element_type=F32)
    o = _bdot(_silu(a) * b, w2_ref[...])
    if moe:
        lane = lax.broadcasted_iota(jnp.int32, gate_ref.shape, 1)
        ge = jnp.sum(jnp.where(lane == e, gate_ref[...], 0.0), axis=-1, keepdims=True)
        o = ge * o
    acc_ref[...] += o

    @pl.when(e == ne - 1)
    def _():
        out = h_ref[...] + acc_ref[...]
        if final_norm:
            out = _rmsnorm(out, gf_ref[...])
        o_ref[...] = out


def _ffn(h, P, l, tm, final_norm):
    T = h.shape[0]
    moe = l % 2 == 1
    j = l // 2
    tf = D_FF_TILE
    row_spec = pl.BlockSpec((tm, D_MODEL), lambda i, e: (i, 0))
    gfinal = P["norm_final"]
    if moe:
        w1, w3, w2 = P["moe_w1"], P["moe_w3"], P["moe_w2"]
        ne = w1.shape[1]
        wspecs = [pl.BlockSpec((None, None, D_MODEL, tf), lambda i, e: (j, e, 0, 0)),
                  pl.BlockSpec((None, None, D_MODEL, tf), lambda i, e: (j, e, 0, 0)),
                  pl.BlockSpec((None, None, tf, D_MODEL), lambda i, e: (j, e, 0, 0))]
        in_specs = ([row_spec, _lspec(P["norm_ffn"], l), _lspec(P["moe_router"], j)] + wspecs
                    + [_const_spec(gfinal)])
        args = (h, P["norm_ffn"], P["moe_router"], w1, w3, w2, gfinal)
        scratch = [pltpu.VMEM((tm, D_MODEL), BF16), pltpu.VMEM((tm, D_MODEL), F32), pltpu.VMEM((tm, LANES), F32)]
    else:
        w1, w3, w2 = P["ffn_w1"], P["ffn_w3"], P["ffn_w2"]
        ne = w1.shape[2] // tf
        wspecs = [pl.BlockSpec((None, D_MODEL, tf), lambda i, e: (j, 0, e)),
                  pl.BlockSpec((None, D_MODEL, tf), lambda i, e: (j, 0, e)),
                  pl.BlockSpec((None, tf, D_MODEL), lambda i, e: (j, e, 0))]
        in_specs = [row_spec, _lspec(P["norm_ffn"], l)] + wspecs + [_const_spec(gfinal)]
        args = (h, P["norm_ffn"], w1, w3, w2, gfinal)
        scratch = [pltpu.VMEM((tm, D_MODEL), BF16), pltpu.VMEM((tm, D_MODEL), F32)]
    return pl.pallas_call(
        functools.partial(_ffn_kernel, moe, final_norm),
        out_shape=jax.ShapeDtypeStruct((T, D_MODEL), F32),
        grid=(T // tm, ne),
        in_specs=in_specs,
        out_specs=row_spec,
        scratch_shapes=scratch,
        compiler_params=_cparams("parallel", "arbitrary"),
        name="moe" if moe else "ffn",
    )(*args)


MOE_ROWS = 512
ROUTE_LANES = ("e1", "e2", "r1", "r2", "w1", "w2")


def _top2(logits):
    lane = lax.broadcasted_iota(jnp.int32, logits.shape, 1).astype(F32)
    big = float(LANES)
    m1 = jnp.max(logits, axis=-1, keepdims=True)
    i1 = jnp.min(jnp.where(logits == m1, lane, big), axis=-1, keepdims=True)
    rest = jnp.where(lane == i1, -jnp.inf, logits)
    m2 = jnp.max(rest, axis=-1, keepdims=True)
    i2 = jnp.min(jnp.where(rest == m2, lane, big), axis=-1, keepdims=True)
    e2 = jnp.exp(m2 - m1)
    den = 1.0 + e2
    return lane, i1, i2, 1.0 / den, e2 / den


def _moe_route_kernel(h_ref, g_ref, rt_ref, ltri_ref, hn_ref, info_ref, cnt_ref, base_ref):
    i = pl.program_id(0)

    @pl.when(i == 0)
    def _():
        base_ref[...] = jnp.zeros(base_ref.shape, F32)

    hn = _rmsnorm(h_ref[...], g_ref[...])
    hn_ref[...] = hn
    lane_i = lax.broadcasted_iota(jnp.int32, (hn.shape[0], LANES), 1)
    logits = jnp.where(lane_i < N_EXPERTS, _hdot(hn, rt_ref[...]), -jnp.inf)
    lane, i1, i2, w1, w2 = _top2(logits)
    oh1 = (lane == i1).astype(F32)
    oh2 = (lane == i2).astype(F32)
    oh = oh1 + oh2
    before = jnp.dot(ltri_ref[...], oh.astype(BF16), preferred_element_type=F32)
    rank = base_ref[...] + before
    r1 = jnp.sum(oh1 * rank, axis=-1, keepdims=True)
    r2 = jnp.sum(oh2 * rank, axis=-1, keepdims=True)
    base_ref[...] += jnp.sum(oh, axis=0, keepdims=True)
    fields = dict(e1=i1, e2=i2, r1=r1, r2=r2, w1=w1, w2=w2)
    info = jnp.zeros(lane.shape, F32)
    for k, name in enumerate(ROUTE_LANES):
        info = jnp.where(lane_i == k, fields[name], info)
    info_ref[...] = info

    @pl.when(i == pl.num_programs(0) - 1)
    def _():
        cnt_ref[...] = base_ref[...]


def _moe_route(h, P, l, tm):
    T = h.shape[0]
    j = l // 2
    ltri = jnp.asarray(np.tril(np.ones((tm, tm), np.float32), -1), BF16)
    return pl.pallas_call(
        _moe_route_kernel,
        out_shape=(jax.ShapeDtypeStruct((T, D_MODEL), F32), jax.ShapeDtypeStruct((T, LANES), F32),
                   jax.ShapeDtypeStruct((1, LANES), F32)),
        grid=(T // tm,),
        in_specs=[pl.BlockSpec((tm, D_MODEL), lambda i: (i, 0)), _lspec(P["norm_ffn"], l),
                  _lspec(P["moe_router"], j), _const_spec(ltri)],
        out_specs=(pl.BlockSpec((tm, D_MODEL), lambda i: (i, 0)), pl.BlockSpec((tm, LANES), lambda i: (i, 0)),
                   pl.BlockSpec((1, LANES), lambda i: (0, 0))),
        scratch_shapes=[pltpu.VMEM((1, LANES), F32)],
        compiler_params=_cparams("arbitrary"),
        name="moe_route",
    )(h, P["norm_ffn"], P["moe_router"], ltri)


def _moe_plan(info, cnt, tm, n_tiles_max):
    T = info.shape[0]
    rows = MOE_ROWS
    count = cnt[0, :N_EXPERTS].astype(jnp.int32)
    tiles_e = (count + rows - 1) // rows
    first_tile = jnp.cumsum(tiles_e) - tiles_e
    start = first_tile * rows
    experts = jnp.arange(N_EXPERTS, dtype=jnp.int32)

    def row_of(e, r):
        sel = e.astype(jnp.int32)[:, None] == experts[None, :]
        return jnp.sum(jnp.where(sel, start[None, :], 0), axis=1) + r.astype(jnp.int32)

    pos = jnp.stack([row_of(info[:, 0], info[:, 2]), row_of(info[:, 1], info[:, 3])], axis=0)
    pos = jnp.transpose(pos.reshape(2, T // tm, tm), (1, 0, 2))
    n_tiles = jnp.sum(tiles_e)
    t = jnp.minimum(jnp.arange(n_tiles_max, dtype=jnp.int32), n_tiles - 1)
    tile_expert = jnp.sum((first_tile[None, :] <= t[:, None]).astype(jnp.int32), axis=1) - 1
    last_tile = first_tile + tiles_e - 1
    return pos, tile_expert, n_tiles.reshape(1), last_tile, tiles_e


def _moe_dispatch_kernel(last_ref, tiles_ref, nt_ref, pos_ref, hn_ref, xs_ref, zero_ref, sem):
    tm = hn_ref.shape[0]

    @pl.when(pl.program_id(0) == 0)
    def _():
        zero_ref[...] = jnp.zeros(zero_ref.shape, F32)

        def clear(tile):
            row0 = pl.multiple_of(tile * MOE_ROWS, MOE_ROWS)
            cp = pltpu.make_async_copy(zero_ref, xs_ref.at[pl.ds(row0, MOE_ROWS), :], sem)
            cp.start()
            cp.wait()

        for e in range(N_EXPERTS):
            @pl.when(tiles_ref[e] > 0)
            def _():
                clear(last_ref[e])

        def clear_tail(tile, c):
            clear(tile)
            return c

        lax.fori_loop(nt_ref[0], xs_ref.shape[0] // MOE_ROWS, clear_tail, 0)

    def row_copy(j, slot):
        return pltpu.make_async_copy(hn_ref.at[pl.ds(j, 1), :], xs_ref.at[pl.ds(pos_ref[slot, j], 1), :], sem)

    def issue(j, c):
        row_copy(j, 0).start()
        row_copy(j, 1).start()
        return c

    lax.fori_loop(0, tm, issue, 0, unroll=8)
    for _ in range(2):
        pltpu.make_async_copy(hn_ref, xs_ref.at[pl.ds(0, tm), :], sem).wait()


def _moe_dispatch(hn, pos, last_tile, tiles_e, n_tiles, n_rows):
    T = hn.shape[0]
    tm = pos.shape[2]
    gs = pltpu.PrefetchScalarGridSpec(
        num_scalar_prefetch=3, grid=(T // tm,),
        in_specs=[pl.BlockSpec((None, 2, tm), lambda i, *_: (i, 0, 0), memory_space=pltpu.SMEM),
                  pl.BlockSpec((tm, D_MODEL), lambda i, *_: (i, 0))],
        out_specs=pl.BlockSpec(memory_space=pl.ANY),
        scratch_shapes=[pltpu.VMEM((MOE_ROWS, D_MODEL), F32), pltpu.SemaphoreType.DMA])
    return pl.pallas_call(
        _moe_dispatch_kernel, grid_spec=gs,
        out_shape=jax.ShapeDtypeStruct((n_rows, D_MODEL), F32),
        compiler_params=_cparams("arbitrary"),
        name="moe_dispatch",
    )(last_tile, tiles_e, n_tiles, pos, hn)


def _moe_group_kernel(te_ref, nt_ref, x_ref, w1_ref, w3_ref, w2_ref, o_ref):
    live = pl.program_id(0) < nt_ref[0]

    @pl.when(live)
    def _():
        x = x_ref[...].astype(BF16)
        a = jnp.dot(x, w1_ref[...], preferred_element_type=F32)
        b = jnp.dot(x, w3_ref[...], preferred_element_type=F32)
        o_ref[...] = _bdot(_silu(a) * b, w2_ref[...])

    @pl.when(jnp.logical_not(live))
    def _():
        o_ref[...] = jnp.zeros(o_ref.shape, F32)


def _moe_group(xs, P, j, tile_expert, n_tiles):
    n_rows = xs.shape[0]
    tf = D_FF_TILE
    wmap = lambda i, te, nt: (j, te[i], 0, 0)
    rmap = lambda i, te, nt: (i, 0)
    gs = pltpu.PrefetchScalarGridSpec(
        num_scalar_prefetch=2, grid=(n_rows // MOE_ROWS,),
        in_specs=[pl.BlockSpec((MOE_ROWS, D_MODEL), rmap),
                  pl.BlockSpec((None, None, D_MODEL, tf), wmap),
                  pl.BlockSpec((None, None, D_MODEL, tf), wmap),
                  pl.BlockSpec((None, None, tf, D_MODEL), wmap)],
        out_specs=pl.BlockSpec((MOE_ROWS, D_MODEL), rmap))
    return pl.pallas_call(
        _moe_group_kernel, grid_spec=gs,
        out_shape=jax.ShapeDtypeStruct((n_rows, D_MODEL), F32),
        compiler_params=_cparams("arbitrary"),
        name="moe_group",
    )(tile_expert, n_tiles, xs, P["moe_w1"], P["moe_w3"], P["moe_w2"])


def _moe_combine_kernel(final_norm, pos_ref, o_ref, h_ref, info_ref, gf_ref, out_ref, a_ref, b_ref, sem):
    tm = h_ref.shape[0]

    def row_copy(j, slot, dst):
        return pltpu.make_async_copy(o_ref.at[pl.ds(pos_ref[slot, j], 1), :], dst.at[pl.ds(j, 1), :], sem)

    def issue(j, c):
        row_copy(j, 0, a_ref).start()
        row_copy(j, 1, b_ref).start()
        return c

    lax.fori_loop(0, tm, issue, 0, unroll=8)
    for dst in (a_ref, b_ref):
        pltpu.make_async_copy(o_ref.at[pl.ds(0, tm), :], dst, sem).wait()
    k1, k2 = ROUTE_LANES.index("w1"), ROUTE_LANES.index("w2")
    info = info_ref[...]
    w1, w2 = info[:, k1:k1 + 1], info[:, k2:k2 + 1]
    out = h_ref[...] + (w1 * a_ref[...] + w2 * b_ref[...])
    if final_norm:
        out = _rmsnorm(out, gf_ref[...])
    out_ref[...] = out


def _moe_combine(o, h, info, pos, P, final_norm):
    T = h.shape[0]
    tm = pos.shape[2]
    gfinal = P["norm_final"]
    row_spec = pl.BlockSpec((tm, D_MODEL), lambda i: (i, 0))
    return pl.pallas_call(
        functools.partial(_moe_combine_kernel, final_norm),
        out_shape=jax.ShapeDtypeStruct((T, D_MODEL), F32),
        grid=(T // tm,),
        in_specs=[pl.BlockSpec((None, 2, tm), lambda i: (i, 0, 0), memory_space=pltpu.SMEM),
                  pl.BlockSpec(memory_space=pl.ANY), row_spec,
                  pl.BlockSpec((tm, LANES), lambda i: (i, 0)), _const_spec(gfinal)],
        out_specs=row_spec,
        scratch_shapes=[pltpu.VMEM((tm, D_MODEL), F32), pltpu.VMEM((tm, D_MODEL), F32), pltpu.SemaphoreType.DMA],
        compiler_params=_cparams("arbitrary"),
        name="moe_combine",
    )(pos, o, h, info, gfinal)


def _moe_routed(h, P, l, tm, final_norm):
    T = h.shape[0]
    n_tiles_max = (2 * T) // MOE_ROWS + N_EXPERTS
    if T % (2 * tm) == 0:
        tm = 2 * tm
    hn, info, cnt = _moe_route(h, P, l, tm)
    pos, tile_expert, n_tiles, last_tile, tiles_e = _moe_plan(info, cnt, tm, n_tiles_max)
    xs = _moe_dispatch(hn, pos, last_tile, tiles_e, n_tiles, n_tiles_max * MOE_ROWS)
    o = _moe_group(xs, P, l // 2, tile_expert, n_tiles)
    return _moe_combine(o, h, info, pos, P, final_norm)


def _conv_step(x, buf_ref, buf_o, w_ref, b_ref):
    out = b_ref[...]
    for k in range(CONV_K - 1):
        out = out + buf_ref[k] * w_ref[k:k + 1, :]
        if k > 0:
            buf_o[k - 1] = buf_ref[k]
    out = out + x * w_ref[CONV_K - 1:CONV_K, :]
    buf_o[CONV_K - 2] = x
    return out


def _step_pre_kernel(pos_cos_ref, pos_sin_ref, gam_ref, ehn_ref, z_ref, sbuf_ref, lbuf_ref, lst_ref, s5r_ref, s5i_ref,
                     scw_ref, scb_ref, dtb_ref, alog_ref,
                     lcw_ref, lcb_ref, wg_ref, bg_ref, lam_ref,
                     wbr_ref, wbi_ref, wcr_ref, wci_ref, pr_ref, pi_ref, s5d_ref, wglu_ref, qr_ref, qi_ref,
                     kqv_ref, dec_ref, sbuf_o, lbuf_o, lst_o, s5r_o, s5i_o, ys5_o, ylru_o, xs_o,
                     xr_ref, xi_ref):
    zz = z_ref[...].astype(F32)
    xbc = zz[:, Z_SSD + MIX:Z_SSD + MIX + SSD_CONV]
    dt_raw = zz[:, Z_SSD + MIX + SSD_CONV:Z_SSD + MIX + SSD_CONV + LANES]
    xc = _silu(_conv_step(xbc, sbuf_ref, sbuf_o, scw_ref, scb_ref))
    xs = xc[:, 0:MIX]
    xs_o[...] = xs
    nbc = SSD_GROUPS * SSD_STATE
    bm = xc[:, MIX:MIX + nbc]
    cm = xc[:, MIX + nbc:MIX + 2 * nbc]
    dt = jax.nn.softplus(dt_raw + dtb_ref[...])
    a = -jnp.exp(alog_ref[...])
    rep = SSD_HEADS // SSD_GROUPS
    kqv_ref[0, 0] = (_group_repeat_lanes(bm, rep) * _split3_dot(dt, ehn_ref[...])).T
    kqv_ref[0, 1] = _group_repeat_lanes(cm, rep).T
    kqv_ref[0, 2] = xs.T
    dec_ref[0] = jnp.exp(dt * a).T[0:SSD_HEADS, :]
    q = _rotary_lanes(zz[:, Z_RET:Z_RET + MIX], pos_cos_ref[...], pos_sin_ref[...]) * (RET_HD ** -0.5)
    k = _rotary_lanes(zz[:, Z_RET + MIX:Z_RET + 2 * MIX], pos_cos_ref[...], pos_sin_ref[...])
    kqv_ref[1, 0] = k.T
    kqv_ref[1, 1] = q.T
    kqv_ref[1, 2] = zz[:, Z_RET + 2 * MIX:Z_RET + 3 * MIX].T
    dec_ref[1] = gam_ref[...]
    gate = zz[:, Z_LRU:Z_LRU + MIX]
    lx = zz[:, Z_LRU + MIX:Z_LRU + 2 * MIX]
    lconv = _conv_step(lx, lbuf_ref, lbuf_o, lcw_ref, lcb_ref)
    la, lbx = _lru_gates(lconv, wg_ref, bg_ref, lam_ref)
    hl = lbx + la * lst_ref[...]
    lst_o[...] = hl
    ylru_o[...] = hl * jax.nn.gelu(gate)
    u = zz[:, Z_S5:Z_S5 + MIX]
    _s5_project_in(u, wbr_ref, wbi_ref, xr_ref, xi_ref, 1)
    lr, li = pr_ref[0:1, :], pi_ref[0:1, :]
    s0r, s0i = s5r_ref[...], s5i_ref[...]
    xr = xr_ref[...] + (lr * s0r - li * s0i)
    xi = xi_ref[...] + (lr * s0i + li * s0r)
    s5r_o[...] = xr
    s5i_o[...] = xi
    ys5_o[...] = _s5_project_out(xr, xi, u, wcr_ref, wci_ref, s5d_ref, wglu_ref)


def _step_state_kernel(kqv_ref, dec_ref, st_ref, o_st_ref, y_ref):
    h = pl.program_id(0)
    d = dec_ref[pl.ds(h, 1), :]
    v = kqv_ref[2]
    acc = jnp.zeros(v.shape, F32)
    for n in range(st_ref.shape[0]):
        s_new = d * st_ref[n] + kqv_ref[0, n:n + 1, :] * v
        o_st_ref[n] = s_new
        acc = acc + kqv_ref[1, n:n + 1, :] * s_new
    y_ref[...] = acc


def _step_post_kernel(yssd_ref, yret_ref, xs_ref, z_ref, dlane_ref, ng_ref, gn_ref, yssd_o, yret_o):
    zz_gate = z_ref[:, Z_SSD:Z_SSD + MIX]
    y = yssd_ref[...].T + dlane_ref[...] * xs_ref[...]
    y = y * _silu(zz_gate)
    yssd_o[...] = _rmsnorm(y, ng_ref[...])
    rgate = z_ref[:, Z_RET + 3 * MIX:Z_RET + 4 * MIX]
    yret_o[...] = _silu(rgate) * (_group_norm_lanes(yret_ref[...].T, RET_HD) * gn_ref[...])


def _sample_mixers(z, views, big, layer, P, pos):
    Bs = z.shape[0]
    H = SSD_HEADS
    cos_l, sin_l = _rope_tables(np.asarray([pos]))
    gam = jnp.asarray(np.repeat(_ret_gammas()[:, None], Bs, axis=1), F32)
    consts = [cos_l, sin_l, gam, _head_expand(SSD_HEADS, SSD_HD)]
    states = [views[k] for k in ("ssd_conv", "lru_conv", "lru", "s5_re", "s5_im")]
    params = [P[k] for k in ("ssd_cw", "ssd_cb", "ssd_dtb", "ssd_alog",
                             "lru_cw", "lru_cb", "lru_wg", "lru_bg", "lru_lam") + S5_PARAMS]
    pre_out = (jax.ShapeDtypeStruct((2, 3, MIX, Bs), F32),
               jax.ShapeDtypeStruct((2, H, Bs), F32),
               jax.ShapeDtypeStruct(views["ssd_conv"].shape[1:], F32),
               jax.ShapeDtypeStruct(views["lru_conv"].shape[1:], F32),
               jax.ShapeDtypeStruct((Bs, MIX), F32),
               jax.ShapeDtypeStruct((Bs, S5_CH), F32), jax.ShapeDtypeStruct((Bs, S5_CH), F32),
               jax.ShapeDtypeStruct((Bs, MIX), F32), jax.ShapeDtypeStruct((Bs, MIX), F32),
               jax.ShapeDtypeStruct((Bs, MIX), F32))
    (kqv, dec, sbuf_n, lbuf_n, lst_n, s5r_n, s5i_n, y_s5, y_lru, xs) = pl.pallas_call(
        _step_pre_kernel,
        out_shape=pre_out,
        grid=(1,),
        in_specs=[_const_spec(a) for a in consts] + [_const_spec(z)]
                 + [_lspec(a, layer) for a in states] + [_lspec(a, layer) for a in params],
        out_specs=tuple(pl.BlockSpec(o.shape, lambda i, n=len(o.shape): (0,) * n) for o in pre_out),
        scratch_shapes=[pltpu.VMEM((Bs, S5_CH), F32), pltpu.VMEM((Bs, S5_CH), F32)],
        compiler_params=_cparams("arbitrary"),
        name="step_pre",
    )(*consts, z, *states, *params)

    new_big, yts = {}, []
    for m, name in enumerate(("ssd", "ret")):
        s_new, y_t = pl.pallas_call(
            _step_state_kernel,
            out_shape=(jax.ShapeDtypeStruct(big[name].shape, F32), jax.ShapeDtypeStruct((MIX, Bs), F32)),
            grid=(H,),
            in_specs=[pl.BlockSpec((None, 3, SSD_STATE, Bs), lambda h, m=m: (m, 0, h, 0)),
                      pl.BlockSpec((None, H, Bs), lambda h, m=m: (m, 0, 0)),
                      pl.BlockSpec((None, None, SSD_STATE, SSD_HD, Bs), lambda h: (layer, h, 0, 0, 0))],
            out_specs=(pl.BlockSpec((None, None, SSD_STATE, SSD_HD, Bs), lambda h: (layer, h, 0, 0, 0)),
                       pl.BlockSpec((SSD_HD, Bs), lambda h: (h, 0))),
            input_output_aliases={2: 0},
            compiler_params=_cparams("parallel"),
            name="step_state_" + name,
        )(kqv, dec, big[name])
        new_big[name] = s_new
        yts.append(y_t)

    post_in = [yts[0], yts[1], xs, z]
    post_par = [P[k] for k in ("ssd_dlane", "ssd_norm", "ret_gn")]
    post_out = (jax.ShapeDtypeStruct((Bs, MIX), F32), jax.ShapeDtypeStruct((Bs, MIX), F32))
    y_ssd, y_ret = pl.pallas_call(
        _step_post_kernel,
        out_shape=post_out,
        grid=(1,),
        in_specs=[_const_spec(a) for a in post_in] + [_lspec(a, layer) for a in post_par],
        out_specs=tuple(pl.BlockSpec(o.shape, lambda i: (0, 0)) for o in post_out),
        compiler_params=_cparams("arbitrary"),
        name="step_post",
    )(*post_in, *post_par)

    new = dict(ssd_conv=sbuf_n, lru_conv=lbuf_n, lru=lst_n, s5_re=s5r_n, s5_im=s5i_n)
    return (y_ssd, y_s5, y_lru, y_ret), new, new_big


def _block_diag8(w):
    lead = w.shape[:-3]
    n, r, c = w.shape[-3:]
    eye = jnp.eye(n, dtype=w.dtype)
    out = w[..., :, :, None, :] * eye[:, None, :, None]
    return out.reshape(lead + (n * r, n * c))


def _row(v):
    return v[:, None, :]


def _cmul(ar, ai, br, bi):
    return ar * br - ai * bi, ar * bi + ai * br


def _prep_params(W):
    depth = W["w_in"].shape[0]
    P = {}
    w_t = jnp.transpose(W["w_in"], (0, 2, 1))
    P["w_in"] = jnp.concatenate(
        [w_t[:, 4872:8968], w_t[:, 2824:4872], w_t[:, 1800:2824], w_t[:, 1288:1800], w_t[:, 0:1288],
         jnp.zeros((depth, Z_WIDTH - 8968, D_MODEL), w_t.dtype)], axis=1).astype(BF16)
    P["norm_mix"] = _row(W["norm_mix"])
    pad = ((0, 0), (0, LANES - SSD_HEADS))
    P["ssd_cw"] = W["ssd_conv_w"]
    P["ssd_cb"] = _row(W["ssd_conv_b"])
    P["ssd_dtb"] = _row(jnp.pad(W["ssd_dt_bias"], pad))
    P["ssd_alog"] = _row(jnp.pad(W["ssd_a_log"], pad))
    P["ssd_dlane"] = _row(jnp.repeat(W["ssd_d"], SSD_HD, axis=1))
    P["ssd_norm"] = _row(W["ssd_norm"])
    lr, li = W["s5_lambda_re"], W["s5_lambda_im"]
    dt = jnp.exp(W["s5_log_dt"])[:, :, None]
    mag = jnp.exp(lr * dt)
    br, bi = mag * jnp.cos(li * dt), mag * jnp.sin(li * dt)
    den = lr * lr + li * li
    qr, qi = _cmul(br - 1.0, bi, lr / den, -li / den)
    wr, wi = _cmul(qr[..., None], qi[..., None], W["s5_b_re"], W["s5_b_im"])
    lag_r, lag_i = [wr], [wi]
    for _ in range(S5_LAGS - 1):
        nr, ni = _cmul(lag_r[-1], lag_i[-1], br[..., None], bi[..., None])
        lag_r.append(nr)
        lag_i.append(ni)
    gb = S5_GROUPS // S5_LB

    def embed_in(lags):
        m = jnp.stack([x.reshape(depth, S5_LB, gb, S5_STATE, S5_GDIM) for x in lags], axis=2)
        m = _block_diag8(jnp.swapaxes(m, -1, -2))
        return m.reshape(depth, S5_LB, len(lags) * gb * S5_GDIM, gb * S5_STATE)

    def embed_out(m):
        m = m.reshape(depth, S5_LB, gb, S5_GDIM, S5_STATE)
        return _block_diag8(jnp.swapaxes(m, -1, -2))

    P["s5_wbr"] = embed_in(lag_r).astype(BF16)
    P["s5_wbi"] = embed_in(lag_i).astype(BF16)
    P["s5_wcr"] = embed_out(W["s5_c_re"]).astype(BF16)
    P["s5_wci"] = embed_out(W["s5_c_im"]).astype(BF16)
    pr, pi = [br.reshape(depth, 1, S5_CH)], [bi.reshape(depth, 1, S5_CH)]
    for _ in range(SUBLANES - 1):
        nr, ni = _cmul(pr[-1], pi[-1], pr[0], pi[0])
        pr.append(nr)
        pi.append(ni)
    P["s5_pr"] = jnp.concatenate(pr, axis=1)
    P["s5_pi"] = jnp.concatenate(pi, axis=1)
    half = jnp.zeros((depth, S5_LAGS, S5_CH), F32)
    P["s5_qr"] = jnp.concatenate([half, P["s5_pr"][:, :SUBLANES - S5_LAGS]], axis=1)
    P["s5_qi"] = jnp.concatenate([half, P["s5_pi"][:, :SUBLANES - S5_LAGS]], axis=1)
    P["s5_d"] = W["s5_d"].reshape(depth, 1, MIX)
    P["s5_glu"] = W["s5_glu"].astype(BF16)
    P["lru_cw"] = W["lru_conv_w"]
    P["lru_cb"] = _row(W["lru_conv_b"])
    P["lru_wg"] = jnp.concatenate([_block_diag8(W["lru_wa"]), _block_diag8(W["lru_wx"])], axis=2).astype(BF16)
    P["lru_bg"] = _row(jnp.concatenate([W["lru_ba"], W["lru_bx"]], axis=1))
    P["lru_lam"] = _row(W["lru_lambda"])
    P["ret_gn"] = _row(W["ret_gn"])
    P["w_branch"] = W["w_branch"].astype(BF16)
    P["w_out"] = W["w_out"].astype(BF16)
    P["norm_ffn"] = _row(W["norm_ffn"])
    P["norm_final"] = W["norm_final"].reshape(1, D_MODEL)
    for k in ("ffn_w1", "ffn_w3", "ffn_w2", "moe_w1", "moe_w3", "moe_w2"):
        P[k] = W[k].astype(BF16)
    P["moe_router"] = jnp.pad(W["moe_router"], ((0, 0), (0, 0), (0, LANES - N_EXPERTS)))
    return P


def _trunk_prompt(x, P):
    B, L, _ = x.shape
    T = B * L
    depth = P["w_in"].shape[0]
    h = x.reshape(T, D_MODEL)
    new = {k: [] for k in ("ssd", "ssd_conv", "s5_re", "s5_im", "lru", "lru_conv", "ret")}
    for l in range(depth):
        z = _inproj(h, P, l, min(1024, T), BF16)
        z3 = z.reshape(B, L, Z_WIDTH)
        y_ssd, s_ssd, buf_ssd = _ssd_prompt(z3, P, l)
        y_s5, s5r, s5i = _s5_prompt(z3, P, l)
        y_lru, s_lru, buf_lru = _lru_prompt(z3, P, l)
        y_ret, s_ret = _ret_prompt(z3, P, l)
        ys = tuple(y.reshape(T, MIX) for y in (y_ssd, y_s5, y_lru, y_ret))
        h = _merge(ys, z, h, P, l, min(512, T))
        mixer = _moe_routed if (l % 2 == 1 and T % MOE_ROWS == 0) else _ffn
        h = mixer(h, P, l, min(512, T), final_norm=(l == depth - 1))
        new["ssd"].append(s_ssd)
        new["ssd_conv"].append(buf_ssd)
        new["s5_re"].append(s5r.reshape(B, S5_GROUPS, S5_STATE))
        new["s5_im"].append(s5i.reshape(B, S5_GROUPS, S5_STATE))
        new["lru"].append(s_lru.reshape(B, MIX))
        new["lru_conv"].append(buf_lru)
        new["ret"].append(s_ret)
    return h.reshape(B, L, D_MODEL), {k: jnp.stack(v) for k, v in new.items()}


def _trunk_sample(x, pos, st, P):
    Bs = x.shape[0]
    depth = P["w_in"].shape[0]
    h = x.reshape(Bs, D_MODEL)
    big = dict(ssd=jnp.transpose(st["ssd"], (0, 2, 3, 4, 1)), ret=jnp.transpose(st["ret"], (0, 2, 3, 4, 1)))
    views = dict(ssd_conv=jnp.transpose(st["ssd_conv"], (0, 2, 1, 3)),
                 lru_conv=jnp.transpose(st["lru_conv"], (0, 2, 1, 3)),
                 lru=st["lru"],
                 s5_re=st["s5_re"].reshape(depth, Bs, S5_CH),
                 s5_im=st["s5_im"].reshape(depth, Bs, S5_CH))
    new = {k: [] for k in views}
    for l in range(depth):
        z = _inproj(h, P, l, Bs, F32)
        ys, nl, big = _sample_mixers(z, views, big, l, P, pos)
        h = _merge(ys, z, h, P, l, Bs)
        h = _ffn(h, P, l, Bs, final_norm=(l == depth - 1))
        for k in new:
            new[k].append(nl[k])
    out = {k: jnp.stack(v) for k, v in new.items()}
    out["ssd_conv"] = jnp.transpose(out["ssd_conv"], (0, 2, 1, 3))
    out["lru_conv"] = jnp.transpose(out["lru_conv"], (0, 2, 1, 3))
    out["s5_re"] = out["s5_re"].reshape(st["s5_re"].shape)
    out["s5_im"] = out["s5_im"].reshape(st["s5_im"].shape)
    out["ssd"] = jnp.transpose(big["ssd"], (0, 4, 1, 2, 3))
    out["ret"] = jnp.transpose(big["ret"], (0, 4, 1, 2, 3))
    return h.reshape(Bs, 1, D_MODEL), out


def kernel(x_prompt, x_sample, state_ssd, state_ssd_conv, state_s5_re, state_s5_im, state_lru, state_lru_conv, state_ret, norm_mix, w_in, ssd_conv_w, ssd_conv_b, ssd_dt_bias, ssd_a_log, ssd_d, ssd_norm, s5_lambda_re, s5_lambda_im, s5_b_re, s5_b_im, s5_c_re, s5_c_im, s5_d, s5_log_dt, s5_glu, lru_conv_w, lru_conv_b, lru_wa, lru_ba, lru_wx, lru_bx, lru_lambda, ret_gn, w_branch, w_out, norm_ffn, ffn_w1, ffn_w3, ffn_w2, moe_router, moe_w1, moe_w3, moe_w2, norm_final):
    W = dict(norm_mix=norm_mix, w_in=w_in, ssd_conv_w=ssd_conv_w, ssd_conv_b=ssd_conv_b, ssd_dt_bias=ssd_dt_bias,
             ssd_a_log=ssd_a_log, ssd_d=ssd_d, ssd_norm=ssd_norm, s5_lambda_re=s5_lambda_re,
             s5_lambda_im=s5_lambda_im, s5_b_re=s5_b_re, s5_b_im=s5_b_im, s5_c_re=s5_c_re, s5_c_im=s5_c_im,
             s5_d=s5_d, s5_log_dt=s5_log_dt, s5_glu=s5_glu, lru_conv_w=lru_conv_w, lru_conv_b=lru_conv_b,
             lru_wa=lru_wa, lru_ba=lru_ba, lru_wx=lru_wx, lru_bx=lru_bx, lru_lambda=lru_lambda, ret_gn=ret_gn,
             w_branch=w_branch, w_out=w_out, norm_ffn=norm_ffn, moe_router=moe_router, norm_final=norm_final,
             ffn_w1=ffn_w1, ffn_w3=ffn_w3, ffn_w2=ffn_w2, moe_w1=moe_w1, moe_w3=moe_w3, moe_w2=moe_w2)
    P = _prep_params(W)
    y_p, sp = _trunk_prompt(x_prompt, P)
    st = dict(ssd=state_ssd, ssd_conv=state_ssd_conv, s5_re=state_s5_re, s5_im=state_s5_im,
              lru=state_lru, lru_conv=state_lru_conv, ret=state_ret)
    past_len = 16384
    y_s, ss = _trunk_sample(x_sample, past_len, st, P)
    names = ("ssd", "ssd_conv", "s5_re", "s5_im", "lru", "lru_conv", "ret")
    return (y_p, y_s) + tuple(sp[n] for n in names) + tuple(ss[n] for n in names)
```

```python
import functools
import math

import jax
import jax.numpy as jnp
import numpy as np
from jax import lax
from jax.experimental import pallas as pl
from jax.experimental.pallas import tpu as pltpu

F32 = jnp.float32
BF16 = jnp.bfloat16
EPS = 1e-6

D_MODEL = 1024
MIX = 512
CONV_K = 4
CHUNK = 128
SSD_HEADS = 8
SSD_HD = 64
SSD_STATE = 64
SSD_GROUPS = 2
SSD_CONV = MIX + 2 * SSD_GROUPS * SSD_STATE
S5_GROUPS = 32
S5_GDIM = 16
S5_STATE = 64
S5_CH = S5_GROUPS * S5_STATE
LRU_BLOCKS = 8
LRU_C = 8.0
RET_HEADS = 8
RET_HD = 64
ROPE_BASE = 10000.0
N_EXPERTS = 8
D_FF_TILE = 1408

Z_MERGE = 0
Z_RET = 4096
Z_LRU = 6144
Z_S5 = 7168
Z_SSD = 7680
Z_WIDTH = 9216

VMEM_LIMIT = 56 * 1024 * 1024
LANES = 128
SUBLANES = 8


def _cparams(*sem):
    return pltpu.CompilerParams(dimension_semantics=sem, vmem_limit_bytes=VMEM_LIMIT)


def _bdot(a, b):
    return jnp.dot(a.astype(BF16), b.astype(BF16), preferred_element_type=F32)


def _bdot_nt(a, b):
    return lax.dot_general(a.astype(BF16), b.astype(BF16), (((1,), (1,)), ((), ())), preferred_element_type=F32)


def _bdot_tn(a, b):
    return lax.dot_general(a.astype(BF16), b.astype(BF16), (((0,), (0,)), ((), ())), preferred_element_type=F32)


def _hdot(a, b):
    return jnp.dot(a, b, precision=lax.Precision.HIGHEST, preferred_element_type=F32)


def _split3_dot(x, m01):
    hi = x.astype(BF16)
    r1 = x - hi.astype(F32)
    mid = r1.astype(BF16)
    lo = (r1 - mid.astype(F32)).astype(BF16)
    m = m01.astype(BF16)
    d = functools.partial(jnp.dot, preferred_element_type=F32)
    return (d(lo, m) + d(mid, m)) + d(hi, m)


def _rmsnorm(x, g):
    ms = jnp.mean(x * x, axis=-1, keepdims=True)
    return x * lax.rsqrt(ms + EPS) * g


def _silu(x):
    return x * jax.nn.sigmoid(x)


def _neg_expm1_2x(log_a, a):
    return jnp.tanh(-log_a) * (1.0 + a * a)


def _inproj_kernel(x_ref, g_ref, w_ref, o_ref, hn_ref):
    @pl.when(pl.program_id(1) == 0)
    def _():
        hn_ref[...] = _rmsnorm(x_ref[...], g_ref[...]).astype(BF16)

    z = lax.dot_general(hn_ref[...], w_ref[...], (((1,), (1,)), ((), ())), preferred_element_type=F32)
    o_ref[...] = z.astype(o_ref.dtype)


def _lspec(a, l):
    rest = tuple(a.shape[1:])
    return pl.BlockSpec((None,) + rest, lambda *_: (l,) + (0,) * len(rest))


def _inproj(x, P, l, tm, out_dtype):
    T = x.shape[0]
    tn = 2304
    return pl.pallas_call(
        _inproj_kernel,
        out_shape=jax.ShapeDtypeStruct((T, Z_WIDTH), out_dtype),
        grid=(T // tm, Z_WIDTH // tn),
        in_specs=[pl.BlockSpec((tm, D_MODEL), lambda i, j: (i, 0)),
                  _lspec(P["norm_mix"], l),
                  pl.BlockSpec((None, tn, D_MODEL), lambda i, j: (l, j, 0))],
        out_specs=pl.BlockSpec((tm, tn), lambda i, j: (i, j)),
        scratch_shapes=[pltpu.VMEM((tm, D_MODEL), BF16)],
        compiler_params=_cparams("parallel", "arbitrary"),
        name="inproj",
    )(x, P["norm_mix"], P["w_in"])


SUBCHUNKS = 4
SSD_SUBCHUNKS = 2


def _sub_chunks(z_ref):
    n = z_ref.shape[0] // CHUNK
    return [(s, n) for s in range(n)]


def _when_first(sub, fn):
    if sub[0] == 0:
        pl.when(pl.program_id(1) == 0)(fn)


def _when_last(sub, fn):
    if sub[0] == sub[1] - 1:
        pl.when(pl.program_id(1) == pl.num_programs(1) - 1)(fn)


def _sub_rows(ref, s):
    return ref.at[pl.ds(s * CHUNK, CHUNK), :]


def _conv_chunk(s, x, pad_ref, w_ref, b_ref):
    Lc = x.shape[0]

    def _():
        pad_ref[0:8, :] = jnp.zeros((8, x.shape[1]), F32)

    _when_first(s, _)
    pad_ref[8:8 + Lc, :] = x
    out = b_ref[...] + pad_ref[5:5 + Lc, :] * w_ref[0:1, :]
    out = out + pad_ref[6:6 + Lc, :] * w_ref[1:2, :]
    out = out + pad_ref[7:7 + Lc, :] * w_ref[2:3, :]
    out = out + x * w_ref[3:4, :]
    return out


def _conv_finish(pad_ref, Lc):
    pad_ref[0:8, :] = pad_ref[Lc:Lc + 8, :]


def _head_masks(nh, hd, L):
    rows = (np.arange(nh * L)[:, None] // L) == (np.arange(nh * hd)[None, :] // hd)
    diag = (np.arange(nh * hd)[:, None] // hd) == (np.arange(nh * hd)[None, :] // hd)
    return jnp.asarray(rows, BF16), jnp.asarray(diag, F32)


def _head_block_rows(x, nh, mask):
    return jnp.concatenate([x.astype(BF16)] * nh, axis=0) * mask


def _pow2_div(x, d):
    return lax.shift_right_logical(x, jnp.int32(int(math.log2(d))))


def _head_block_rows_sel(x, nh, hd):
    L = x.shape[0]
    xt = jnp.concatenate([x.astype(BF16)] * nh, axis=0)
    row = lax.broadcasted_iota(jnp.int32, xt.shape, 0)
    col = lax.broadcasted_iota(jnp.int32, xt.shape, 1)
    return jnp.where(_pow2_div(row, L) == _pow2_div(col, hd), xt, jnp.zeros_like(xt))


def _head_block_cols(xt, nh, hd):
    L = xt.shape[1]
    xc = jnp.concatenate([xt.astype(BF16)] * nh, axis=1)
    row = lax.broadcasted_iota(jnp.int32, xc.shape, 0)
    col = lax.broadcasted_iota(jnp.int32, xc.shape, 1)
    return jnp.where(_pow2_div(row, hd) == _pow2_div(col, L), xc, jnp.zeros_like(xc))


def _group_repeat_lanes(m, rep):
    lane = lax.broadcasted_iota(jnp.int32, m.shape, 1)
    swapped = pltpu.roll(m, m.shape[1] // 2, 1)
    low = lane < m.shape[1] // 2
    g0 = jnp.where(low, m, swapped)
    g1 = jnp.where(low, swapped, m)
    return jnp.concatenate([g0] * (rep // 2) + [g1] * (rep // 2), axis=1)


def _ssd_kernel(z_ref, *refs):
    y_ref = refs[-5]
    for sub in _sub_chunks(z_ref):
        _ssd_chunk(sub, _sub_rows(z_ref, sub[0]), *refs[:-5], _sub_rows(y_ref, sub[0]), *refs[-4:])


def _ssd_chunk(s, z_ref, cw_ref, cb_ref, dtb_ref, alog_ref, dlane_ref, ng_ref, tri_ref, ehj_ref, ehn_ref,
               cbias_ref, hrows_ref, hdiag_ref, y_ref, sto_ref, buf_ref, pad_ref, st_ref):
    Lc = z_ref.shape[0]
    zz = z_ref[...].astype(F32)
    zgate = zz[:, 0:MIX]
    xbc = zz[:, MIX:MIX + SSD_CONV]
    dt_raw = zz[:, MIX + SSD_CONV:MIX + SSD_CONV + LANES]

    def _():
        st_ref[...] = jnp.zeros(st_ref.shape, F32)

    _when_first(s, _)
    conv = _conv_chunk(s, xbc, pad_ref, cw_ref, cb_ref)
    xc = _silu(conv)
    xs = xc[:, 0:MIX]
    nbc = SSD_GROUPS * SSD_STATE
    bm = xc[:, MIX:MIX + nbc]
    cm = xc[:, MIX + nbc:MIX + 2 * nbc]
    dt = jax.nn.softplus(dt_raw + dtb_ref[...])
    a = -jnp.exp(alog_ref[...])
    ld = dt * a
    acum = _hdot(tri_ref[...], ld)
    acum_t = acum.T
    dt_t = dt.T
    a_row = jnp.concatenate([acum_t[h:h + 1, :] for h in range(SSD_HEADS)], axis=1)
    dt_row = jnp.concatenate([dt_t[h:h + 1, :] for h in range(SSD_HEADS)], axis=1)
    a_col = _split3_dot(acum, ehj_ref[...])
    decay = jnp.exp((a_col - a_row) + cbias_ref[...])
    rep = SSD_HEADS // SSD_GROUPS
    gmats = []
    for g in range(SSD_GROUPS):
        cg = cm[:, g * SSD_STATE:(g + 1) * SSD_STATE]
        bg = bm[:, g * SSD_STATE:(g + 1) * SSD_STATE]
        gmats.append(_bdot_nt(cg, bg))
    g_all = jnp.concatenate([gmats[h // rep] for h in range(SSD_HEADS)], axis=1)
    m_all = g_all * decay * dt_row
    y = _bdot(m_all, _head_block_rows(xs, SSD_HEADS, hrows_ref[...]))
    exp_a = jnp.exp(acum)
    exp_a_l = _split3_dot(exp_a, ehn_ref[...])
    c_rep = _group_repeat_lanes(cm, rep)
    y = y + _bdot_nt(c_rep * exp_a_l, st_ref[...])
    a_last = acum[Lc - 1:Lc, :]
    w_end_l = _split3_dot(jnp.exp(a_last - acum) * dt, ehn_ref[...])
    upd = _bdot_tn(xs, _group_repeat_lanes(bm, rep) * w_end_l)
    st_ref[...] = exp_a_l[Lc - 1:Lc, :] * st_ref[...] + upd * hdiag_ref[...]
    y = y + dlane_ref[...] * xs
    y = y * _silu(zgate)
    y_ref[...] = _rmsnorm(y, ng_ref[...]).astype(y_ref.dtype)

    def _():
        buf_ref[...] = pad_ref[Lc + 8 - (CONV_K - 1):Lc + 8, :]
        s_t = st_ref[...].T
        for h in range(SSD_HEADS):
            sl = slice(h * SSD_HD, (h + 1) * SSD_HD)
            sto_ref[h] = s_t[sl, sl]

    _when_last(s, _)
    _conv_finish(pad_ref, Lc)


def _const_spec(a):
    return pl.BlockSpec(a.shape, lambda *_: (0,) * a.ndim)


def _ssd_prompt(z3, P, l):
    B, L, _ = z3.shape
    Lc = CHUNK
    assert L % (SUBCHUNKS * Lc) == 0 and L % (SSD_SUBCHUNKS * Lc) == 0, "sequence length must fill whole grid steps"
    params = [P[k] for k in ("ssd_cw", "ssd_cb", "ssd_dtb", "ssd_alog", "ssd_dlane", "ssd_norm")]
    j = np.arange(SSD_HEADS * Lc) % Lc
    cbias = jnp.asarray(np.where(np.arange(Lc)[:, None] >= j[None, :], 0.0, -1e30), F32)
    consts = [_tri(Lc), _head_expand(SSD_HEADS, Lc), _head_expand(SSD_HEADS, SSD_HD), cbias,
              *_head_masks(SSD_HEADS, SSD_HD, Lc)]
    return pl.pallas_call(
        _ssd_kernel,
        out_shape=(jax.ShapeDtypeStruct((B, L, MIX), BF16),
                   jax.ShapeDtypeStruct((B, SSD_HEADS, SSD_STATE, SSD_HD), F32),
                   jax.ShapeDtypeStruct((B, CONV_K - 1, SSD_CONV), F32)),
        grid=(B, L // (SSD_SUBCHUNKS * Lc)),
        in_specs=[pl.BlockSpec((None, SSD_SUBCHUNKS * Lc, 1536), lambda b, c: (b, c, Z_SSD // 1536))]
                 + [_lspec(a, l) for a in params] + [_const_spec(a) for a in consts],
        out_specs=(pl.BlockSpec((None, SSD_SUBCHUNKS * Lc, MIX), lambda b, c: (b, c, 0)),
                   pl.BlockSpec((None, SSD_HEADS, SSD_STATE, SSD_HD), lambda b, c: (b, 0, 0, 0)),
                   pl.BlockSpec((None, CONV_K - 1, SSD_CONV), lambda b, c: (b, 0, 0))),
        scratch_shapes=[pltpu.VMEM((Lc + 8, SSD_CONV), F32), pltpu.VMEM((MIX, MIX), F32)],
        compiler_params=_cparams("parallel", "arbitrary"),
        name="ssd_prompt",
    )(z3, *params, *consts)


def _tri(Lc):
    return jnp.asarray(np.tril(np.ones((Lc, Lc), np.float32)))


def _head_expand(nh, width):
    e = np.zeros((LANES, nh * width), np.float32)
    for h in range(nh):
        e[h, h * width:(h + 1) * width] = 1.0
    return jnp.asarray(e, BF16)


def _ret_gammas():
    return 1.0 - np.exp2(-5.0 - np.arange(RET_HEADS, dtype=np.float64))


def _ret_tables(Lc):
    gam = _ret_gammas()
    i = np.arange(Lc)
    d = i[:, None] - i[None, :]
    decay = np.where(d >= 0, gam[:, None, None] ** np.maximum(d, 0)[None], 0.0)
    decay_l = np.transpose(decay, (1, 0, 2)).reshape(Lc, RET_HEADS * Lc)
    grow_l = np.repeat(gam[None, :] ** (i[:, None] + 1), RET_HD, axis=1)
    toend_t = np.repeat(gam[:, None] ** (Lc - 1 - i[None, :]), RET_HD, axis=0)
    hd = np.arange(MIX) // RET_HD
    state_decay = np.where(hd[:, None] == hd[None, :], (gam ** Lc)[hd][:, None], 0.0)
    return tuple(jnp.asarray(t, F32) for t in (decay_l, grow_l, toend_t, state_decay))


def _rope_tables(pos):
    half = RET_HD // 2
    inv = ROPE_BASE ** (-np.arange(half, dtype=np.float64) / half)
    ang = np.asarray(pos, np.float64)[:, None] * inv[None, :]
    cos = np.cos(ang)
    sin = np.sin(ang)
    cos_l = np.tile(np.concatenate([cos, cos], axis=1), (1, RET_HEADS))
    sin_l = np.tile(np.concatenate([-sin, sin], axis=1), (1, RET_HEADS))
    return jnp.asarray(cos_l, F32), jnp.asarray(sin_l, F32)


def _rotary_lanes(x, cos_l, sin_l):
    lane = lax.broadcasted_iota(jnp.int32, x.shape, 1)
    first = (lane & (RET_HD - 1)) < (RET_HD // 2)
    n = x.shape[1]
    swapped = jnp.where(first, pltpu.roll(x, n - RET_HD // 2, 1), pltpu.roll(x, RET_HD // 2, 1))
    return x * cos_l + swapped * sin_l


def _group_norm_head(o):
    mu = jnp.mean(o, axis=-1, keepdims=True)
    d = o - mu
    var = jnp.mean(d * d, axis=-1, keepdims=True)
    return d * lax.rsqrt(var + 1e-5)


def _group_norm_lanes(o, hd):
    cols = []
    for cb in range(o.shape[1] // LANES):
        x = o[:, cb * LANES:(cb + 1) * LANES]
        low = lax.broadcasted_iota(jnp.int32, x.shape, 1) < hd

        def seg_mean(t):
            lo = jnp.sum(jnp.where(low, t, 0.0), axis=1, keepdims=True)
            hi = jnp.sum(jnp.where(low, 0.0, t), axis=1, keepdims=True)
            return jnp.where(low, lo, hi) * (1.0 / hd)

        d = x - seg_mean(x)
        cols.append(d * lax.rsqrt(seg_mean(d * d) + 1e-5))
    return jnp.concatenate(cols, axis=1)


def _ret_kernel(z_ref, cos_ref, sin_ref, *refs):
    y_ref = refs[-3]
    for sub in _sub_chunks(z_ref):
        s = sub[0]
        _ret_chunk(sub, _sub_rows(z_ref, s), _sub_rows(cos_ref, s), _sub_rows(sin_ref, s), *refs[:-3],
                   _sub_rows(y_ref, s), *refs[-2:])


def _ret_chunk(s, z_ref, cos_ref, sin_ref, dec_ref, grow_ref, toend_ref, sdec_ref, hdiag_ref,
               gn_ref, y_ref, sto_ref, st_ref):
    zz = z_ref[...].astype(F32)
    q = _rotary_lanes(zz[:, 0:MIX], cos_ref[...], sin_ref[...]) * (RET_HD ** -0.5)
    k = _rotary_lanes(zz[:, MIX:2 * MIX], cos_ref[...], sin_ref[...])
    v = zz[:, 2 * MIX:3 * MIX]
    gate = zz[:, 3 * MIX:4 * MIX]

    def _():
        st_ref[...] = jnp.zeros(st_ref.shape, F32)

    _when_first(s, _)
    k_t = k.T
    g_all = _bdot(q, _head_block_cols(k_t, RET_HEADS, RET_HD))
    o = _bdot(g_all * dec_ref[...], _head_block_rows_sel(v, RET_HEADS, RET_HD))
    o = o + _bdot(q * grow_ref[...], st_ref[...])
    upd = _bdot(k_t * toend_ref[...], v)
    st_ref[...] = sdec_ref[...] * st_ref[...] + upd * hdiag_ref[...]
    y_ref[...] = (_silu(gate) * (_group_norm_lanes(o, RET_HD) * gn_ref[...])).astype(y_ref.dtype)

    def _():
        for h in range(RET_HEADS):
            sl = slice(h * RET_HD, (h + 1) * RET_HD)
            sto_ref[h] = st_ref[sl, sl]

    _when_last(s, _)


def _ret_prompt(z3, P, l):
    B, L, _ = z3.shape
    Lc = CHUNK
    assert L % (SUBCHUNKS * Lc) == 0 and L % (SSD_SUBCHUNKS * Lc) == 0, "sequence length must fill whole grid steps"
    cos_l, sin_l = _rope_tables(np.arange(L))
    consts = list(_ret_tables(Lc)) + [_head_masks(RET_HEADS, RET_HD, Lc)[1]]
    return pl.pallas_call(
        _ret_kernel,
        out_shape=(jax.ShapeDtypeStruct((B, L, MIX), BF16),
                   jax.ShapeDtypeStruct((B, RET_HEADS, RET_HD, RET_HD), F32)),
        grid=(B, L // (SUBCHUNKS * Lc)),
        in_specs=[pl.BlockSpec((None, SUBCHUNKS * Lc,2048), lambda b, c: (b, c, Z_RET // 2048)),
                  pl.BlockSpec((SUBCHUNKS * Lc, MIX), lambda b, c: (c, 0)),
                  pl.BlockSpec((SUBCHUNKS * Lc, MIX), lambda b, c: (c, 0))]
                 + [_const_spec(a) for a in consts] + [_lspec(P["ret_gn"], l)],
        out_specs=(pl.BlockSpec((None, SUBCHUNKS * Lc,MIX), lambda b, c: (b, c, 0)),
                   pl.BlockSpec((None, RET_HEADS, RET_HD, RET_HD), lambda b, c: (b, 0, 0, 0))),
        scratch_shapes=[pltpu.VMEM((MIX, MIX), F32)],
        compiler_params=_cparams("parallel", "arbitrary"),
        name="ret_prompt",
    )(z3, cos_l, sin_l, *consts, P["ret_gn"])


def _lru_gates(xc, wg_ref, bg_ref, lam_ref):
    rg = _bdot(xc, wg_ref[...]) + bg_ref[...]
    r = jax.nn.sigmoid(rg[:, 0:MIX])
    i = jax.nn.sigmoid(rg[:, MIX:2 * MIX])
    log_a = -LRU_C * r * jax.nn.softplus(-lam_ref[...])
    a = jnp.exp(log_a)
    bx = jnp.sqrt(_neg_expm1_2x(log_a, a)) * (i * xc)
    return a, bx


def _lru_kernel(z_ref, *refs):
    y_ref = refs[-5]
    for sub in _sub_chunks(z_ref):
        _lru_chunk(sub, _sub_rows(z_ref, sub[0]), *refs[:-5], _sub_rows(y_ref, sub[0]), *refs[-4:])


def _lru_chunk(s, z_ref, cw_ref, cb_ref, wg_ref, bg_ref, lam_ref,
               y_ref, st_ref, buf_ref, pad_ref, h_ref):
    Lc = z_ref.shape[0]
    zz = z_ref[...].astype(F32)
    gate = zz[:, 0:MIX]
    x = zz[:, MIX:2 * MIX]

    def _():
        st_ref[...] = jnp.zeros(st_ref.shape, F32)

    _when_first(s, _)
    xc = _conv_chunk(s, x, pad_ref, cw_ref, cb_ref)
    a, bx = _lru_gates(xc, wg_ref, bg_ref, lam_ref)
    ng = Lc // SUBLANES
    a3 = a.reshape(ng, SUBLANES, MIX)
    b3 = bx.reshape(ng, SUBLANES, MIX)
    sub = lax.broadcasted_iota(jnp.int32, a3.shape, 1)
    step = 1
    while step < SUBLANES:
        keep = sub >= step
        b3 = jnp.where(keep, b3 + a3 * pltpu.roll(b3, step, 1), b3)
        a3 = jnp.where(keep, a3 * pltpu.roll(a3, step, 1), a3)
        step *= 2
    carry = st_ref[...]
    for g in range(ng):
        hg = b3[g] + a3[g] * jnp.broadcast_to(carry, (SUBLANES, MIX))
        h_ref[g * SUBLANES:(g + 1) * SUBLANES, :] = hg
        carry = hg[SUBLANES - 1:SUBLANES, :]
    st_ref[...] = carry
    y_ref[...] = (h_ref[...] * jax.nn.gelu(gate)).astype(y_ref.dtype)

    def _():
        buf_ref[...] = pad_ref[Lc + 8 - (CONV_K - 1):Lc + 8, :]

    _when_last(s, _)
    _conv_finish(pad_ref, Lc)


def _lru_prompt(z3, P, l):
    B, L, _ = z3.shape
    Lc = CHUNK
    assert L % (SUBCHUNKS * Lc) == 0 and L % (SSD_SUBCHUNKS * Lc) == 0, "sequence length must fill whole grid steps"
    params = [P[k] for k in ("lru_cw", "lru_cb", "lru_wg", "lru_bg", "lru_lam")]
    return pl.pallas_call(
        _lru_kernel,
        out_shape=(jax.ShapeDtypeStruct((B, L, MIX), BF16),
                   jax.ShapeDtypeStruct((B, 1, MIX), F32),
                   jax.ShapeDtypeStruct((B, CONV_K - 1, MIX), F32)),
        grid=(B, L // (SUBCHUNKS * Lc)),
        in_specs=[pl.BlockSpec((None, SUBCHUNKS * Lc,1024), lambda b, c: (b, c, Z_LRU // 1024))]
                 + [_lspec(a, l) for a in params],
        out_specs=(pl.BlockSpec((None, SUBCHUNKS * Lc,MIX), lambda b, c: (b, c, 0)),
                   pl.BlockSpec((None, 1, MIX), lambda b, c: (b, 0, 0)),
                   pl.BlockSpec((None, CONV_K - 1, MIX), lambda b, c: (b, 0, 0))),
        scratch_shapes=[pltpu.VMEM((Lc + 8, MIX), F32), pltpu.VMEM((Lc, MIX), F32)],
        compiler_params=_cparams("parallel", "arbitrary"),
        name="lru_prompt",
    )(z3, *params)


S5_LB = 4
S5_LAGS = 4
assert 2 * S5_LAGS == SUBLANES


def _s5_project_in(u, wbr_ref, wbi_ref, xr_ref, xi_ref, lags):
    shifted = [u]
    if lags > 1:
        sub = lax.broadcasted_iota(jnp.int32, u.shape, 0) & (lags - 1)
        shifted += [jnp.where(sub >= d, pltpu.roll(u, d, 0), 0.0) for d in range(1, lags)]
    shifted = [s.astype(BF16) for s in shifted]
    for kb in range(S5_LB):
        lhs = jnp.concatenate([s[:, kb * 128:(kb + 1) * 128] for s in shifted], axis=1)
        k = lags * 128
        xr_ref[:, kb * 512:(kb + 1) * 512] = jnp.dot(lhs, wbr_ref[kb, 0:k, :], preferred_element_type=F32)
        xi_ref[:, kb * 512:(kb + 1) * 512] = jnp.dot(lhs, wbi_ref[kb, 0:k, :], preferred_element_type=F32)


def _s5_project_out(xr, xi, u, wcr_ref, wci_ref, d_ref, wglu_ref):
    ys = []
    for kb in range(S5_LB):
        sl = slice(kb * 512, (kb + 1) * 512)
        ys.append(_bdot(xr[:, sl], wcr_ref[kb]) - _bdot(xi[:, sl], wci_ref[kb]))
    y = jnp.concatenate(ys, axis=1) + d_ref[...] * u
    y = jax.nn.gelu(y)
    return y * jax.nn.sigmoid(_bdot(y, wglu_ref[...]))


def _s5_kernel(z_ref, *refs):
    y_ref = refs[-5]
    for sub in _sub_chunks(z_ref):
        _s5_chunk(sub, _sub_rows(z_ref, sub[0]), *refs[:-5], _sub_rows(y_ref, sub[0]), *refs[-4:])


def _s5_chunk(s, z_ref, wbr_ref, wbi_ref, wcr_ref, wci_ref, pr_ref, pi_ref, d_ref, wglu_ref, qr_ref, qi_ref,
              y_ref, sr_ref, si_ref, xr_ref, xi_ref):
    Lc = z_ref.shape[0]
    u = z_ref[...].astype(F32)

    def _():
        sr_ref[...] = jnp.zeros(sr_ref.shape, F32)
        si_ref[...] = jnp.zeros(si_ref.shape, F32)

    _when_first(s, _)
    _s5_project_in(u, wbr_ref, wbi_ref, xr_ref, xi_ref, S5_LAGS)
    ng = Lc // SUBLANES
    x3r = xr_ref[...].reshape(ng, SUBLANES, S5_CH)
    x3i = xi_ref[...].reshape(ng, SUBLANES, S5_CH)
    tr = jnp.broadcast_to(x3r[:, S5_LAGS - 1:S5_LAGS, :], x3r.shape)
    ti = jnp.broadcast_to(x3i[:, S5_LAGS - 1:S5_LAGS, :], x3i.shape)
    mr, mi = qr_ref[...][None], qi_ref[...][None]
    x3r, x3i = x3r + (mr * tr - mi * ti), x3i + (mr * ti + mi * tr)
    pcr, pci = pr_ref[...], pi_ref[...]
    cr, ci = sr_ref[...], si_ref[...]
    for g in range(ng):
        br = jnp.broadcast_to(cr, (SUBLANES, S5_CH))
        bi = jnp.broadcast_to(ci, (SUBLANES, S5_CH))
        gr = x3r[g] + (pcr * br - pci * bi)
        gi = x3i[g] + (pcr * bi + pci * br)
        xr_ref[g * SUBLANES:(g + 1) * SUBLANES, :] = gr
        xi_ref[g * SUBLANES:(g + 1) * SUBLANES, :] = gi
        cr, ci = gr[SUBLANES - 1:SUBLANES, :], gi[SUBLANES - 1:SUBLANES, :]
    sr_ref[...] = cr
    si_ref[...] = ci
    y_ref[...] = _s5_project_out(xr_ref[...], xi_ref[...], u, wcr_ref, wci_ref, d_ref, wglu_ref).astype(y_ref.dtype)


S5_PARAMS = ("s5_wbr", "s5_wbi", "s5_wcr", "s5_wci", "s5_pr", "s5_pi", "s5_d", "s5_glu", "s5_qr", "s5_qi")


def _s5_prompt(z3, P, l):
    B, L, _ = z3.shape
    Lc = CHUNK
    assert L % (SUBCHUNKS * Lc) == 0 and L % (SSD_SUBCHUNKS * Lc) == 0, "sequence length must fill whole grid steps"
    params = [P[k] for k in S5_PARAMS]
    return pl.pallas_call(
        _s5_kernel,
        out_shape=(jax.ShapeDtypeStruct((B, L, MIX), BF16),
                   jax.ShapeDtypeStruct((B, 1, S5_CH), F32),
                   jax.ShapeDtypeStruct((B, 1, S5_CH), F32)),
        grid=(B, L // (SUBCHUNKS * Lc)),
        in_specs=[pl.BlockSpec((None, SUBCHUNKS * Lc,MIX), lambda b, c: (b, c, Z_S5 // MIX))]
                 + [_lspec(a, l) for a in params],
        out_specs=(pl.BlockSpec((None, SUBCHUNKS * Lc,MIX), lambda b, c: (b, c, 0)),
                   pl.BlockSpec((None, 1, S5_CH), lambda b, c: (b, 0, 0)),
                   pl.BlockSpec((None, 1, S5_CH), lambda b, c: (b, 0, 0))),
        scratch_shapes=[pltpu.VMEM((Lc, S5_CH), F32), pltpu.VMEM((Lc, S5_CH), F32)],
        compiler_params=_cparams("parallel", "arbitrary"),
        name="s5_prompt",
    )(z3, *params)


def _merge_kernel(y0_ref, y1_ref, y2_ref, y3_ref, zg_ref, h_ref, wb_ref, wo_ref, o_ref):
    acc = None
    for k, y_ref in enumerate((y0_ref, y1_ref, y2_ref, y3_ref)):
        br = _bdot(y_ref[...], wb_ref[k])
        t = jax.nn.sigmoid(zg_ref[:, k * D_MODEL:(k + 1) * D_MODEL].astype(F32)) * br
        acc = t if acc is None else acc + t
    o_ref[...] = h_ref[...] + _bdot(acc, wo_ref[...])


def _merge(ys, z, h, P, l, tm):
    T = h.shape[0]
    rows = lambda w: pl.BlockSpec((tm, w), lambda i: (i, 0))
    return pl.pallas_call(
        _merge_kernel,
        out_shape=jax.ShapeDtypeStruct((T, D_MODEL), F32),
        grid=(T // tm,),
        in_specs=[rows(MIX), rows(MIX), rows(MIX), rows(MIX),
                  pl.BlockSpec((tm, 4 * D_MODEL), lambda i: (i, Z_MERGE)),
                  rows(D_MODEL),
                  _lspec(P["w_branch"], l), _lspec(P["w_out"], l)],
        out_specs=rows(D_MODEL),
        compiler_params=_cparams("parallel"),
        name="merge",
    )(*ys, z, h, P["w_branch"], P["w_out"])


def _top2_gates(logits):
    lane, i1, i2, w1, w2 = _top2(logits)
    return jnp.where(lane == i1, w1, 0.0) + jnp.where(lane == i2, w2, 0.0)


def _ffn_kernel(moe, final_norm, *refs):
    if moe:
        h_ref, g_ref, rt_ref, w1_ref, w3_ref, w2_ref, gf_ref, o_ref, hn_ref, acc_ref, gate_ref = refs
    else:
        h_ref, g_ref, w1_ref, w3_ref, w2_ref, gf_ref, o_ref, hn_ref, acc_ref = refs
    e = pl.program_id(1)
    ne = pl.num_programs(1)

    @pl.when(e == 0)
    def _():
        hn = _rmsnorm(h_ref[...], g_ref[...])
        hn_ref[...] = hn.astype(BF16)
        acc_ref[...] = jnp.zeros(acc_ref.shape, F32)
        if moe:
            lane = lax.broadcasted_iota(jnp.int32, (hn.shape[0], LANES), 1)
            logits = jnp.where(lane < N_EXPERTS, _hdot(hn, rt_ref[...]), -jnp.inf)
            gate_ref[...] = _top2_gates(logits)

    hn = hn_ref[...]
    a = jnp.dot(hn, w1_ref[...], preferred_element_type=F32)
    b = jnp.dot(hn, w3_ref[...], preferred_element_type=F32)
    o = _bdot(_silu(a) * b, w2_ref[...])
    if moe:
        lane = lax.broadcasted_iota(jnp.int32, gate_ref.shape, 1)
        ge = jnp.sum(jnp.where(lane == e, gate_ref[...], 0.0), axis=-1, keepdims=True)
        o = ge * o
    acc_ref[...] += o

    @pl.when(e == ne - 1)
    def _():
        out = h_ref[...] + acc_ref[...]
        if final_norm:
            out = _rmsnorm(out, gf_ref[...])
        o_ref[...] = out


def _ffn(h, P, l, tm, final_norm):
    T = h.shape[0]
    moe = l % 2 == 1
    j = l // 2
    tf = D_FF_TILE
    row_spec = pl.BlockSpec((tm, D_MODEL), lambda i, e: (i, 0))
    gfinal = P["norm_final"]
    if moe:
        w1, w3, w2 = P["moe_w1"], P["moe_w3"], P["moe_w2"]
        ne = w1.shape[1]
        wspecs = [pl.BlockSpec((None, None, D_MODEL, tf), lambda i, e: (j, e, 0, 0)),
                  pl.BlockSpec((None, None, D_MODEL, tf), lambda i, e: (j, e, 0, 0)),
                  pl.BlockSpec((None, None, tf, D_MODEL), lambda i, e: (j, e, 0, 0))]
        in_specs = ([row_spec, _lspec(P["norm_ffn"], l), _lspec(P["moe_router"], j)] + wspecs
                    + [_const_spec(gfinal)])
        args = (h, P["norm_ffn"], P["moe_router"], w1, w3, w2, gfinal)
        scratch = [pltpu.VMEM((tm, D_MODEL), BF16), pltpu.VMEM((tm, D_MODEL), F32), pltpu.VMEM((tm, LANES), F32)]
    else:
        w1, w3, w2 = P["ffn_w1"], P["ffn_w3"], P["ffn_w2"]
        ne = w1.shape[2] // tf
        wspecs = [pl.BlockSpec((None, D_MODEL, tf), lambda i, e: (j, 0, e)),
                  pl.BlockSpec((None, D_MODEL, tf), lambda i, e: (j, 0, e)),
                  pl.BlockSpec((None, tf, D_MODEL), lambda i, e: (j, e, 0))]
        in_specs = [row_spec, _lspec(P["norm_ffn"], l)] + wspecs + [_const_spec(gfinal)]
        args = (h, P["norm_ffn"], w1, w3, w2, gfinal)
        scratch = [pltpu.VMEM((tm, D_MODEL), BF16), pltpu.VMEM((tm, D_MODEL), F32)]
    return pl.pallas_call(
        functools.partial(_ffn_kernel, moe, final_norm),
        out_shape=jax.ShapeDtypeStruct((T, D_MODEL), F32),
        grid=(T // tm, ne),
        in_specs=in_specs,
        out_specs=row_spec,
        scratch_shapes=scratch,
        compiler_params=_cparams("parallel", "arbitrary"),
        name="moe" if moe else "ffn",
    )(*args)


MOE_ROWS = 512
ROUTE_LANES = ("e1", "e2", "r1", "r2", "w1", "w2")


def _top2(logits):
    lane = lax.broadcasted_iota(jnp.int32, logits.shape, 1).astype(F32)
    big = float(LANES)
    m1 = jnp.max(logits, axis=-1, keepdims=True)
    i1 = jnp.min(jnp.where(logits == m1, lane, big), axis=-1, keepdims=True)
    rest = jnp.where(lane == i1, -jnp.inf, logits)
    m2 = jnp.max(rest, axis=-1, keepdims=True)
    i2 = jnp.min(jnp.where(rest == m2, lane, big), axis=-1, keepdims=True)
    e2 = jnp.exp(m2 - m1)
    den = 1.0 + e2
    return lane, i1, i2, 1.0 / den, e2 / den


def _moe_route_kernel(h_ref, g_ref, rt_ref, ltri_ref, hn_ref, info_ref, cnt_ref, base_ref):
    i = pl.program_id(0)

    @pl.when(i == 0)
    def _():
        base_ref[...] = jnp.zeros(base_ref.shape, F32)

    hn = _rmsnorm(h_ref[...], g_ref[...])
    hn_ref[...] = hn
    lane_i = lax.broadcasted_iota(jnp.int32, (hn.shape[0], LANES), 1)
    logits = jnp.where(lane_i < N_EXPERTS, _hdot(hn, rt_ref[...]), -jnp.inf)
    lane, i1, i2, w1, w2 = _top2(logits)
    oh1 = (lane == i1).astype(F32)
    oh2 = (lane == i2).astype(F32)
    oh = oh1 + oh2
    before = jnp.dot(ltri_ref[...], oh.astype(BF16), preferred_element_type=F32)
    rank = base_ref[...] + before
    r1 = jnp.sum(oh1 * rank, axis=-1, keepdims=True)
    r2 = jnp.sum(oh2 * rank, axis=-1, keepdims=True)
    base_ref[...] += jnp.sum(oh, axis=0, keepdims=True)
    fields = dict(e1=i1, e2=i2, r1=r1, r2=r2, w1=w1, w2=w2)
    info = jnp.zeros(lane.shape, F32)
    for k, name in enumerate(ROUTE_LANES):
        info = jnp.where(lane_i == k, fields[name], info)
    info_ref[...] = info

    @pl.when(i == pl.num_programs(0) - 1)
    def _():
        cnt_ref[...] = base_ref[...]


def _moe_route(h, P, l, tm):
    T = h.shape[0]
    j = l // 2
    ltri = jnp.asarray(np.tril(np.ones((tm, tm), np.float32), -1), BF16)
    return pl.pallas_call(
        _moe_route_kernel,
        out_shape=(jax.ShapeDtypeStruct((T, D_MODEL), F32), jax.ShapeDtypeStruct((T, LANES), F32),
                   jax.ShapeDtypeStruct((1, LANES), F32)),
        grid=(T // tm,),
        in_specs=[pl.BlockSpec((tm, D_MODEL), lambda i: (i, 0)), _lspec(P["norm_ffn"], l),
                  _lspec(P["moe_router"], j), _const_spec(ltri)],
        out_specs=(pl.BlockSpec((tm, D_MODEL), lambda i: (i, 0)), pl.BlockSpec((tm, LANES), lambda i: (i, 0)),
                   pl.BlockSpec((1, LANES), lambda i: (0, 0))),
        scratch_shapes=[pltpu.VMEM((1, LANES), F32)],
        compiler_params=_cparams("arbitrary"),
        name="moe_route",
    )(h, P["norm_ffn"], P["moe_router"], ltri)


def _moe_plan(info, cnt, tm, n_tiles_max):
    T = info.shape[0]
    rows = MOE_ROWS
    count = cnt[0, :N_EXPERTS].astype(jnp.int32)
    tiles_e = (count + rows - 1) // rows
    first_tile = jnp.cumsum(tiles_e) - tiles_e
    start = first_tile * rows
    experts = jnp.arange(N_EXPERTS, dtype=jnp.int32)

    def row_of(e, r):
        sel = e.astype(jnp.int32)[:, None] == experts[None, :]
        return jnp.sum(jnp.where(sel, start[None, :], 0), axis=1) + r.astype(jnp.int32)

    pos = jnp.stack([row_of(info[:, 0], info[:, 2]), row_of(info[:, 1], info[:, 3])], axis=0)
    pos = jnp.transpose(pos.reshape(2, T // tm, tm), (1, 0, 2))
    n_tiles = jnp.sum(tiles_e)
    t = jnp.minimum(jnp.arange(n_tiles_max, dtype=jnp.int32), n_tiles - 1)
    tile_expert = jnp.sum((first_tile[None, :] <= t[:, None]).astype(jnp.int32), axis=1) - 1
    last_tile = first_tile + tiles_e - 1
    return pos, tile_expert, n_tiles.reshape(1), last_tile, tiles_e


def _moe_dispatch_kernel(last_ref, tiles_ref, nt_ref, pos_ref, hn_ref, xs_ref, zero_ref, sem):
    tm = hn_ref.shape[0]

    @pl.when(pl.program_id(0) == 0)
    def _():
        zero_ref[...] = jnp.zeros(zero_ref.shape, F32)

        def clear(tile):
            row0 = pl.multiple_of(tile * MOE_ROWS, MOE_ROWS)
            cp = pltpu.make_async_copy(zero_ref, xs_ref.at[pl.ds(row0, MOE_ROWS), :], sem)
            cp.start()
            cp.wait()

        for e in range(N_EXPERTS):
            @pl.when(tiles_ref[e] > 0)
            def _():
                clear(last_ref[e])

        def clear_tail(tile, c):
            clear(tile)
            return c

        lax.fori_loop(nt_ref[0], xs_ref.shape[0] // MOE_ROWS, clear_tail, 0)

    def row_copy(j, slot):
        return pltpu.make_async_copy(hn_ref.at[pl.ds(j, 1), :], xs_ref.at[pl.ds(pos_ref[slot, j], 1), :], sem)

    def issue(j, c):
        row_copy(j, 0).start(priority=0)
        row_copy(j, 1).start(priority=1)
        return c

    lax.fori_loop(0, tm, issue, 0, unroll=8)
    for _ in range(2):
        pltpu.make_async_copy(hn_ref, xs_ref.at[pl.ds(0, tm), :], sem).wait()


def _moe_dispatch(hn, pos, last_tile, tiles_e, n_tiles, n_rows):
    T = hn.shape[0]
    tm = pos.shape[2]
    gs = pltpu.PrefetchScalarGridSpec(
        num_scalar_prefetch=3, grid=(T // tm,),
        in_specs=[pl.BlockSpec((None, 2, tm), lambda i, *_: (i, 0, 0), memory_space=pltpu.SMEM),
                  pl.BlockSpec((tm, D_MODEL), lambda i, *_: (i, 0))],
        out_specs=pl.BlockSpec(memory_space=pl.ANY),
        scratch_shapes=[pltpu.VMEM((MOE_ROWS, D_MODEL), F32), pltpu.SemaphoreType.DMA])
    return pl.pallas_call(
        _moe_dispatch_kernel, grid_spec=gs,
        out_shape=jax.ShapeDtypeStruct((n_rows, D_MODEL), F32),
        compiler_params=_cparams("arbitrary"),
        name="moe_dispatch",
    )(last_tile, tiles_e, n_tiles, pos, hn)


def _moe_group_kernel(te_ref, nt_ref, x_ref, w1_ref, w3_ref, w2_ref, o_ref):
    live = pl.program_id(0) < nt_ref[0]

    @pl.when(live)
    def _():
        x = x_ref[...].astype(BF16)
        a = jnp.dot(x, w1_ref[...], preferred_element_type=F32)
        b = jnp.dot(x, w3_ref[...], preferred_element_type=F32)
        o_ref[...] = _bdot(_silu(a) * b, w2_ref[...])

    @pl.when(jnp.logical_not(live))
    def _():
        o_ref[...] = jnp.zeros(o_ref.shape, F32)


def _moe_group(xs, P, j, tile_expert, n_tiles):
    n_rows = xs.shape[0]
    tf = D_FF_TILE
    wmap = lambda i, te, nt: (j, te[i], 0, 0)
    rmap = lambda i, te, nt: (i, 0)
    gs = pltpu.PrefetchScalarGridSpec(
        num_scalar_prefetch=2, grid=(n_rows // MOE_ROWS,),
        in_specs=[pl.BlockSpec((MOE_ROWS, D_MODEL), rmap),
                  pl.BlockSpec((None, None, D_MODEL, tf), wmap),
                  pl.BlockSpec((None, None, D_MODEL, tf), wmap),
                  pl.BlockSpec((None, None, tf, D_MODEL), wmap)],
        out_specs=pl.BlockSpec((MOE_ROWS, D_MODEL), rmap))
    return pl.pallas_call(
        _moe_group_kernel, grid_spec=gs,
        out_shape=jax.ShapeDtypeStruct((n_rows, D_MODEL), F32),
        compiler_params=_cparams("arbitrary"),
        name="moe_group",
    )(tile_expert, n_tiles, xs, P["moe_w1"], P["moe_w3"], P["moe_w2"])


def _moe_combine_kernel(final_norm, pos_ref, o_ref, h_ref, info_ref, gf_ref, out_ref, a_ref, b_ref, sem):
    tm = h_ref.shape[0]

    def row_copy(j, slot, dst):
        return pltpu.make_async_copy(o_ref.at[pl.ds(pos_ref[slot, j], 1), :], dst.at[pl.ds(j, 1), :], sem)

    def issue(j, c):
        row_copy(j, 0, a_ref).start(priority=0)
        row_copy(j, 1, b_ref).start(priority=1)
        return c

    lax.fori_loop(0, tm, issue, 0, unroll=8)
    for dst in (a_ref, b_ref):
        pltpu.make_async_copy(o_ref.at[pl.ds(0, tm), :], dst, sem).wait()
    k1, k2 = ROUTE_LANES.index("w1"), ROUTE_LANES.index("w2")
    info = info_ref[...]
    w1, w2 = info[:, k1:k1 + 1], info[:, k2:k2 + 1]
    out = h_ref[...] + (w1 * a_ref[...] + w2 * b_ref[...])
    if final_norm:
        out = _rmsnorm(out, gf_ref[...])
    out_ref[...] = out


def _moe_combine(o, h, info, pos, P, final_norm):
    T = h.shape[0]
    tm = pos.shape[2]
    gfinal = P["norm_final"]
    row_spec = pl.BlockSpec((tm, D_MODEL), lambda i: (i, 0))
    return pl.pallas_call(
        functools.partial(_moe_combine_kernel, final_norm),
        out_shape=jax.ShapeDtypeStruct((T, D_MODEL), F32),
        grid=(T // tm,),
        in_specs=[pl.BlockSpec((None, 2, tm), lambda i: (i, 0, 0), memory_space=pltpu.SMEM),
                  pl.BlockSpec(memory_space=pl.ANY), row_spec,
                  pl.BlockSpec((tm, LANES), lambda i: (i, 0)), _const_spec(gfinal)],
        out_specs=row_spec,
        scratch_shapes=[pltpu.VMEM((tm, D_MODEL), F32), pltpu.VMEM((tm, D_MODEL), F32), pltpu.SemaphoreType.DMA],
        compiler_params=_cparams("arbitrary"),
        name="moe_combine",
    )(pos, o, h, info, gfinal)


def _moe_routed(h, P, l, tm, final_norm):
    T = h.shape[0]
    n_tiles_max = (2 * T) // MOE_ROWS + N_EXPERTS
    if T % (2 * tm) == 0:
        tm = 2 * tm
    hn, info, cnt = _moe_route(h, P, l, tm)
    pos, tile_expert, n_tiles, last_tile, tiles_e = _moe_plan(info, cnt, tm, n_tiles_max)
    xs = _moe_dispatch(hn, pos, last_tile, tiles_e, n_tiles, n_tiles_max * MOE_ROWS)
    o = _moe_group(xs, P, l // 2, tile_expert, n_tiles)
    return _moe_combine(o, h, info, pos, P, final_norm)


def _conv_step(x, buf_ref, buf_o, w_ref, b_ref):
    out = b_ref[...]
    for k in range(CONV_K - 1):
        out = out + buf_ref[k] * w_ref[k:k + 1, :]
        if k > 0:
            buf_o[k - 1] = buf_ref[k]
    out = out + x * w_ref[CONV_K - 1:CONV_K, :]
    buf_o[CONV_K - 2] = x
    return out


def _step_pre_kernel(pos_cos_ref, pos_sin_ref, gam_ref, ehn_ref, z_ref, sbuf_ref, lbuf_ref, lst_ref, s5r_ref, s5i_ref,
                     scw_ref, scb_ref, dtb_ref, alog_ref,
                     lcw_ref, lcb_ref, wg_ref, bg_ref, lam_ref,
                     wbr_ref, wbi_ref, wcr_ref, wci_ref, pr_ref, pi_ref, s5d_ref, wglu_ref, qr_ref, qi_ref,
                     kqv_ref, dec_ref, sbuf_o, lbuf_o, lst_o, s5r_o, s5i_o, ys5_o, ylru_o, xs_o,
                     xr_ref, xi_ref):
    zz = z_ref[...].astype(F32)
    xbc = zz[:, Z_SSD + MIX:Z_SSD + MIX + SSD_CONV]
    dt_raw = zz[:, Z_SSD + MIX + SSD_CONV:Z_SSD + MIX + SSD_CONV + LANES]
    xc = _silu(_conv_step(xbc, sbuf_ref, sbuf_o, scw_ref, scb_ref))
    xs = xc[:, 0:MIX]
    xs_o[...] = xs
    nbc = SSD_GROUPS * SSD_STATE
    bm = xc[:, MIX:MIX + nbc]
    cm = xc[:, MIX + nbc:MIX + 2 * nbc]
    dt = jax.nn.softplus(dt_raw + dtb_ref[...])
    a = -jnp.exp(alog_ref[...])
    rep = SSD_HEADS // SSD_GROUPS
    kqv_ref[0, 0] = (_group_repeat_lanes(bm, rep) * _split3_dot(dt, ehn_ref[...])).T
    kqv_ref[0, 1] = _group_repeat_lanes(cm, rep).T
    kqv_ref[0, 2] = xs.T
    dec_ref[0] = jnp.exp(dt * a).T[0:SSD_HEADS, :]
    q = _rotary_lanes(zz[:, Z_RET:Z_RET + MIX], pos_cos_ref[...], pos_sin_ref[...]) * (RET_HD ** -0.5)
    k = _rotary_lanes(zz[:, Z_RET + MIX:Z_RET + 2 * MIX], pos_cos_ref[...], pos_sin_ref[...])
    kqv_ref[1, 0] = k.T
    kqv_ref[1, 1] = q.T
    kqv_ref[1, 2] = zz[:, Z_RET + 2 * MIX:Z_RET + 3 * MIX].T
    dec_ref[1] = gam_ref[...]
    gate = zz[:, Z_LRU:Z_LRU + MIX]
    lx = zz[:, Z_LRU + MIX:Z_LRU + 2 * MIX]
    lconv = _conv_step(lx, lbuf_ref, lbuf_o, lcw_ref, lcb_ref)
    la, lbx = _lru_gates(lconv, wg_ref, bg_ref, lam_ref)
    hl = lbx + la * lst_ref[...]
    lst_o[...] = hl
    ylru_o[...] = hl * jax.nn.gelu(gate)
    u = zz[:, Z_S5:Z_S5 + MIX]
    _s5_project_in(u, wbr_ref, wbi_ref, xr_ref, xi_ref, 1)
    lr, li = pr_ref[0:1, :], pi_ref[0:1, :]
    s0r, s0i = s5r_ref[...], s5i_ref[...]
    xr = xr_ref[...] + (lr * s0r - li * s0i)
    xi = xi_ref[...] + (lr * s0i + li * s0r)
    s5r_o[...] = xr
    s5i_o[...] = xi
    ys5_o[...] = _s5_project_out(xr, xi, u, wcr_ref, wci_ref, s5d_ref, wglu_ref)


def _step_state_kernel(kqv_ref, dec_ref, st_ref, o_st_ref, y_ref):
    h = pl.program_id(0)
    d = dec_ref[pl.ds(h, 1), :]
    v = kqv_ref[2]
    acc = jnp.zeros(v.shape, F32)
    for n in range(st_ref.shape[0]):
        s_new = d * st_ref[n] + kqv_ref[0, n:n + 1, :] * v
        o_st_ref[n] = s_new
        acc = acc + kqv_ref[1, n:n + 1, :] * s_new
    y_ref[...] = acc


def _step_post_kernel(yssd_ref, yret_ref, xs_ref, z_ref, dlane_ref, ng_ref, gn_ref, yssd_o, yret_o):
    zz_gate = z_ref[:, Z_SSD:Z_SSD + MIX]
    y = yssd_ref[...].T + dlane_ref[...] * xs_ref[...]
    y = y * _silu(zz_gate)
    yssd_o[...] = _rmsnorm(y, ng_ref[...])
    rgate = z_ref[:, Z_RET + 3 * MIX:Z_RET + 4 * MIX]
    yret_o[...] = _silu(rgate) * (_group_norm_lanes(yret_ref[...].T, RET_HD) * gn_ref[...])


def _sample_mixers(z, views, big, layer, P, pos):
    Bs = z.shape[0]
    H = SSD_HEADS
    cos_l, sin_l = _rope_tables(np.asarray([pos]))
    gam = jnp.asarray(np.repeat(_ret_gammas()[:, None], Bs, axis=1), F32)
    consts = [cos_l, sin_l, gam, _head_expand(SSD_HEADS, SSD_HD)]
    states = [views[k] for k in ("ssd_conv", "lru_conv", "lru", "s5_re", "s5_im")]
    params = [P[k] for k in ("ssd_cw", "ssd_cb", "ssd_dtb", "ssd_alog",
                             "lru_cw", "lru_cb", "lru_wg", "lru_bg", "lru_lam") + S5_PARAMS]
    pre_out = (jax.ShapeDtypeStruct((2, 3, MIX, Bs), F32),
               jax.ShapeDtypeStruct((2, H, Bs), F32),
               jax.ShapeDtypeStruct(views["ssd_conv"].shape[1:], F32),
               jax.ShapeDtypeStruct(views["lru_conv"].shape[1:], F32),
               jax.ShapeDtypeStruct((Bs, MIX), F32),
               jax.ShapeDtypeStruct((Bs, S5_CH), F32), jax.ShapeDtypeStruct((Bs, S5_CH), F32),
               jax.ShapeDtypeStruct((Bs, MIX), F32), jax.ShapeDtypeStruct((Bs, MIX), F32),
               jax.ShapeDtypeStruct((Bs, MIX), F32))
    (kqv, dec, sbuf_n, lbuf_n, lst_n, s5r_n, s5i_n, y_s5, y_lru, xs) = pl.pallas_call(
        _step_pre_kernel,
        out_shape=pre_out,
        grid=(1,),
        in_specs=[_const_spec(a) for a in consts] + [_const_spec(z)]
                 + [_lspec(a, layer) for a in states] + [_lspec(a, layer) for a in params],
        out_specs=tuple(pl.BlockSpec(o.shape, lambda i, n=len(o.shape): (0,) * n) for o in pre_out),
        scratch_shapes=[pltpu.VMEM((Bs, S5_CH), F32), pltpu.VMEM((Bs, S5_CH), F32)],
        compiler_params=_cparams("arbitrary"),
        name="step_pre",
    )(*consts, z, *states, *params)

    new_big, yts = {}, []
    for m, name in enumerate(("ssd", "ret")):
        s_new, y_t = pl.pallas_call(
            _step_state_kernel,
            out_shape=(jax.ShapeDtypeStruct(big[name].shape, F32), jax.ShapeDtypeStruct((MIX, Bs), F32)),
            grid=(H,),
            in_specs=[pl.BlockSpec((None, 3, SSD_STATE, Bs), lambda h, m=m: (m, 0, h, 0)),
                      pl.BlockSpec((None, H, Bs), lambda h, m=m: (m, 0, 0)),
                      pl.BlockSpec((None, None, SSD_STATE, SSD_HD, Bs), lambda h: (layer, h, 0, 0, 0))],
            out_specs=(pl.BlockSpec((None, None, SSD_STATE, SSD_HD, Bs), lambda h: (layer, h, 0, 0, 0)),
                       pl.BlockSpec((SSD_HD, Bs), lambda h: (h, 0))),
            input_output_aliases={2: 0},
            compiler_params=_cparams("parallel"),
            name="step_state_" + name,
        )(kqv, dec, big[name])
        new_big[name] = s_new
        yts.append(y_t)

    post_in = [yts[0], yts[1], xs, z]
    post_par = [P[k] for k in ("ssd_dlane", "ssd_norm", "ret_gn")]
    post_out = (jax.ShapeDtypeStruct((Bs, MIX), F32), jax.ShapeDtypeStruct((Bs, MIX), F32))
    y_ssd, y_ret = pl.pallas_call(
        _step_post_kernel,
        out_shape=post_out,
        grid=(1,),
        in_specs=[_const_spec(a) for a in post_in] + [_lspec(a, layer) for a in post_par],
        out_specs=tuple(pl.BlockSpec(o.shape, lambda i: (0, 0)) for o in post_out),
        compiler_params=_cparams("arbitrary"),
        name="step_post",
    )(*post_in, *post_par)

    new = dict(ssd_conv=sbuf_n, lru_conv=lbuf_n, lru=lst_n, s5_re=s5r_n, s5_im=s5i_n)
    return (y_ssd, y_s5, y_lru, y_ret), new, new_big


def _block_diag8(w):
    lead = w.shape[:-3]
    n, r, c = w.shape[-3:]
    eye = jnp.eye(n, dtype=w.dtype)
    out = w[..., :, :, None, :] * eye[:, None, :, None]
    return out.reshape(lead + (n * r, n * c))


def _row(v):
    return v[:, None, :]


def _cmul(ar, ai, br, bi):
    return ar * br - ai * bi, ar * bi + ai * br


def _prep_params(W):
    depth = W["w_in"].shape[0]
    P = {}
    w_t = jnp.transpose(W["w_in"], (0, 2, 1))
    P["w_in"] = jnp.concatenate(
        [w_t[:, 4872:8968], w_t[:, 2824:4872], w_t[:, 1800:2824], w_t[:, 1288:1800], w_t[:, 0:1288],
         jnp.zeros((depth, Z_WIDTH - 8968, D_MODEL), w_t.dtype)], axis=1).astype(BF16)
    P["norm_mix"] = _row(W["norm_mix"])
    pad = ((0, 0), (0, LANES - SSD_HEADS))
    P["ssd_cw"] = W["ssd_conv_w"]
    P["ssd_cb"] = _row(W["ssd_conv_b"])
    P["ssd_dtb"] = _row(jnp.pad(W["ssd_dt_bias"], pad))
    P["ssd_alog"] = _row(jnp.pad(W["ssd_a_log"], pad))
    P["ssd_dlane"] = _row(jnp.repeat(W["ssd_d"], SSD_HD, axis=1))
    P["ssd_norm"] = _row(W["ssd_norm"])
    lr, li = W["s5_lambda_re"], W["s5_lambda_im"]
    dt = jnp.exp(W["s5_log_dt"])[:, :, None]
    mag = jnp.exp(lr * dt)
    br, bi = mag * jnp.cos(li * dt), mag * jnp.sin(li * dt)
    den = lr * lr + li * li
    qr, qi = _cmul(br - 1.0, bi, lr / den, -li / den)
    wr, wi = _cmul(qr[..., None], qi[..., None], W["s5_b_re"], W["s5_b_im"])
    lag_r, lag_i = [wr], [wi]
    for _ in range(S5_LAGS - 1):
        nr, ni = _cmul(lag_r[-1], lag_i[-1], br[..., None], bi[..., None])
        lag_r.append(nr)
        lag_i.append(ni)
    gb = S5_GROUPS // S5_LB

    def embed_in(lags):
        m = jnp.stack([x.reshape(depth, S5_LB, gb, S5_STATE, S5_GDIM) for x in lags], axis=2)
        m = _block_diag8(jnp.swapaxes(m, -1, -2))
        return m.reshape(depth, S5_LB, len(lags) * gb * S5_GDIM, gb * S5_STATE)

    def embed_out(m):
        m = m.reshape(depth, S5_LB, gb, S5_GDIM, S5_STATE)
        return _block_diag8(jnp.swapaxes(m, -1, -2))

    P["s5_wbr"] = embed_in(lag_r).astype(BF16)
    P["s5_wbi"] = embed_in(lag_i).astype(BF16)
    P["s5_wcr"] = embed_out(W["s5_c_re"]).astype(BF16)
    P["s5_wci"] = embed_out(W["s5_c_im"]).astype(BF16)
    pr, pi = [br.reshape(depth, 1, S5_CH)], [bi.reshape(depth, 1, S5_CH)]
    for _ in range(SUBLANES - 1):
        nr, ni = _cmul(pr[-1], pi[-1], pr[0], pi[0])
        pr.append(nr)
        pi.append(ni)
    P["s5_pr"] = jnp.concatenate(pr, axis=1)
    P["s5_pi"] = jnp.concatenate(pi, axis=1)
    half = jnp.zeros((depth, S5_LAGS, S5_CH), F32)
    P["s5_qr"] = jnp.concatenate([half, P["s5_pr"][:, :SUBLANES - S5_LAGS]], axis=1)
    P["s5_qi"] = jnp.concatenate([half, P["s5_pi"][:, :SUBLANES - S5_LAGS]], axis=1)
    P["s5_d"] = W["s5_d"].reshape(depth, 1, MIX)
    P["s5_glu"] = W["s5_glu"].astype(BF16)
    P["lru_cw"] = W["lru_conv_w"]
    P["lru_cb"] = _row(W["lru_conv_b"])
    P["lru_wg"] = jnp.concatenate([_block_diag8(W["lru_wa"]), _block_diag8(W["lru_wx"])], axis=2).astype(BF16)
    P["lru_bg"] = _row(jnp.concatenate([W["lru_ba"], W["lru_bx"]], axis=1))
    P["lru_lam"] = _row(W["lru_lambda"])
    P["ret_gn"] = _row(W["ret_gn"])
    P["w_branch"] = W["w_branch"].astype(BF16)
    P["w_out"] = W["w_out"].astype(BF16)
    P["norm_ffn"] = _row(W["norm_ffn"])
    P["norm_final"] = W["norm_final"].reshape(1, D_MODEL)
    for k in ("ffn_w1", "ffn_w3", "ffn_w2", "moe_w1", "moe_w3", "moe_w2"):
        P[k] = W[k].astype(BF16)
    P["moe_router"] = jnp.pad(W["moe_router"], ((0, 0), (0, 0), (0, LANES - N_EXPERTS)))
    return P


def _trunk_prompt(x, P):
    B, L, _ = x.shape
    T = B * L
    depth = P["w_in"].shape[0]
    h = x.reshape(T, D_MODEL)
    new = {k: [] for k in ("ssd", "ssd_conv", "s5_re", "s5_im", "lru", "lru_conv", "ret")}
    for l in range(depth):
        z = _inproj(h, P, l, min(1024, T), BF16)
        z3 = z.reshape(B, L, Z_WIDTH)
        y_ssd, s_ssd, buf_ssd = _ssd_prompt(z3, P, l)
        y_s5, s5r, s5i = _s5_prompt(z3, P, l)
        y_lru, s_lru, buf_lru = _lru_prompt(z3, P, l)
        y_ret, s_ret = _ret_prompt(z3, P, l)
        ys = tuple(y.reshape(T, MIX) for y in (y_ssd, y_s5, y_lru, y_ret))
        h = _merge(ys, z, h, P, l, min(512, T))
        mixer = _moe_routed if (l % 2 == 1 and T % MOE_ROWS == 0) else _ffn
        h = mixer(h, P, l, min(512, T), final_norm=(l == depth - 1))
        new["ssd"].append(s_ssd)
        new["ssd_conv"].append(buf_ssd)
        new["s5_re"].append(s5r.reshape(B, S5_GROUPS, S5_STATE))
        new["s5_im"].append(s5i.reshape(B, S5_GROUPS, S5_STATE))
        new["lru"].append(s_lru.reshape(B, MIX))
        new["lru_conv"].append(buf_lru)
        new["ret"].append(s_ret)
    return h.reshape(B, L, D_MODEL), {k: jnp.stack(v) for k, v in new.items()}


def _trunk_sample(x, pos, st, P):
    Bs = x.shape[0]
    depth = P["w_in"].shape[0]
    h = x.reshape(Bs, D_MODEL)
    big = dict(ssd=jnp.transpose(st["ssd"], (0, 2, 3, 4, 1)), ret=jnp.transpose(st["ret"], (0, 2, 3, 4, 1)))
    views = dict(ssd_conv=jnp.transpose(st["ssd_conv"], (0, 2, 1, 3)),
                 lru_conv=jnp.transpose(st["lru_conv"], (0, 2, 1, 3)),
                 lru=st["lru"],
                 s5_re=st["s5_re"].reshape(depth, Bs, S5_CH),
                 s5_im=st["s5_im"].reshape(depth, Bs, S5_CH))
    new = {k: [] for k in views}
    for l in range(depth):
        z = _inproj(h, P, l, Bs, F32)
        ys, nl, big = _sample_mixers(z, views, big, l, P, pos)
        h = _merge(ys, z, h, P, l, Bs)
        h = _ffn(h, P, l, Bs, final_norm=(l == depth - 1))
        for k in new:
            new[k].append(nl[k])
    out = {k: jnp.stack(v) for k, v in new.items()}
    out["ssd_conv"] = jnp.transpose(out["ssd_conv"], (0, 2, 1, 3))
    out["lru_conv"] = jnp.transpose(out["lru_conv"], (0, 2, 1, 3))
    out["s5_re"] = out["s5_re"].reshape(st["s5_re"].shape)
    out["s5_im"] = out["s5_im"].reshape(st["s5_im"].shape)
    out["ssd"] = jnp.transpose(big["ssd"], (0, 4, 1, 2, 3))
    out["ret"] = jnp.transpose(big["ret"], (0, 4, 1, 2, 3))
    return h.reshape(Bs, 1, D_MODEL), out


def kernel(x_prompt, x_sample, state_ssd, state_ssd_conv, state_s5_re, state_s5_im, state_lru, state_lru_conv, state_ret, norm_mix, w_in, ssd_conv_w, ssd_conv_b, ssd_dt_bias, ssd_a_log, ssd_d, ssd_norm, s5_lambda_re, s5_lambda_im, s5_b_re, s5_b_im, s5_c_re, s5_c_im, s5_d, s5_log_dt, s5_glu, lru_conv_w, lru_conv_b, lru_wa, lru_ba, lru_wx, lru_bx, lru_lambda, ret_gn, w_branch, w_out, norm_ffn, ffn_w1, ffn_w3, ffn_w2, moe_router, moe_w1, moe_w3, moe_w2, norm_final):
    W = dict(norm_mix=norm_mix, w_in=w_in, ssd_conv_w=ssd_conv_w, ssd_conv_b=ssd_conv_b, ssd_dt_bias=ssd_dt_bias,
             ssd_a_log=ssd_a_log, ssd_d=ssd_d, ssd_norm=ssd_norm, s5_lambda_re=s5_lambda_re,
             s5_lambda_im=s5_lambda_im, s5_b_re=s5_b_re, s5_b_im=s5_b_im, s5_c_re=s5_c_re, s5_c_im=s5_c_im,
             s5_d=s5_d, s5_log_dt=s5_log_dt, s5_glu=s5_glu, lru_conv_w=lru_conv_w, lru_conv_b=lru_conv_b,
             lru_wa=lru_wa, lru_ba=lru_ba, lru_wx=lru_wx, lru_bx=lru_bx, lru_lambda=lru_lambda, ret_gn=ret_gn,
             w_branch=w_branch, w_out=w_out, norm_ffn=norm_ffn, moe_router=moe_router, norm_final=norm_final,
             ffn_w1=ffn_w1, ffn_w3=ffn_w3, ffn_w2=ffn_w2, moe_w1=moe_w1, moe_w3=moe_w3, moe_w2=moe_w2)
    P = _prep_params(W)
    y_p, sp = _trunk_prompt(x_prompt, P)
    st = dict(ssd=state_ssd, ssd_conv=state_ssd_conv, s5_re=state_s5_re, s5_im=state_s5_im,
              lru=state_lru, lru_conv=state_lru_conv, ret=state_ret)
    past_len = 16384
    y_s, ss = _trunk_sample(x_sample, past_len, st, P)
    names = ("ssd", "ssd_conv", "s5_re", "s5_im", "lru", "lru_conv", "ret")
    return (y_p, y_s) + tuple(sp[n] for n in names) + tuple(ss[n] for n in names)
```
